```python
import math
import jax, jax.numpy as jnp
from jax import lax
import numpy as np

D_MODEL = 1024
BATCH = 8
SEQ = 2048
DEPTH = 4
DEC_BATCH = 8
DEC_SEQ = 64
PAST_LEN = 1024

CHUNK = 64
N_MIXERS = 2
N_A = (DEPTH + 1) // 2
N_B = DEPTH // 2
A_HEADS = 8
A_DK = D_MODEL // A_HEADS // 2
A_DV = 2 * A_DK
A_QK = A_HEADS * 2 * A_DK
A_V = A_HEADS * A_DV
B_Q_HEADS = 16
B_KV_HEADS = 2
B_GROUP = B_Q_HEADS // B_KV_HEADS
B_HD = 64
WINDOW = 128
W_CHUNKS = WINDOW // CHUNK
NUM_BUCKETS = 32
MAX_DIST = 128
N_MAPS = 16
D_FF = 4 * D_MODEL
QBLOCK = 128
EPS = 1e-6
NEG = -1e30

kernel_name = "hybrid_diffattn_swa_sink_stream_step"


def rms_norm(x, g):
    xf = x.astype(jnp.float32)
    y = xf * lax.rsqrt(jnp.mean(xf * xf, axis=-1, keepdims=True) + EPS)
    return (y * g.astype(jnp.float32)).astype(x.dtype)


def t5_bucket(rel):
    nb = NUM_BUCKETS // 2
    n = -rel
    ret = jnp.where(n < 0, nb, 0)
    n = jnp.abs(n)
    max_exact = nb // 2
    nf = jnp.maximum(n, 1).astype(jnp.float32)
    large = max_exact + (jnp.log(nf / max_exact) / math.log(MAX_DIST / max_exact)
                         * (nb - max_exact)).astype(jnp.int32)
    large = jnp.minimum(large, nb - 1)
    return ret + jnp.where(n < max_exact, n, large)


def rel_bias(q_pos, k_pos, table):
    return table[t5_bucket(k_pos[None, :] - q_pos[:, None])].astype(jnp.float32)


def lambda_init(layer):
    return 0.8 - 0.6 * math.exp(-0.3 * layer)


def diff_project(h, w_qkv):
    B, S = h.shape[:2]
    qkv = h @ w_qkv
    q = qkv[..., :A_QK].reshape(B, S, A_HEADS, 2, A_DK)
    k = qkv[..., A_QK:2 * A_QK].reshape(B, S, A_HEADS, 2, A_DK)
    v = qkv[..., 2 * A_QK:].reshape(B, S, A_HEADS, A_DV)
    return q, k, v


def diff_lambda(lam_p, lam_init):
    lp = lam_p.astype(jnp.float32)
    return jnp.exp(jnp.sum(lp[0] * lp[1])) - jnp.exp(jnp.sum(lp[2] * lp[3])) + lam_init


def diff_attn_core(q, k, v, bias, mask, lam):
    s = jnp.einsum('bqhmd,bkhmd->bhmqk', q, k, preferred_element_type=jnp.float32) * (A_DK ** -0.5)
    s = s + jnp.transpose(bias, (2, 3, 0, 1))[None]
    if mask is not None:
        s = jnp.where(mask, s, NEG)
    p = jax.nn.softmax(s, axis=-1)
    a = p[:, :, 0] - lam * p[:, :, 1]
    return jnp.einsum('bhqk,bkhe->bqhe', a.astype(v.dtype), v)


def diff_out(o, subln_g, w_o, lam_init):
    B, S = o.shape[:2]
    o = rms_norm(o, subln_g) * (1.0 - lam_init)
    return o.reshape(B, S, A_V) @ w_o


def diff_prompt(h, w_qkv, lam_p, subln_g, w_o, table, lam_init):
    B, S = h.shape[:2]
    q, k, v = diff_project(h, w_qkv)
    lam = diff_lambda(lam_p, lam_init)
    kpos = jnp.arange(S)
    kchunk = kpos // CHUNK
    nblk = S // QBLOCK
    qb = jnp.moveaxis(q.reshape(B, nblk, QBLOCK, A_HEADS, 2, A_DK), 1, 0)

    def one_block(args):
        qi, bi = args
        qpos = bi * QBLOCK + jnp.arange(QBLOCK)
        mask = kchunk[None, :] <= (qpos // CHUNK)[:, None]
        bias = rel_bias(qpos, kpos, table).reshape(QBLOCK, S, A_HEADS, 2)
        return diff_attn_core(qi, k, v, bias, mask, lam)

    o = lax.map(one_block, (qb, jnp.arange(nblk)))
    o = jnp.moveaxis(o, 0, 1).reshape(B, S, A_HEADS, A_DV)
    return diff_out(o, subln_g, w_o, lam_init), k, v


def diff_sample(h, cache_k, cache_v, w_qkv, lam_p, subln_g, w_o, table, lam_init):
    T = h.shape[1]
    P = cache_k.shape[1]
    q, k, v = diff_project(h, w_qkv)
    lam = diff_lambda(lam_p, lam_init)
    kk = jnp.concatenate([cache_k, k], axis=1)
    vv = jnp.concatenate([cache_v, v], axis=1)
    bias = rel_bias(P + jnp.arange(T), jnp.arange(P + T), table).reshape(T, P + T, A_HEADS, 2)
    o = diff_attn_core(q, kk, vv, bias, None, lam)
    return diff_out(o, subln_g, w_o, lam_init), k, v


def swa_project(h, w_qkv):
    B, S = h.shape[:2]
    qkv = h @ w_qkv
    nq, nk = B_Q_HEADS * B_HD, B_KV_HEADS * B_HD
    q = qkv[..., :nq].reshape(B, S, B_KV_HEADS, B_GROUP, B_HD)
    k = qkv[..., nq:nq + nk].reshape(B, S, B_KV_HEADS, B_HD)
    v = qkv[..., nq + nk:].reshape(B, S, B_KV_HEADS, B_HD)
    return q, k, v


def sink_attn_core(q, k, v, bias, mask, sink):
    s = jnp.einsum('bqngd,bknd->bngqk', q, k, preferred_element_type=jnp.float32) * (B_HD ** -0.5)
    s = s + jnp.transpose(bias, (2, 3, 0, 1))[None]
    if mask is not None:
        s = jnp.where(mask, s, NEG)
    sk = jnp.broadcast_to(sink.astype(jnp.float32)[None, :, :, None, None], s.shape[:-1] + (1,))
    p = jax.nn.softmax(jnp.concatenate([s, sk], axis=-1), axis=-1)[..., :-1]
    return jnp.einsum('bngqk,bknd->bqngd', p.astype(v.dtype), v)


def swa_prompt(h, w_qkv, sinks, w_o, table):
    B, S = h.shape[:2]
    q, k, v = swa_project(h, w_qkv)
    nc = S // CHUNK
    band_len = (W_CHUNKS + 1) * CHUNK

    def band(t):
        tc = t.reshape(B, nc, CHUNK, B_KV_HEADS, B_HD)
        tp = jnp.pad(tc, ((0, 0), (W_CHUNKS, 0), (0, 0), (0, 0), (0, 0)))
        return jnp.concatenate([tp[:, j:j + nc] for j in range(W_CHUNKS + 1)], axis=2)

    kb, vb = band(k), band(v)
    key_chunk = jnp.arange(nc)[:, None] - W_CHUNKS + jnp.arange(band_len)[None, :] // CHUNK
    mask = key_chunk >= 0
    bias = rel_bias(jnp.arange(CHUNK), jnp.arange(band_len) - W_CHUNKS * CHUNK, table)
    bias = bias.reshape(CHUNK, band_len, B_KV_HEADS, B_GROUP)
    qc = q.reshape(B, nc, CHUNK, B_KV_HEADS, B_GROUP, B_HD)
    sink = sinks.reshape(B_KV_HEADS, B_GROUP)
    o = jax.vmap(sink_attn_core, in_axes=(1, 1, 1, None, 0, None), out_axes=1)(qc, kb, vb, bias, mask, sink)
    o = o.reshape(B, S, B_Q_HEADS * B_HD)
    return o @ w_o, k[:, S - WINDOW:], v[:, S - WINDOW:]


def swa_sample(h, cache_k, cache_v, w_qkv, sinks, w_o, table):
    B, T = h.shape[:2]
    W = cache_k.shape[1]
    q, k, v = swa_project(h, w_qkv)
    kk = jnp.concatenate([cache_k, k], axis=1)
    vv = jnp.concatenate([cache_v, v], axis=1)
    bias = rel_bias(jnp.arange(T), jnp.arange(W + T) - W, table).reshape(T, W + T, B_KV_HEADS, B_GROUP)
    o = sink_attn_core(q, kk, vv, bias, None, sinks.reshape(B_KV_HEADS, B_GROUP))
    o = o.reshape(B, T, B_Q_HEADS * B_HD)
    return o @ w_o, kk[:, T:], vv[:, T:]


def sq_relu_mlp(h, w_up, w_down):
    return jnp.square(jax.nn.relu(h @ w_up)) @ w_down


def run_trunk(x, sample, cache_a_k, cache_a_v, cache_b_k, cache_b_v, rel_table,
              norm_mix_g, norm_mlp_g, final_norm_g, a_w_qkv, a_lambda, a_subln_g, a_w_o,
              b_w_qkv, b_sinks, b_w_o, mlp_w_up, mlp_w_down):
    ak, av, bk, bv = [], [], [], []
    for i in range(DEPTH):
        j = i // N_MIXERS
        h = rms_norm(x, norm_mix_g[i])
        if i % N_MIXERS == 0:
            li = lambda_init(i)
            if sample:
                o, k, v = diff_sample(h, cache_a_k[j], cache_a_v[j], a_w_qkv[j], a_lambda[j],
                                      a_subln_g[j], a_w_o[j], rel_table, li)
            else:
                o, k, v = diff_prompt(h, a_w_qkv[j], a_lambda[j], a_subln_g[j], a_w_o[j], rel_table, li)
            ak.append(k)
            av.append(v)
        else:
            if sample:
                o, k, v = swa_sample(h, cache_b_k[j], cache_b_v[j], b_w_qkv[j], b_sinks[j], b_w_o[j], rel_table)
            else:
                o, k, v = swa_prompt(h, b_w_qkv[j], b_sinks[j], b_w_o[j], rel_table)
            bk.append(k)
            bv.append(v)
        x = x + o
        x = x + sq_relu_mlp(rms_norm(x, norm_mlp_g[i]), mlp_w_up[i], mlp_w_down[i])
    return rms_norm(x, final_norm_g), jnp.stack(ak), jnp.stack(av), jnp.stack(bk), jnp.stack(bv)


def setup_inputs(seed: int = 0) -> dict:
    key = jax.random.key(seed)
    ks = jax.random.split(key, 20)
    f32 = jnp.float32
    nrm = lambda k, shape, s: (jax.random.normal(k, shape, f32) * s)
    b_cols = (B_Q_HEADS + 2 * B_KV_HEADS) * B_HD
    return {
        "x_prompt": nrm(ks[0], (BATCH, SEQ, D_MODEL), 1.0),
        "x_sample": nrm(ks[1], (DEC_BATCH, DEC_SEQ, D_MODEL), 1.0),
        "cache_a_k": nrm(ks[2], (N_A, DEC_BATCH, PAST_LEN, A_HEADS, 2, A_DK), 1.0),
        "cache_a_v": nrm(ks[3], (N_A, DEC_BATCH, PAST_LEN, A_HEADS, A_DV), 1.0),
        "cache_b_k": nrm(ks[4], (N_B, DEC_BATCH, WINDOW, B_KV_HEADS, B_HD), 1.0),
        "cache_b_v": nrm(ks[5], (N_B, DEC_BATCH, WINDOW, B_KV_HEADS, B_HD), 1.0),
        "rel_table": nrm(ks[6], (NUM_BUCKETS, N_MAPS), 0.5),
        "norm_mix_g": 1.0 + nrm(ks[7], (DEPTH, D_MODEL), 0.01),
        "norm_mlp_g": 1.0 + nrm(ks[8], (DEPTH, D_MODEL), 0.01),
        "final_norm_g": 1.0 + nrm(ks[9], (D_MODEL,), 0.01),
        "a_w_qkv": nrm(ks[10], (N_A, D_MODEL, 2 * A_QK + A_V), D_MODEL ** -0.5),
        "a_lambda": nrm(ks[11], (N_A, 4, A_DK), 0.1),
        "a_subln_g": 1.0 + nrm(ks[12], (N_A, A_DV), 0.01),
        "a_w_o": nrm(ks[13], (N_A, A_V, D_MODEL), A_V ** -0.5),
        "b_w_qkv": nrm(ks[14], (N_B, D_MODEL, b_cols), D_MODEL ** -0.5),
        "b_sinks": nrm(ks[15], (N_B, B_Q_HEADS), 0.5),
        "b_w_o": nrm(ks[16], (N_B, B_Q_HEADS * B_HD, D_MODEL), (B_Q_HEADS * B_HD) ** -0.5),
        "mlp_w_up": nrm(ks[17], (DEPTH, D_MODEL, D_FF), D_MODEL ** -0.5),
        "mlp_w_down": nrm(ks[18], (DEPTH, D_FF, D_MODEL), D_FF ** -0.5),
    }


def reference(x_prompt, x_sample, cache_a_k, cache_a_v, cache_b_k, cache_b_v, rel_table,
              norm_mix_g, norm_mlp_g, final_norm_g, a_w_qkv, a_lambda, a_subln_g, a_w_o,
              b_w_qkv, b_sinks, b_w_o, mlp_w_up, mlp_w_down):
    y_prompt, a_k_prompt, a_v_prompt, b_k_prompt, b_v_prompt = run_trunk(
        x_prompt, False, None, None, None, None, rel_table, norm_mix_g, norm_mlp_g, final_norm_g,
        a_w_qkv, a_lambda, a_subln_g, a_w_o, b_w_qkv, b_sinks, b_w_o, mlp_w_up, mlp_w_down)
    y_sample, a_k_sample, a_v_sample, b_k_sample, b_v_sample = run_trunk(
        x_sample, True, cache_a_k, cache_a_v, cache_b_k, cache_b_v, rel_table, norm_mix_g, norm_mlp_g,
        final_norm_g, a_w_qkv, a_lambda, a_subln_g, a_w_o, b_w_qkv, b_sinks, b_w_o, mlp_w_up, mlp_w_down)
    return (y_prompt, y_sample, a_k_prompt, a_v_prompt, b_k_prompt, b_v_prompt,
            a_k_sample, a_v_sample, b_k_sample, b_v_sample)
```

```python
import functools
import math

import jax
import jax.numpy as jnp
import numpy as np
from jax import lax
from jax.experimental import pallas as pl
from jax.experimental.pallas import tpu as pltpu

BF16 = jnp.bfloat16
F32 = jnp.float32

D_MODEL = 1024
DEPTH = 4
CHUNK = 64
A_HEADS = 8
A_DK = 64
A_DV = 128
A_QK = A_HEADS * 2 * A_DK
A_V = A_HEADS * A_DV
B_Q_HEADS = 16
B_KV_HEADS = 2
B_GROUP = B_Q_HEADS // B_KV_HEADS
B_HD = 64
B_Q = B_Q_HEADS * B_HD
B_KV = B_KV_HEADS * B_HD
WINDOW = 128
NUM_BUCKETS = 32
MAX_DIST = 128
N_MAPS = 16
D_FF = 4 * D_MODEL
EPS = 1e-6
NEG = -1e30

LANES = 128
HALF = LANES // 2
VMEM_LIMIT = 56 * 1024 * 1024

NT_DIMS = (((1,), (1,)), ((), ()))


def _lambda_init(layer):
    return 0.8 - 0.6 * math.exp(-0.3 * layer)


def _rms(x, g):
    return x * lax.rsqrt(jnp.mean(x * x, axis=-1, keepdims=True) + EPS) * g


def _params(n_axes):
    return pltpu.CompilerParams(
        dimension_semantics=("arbitrary",) * n_axes, vmem_limit_bytes=VMEM_LIMIT)


def _proj_a_body(x_ref, g_ref, w_ref, q_ref, k_ref, v_ref, kb_ref, vb_ref):
    h = _rms(x_ref[...], g_ref[...]).astype(BF16)
    q = jnp.dot(h, w_ref[:, 0:A_QK], preferred_element_type=F32)
    q_ref[...] = (q * (A_DK ** -0.5)).astype(BF16)
    k = jnp.dot(h, w_ref[:, A_QK:2 * A_QK], preferred_element_type=F32)
    k_ref[...] = k
    kb_ref[...] = k.astype(BF16)
    v = jnp.dot(h, w_ref[:, 2 * A_QK:], preferred_element_type=F32)
    v_ref[...] = v
    vb_ref[...] = v.astype(BF16)


def _proj_a(x, g, w, tm):
    m = x.shape[0]
    row = lambda i: (i, 0)
    full = lambda i: (0, 0)
    return pl.pallas_call(
        _proj_a_body,
        grid=(m // tm,),
        in_specs=[pl.BlockSpec((tm, D_MODEL), row),
                  pl.BlockSpec((1, D_MODEL), full),
                  pl.BlockSpec((D_MODEL, 2 * A_QK + A_V), full)],
        out_specs=[pl.BlockSpec((tm, A_QK), row),
                   pl.BlockSpec((tm, A_QK), row),
                   pl.BlockSpec((tm, A_V), row),
                   pl.BlockSpec((tm, A_QK), row),
                   pl.BlockSpec((tm, A_V), row)],
        out_shape=[jax.ShapeDtypeStruct((m, A_QK), BF16),
                   jax.ShapeDtypeStruct((m, A_QK), F32),
                   jax.ShapeDtypeStruct((m, A_V), F32),
                   jax.ShapeDtypeStruct((m, A_QK), BF16),
                   jax.ShapeDtypeStruct((m, A_V), BF16)],
        compiler_params=_params(1),
        name="proj_a",
    )(x, g, w)


def _proj_b_body(x_ref, g_ref, w_ref, q_ref, k_ref, v_ref, kb_ref, vb_ref):
    h = _rms(x_ref[...], g_ref[...]).astype(BF16)
    q = jnp.dot(h, w_ref[:, 0:B_Q], preferred_element_type=F32)
    q_ref[...] = (q * (B_HD ** -0.5)).astype(BF16)
    k = jnp.dot(h, w_ref[:, B_Q:B_Q + B_KV], preferred_element_type=F32)
    k_ref[...] = k
    kb_ref[...] = jnp.concatenate([k, pltpu.roll(k, HALF, axis=1)], axis=1).astype(BF16)
    v = jnp.dot(h, w_ref[:, B_Q + B_KV:], preferred_element_type=F32)
    v_ref[...] = v
    vb_ref[...] = jnp.concatenate([v, pltpu.roll(v, HALF, axis=1)], axis=1).astype(BF16)


def _proj_b(x, g, w, tm):
    m = x.shape[0]
    row = lambda i: (i, 0)
    full = lambda i: (0, 0)
    return pl.pallas_call(
        _proj_b_body,
        grid=(m // tm,),
        in_specs=[pl.BlockSpec((tm, D_MODEL), row),
                  pl.BlockSpec((1, D_MODEL), full),
                  pl.BlockSpec((D_MODEL, B_Q + 2 * B_KV), full)],
        out_specs=[pl.BlockSpec((tm, B_Q), row),
                   pl.BlockSpec((tm, B_KV), row),
                   pl.BlockSpec((tm, B_KV), row),
                   pl.BlockSpec((tm, 2 * B_KV), row),
                   pl.BlockSpec((tm, 2 * B_KV), row)],
        out_shape=[jax.ShapeDtypeStruct((m, B_Q), BF16),
                   jax.ShapeDtypeStruct((m, B_KV), F32),
                   jax.ShapeDtypeStruct((m, B_KV), F32),
                   jax.ShapeDtypeStruct((m, 2 * B_KV), BF16),
                   jax.ShapeDtypeStruct((m, 2 * B_KV), BF16)],
        compiler_params=_params(1),
        name="proj_b",
    )(x, g, w)


def _t5_bucket(rel):
    nb = NUM_BUCKETS // 2
    n = -rel
    ret = jnp.where(n < 0, nb, 0)
    n = jnp.abs(n)
    max_exact = nb // 2
    nf = jnp.maximum(n, 1).astype(jnp.float32)
    large = max_exact + (jnp.log(nf / max_exact) / math.log(MAX_DIST / max_exact)
                         * (nb - max_exact)).astype(jnp.int32)
    large = jnp.minimum(large, nb - 1)
    return ret + jnp.where(n < max_exact, n, large)


def _check_far_bucket(min_dist):
    nb, max_exact = NUM_BUCKETS // 2, NUM_BUCKETS // 4
    ratio = math.log(min_dist / max_exact) / math.log(MAX_DIST / max_exact)
    assert ratio * (nb - max_exact) >= nb - 1 - max_exact + 1e-3, "far keys need one shared bucket"


def _near_bias(table, tq, w, back, band):
    r = jnp.arange(tq)[:, None]
    j = jnp.arange(w)[None, :] - back
    bias = table[_t5_bucket(j - r)].astype(F32)
    kc, qc = j // CHUNK, r // CHUNK
    mask = kc <= qc
    if band is not None:
        mask = mask & (kc >= qc - band)
    bias = jnp.where(mask[:, :, None], bias, NEG)
    return jnp.transpose(bias, (2, 0, 1))


def _attn_a_body(far_ref, lamp_ref, q_ref, k_ref, v_ref, bias_ref, g_ref, o_ref,
                 sfar_ref, m_ref, l_ref, acc_ref, *, tq, w, back, tkf, i_off, lam_init):
    hh = pl.program_id(1)
    q0 = (pl.program_id(2) + i_off) * tq
    near_start = pl.multiple_of(jnp.maximum(q0 - back, 0), tkf)
    n_far = near_start // tkf
    groups = tkf // LANES

    lp = lamp_ref[...]
    lam = (jnp.exp(jnp.sum(lp[0:1] * lp[1:2], axis=-1, keepdims=True))
           - jnp.exp(jnp.sum(lp[2:3] * lp[3:4], axis=-1, keepdims=True)) + lam_init)

    q = q_ref[0]
    lane = lax.broadcasted_iota(jnp.int32, (tq, LANES), 1)
    zero = jnp.zeros_like(q)
    qs = jnp.concatenate([jnp.where(lane < HALF, q, zero),
                          jnp.where(lane >= HALF, q, zero)], axis=0)
    row = lax.broadcasted_iota(jnp.int32, (2 * tq, 1), 0)
    far_bias = jnp.where(row < tq, far_ref[2 * hh], far_ref[2 * hh + 1])

    m_ref[...] = jnp.full((2 * tq, LANES), -jnp.inf, F32)

    def far_scores(j, carry):
        ks = pl.multiple_of(j * tkf, tkf)
        s = lax.dot_general(qs, k_ref[0, pl.ds(ks, tkf), :], NT_DIMS, preferred_element_type=F32)
        sfar_ref[j] = s
        mx = s[:, 0:LANES]
        for gi in range(1, groups):
            mx = jnp.maximum(mx, s[:, gi * LANES:(gi + 1) * LANES])
        m_ref[...] = jnp.maximum(m_ref[...], mx)
        return carry

    lax.fori_loop(0, n_far, far_scores, 0)

    sn = lax.dot_general(qs, k_ref[0, pl.ds(near_start, w), :], NT_DIMS, preferred_element_type=F32)
    sn = sn + bias_ref[0].reshape(2 * tq, w)
    m_far = jnp.max(m_ref[...], axis=-1, keepdims=True) + far_bias
    m = jnp.maximum(jnp.max(sn, axis=-1, keepdims=True), m_far)

    pn = jnp.exp(sn - m)
    l_near = jnp.sum(pn, axis=-1, keepdims=True)
    acc_ref[...] = jnp.dot(pn.astype(BF16), v_ref[0, pl.ds(near_start, w), :],
                           preferred_element_type=F32)
    l_ref[...] = jnp.zeros((2 * tq, LANES), F32)
    m_ref[...] = jnp.broadcast_to(m - far_bias, (2 * tq, LANES))

    def far_pv(j, carry):
        ks = pl.multiple_of(j * tkf, tkf)
        s = sfar_ref[j]
        mm = m_ref[...]
        ps = [jnp.exp(s[:, gi * LANES:(gi + 1) * LANES] - mm) for gi in range(groups)]
        lsum = ps[0]
        for gi in range(1, groups):
            lsum = lsum + ps[gi]
        l_ref[...] += lsum
        p = jnp.concatenate(ps, axis=1).astype(BF16)
        acc_ref[...] += jnp.dot(p, v_ref[0, pl.ds(ks, tkf), :], preferred_element_type=F32)
        return carry

    lax.fori_loop(0, n_far, far_pv, 0)

    l = l_near + jnp.sum(l_ref[...], axis=-1, keepdims=True)
    o_all = acc_ref[...] / l
    o = o_all[0:tq] - lam * o_all[tq:2 * tq]
    o = _rms(o, g_ref[...]) * (1.0 - lam_init)
    o_ref[0] = o.astype(BF16)


def _attn_a(q, k, v, bias, far, lamp, g, *, tq, w, back, tkf, i_off, lam_init):
    b, sq, _ = q.shape
    sk = k.shape[1]
    nq = sq // tq
    assert tkf % LANES == 0
    assert all(max((i + i_off) * tq - back, 0) % tkf == 0 for i in range(nq))
    assert (nq - 1 + i_off) * tq - back + w <= sk
    n_far_max = max(((nq - 1 + i_off) * tq - back) // tkf, 1)
    body = functools.partial(_attn_a_body, tq=tq, w=w, back=back, tkf=tkf, i_off=i_off,
                             lam_init=lam_init)
    variant = (lambda i: jnp.minimum(i + i_off, 1))
    return pl.pallas_call(
        body,
        grid=(b, A_HEADS, nq),
        in_specs=[pl.BlockSpec(memory_space=pltpu.SMEM),
                  pl.BlockSpec((4, A_DK), lambda bi, h, i: (0, 0)),
                  pl.BlockSpec((1, tq, LANES), lambda bi, h, i: (bi, i, h)),
                  pl.BlockSpec((1, sk, LANES), lambda bi, h, i: (bi, 0, h)),
                  pl.BlockSpec((1, sk, LANES), lambda bi, h, i: (bi, 0, h)),
                  pl.BlockSpec((1, 2, tq, w), lambda bi, h, i: (variant(i), h, 0, 0)),
                  pl.BlockSpec((1, A_DV), lambda bi, h, i: (0, 0))],
        out_specs=pl.BlockSpec((1, tq, LANES), lambda bi, h, i: (bi, i, h)),
        out_shape=jax.ShapeDtypeStruct((b, sq, A_V), BF16),
        scratch_shapes=[pltpu.VMEM((n_far_max, 2 * tq, tkf), F32),
                        pltpu.VMEM((2 * tq, LANES), F32),
                        pltpu.VMEM((2 * tq, LANES), F32),
                        pltpu.VMEM((2 * tq, A_DV), F32)],
        compiler_params=_params(3),
        name="attn_a",
    )(far, lamp, q, k, v, bias, g)


_B_ORDER_A = (0, 2, 4, 6, 9, 11, 13, 15)
_B_ORDER_B = (1, 3, 5, 7, 8, 10, 12, 14)


def _attn_b_body(sink_ref, q_ref, k_ref, v_ref, bias_ref, o_ref, *, tq, w, back, i_off):
    q0 = (pl.program_id(1) + i_off) * tq
    near_start = pl.multiple_of(jnp.maximum(q0 - back, 0), CHUNK)
    kk = k_ref[0, pl.ds(near_start, w), :]
    vv = v_ref[0, pl.ds(near_start, w), :]
    lane = lax.broadcasted_iota(jnp.int32, (tq, LANES), 1)
    lo = lane < HALF

    q_even, q_odd = [], []
    for c in range(B_Q_HEADS // 2):
        qc = q_ref[0, :, c * LANES:(c + 1) * LANES]
        zero = jnp.zeros_like(qc)
        q_even.append(jnp.where(lo, qc, zero))
        q_odd.append(jnp.where(lo, zero, qc))
    half = B_Q_HEADS // 4
    q_sets = (q_even[:half] + q_odd[half:], q_odd[:half] + q_even[half:])
    orders = (_B_ORDER_A, _B_ORDER_B)

    outs = []
    for t in range(2):
        qs = jnp.concatenate(q_sets[t], axis=0)
        s = lax.dot_general(qs, kk[:, t * LANES:(t + 1) * LANES], NT_DIMS,
                            preferred_element_type=F32)
        s = s + bias_ref[0, t].reshape((B_Q_HEADS // 2) * tq, w)
        sink = jnp.concatenate([jnp.full((tq, 1), sink_ref[h], F32) for h in orders[t]], axis=0)
        m = jnp.maximum(jnp.max(s, axis=-1, keepdims=True), sink)
        p = jnp.exp(s - m)
        l = jnp.sum(p, axis=-1, keepdims=True) + jnp.exp(sink - m)
        pv = jnp.dot(p.astype(BF16), vv[:, t * LANES:(t + 1) * LANES], preferred_element_type=F32)
        outs.append(pv / l)

    for c in range(B_Q_HEADS // 2):
        oa = outs[0][c * tq:(c + 1) * tq]
        ob = outs[1][c * tq:(c + 1) * tq]
        oc = jnp.where(lo, oa, ob) if c < half else jnp.where(lo, ob, oa)
        o_ref[0, :, c * LANES:(c + 1) * LANES] = oc.astype(BF16)


def _attn_b(q, k, v, bias, sinks, *, tq, w, back, i_off):
    b, sq, _ = q.shape
    sk = k.shape[1]
    nq = sq // tq
    assert (nq - 1 + i_off) * tq - back + w <= sk
    body = functools.partial(_attn_b_body, tq=tq, w=w, back=back, i_off=i_off)
    variant = (lambda i: jnp.minimum(i + i_off, 1))
    return pl.pallas_call(
        body,
        grid=(b, nq),
        in_specs=[pl.BlockSpec(memory_space=pltpu.SMEM),
                  pl.BlockSpec((1, tq, B_Q), lambda bi, i: (bi, i, 0)),
                  pl.BlockSpec((1, sk, 2 * B_KV), lambda bi, i: (bi, 0, 0)),
                  pl.BlockSpec((1, sk, 2 * B_KV), lambda bi, i: (bi, 0, 0)),
                  pl.BlockSpec((1, 2, B_Q_HEADS // 2, tq, w), lambda bi, i: (variant(i), 0, 0, 0, 0))],
        out_specs=pl.BlockSpec((1, tq, B_Q), lambda bi, i: (bi, i, 0)),
        out_shape=jax.ShapeDtypeStruct((b, sq, B_Q), BF16),
        compiler_params=_params(2),
        name="attn_b",
    )(sinks, q, k, v, bias)


def _post_body(x_ref, o_ref, wo_ref, g_ref, wu_ref, wd_ref, gf_ref, y_ref, hn_ref, *, final):
    f = pl.program_id(1)

    @pl.when(f == 0)
    def _():
        x1 = x_ref[...] + jnp.dot(o_ref[...], wo_ref[...], preferred_element_type=F32)
        y_ref[...] = x1
        hn_ref[...] = _rms(x1, g_ref[...]).astype(BF16)

    u = jnp.dot(hn_ref[...], wu_ref[...], preferred_element_type=F32)
    a = jnp.square(jnp.maximum(u, 0.0)).astype(BF16)
    y_ref[...] += jnp.dot(a, wd_ref[...], preferred_element_type=F32)

    if final:
        @pl.when(f == pl.num_programs(1) - 1)
        def _():
            y_ref[...] = _rms(y_ref[...], gf_ref[...])


def _post(x, o, wo, g, wu, wd, gf, *, tm, tf, final):
    m = x.shape[0]
    row = lambda i, f: (i, 0)
    full = lambda i, f: (0, 0)
    return pl.pallas_call(
        functools.partial(_post_body, final=final),
        grid=(m // tm, D_FF // tf),
        in_specs=[pl.BlockSpec((tm, D_MODEL), row),
                  pl.BlockSpec((tm, D_MODEL), row),
                  pl.BlockSpec((D_MODEL, D_MODEL), full),
                  pl.BlockSpec((1, D_MODEL), full),
                  pl.BlockSpec((D_MODEL, tf), lambda i, f: (0, f)),
                  pl.BlockSpec((tf, D_MODEL), lambda i, f: (f, 0)),
                  pl.BlockSpec((1, D_MODEL), full)],
        out_specs=pl.BlockSpec((tm, D_MODEL), row),
        out_shape=jax.ShapeDtypeStruct((m, D_MODEL), F32),
        scratch_shapes=[pltpu.VMEM((tm, D_MODEL), BF16)],
        compiler_params=_params(2),
        name="post",
    )(x, o, wo, g, wu, wd, gf)


_A_PROMPT = dict(tq=256, w=512, back=256, tkf=256)
_A_SAMPLE = dict(tq=64, w=256, back=128, tkf=128)
_B_PROMPT = dict(tq=128, w=256, back=128)
_B_SAMPLE = dict(tq=64, w=256, back=128)


def _a_bias(table, tq, w, back):
    _check_far_bucket(back + 1)
    return jnp.stack([_near_bias(table, tq, w, 0, None),
                      _near_bias(table, tq, w, back, None)])


def _b_bias(table, tq, w, back):
    def one(bk):
        t = _near_bias(table, tq, w, bk, WINDOW // CHUNK)
        return jnp.stack([t[jnp.array(_B_ORDER_A)], t[jnp.array(_B_ORDER_B)]])
    return jnp.stack([one(0), one(back)])


def _pad_keys(t, total):
    return jnp.pad(t, ((0, 0), (0, total - t.shape[1]), (0, 0)))


def _trunk(x, caches, wts, table, *, tm, tf):
    b, s, _ = x.shape
    sample = caches is not None
    xm = x.reshape(b * s, D_MODEL)
    far = table[NUM_BUCKETS // 2 - 1].astype(F32)
    a_cfg = _A_SAMPLE if sample else _A_PROMPT
    b_cfg = _B_SAMPLE if sample else _B_PROMPT
    a_bias = _a_bias(table, a_cfg["tq"], a_cfg["w"], a_cfg["back"])
    b_bias = _b_bias(table, b_cfg["tq"], b_cfg["w"], b_cfg["back"])
    ak, av, bk, bv = [], [], [], []
    for i in range(DEPTH):
        j = i // 2
        g_mix = wts["norm_mix_g"][i][None, :]
        if i % 2 == 0:
            q, k, v, kb, vb = _proj_a(xm, g_mix, wts["a_w_qkv"][j], tm)
            ak.append(k.reshape(b, s, A_HEADS, 2, A_DK))
            av.append(v.reshape(b, s, A_HEADS, A_DV))
            kb = kb.reshape(b, s, A_QK)
            vb = vb.reshape(b, s, A_V)
            i_off = 0
            if sample:
                ck, cv = caches["a"][j]
                i_off = ck.shape[1] // a_cfg["tq"]
                total = ck.shape[1] - a_cfg["back"] + a_cfg["w"]
                kb = _pad_keys(jnp.concatenate([ck, kb], axis=1), total)
                vb = _pad_keys(jnp.concatenate([cv, vb], axis=1), total)
            o = _attn_a(q.reshape(b, s, A_QK), kb, vb, a_bias, far, wts["a_lambda"][j],
                        wts["a_subln_g"][j][None, :], i_off=i_off, lam_init=_lambda_init(i), **a_cfg)
            wo = wts["a_w_o"][j]
        else:
            q, k, v, kb, vb = _proj_b(xm, g_mix, wts["b_w_qkv"][j], tm)
            k = k.reshape(b, s, B_KV_HEADS, B_HD)
            v = v.reshape(b, s, B_KV_HEADS, B_HD)
            kb = kb.reshape(b, s, 2 * B_KV)
            vb = vb.reshape(b, s, 2 * B_KV)
            i_off = 0
            if sample:
                ck, cv, ckb, cvb = caches["b"][j]
                bk.append(jnp.concatenate([ck, k], axis=1)[:, s:])
                bv.append(jnp.concatenate([cv, v], axis=1)[:, s:])
                i_off = ckb.shape[1] // b_cfg["tq"]
                total = ckb.shape[1] - b_cfg["back"] + b_cfg["w"]
                kb = _pad_keys(jnp.concatenate([ckb, kb], axis=1), total)
                vb = _pad_keys(jnp.concatenate([cvb, vb], axis=1), total)
            else:
                bk.append(k[:, s - WINDOW:])
                bv.append(v[:, s - WINDOW:])
            o = _attn_b(q.reshape(b, s, B_Q), kb, vb, b_bias, wts["b_sinks"][j], i_off=i_off, **b_cfg)
            wo = wts["b_w_o"][j]
        xm = _post(xm, o.reshape(b * s, D_MODEL), wo, wts["norm_mlp_g"][i][None, :],
                   wts["mlp_w_up"][i], wts["mlp_w_down"][i], wts["final_norm_g"][None, :],
                   tm=tm, tf=tf, final=(i == DEPTH - 1))
    return xm.reshape(b, s, D_MODEL), jnp.stack(ak), jnp.stack(av), jnp.stack(bk), jnp.stack(bv)


def _swap_halves(t):
    flat = t.reshape(t.shape[:-2] + (B_KV,))
    swapped = t[..., ::-1, :].reshape(t.shape[:-2] + (B_KV,))
    return jnp.concatenate([flat, swapped], axis=-1).astype(BF16)


def kernel(x_prompt, x_sample, cache_a_k, cache_a_v, cache_b_k, cache_b_v, rel_table,
           norm_mix_g, norm_mlp_g, final_norm_g, a_w_qkv, a_lambda, a_subln_g, a_w_o,
           b_w_qkv, b_sinks, b_w_o, mlp_w_up, mlp_w_down):
    wts = dict(norm_mix_g=norm_mix_g, norm_mlp_g=norm_mlp_g, final_norm_g=final_norm_g,
               a_w_qkv=a_w_qkv.astype(BF16), a_lambda=a_lambda, a_subln_g=a_subln_g,
               a_w_o=a_w_o.astype(BF16), b_w_qkv=b_w_qkv.astype(BF16), b_sinks=b_sinks,
               b_w_o=b_w_o.astype(BF16), mlp_w_up=mlp_w_up.astype(BF16),
               mlp_w_down=mlp_w_down.astype(BF16))
    n_a, db, past = cache_a_k.shape[:3]
    caches = dict(
        a=[(cache_a_k[j].reshape(db, past, A_QK).astype(BF16),
            cache_a_v[j].reshape(db, past, A_V).astype(BF16)) for j in range(n_a)],
        b=[(cache_b_k[j], cache_b_v[j], _swap_halves(cache_b_k[j]), _swap_halves(cache_b_v[j]))
           for j in range(cache_b_k.shape[0])])
    yp, akp, avp, bkp, bvp = _trunk(x_prompt, None, wts, rel_table, tm=512, tf=1024)
    ys, aks, avs, bks, bvs = _trunk(x_sample, caches, wts, rel_table, tm=512, tf=1024)
    return (yp, ys, akp, avp, bkp, bvp, aks, avs, bks, bvs)
```

```python
import functools
import math

import jax
import jax.numpy as jnp
from jax import lax
from jax.experimental import pallas as pl
from jax.experimental.pallas import tpu as pltpu

BF16 = jnp.bfloat16
F32 = jnp.float32

D_MODEL = 1024
DEPTH = 4
CHUNK = 64
A_HEADS = 8
A_DK = 64
A_DV = 128
A_QK = A_HEADS * 2 * A_DK
A_V = A_HEADS * A_DV
B_Q_HEADS = 16
B_KV_HEADS = 2
B_HD = 64
B_Q = B_Q_HEADS * B_HD
B_KV = B_KV_HEADS * B_HD
WINDOW = 128
NUM_BUCKETS = 32
MAX_DIST = 128
N_MAPS = 16
D_FF = 4 * D_MODEL
EPS = 1e-6
NEG = -1e30

LANES = 128
HALF = LANES // 2
VMEM_LIMIT = 56 * 1024 * 1024

NT_DIMS = (((1,), (1,)), ((), ()))


def _lambda_init(layer):
    return 0.8 - 0.6 * math.exp(-0.3 * layer)


def _rms(x, g):
    return x * lax.rsqrt(jnp.mean(x * x, axis=-1, keepdims=True) + EPS) * g


def _params(n_axes):
    return pltpu.CompilerParams(
        dimension_semantics=("arbitrary",) * n_axes, vmem_limit_bytes=VMEM_LIMIT)


def _lane_fold_max(s):
    mx = s[:, 0:LANES]
    for gi in range(1, s.shape[1] // LANES):
        mx = jnp.maximum(mx, s[:, gi * LANES:(gi + 1) * LANES])
    return mx


def _with_ones(v):
    return jnp.concatenate([v, jnp.ones(v.shape, v.dtype)], axis=1)


def _proj_a_body(x_ref, g_ref, w_ref, q_ref, k_ref, v_ref, kb_ref, vb_ref):
    h = _rms(x_ref[...], g_ref[...]).astype(BF16)
    q = jnp.dot(h, w_ref[:, 0:A_QK], preferred_element_type=F32)
    q_ref[...] = (q * (A_DK ** -0.5)).astype(BF16)
    k = jnp.dot(h, w_ref[:, A_QK:2 * A_QK], preferred_element_type=F32)
    k_ref[...] = k
    kb_ref[...] = k.astype(BF16)
    v = jnp.dot(h, w_ref[:, 2 * A_QK:], preferred_element_type=F32)
    v_ref[...] = v
    vb_ref[...] = v.astype(BF16)


def _proj_a(x, g, w, tm):
    m = x.shape[0]
    row = lambda i: (i, 0)
    full = lambda i: (0, 0)
    return pl.pallas_call(
        _proj_a_body,
        grid=(m // tm,),
        in_specs=[pl.BlockSpec((tm, D_MODEL), row),
                  pl.BlockSpec((1, D_MODEL), full),
                  pl.BlockSpec((D_MODEL, 2 * A_QK + A_V), full)],
        out_specs=[pl.BlockSpec((tm, A_QK), row),
                   pl.BlockSpec((tm, A_QK), row),
                   pl.BlockSpec((tm, A_V), row),
                   pl.BlockSpec((tm, A_QK), row),
                   pl.BlockSpec((tm, A_V), row)],
        out_shape=[jax.ShapeDtypeStruct((m, A_QK), BF16),
                   jax.ShapeDtypeStruct((m, A_QK), F32),
                   jax.ShapeDtypeStruct((m, A_V), F32),
                   jax.ShapeDtypeStruct((m, A_QK), BF16),
                   jax.ShapeDtypeStruct((m, A_V), BF16)],
        compiler_params=_params(1),
        name="proj_a",
    )(x, g, w)


def _proj_b_body(x_ref, g_ref, w_ref, q_ref, k_ref, v_ref, kb_ref, vb_ref):
    h = _rms(x_ref[...], g_ref[...]).astype(BF16)
    q = jnp.dot(h, w_ref[:, 0:B_Q], preferred_element_type=F32)
    q_ref[...] = (q * (B_HD ** -0.5)).astype(BF16)
    k = jnp.dot(h, w_ref[:, B_Q:B_Q + B_KV], preferred_element_type=F32)
    k_ref[...] = k
    kb_ref[...] = jnp.concatenate([k, pltpu.roll(k, HALF, axis=1)], axis=1).astype(BF16)
    v = jnp.dot(h, w_ref[:, B_Q + B_KV:], preferred_element_type=F32)
    v_ref[...] = v
    vb_ref[...] = jnp.concatenate([v, pltpu.roll(v, HALF, axis=1)], axis=1).astype(BF16)


def _proj_b(x, g, w, tm):
    m = x.shape[0]
    row = lambda i: (i, 0)
    full = lambda i: (0, 0)
    return pl.pallas_call(
        _proj_b_body,
        grid=(m // tm,),
        in_specs=[pl.BlockSpec((tm, D_MODEL), row),
                  pl.BlockSpec((1, D_MODEL), full),
                  pl.BlockSpec((D_MODEL, B_Q + 2 * B_KV), full)],
        out_specs=[pl.BlockSpec((tm, B_Q), row),
                   pl.BlockSpec((tm, B_KV), row),
                   pl.BlockSpec((tm, B_KV), row),
                   pl.BlockSpec((tm, 2 * B_KV), row),
                   pl.BlockSpec((tm, 2 * B_KV), row)],
        out_shape=[jax.ShapeDtypeStruct((m, B_Q), BF16),
                   jax.ShapeDtypeStruct((m, B_KV), F32),
                   jax.ShapeDtypeStruct((m, B_KV), F32),
                   jax.ShapeDtypeStruct((m, 2 * B_KV), BF16),
                   jax.ShapeDtypeStruct((m, 2 * B_KV), BF16)],
        compiler_params=_params(1),
        name="proj_b",
    )(x, g, w)


def _t5_bucket(rel):
    nb = NUM_BUCKETS // 2
    n = -rel
    ret = jnp.where(n < 0, nb, 0)
    n = jnp.abs(n)
    max_exact = nb // 2
    nf = jnp.maximum(n, 1).astype(jnp.float32)
    large = max_exact + (jnp.log(nf / max_exact) / math.log(MAX_DIST / max_exact)
                         * (nb - max_exact)).astype(jnp.int32)
    large = jnp.minimum(large, nb - 1)
    return ret + jnp.where(n < max_exact, n, large)


def _check_far_bucket(min_dist):
    nb, max_exact = NUM_BUCKETS // 2, NUM_BUCKETS // 4
    ratio = math.log(min_dist / max_exact) / math.log(MAX_DIST / max_exact)
    assert ratio * (nb - max_exact) >= nb - 1 - max_exact + 1e-3, "far keys need one shared bucket"


def _far_bias(table, tq, w):
    return jnp.broadcast_to(table[NUM_BUCKETS // 2 - 1].astype(F32)[:, None, None], (N_MAPS, tq, w))


def _near_bias(table, tq, w, back, band):
    span = tq + w - 1
    rel = jnp.arange(span) - (tq - 1) - back
    vec = table[_t5_bucket(rel)].astype(F32).T
    period = jnp.concatenate([vec, jnp.zeros((N_MAPS, 1), F32)], axis=1)
    flat = jnp.tile(period, (1, tq))
    bias = flat[:, tq - 1:tq - 1 + tq * span].reshape(N_MAPS, tq, span)[:, :, :w]
    r = jnp.arange(tq)[:, None]
    j = jnp.arange(w)[None, :] - back
    kc, qc = j // CHUNK, r // CHUNK
    mask = kc <= qc
    if band is not None:
        mask = mask & (kc >= qc - band)
    return jnp.where(mask[None], bias, NEG)


A_TILE = 256


def _diff_lambda(lamp_ref, lam_init):
    lp = lamp_ref[...]
    return (jnp.exp(jnp.sum(lp[0:1] * lp[1:2], axis=-1, keepdims=True))
            - jnp.exp(jnp.sum(lp[2:3] * lp[3:4], axis=-1, keepdims=True)) + lam_init)


def _stack_maps(q):
    lane = lax.broadcasted_iota(jnp.int32, q.shape, 1)
    zero = jnp.zeros_like(q)
    return jnp.concatenate([jnp.where(lane < HALF, q, zero), jnp.where(lane >= HALF, q, zero)], axis=0)


def _diff_epilogue(acc, lam, g, lam_init):
    tq = acc.shape[0] // 2
    o_all = acc[:, :A_DV] / acc[:, A_DV:]
    o = o_all[:tq] - lam * o_all[tq:]
    return (_rms(o, g) * (1.0 - lam_init)).astype(BF16)


def _attn_a_prompt_body(lamp_ref, q_ref, k_ref, v_ref, bias_ref, g_ref, o_ref,
                        s_ref, p_ref, v1_ref, *, nt, lam_init):
    tl = A_TILE
    lam = _diff_lambda(lamp_ref, lam_init)
    v1_ref[:, 0:A_DV] = v_ref[0]
    v1_ref[:, A_DV:] = jnp.ones((nt * tl, A_DV), BF16)
    for i in range(nt):
        buf = i % 2
        qs = _stack_maps(q_ref[0, i * tl:(i + 1) * tl, :])
        mx = None
        for kt in range(i + 1):
            cols = slice(kt * tl, (kt + 1) * tl)
            s = lax.dot_general(qs, k_ref[0, cols, :], NT_DIMS, preferred_element_type=F32)
            s = s + bias_ref[2 - min(i - kt, 2)].reshape(2 * tl, tl)
            s_ref[buf, :, cols] = s
            fold = _lane_fold_max(s)
            mx = fold if mx is None else jnp.maximum(mx, fold)
        m = jnp.broadcast_to(jnp.max(mx, axis=-1, keepdims=True), (2 * tl, LANES))
        for kt in range(i + 1):
            for gi in range(kt * tl // LANES, (kt + 1) * tl // LANES):
                cols = slice(gi * LANES, (gi + 1) * LANES)
                p_ref[buf, :, cols] = jnp.exp(s_ref[buf, :, cols] - m).astype(BF16)
        nk = (i + 1) * tl
        acc = jnp.dot(p_ref[buf, :, 0:nk], v1_ref[0:nk, :], preferred_element_type=F32)
        o_ref[0, i * tl:(i + 1) * tl, :] = _diff_epilogue(acc, lam, g_ref[...], lam_init)


def _attn_a_prompt(q, k, v, bias, lamp, g, *, lam_init):
    b, s, _ = q.shape
    nt = s // A_TILE
    assert s % A_TILE == 0
    seq = pl.BlockSpec((1, s, LANES), lambda bi, h: (bi, 0, h))
    return pl.pallas_call(
        functools.partial(_attn_a_prompt_body, nt=nt, lam_init=lam_init),
        grid=(b, A_HEADS),
        in_specs=[pl.BlockSpec((4, A_DK), lambda bi, h: (0, 0)),
                  seq, seq, seq,
                  pl.BlockSpec((3, 2, A_TILE, A_TILE), lambda bi, h: (0, h, 0, 0)),
                  pl.BlockSpec((1, A_DV), lambda bi, h: (0, 0))],
        out_specs=seq,
        out_shape=jax.ShapeDtypeStruct((b, s, A_V), BF16),
        scratch_shapes=[pltpu.VMEM((2, 2 * A_TILE, s), F32),
                        pltpu.VMEM((2, 2 * A_TILE, s), BF16),
                        pltpu.VMEM((s, 2 * A_DV), BF16)],
        compiler_params=_params(2),
        name="attn_a_prompt",
    )(lamp, q, k, v, bias, g)


def _attn_a_sample_body(lamp_ref, q_ref, k_ref, v_ref, bias_ref, g_ref, o_ref, *, lam_init):
    tq, sk = q_ref.shape[1], k_ref.shape[1]
    lam = _diff_lambda(lamp_ref, lam_init)
    for h in range(A_HEADS):
        cols = slice(h * LANES, (h + 1) * LANES)
        s = lax.dot_general(_stack_maps(q_ref[0, :, cols]), k_ref[0, :, cols], NT_DIMS,
                            preferred_element_type=F32)
        s = s + bias_ref[2 * h:2 * h + 2].reshape(2 * tq, sk)
        m = jnp.broadcast_to(jnp.max(_lane_fold_max(s), axis=-1, keepdims=True), (2 * tq, LANES))
        p = jnp.concatenate([jnp.exp(s[:, gi * LANES:(gi + 1) * LANES] - m)
                             for gi in range(sk // LANES)], axis=1).astype(BF16)
        acc = jnp.dot(p, _with_ones(v_ref[0, :, cols]), preferred_element_type=F32)
        o_ref[0, :, cols] = _diff_epilogue(acc, lam, g_ref[...], lam_init)


def _attn_a_sample(q, k, v, bias, lamp, g, *, lam_init):
    b, tq, _ = q.shape
    sk = k.shape[1]
    return pl.pallas_call(
        functools.partial(_attn_a_sample_body, lam_init=lam_init),
        grid=(b,),
        in_specs=[pl.BlockSpec((4, A_DK), lambda bi: (0, 0)),
                  pl.BlockSpec((1, tq, A_QK), lambda bi: (bi, 0, 0)),
                  pl.BlockSpec((1, sk, A_QK), lambda bi: (bi, 0, 0)),
                  pl.BlockSpec((1, sk, A_V), lambda bi: (bi, 0, 0)),
                  pl.BlockSpec((N_MAPS, tq, sk), lambda bi: (0, 0, 0)),
                  pl.BlockSpec((1, A_DV), lambda bi: (0, 0))],
        out_specs=pl.BlockSpec((1, tq, A_V), lambda bi: (bi, 0, 0)),
        out_shape=jax.ShapeDtypeStruct((b, tq, A_V), BF16),
        compiler_params=_params(1),
        name="attn_a_sample",
    )(lamp, q, k, v, bias, g)


_B_ORDER_A = (0, 2, 4, 6, 9, 11, 13, 15)
_B_ORDER_B = (1, 3, 5, 7, 8, 10, 12, 14)


def _attn_b_body(sink_ref, q_ref, k_ref, v_ref, bias_ref, o_ref, *, tq, w, back, i_off):
    q0 = (pl.program_id(1) + i_off) * tq
    near_start = pl.multiple_of(jnp.maximum(q0 - back, 0), CHUNK)
    kk = k_ref[0, pl.ds(near_start, w), :]
    vv = v_ref[0, pl.ds(near_start, w), :]
    lane = lax.broadcasted_iota(jnp.int32, (tq, LANES), 1)
    lo = lane < HALF

    q_even, q_odd = [], []
    for c in range(B_Q_HEADS // 2):
        qc = q_ref[0, :, c * LANES:(c + 1) * LANES]
        zero = jnp.zeros_like(qc)
        q_even.append(jnp.where(lo, qc, zero))
        q_odd.append(jnp.where(lo, zero, qc))
    half = B_Q_HEADS // 4
    q_sets = (q_even[:half] + q_odd[half:], q_odd[:half] + q_even[half:])
    orders = (_B_ORDER_A, _B_ORDER_B)

    outs = []
    for t in range(2):
        qs = jnp.concatenate(q_sets[t], axis=0)
        s = lax.dot_general(qs, kk[:, t * LANES:(t + 1) * LANES], NT_DIMS,
                            preferred_element_type=F32)
        s = s + bias_ref[0, t].reshape((B_Q_HEADS // 2) * tq, w)
        sink = jnp.concatenate([jnp.full((tq, LANES), sink_ref[h], F32) for h in orders[t]], axis=0)
        m = jnp.broadcast_to(jnp.max(_lane_fold_max(s), axis=-1, keepdims=True), sink.shape)
        m = jnp.maximum(m, sink)
        p = jnp.concatenate([jnp.exp(s[:, gi * LANES:(gi + 1) * LANES] - m)
                             for gi in range(w // LANES)], axis=1).astype(BF16)
        acc = jnp.dot(p, _with_ones(vv[:, t * LANES:(t + 1) * LANES]), preferred_element_type=F32)
        outs.append(acc[:, :LANES] / (acc[:, LANES:] + jnp.exp(sink - m)))

    for c in range(B_Q_HEADS // 2):
        oa = outs[0][c * tq:(c + 1) * tq]
        ob = outs[1][c * tq:(c + 1) * tq]
        oc = jnp.where(lo, oa, ob) if c < half else jnp.where(lo, ob, oa)
        o_ref[0, :, c * LANES:(c + 1) * LANES] = oc.astype(BF16)


def _attn_b(q, k, v, bias, sinks, *, tq, w, back, i_off):
    b, sq, _ = q.shape
    sk = k.shape[1]
    nq = sq // tq
    assert (nq - 1 + i_off) * tq - back + w <= sk
    body = functools.partial(_attn_b_body, tq=tq, w=w, back=back, i_off=i_off)
    variant = (lambda i: jnp.minimum(i + i_off, 1))
    return pl.pallas_call(
        body,
        grid=(b, nq),
        in_specs=[pl.BlockSpec(memory_space=pltpu.SMEM),
                  pl.BlockSpec((1, tq, B_Q), lambda bi, i: (bi, i, 0)),
                  pl.BlockSpec((1, sk, 2 * B_KV), lambda bi, i: (bi, 0, 0)),
                  pl.BlockSpec((1, sk, 2 * B_KV), lambda bi, i: (bi, 0, 0)),
                  pl.BlockSpec((1, 2, B_Q_HEADS // 2, tq, w), lambda bi, i: (variant(i), 0, 0, 0, 0))],
        out_specs=pl.BlockSpec((1, tq, B_Q), lambda bi, i: (bi, i, 0)),
        out_shape=jax.ShapeDtypeStruct((b, sq, B_Q), BF16),
        compiler_params=_params(2),
        name="attn_b",
    )(sinks, q, k, v, bias)


def _post_body(x_ref, o_ref, wo_ref, g_ref, wu_ref, wd_ref, gf_ref, y_ref, hn_ref, *, final):
    f = pl.program_id(1)

    @pl.when(f == 0)
    def _():
        x1 = x_ref[...] + jnp.dot(o_ref[...], wo_ref[...], preferred_element_type=F32)
        y_ref[...] = x1
        hn_ref[...] = _rms(x1, g_ref[...]).astype(BF16)

    u = jnp.dot(hn_ref[...], wu_ref[...], preferred_element_type=F32)
    a = jnp.square(jnp.maximum(u, 0.0)).astype(BF16)
    y_ref[...] += jnp.dot(a, wd_ref[...], preferred_element_type=F32)

    if final:
        @pl.when(f == pl.num_programs(1) - 1)
        def _():
            y_ref[...] = _rms(y_ref[...], gf_ref[...])


def _post(x, o, wo, g, wu, wd, gf, *, tm, tf, final):
    m = x.shape[0]
    row = lambda i, f: (i, 0)
    full = lambda i, f: (0, 0)
    return pl.pallas_call(
        functools.partial(_post_body, final=final),
        grid=(m // tm, D_FF // tf),
        in_specs=[pl.BlockSpec((tm, D_MODEL), row),
                  pl.BlockSpec((tm, D_MODEL), row),
                  pl.BlockSpec((D_MODEL, D_MODEL), full),
                  pl.BlockSpec((1, D_MODEL), full),
                  pl.BlockSpec((D_MODEL, tf), lambda i, f: (0, f)),
                  pl.BlockSpec((tf, D_MODEL), lambda i, f: (f, 0)),
                  pl.BlockSpec((1, D_MODEL), full)],
        out_specs=pl.BlockSpec((tm, D_MODEL), row),
        out_shape=jax.ShapeDtypeStruct((m, D_MODEL), F32),
        scratch_shapes=[pltpu.VMEM((tm, D_MODEL), BF16)],
        compiler_params=_params(2),
        name="post",
    )(x, o, wo, g, wu, wd, gf)


_B_PROMPT = dict(tq=128, w=256, back=128)
_B_SAMPLE = dict(tq=64, w=256, back=128)
_A_SAMPLE_BACK = 128


def _a_prompt_bias(table):
    _check_far_bucket(A_TILE + 1)
    near = _near_bias(table, A_TILE, 2 * A_TILE, A_TILE, None)
    return jnp.stack([_far_bias(table, A_TILE, A_TILE), near[:, :, :A_TILE], near[:, :, A_TILE:]])


def _a_sample_bias(table, past, t):
    _check_far_bucket(_A_SAMPLE_BACK + 1)
    far = _far_bias(table, t, past - _A_SAMPLE_BACK)
    return jnp.concatenate([far, _near_bias(table, t, _A_SAMPLE_BACK + 2 * t, _A_SAMPLE_BACK, None)],
                           axis=2)


def _b_bias(table, tq, w, back):
    def one(bk):
        t = _near_bias(table, tq, w, bk, WINDOW // CHUNK)
        return jnp.stack([t[jnp.array(_B_ORDER_A)], t[jnp.array(_B_ORDER_B)]])
    return jnp.stack([one(0), one(back)])


def _pad_keys(t, total):
    return jnp.pad(t, ((0, 0), (0, total - t.shape[1]), (0, 0)))


def _trunk(x, caches, wts, table, *, tm, tf):
    b, s, _ = x.shape
    sample = caches is not None
    xm = x.reshape(b * s, D_MODEL)
    b_cfg = _B_SAMPLE if sample else _B_PROMPT
    b_bias = _b_bias(table, **b_cfg)
    if sample:
        past = caches["a"][0][0].shape[1]
        a_bias = _a_sample_bias(table, past, s)
    else:
        a_bias = _a_prompt_bias(table)
    ak, av, bk, bv = [], [], [], []
    for i in range(DEPTH):
        j = i // 2
        g_mix = wts["norm_mix_g"][i][None, :]
        if i % 2 == 0:
            q, k, v, kb, vb = _proj_a(xm, g_mix, wts["a_w_qkv"][j], tm)
            ak.append(k.reshape(b, s, A_HEADS, 2, A_DK))
            av.append(v.reshape(b, s, A_HEADS, A_DV))
            q = q.reshape(b, s, A_QK)
            kb = kb.reshape(b, s, A_QK)
            vb = vb.reshape(b, s, A_V)
            args = (wts["a_lambda"][j], wts["a_subln_g"][j][None, :])
            if sample:
                ck, cv = caches["a"][j]
                kb = _pad_keys(jnp.concatenate([ck, kb], axis=1), past + 2 * s)
                vb = _pad_keys(jnp.concatenate([cv, vb], axis=1), past + 2 * s)
                o = _attn_a_sample(q, kb, vb, a_bias, *args, lam_init=_lambda_init(i))
            else:
                o = _attn_a_prompt(q, kb, vb, a_bias, *args, lam_init=_lambda_init(i))
            wo = wts["a_w_o"][j]
        else:
            q, k, v, kb, vb = _proj_b(xm, g_mix, wts["b_w_qkv"][j], tm)
            k = k.reshape(b, s, B_KV_HEADS, B_HD)
            v = v.reshape(b, s, B_KV_HEADS, B_HD)
            kb = kb.reshape(b, s, 2 * B_KV)
            vb = vb.reshape(b, s, 2 * B_KV)
            i_off = 0
            if sample:
                ck, cv, ckb, cvb = caches["b"][j]
                bk.append(jnp.concatenate([ck, k], axis=1)[:, s:])
                bv.append(jnp.concatenate([cv, v], axis=1)[:, s:])
                i_off = ckb.shape[1] // b_cfg["tq"]
                total = ckb.shape[1] - b_cfg["back"] + b_cfg["w"]
                kb = _pad_keys(jnp.concatenate([ckb, kb], axis=1), total)
                vb = _pad_keys(jnp.concatenate([cvb, vb], axis=1), total)
            else:
                bk.append(k[:, s - WINDOW:])
                bv.append(v[:, s - WINDOW:])
            o = _attn_b(q.reshape(b, s, B_Q), kb, vb, b_bias, wts["b_sinks"][j], i_off=i_off, **b_cfg)
            wo = wts["b_w_o"][j]
        xm = _post(xm, o.reshape(b * s, D_MODEL), wo, wts["norm_mlp_g"][i][None, :],
                   wts["mlp_w_up"][i], wts["mlp_w_down"][i], wts["final_norm_g"][None, :],
                   tm=tm, tf=tf, final=(i == DEPTH - 1))
    return xm.reshape(b, s, D_MODEL), jnp.stack(ak), jnp.stack(av), jnp.stack(bk), jnp.stack(bv)


def _swap_halves(t):
    flat = t.reshape(t.shape[:-2] + (B_KV,))
    swapped = t[..., ::-1, :].reshape(t.shape[:-2] + (B_KV,))
    return jnp.concatenate([flat, swapped], axis=-1).astype(BF16)


def kernel(x_prompt, x_sample, cache_a_k, cache_a_v, cache_b_k, cache_b_v, rel_table,
           norm_mix_g, norm_mlp_g, final_norm_g, a_w_qkv, a_lambda, a_subln_g, a_w_o,
           b_w_qkv, b_sinks, b_w_o, mlp_w_up, mlp_w_down):
    wts = dict(norm_mix_g=norm_mix_g, norm_mlp_g=norm_mlp_g, final_norm_g=final_norm_g,
               a_w_qkv=a_w_qkv.astype(BF16), a_lambda=a_lambda, a_subln_g=a_subln_g,
               a_w_o=a_w_o.astype(BF16), b_w_qkv=b_w_qkv.astype(BF16), b_sinks=b_sinks,
               b_w_o=b_w_o.astype(BF16), mlp_w_up=mlp_w_up.astype(BF16),
               mlp_w_down=mlp_w_down.astype(BF16))
    n_a, db, past = cache_a_k.shape[:3]
    caches = dict(
        a=[(cache_a_k[j].reshape(db, past, A_QK).astype(BF16),
            cache_a_v[j].reshape(db, past, A_V).astype(BF16)) for j in range(n_a)],
        b=[(cache_b_k[j], cache_b_v[j], _swap_halves(cache_b_k[j]), _swap_halves(cache_b_v[j]))
           for j in range(cache_b_k.shape[0])])
    yp, akp, avp, bkp, bvp = _trunk(x_prompt, None, wts, rel_table, tm=512, tf=1024)
    ys, aks, avs, bks, bvs = _trunk(x_sample, caches, wts, rel_table, tm=512, tf=1024)
    return (yp, ys, akp, avp, bkp, bvp, aks, avs, bks, bvs)
```

```python
import functools
import math

import jax
import jax.numpy as jnp
from jax import lax
from jax.experimental import pallas as pl
from jax.experimental.pallas import tpu as pltpu

BF16 = jnp.bfloat16
F32 = jnp.float32

D_MODEL = 1024
DEPTH = 4
CHUNK = 64
A_HEADS = 8
A_DK = 64
A_DV = 128
A_QK = A_HEADS * 2 * A_DK
A_V = A_HEADS * A_DV
B_Q_HEADS = 16
B_KV_HEADS = 2
B_HD = 64
B_Q = B_Q_HEADS * B_HD
B_KV = B_KV_HEADS * B_HD
WINDOW = 128
NUM_BUCKETS = 32
MAX_DIST = 128
N_MAPS = 16
D_FF = 4 * D_MODEL
EPS = 1e-6
NEG = -1e30

LANES = 128
SUBLANES = 8
HALF = LANES // 2
VMEM_LIMIT = 56 * 1024 * 1024

NT_DIMS = (((1,), (1,)), ((), ()))


def _lambda_init(layer):
    return 0.8 - 0.6 * math.exp(-0.3 * layer)


def _rms(x, g):
    return x * lax.rsqrt(jnp.mean(x * x, axis=-1, keepdims=True) + EPS) * g


def _params(n_axes):
    return pltpu.CompilerParams(
        dimension_semantics=("arbitrary",) * n_axes, vmem_limit_bytes=VMEM_LIMIT)


def _lane_fold_max(s):
    mx = s[:, 0:LANES]
    for gi in range(1, s.shape[1] // LANES):
        mx = jnp.maximum(mx, s[:, gi * LANES:(gi + 1) * LANES])
    return mx


def _row_max(s):
    return jnp.broadcast_to(jnp.max(_lane_fold_max(s), axis=-1, keepdims=True), (s.shape[0], LANES))


def _exp_rows(s, m):
    return jnp.concatenate([jnp.exp(s[:, gi * LANES:(gi + 1) * LANES] - m)
                            for gi in range(s.shape[1] // LANES)], axis=1).astype(BF16)


def _with_ones(v):
    return jnp.concatenate([v, jnp.ones(v.shape, v.dtype)], axis=1)


def _proj_a_prompt_body(x_ref, g_ref, w_ref, wkt_ref, kt_prev_ref, v_prev_ref,
                        q_ref, kt_ref, v_ref, ktb_ref, vb_ref):
    del kt_prev_ref, v_prev_ref
    h = _rms(x_ref[...], g_ref[...]).astype(BF16)
    q = jnp.dot(h, w_ref[:, 0:A_QK], preferred_element_type=F32)
    q_ref[...] = (q * (A_DK ** -0.5)).astype(BF16)
    kt = lax.dot_general(wkt_ref[...], h, NT_DIMS, preferred_element_type=F32)
    kt_ref[...] = kt
    ktb_ref[...] = kt.astype(BF16)
    v = jnp.dot(h, w_ref[:, 2 * A_QK:], preferred_element_type=F32)
    vb_ref[...] = v.astype(BF16)
    tm = v.shape[0]
    for hh in range(A_HEADS):
        v_ref[pl.ds(hh, tm, stride=A_HEADS), :] = v[:, hh * A_DV:(hh + 1) * A_DV]


def _proj_a_prompt(x, g, w, wkt, layer, kt_prev, v_prev, *, b, s, tm, n_a):
    m = b * s
    per_b = s // tm
    row = lambda i: (i, 0)
    full = lambda i: (0, 0)
    stacked = lambda i: (layer, 0, 0)
    any_spec = pl.BlockSpec(memory_space=pl.ANY)
    first = kt_prev is None
    if first:
        kt_prev = jnp.zeros((1,), F32)
        v_prev = jnp.zeros((1,), F32)
    return pl.pallas_call(
        _proj_a_prompt_body,
        grid=(m // tm,),
        in_specs=[pl.BlockSpec((tm, D_MODEL), row),
                  pl.BlockSpec((1, D_MODEL), full),
                  pl.BlockSpec((None, D_MODEL, 2 * A_QK + A_V), stacked),
                  pl.BlockSpec((None, A_QK, D_MODEL), stacked),
                  any_spec, any_spec],
        out_specs=[pl.BlockSpec((tm, A_QK), row),
                   pl.BlockSpec((None, None, A_QK, tm), lambda i: (layer, i // per_b, 0, i % per_b)),
                   pl.BlockSpec((None, tm * A_HEADS, A_DV), lambda i: (layer, i, 0)),
                   pl.BlockSpec((None, A_QK, tm), lambda i: (i // per_b, 0, i % per_b)),
                   pl.BlockSpec((tm, A_V), row)],
        out_shape=[jax.ShapeDtypeStruct((m, A_QK), BF16),
                   jax.ShapeDtypeStruct((n_a, b, A_QK, s), F32),
                   jax.ShapeDtypeStruct((n_a, m * A_HEADS, A_DV), F32),
                   jax.ShapeDtypeStruct((b, A_QK, s), BF16),
                   jax.ShapeDtypeStruct((m, A_V), BF16)],
        input_output_aliases={} if first else {4: 1, 5: 2},
        compiler_params=_params(1),
        name="proj_a_prompt",
    )(x, g, w, wkt, kt_prev, v_prev)


def _proj_a_sample_body(x_ref, g_ref, w_ref, q_ref, k_ref, v_ref, kb_ref, vb_ref):
    h = _rms(x_ref[...], g_ref[...]).astype(BF16)
    q = jnp.dot(h, w_ref[:, 0:A_QK], preferred_element_type=F32)
    q_ref[...] = (q * (A_DK ** -0.5)).astype(BF16)
    k = jnp.dot(h, w_ref[:, A_QK:2 * A_QK], preferred_element_type=F32)
    k_ref[...] = k
    kb_ref[...] = k.astype(BF16)
    v = jnp.dot(h, w_ref[:, 2 * A_QK:], preferred_element_type=F32)
    v_ref[...] = v
    vb_ref[...] = v.astype(BF16)


def _proj_a_sample(x, g, w, layer, tm):
    m = x.shape[0]
    row = lambda i: (i, 0)
    full = lambda i: (0, 0)
    return pl.pallas_call(
        _proj_a_sample_body,
        grid=(m // tm,),
        in_specs=[pl.BlockSpec((tm, D_MODEL), row),
                  pl.BlockSpec((1, D_MODEL), full),
                  pl.BlockSpec((None, D_MODEL, 2 * A_QK + A_V), lambda i: (layer, 0, 0))],
        out_specs=[pl.BlockSpec((tm, A_QK), row),
                   pl.BlockSpec((tm, A_QK), row),
                   pl.BlockSpec((tm, A_V), row),
                   pl.BlockSpec((tm, A_QK), row),
                   pl.BlockSpec((tm, A_V), row)],
        out_shape=[jax.ShapeDtypeStruct((m, A_QK), BF16),
                   jax.ShapeDtypeStruct((m, A_QK), F32),
                   jax.ShapeDtypeStruct((m, A_V), F32),
                   jax.ShapeDtypeStruct((m, A_QK), BF16),
                   jax.ShapeDtypeStruct((m, A_V), BF16)],
        compiler_params=_params(1),
        name="proj_a_sample",
    )(x, g, w)


def _proj_b_body(x_ref, g_ref, w_ref, q_ref, k_ref, v_ref, kb_ref, vb_ref):
    h = _rms(x_ref[...], g_ref[...]).astype(BF16)
    q = jnp.dot(h, w_ref[:, 0:B_Q], preferred_element_type=F32)
    q_ref[...] = (q * (B_HD ** -0.5)).astype(BF16)
    k = jnp.dot(h, w_ref[:, B_Q:B_Q + B_KV], preferred_element_type=F32)
    k_ref[...] = k
    kb_ref[...] = jnp.concatenate([k, pltpu.roll(k, HALF, axis=1)], axis=1).astype(BF16)
    v = jnp.dot(h, w_ref[:, B_Q + B_KV:], preferred_element_type=F32)
    v_ref[...] = v
    vb_ref[...] = jnp.concatenate([v, pltpu.roll(v, HALF, axis=1)], axis=1).astype(BF16)


def _proj_b(x, g, w, layer, tm):
    m = x.shape[0]
    row = lambda i: (i, 0)
    full = lambda i: (0, 0)
    return pl.pallas_call(
        _proj_b_body,
        grid=(m // tm,),
        in_specs=[pl.BlockSpec((tm, D_MODEL), row),
                  pl.BlockSpec((1, D_MODEL), full),
                  pl.BlockSpec((None, D_MODEL, B_Q + 2 * B_KV), lambda i: (layer, 0, 0))],
        out_specs=[pl.BlockSpec((tm, B_Q), row),
                   pl.BlockSpec((tm, B_KV), row),
                   pl.BlockSpec((tm, B_KV), row),
                   pl.BlockSpec((tm, 2 * B_KV), row),
                   pl.BlockSpec((tm, 2 * B_KV), row)],
        out_shape=[jax.ShapeDtypeStruct((m, B_Q), BF16),
                   jax.ShapeDtypeStruct((m, B_KV), F32),
                   jax.ShapeDtypeStruct((m, B_KV), F32),
                   jax.ShapeDtypeStruct((m, 2 * B_KV), BF16),
                   jax.ShapeDtypeStruct((m, 2 * B_KV), BF16)],
        compiler_params=_params(1),
        name="proj_b",
    )(x, g, w)


def _t5_bucket(rel):
    nb = NUM_BUCKETS // 2
    n = -rel
    ret = jnp.where(n < 0, nb, 0)
    n = jnp.abs(n)
    max_exact = nb // 2
    nf = jnp.maximum(n, 1).astype(jnp.float32)
    large = max_exact + (jnp.log(nf / max_exact) / math.log(MAX_DIST / max_exact)
                         * (nb - max_exact)).astype(jnp.int32)
    large = jnp.minimum(large, nb - 1)
    return ret + jnp.where(n < max_exact, n, large)


def _check_far_bucket(min_dist):
    nb, max_exact = NUM_BUCKETS // 2, NUM_BUCKETS // 4
    ratio = math.log(min_dist / max_exact) / math.log(MAX_DIST / max_exact)
    assert ratio * (nb - max_exact) >= nb - 1 - max_exact + 1e-3, "far keys need one shared bucket"


def _far_bias(table, tq, w):
    return jnp.broadcast_to(table[NUM_BUCKETS // 2 - 1].astype(F32)[:, None, None], (N_MAPS, tq, w))


def _near_bias(table, tq, w, back, band):
    span = tq + w - 1
    rel = jnp.arange(span) - (tq - 1) - back
    vec = table[_t5_bucket(rel)].astype(F32).T
    period = jnp.concatenate([vec, jnp.zeros((N_MAPS, 1), F32)], axis=1)
    flat = jnp.tile(period, (1, tq))
    bias = flat[:, tq - 1:tq - 1 + tq * span].reshape(N_MAPS, tq, span)[:, :, :w]
    r = jnp.arange(tq)[:, None]
    j = jnp.arange(w)[None, :] - back
    kc, qc = j // CHUNK, r // CHUNK
    mask = kc <= qc
    if band is not None:
        mask = mask & (kc >= qc - band)
    return jnp.where(mask[None], bias, NEG)


A_TILE = 256


def _diff_lambda(lamp_ref, lam_init):
    lp = lamp_ref[...]
    return (jnp.exp(jnp.sum(lp[0:1] * lp[1:2], axis=-1, keepdims=True))
            - jnp.exp(jnp.sum(lp[2:3] * lp[3:4], axis=-1, keepdims=True)) + lam_init)


def _stack_maps(q):
    lane = lax.broadcasted_iota(jnp.int32, q.shape, 1)
    zero = jnp.zeros_like(q)
    return jnp.concatenate([jnp.where(lane < HALF, q, zero), jnp.where(lane >= HALF, q, zero)], axis=0)


def _diff_epilogue(acc, lam, g, lam_init):
    tq = acc.shape[0] // 2
    o_all = acc[:, :A_DV] / acc[:, A_DV:]
    o = o_all[:tq] - lam * o_all[tq:]
    return (_rms(o, g) * (1.0 - lam_init)).astype(BF16)


def _attn_a_prompt_body(lamp_ref, q_ref, kt_ref, v_ref, bias_ref, g_ref, o_ref,
                        s_ref, p_ref, v1_ref, *, nt, lam_init):
    tl = A_TILE
    lam = _diff_lambda(lamp_ref, lam_init)
    v1_ref[:, 0:A_DV] = v_ref[0]
    v1_ref[:, A_DV:] = jnp.ones((nt * tl, A_DV), BF16)
    for i in range(nt):
        buf = i % 2
        qs = _stack_maps(q_ref[0, i * tl:(i + 1) * tl, :])
        mx = None
        for kt in range(i + 1):
            cols = slice(kt * tl, (kt + 1) * tl)
            s = jnp.dot(qs, kt_ref[0, :, cols], preferred_element_type=F32)
            s = s + bias_ref[2 - min(i - kt, 2)].reshape(2 * tl, tl)
            s_ref[buf, :, cols] = s
            fold = _lane_fold_max(s)
            mx = fold if mx is None else jnp.maximum(mx, fold)
        m = jnp.broadcast_to(jnp.max(mx, axis=-1, keepdims=True), (2 * tl, LANES))
        for gi in range((i + 1) * tl // LANES):
            cols = slice(gi * LANES, (gi + 1) * LANES)
            p_ref[buf, :, cols] = jnp.exp(s_ref[buf, :, cols] - m).astype(BF16)
        nk = (i + 1) * tl
        acc = jnp.dot(p_ref[buf, :, 0:nk], v1_ref[0:nk, :], preferred_element_type=F32)
        o_ref[0, i * tl:(i + 1) * tl, :] = _diff_epilogue(acc, lam, g_ref[...], lam_init)


def _attn_a_prompt(q, kt, v, bias, lamp, g, layer, *, lam_init):
    b, s, _ = q.shape
    nt = s // A_TILE
    assert s % A_TILE == 0
    seq = pl.BlockSpec((1, s, LANES), lambda bi, h: (bi, 0, h))
    return pl.pallas_call(
        functools.partial(_attn_a_prompt_body, nt=nt, lam_init=lam_init),
        grid=(b, A_HEADS),
        in_specs=[pl.BlockSpec((None, 4, A_DK), lambda bi, h: (layer, 0, 0)),
                  seq,
                  pl.BlockSpec((1, LANES, s), lambda bi, h: (bi, h, 0)),
                  seq,
                  pl.BlockSpec((3, 2, A_TILE, A_TILE), lambda bi, h: (0, h, 0, 0)),
                  pl.BlockSpec((None, 1, A_DV), lambda bi, h: (layer, 0, 0))],
        out_specs=seq,
        out_shape=jax.ShapeDtypeStruct((b, s, A_V), BF16),
        scratch_shapes=[pltpu.VMEM((2, 2 * A_TILE, s), F32),
                        pltpu.VMEM((2, 2 * A_TILE, s), BF16),
                        pltpu.VMEM((s, 2 * A_DV), BF16)],
        compiler_params=_params(2),
        name="attn_a_prompt",
    )(lamp, q, kt, v, bias, g)


def _attn_a_sample_body(lamp_ref, q_ref, ktc_ref, vc_ref, kn_ref, vn_ref, bias_ref, g_ref, o_ref,
                        *, lam_init):
    tq = q_ref.shape[1]
    past = ktc_ref.shape[1]
    lam = _diff_lambda(lamp_ref, lam_init)
    for h in range(A_HEADS):
        cols = slice(h * LANES, (h + 1) * LANES)
        qs = _stack_maps(q_ref[0, :, cols])
        bias = bias_ref[2 * h:2 * h + 2].reshape(2 * tq, past + LANES)
        s_c = jnp.dot(qs, ktc_ref[cols, :].astype(BF16), preferred_element_type=F32)
        s_c = s_c + bias[:, 0:past]
        s_n = lax.dot_general(qs, kn_ref[0, :, cols], NT_DIMS, preferred_element_type=F32)
        s_n = s_n + bias[:, past:past + tq]
        m = jnp.maximum(_row_max(s_c), jnp.broadcast_to(jnp.max(s_n, axis=-1, keepdims=True),
                                                        (2 * tq, LANES)))
        p_c = _exp_rows(s_c, m)
        p_n = jnp.exp(s_n - m[:, 0:tq]).astype(BF16)
        v_c = vc_ref[pl.ds(h, past, stride=A_HEADS), :].astype(BF16)
        acc = (jnp.dot(p_c, _with_ones(v_c), preferred_element_type=F32)
               + jnp.dot(p_n, _with_ones(vn_ref[0, :, cols]), preferred_element_type=F32))
        o_ref[0, :, cols] = _diff_epilogue(acc, lam, g_ref[...], lam_init)


def _attn_a_sample(q, ktc, vc, kn, vn, bias, lamp, g, layer, *, lam_init):
    b, tq, _ = q.shape
    past = ktc.shape[3]
    new = lambda bi: (bi, 0, 0)
    return pl.pallas_call(
        functools.partial(_attn_a_sample_body, lam_init=lam_init),
        grid=(b,),
        in_specs=[pl.BlockSpec((None, 4, A_DK), lambda bi: (layer, 0, 0)),
                  pl.BlockSpec((1, tq, A_QK), new),
                  pl.BlockSpec((None, None, A_QK, past), lambda bi: (layer, bi, 0, 0)),
                  pl.BlockSpec((None, None, past * A_HEADS, A_DV), lambda bi: (layer, bi, 0, 0)),
                  pl.BlockSpec((1, tq, A_QK), new),
                  pl.BlockSpec((1, tq, A_V), new),
                  pl.BlockSpec((N_MAPS, tq, past + LANES), lambda bi: (0, 0, 0)),
                  pl.BlockSpec((None, 1, A_DV), lambda bi: (layer, 0, 0))],
        out_specs=pl.BlockSpec((1, tq, A_V), new),
        out_shape=jax.ShapeDtypeStruct((b, tq, A_V), BF16),
        compiler_params=_params(1),
        name="attn_a_sample",
    )(lamp, q, ktc, vc, kn, vn, bias, g)


_B_ORDER_A = (0, 2, 4, 6, 9, 11, 13, 15)
_B_ORDER_B = (1, 3, 5, 7, 8, 10, 12, 14)


def _attn_b_body(sink_ref, q_ref, k_ref, v_ref, bias_ref, o_ref, *, layer, tq, w, back, i_off):
    q0 = (pl.program_id(1) + i_off) * tq
    near_start = pl.multiple_of(jnp.maximum(q0 - back, 0), CHUNK)
    kk = k_ref[0, pl.ds(near_start, w), :]
    vv = v_ref[0, pl.ds(near_start, w), :]
    lane = lax.broadcasted_iota(jnp.int32, (tq, LANES), 1)
    lo = lane < HALF

    q_even, q_odd = [], []
    for c in range(B_Q_HEADS // 2):
        qc = q_ref[0, :, c * LANES:(c + 1) * LANES]
        zero = jnp.zeros_like(qc)
        q_even.append(jnp.where(lo, qc, zero))
        q_odd.append(jnp.where(lo, zero, qc))
    half = B_Q_HEADS // 4
    q_sets = (q_even[:half] + q_odd[half:], q_odd[:half] + q_even[half:])
    orders = (_B_ORDER_A, _B_ORDER_B)

    outs = []
    for t in range(2):
        qs = jnp.concatenate(q_sets[t], axis=0)
        s = lax.dot_general(qs, kk[:, t * LANES:(t + 1) * LANES], NT_DIMS,
                            preferred_element_type=F32)
        s = s + bias_ref[0, t].reshape((B_Q_HEADS // 2) * tq, w)
        sink = jnp.concatenate([jnp.full((tq, LANES), sink_ref[layer, h], F32) for h in orders[t]],
                               axis=0)
        m = jnp.maximum(_row_max(s), sink)
        p = _exp_rows(s, m)
        acc = jnp.dot(p, _with_ones(vv[:, t * LANES:(t + 1) * LANES]), preferred_element_type=F32)
        outs.append(acc[:, :LANES] / (acc[:, LANES:] + jnp.exp(sink - m)))

    for c in range(B_Q_HEADS // 2):
        oa = outs[0][c * tq:(c + 1) * tq]
        ob = outs[1][c * tq:(c + 1) * tq]
        oc = jnp.where(lo, oa, ob) if c < half else jnp.where(lo, ob, oa)
        o_ref[0, :, c * LANES:(c + 1) * LANES] = oc.astype(BF16)


def _attn_b(q, k, v, bias, sinks, layer, *, tq, w, back, i_off):
    b, sq, _ = q.shape
    sk = k.shape[1]
    nq = sq // tq
    assert (nq - 1 + i_off) * tq - back + w <= sk
    body = functools.partial(_attn_b_body, layer=layer, tq=tq, w=w, back=back, i_off=i_off)
    variant = (lambda i: jnp.minimum(i + i_off, 1))
    return pl.pallas_call(
        body,
        grid=(b, nq),
        in_specs=[pl.BlockSpec(memory_space=pltpu.SMEM),
                  pl.BlockSpec((1, tq, B_Q), lambda bi, i: (bi, i, 0)),
                  pl.BlockSpec((1, sk, 2 * B_KV), lambda bi, i: (bi, 0, 0)),
                  pl.BlockSpec((1, sk, 2 * B_KV), lambda bi, i: (bi, 0, 0)),
                  pl.BlockSpec((1, 2, B_Q_HEADS // 2, tq, w), lambda bi, i: (variant(i), 0, 0, 0, 0))],
        out_specs=pl.BlockSpec((1, tq, B_Q), lambda bi, i: (bi, i, 0)),
        out_shape=jax.ShapeDtypeStruct((b, sq, B_Q), BF16),
        compiler_params=_params(2),
        name="attn_b",
    )(sinks, q, k, v, bias)


def _post_body(x_ref, o_ref, wo_ref, g_ref, wu_ref, wd_ref, gf_ref, y_ref, hn_ref, *, final):
    f = pl.program_id(1)

    @pl.when(f == 0)
    def _():
        x1 = x_ref[...] + jnp.dot(o_ref[...], wo_ref[...], preferred_element_type=F32)
        y_ref[...] = x1
        hn_ref[...] = _rms(x1, g_ref[...]).astype(BF16)

    u = jnp.dot(hn_ref[...], wu_ref[...], preferred_element_type=F32)
    a = jnp.square(jnp.maximum(u, 0.0)).astype(BF16)
    y_ref[...] += jnp.dot(a, wd_ref[...], preferred_element_type=F32)

    if final:
        @pl.when(f == pl.num_programs(1) - 1)
        def _():
            y_ref[...] = _rms(y_ref[...], gf_ref[...])


def _post(x, o, wo, wo_layer, g, wu, wd, layer, gf, *, tm, tf, final):
    m = x.shape[0]
    row = lambda i, f: (i, 0)
    full = lambda i, f: (0, 0)
    return pl.pallas_call(
        functools.partial(_post_body, final=final),
        grid=(m // tm, D_FF // tf),
        in_specs=[pl.BlockSpec((tm, D_MODEL), row),
                  pl.BlockSpec((tm, D_MODEL), row),
                  pl.BlockSpec((None, D_MODEL, D_MODEL), lambda i, f: (wo_layer, 0, 0)),
                  pl.BlockSpec((None, 1, D_MODEL), lambda i, f: (layer, 0, 0)),
                  pl.BlockSpec((None, D_MODEL, tf), lambda i, f: (layer, 0, f)),
                  pl.BlockSpec((None, tf, D_MODEL), lambda i, f: (layer, f, 0)),
                  pl.BlockSpec((1, D_MODEL), full)],
        out_specs=pl.BlockSpec((tm, D_MODEL), row),
        out_shape=jax.ShapeDtypeStruct((m, D_MODEL), F32),
        scratch_shapes=[pltpu.VMEM((tm, D_MODEL), BF16)],
        compiler_params=_params(2),
        name="post",
    )(x, o, wo, g, wu, wd, gf)


_B_PROMPT = dict(tq=128, w=256, back=128)
_B_SAMPLE = dict(tq=64, w=256, back=128)
_A_SAMPLE_BACK = 128


def _a_prompt_bias(table):
    _check_far_bucket(A_TILE + 1)
    near = _near_bias(table, A_TILE, 2 * A_TILE, A_TILE, None)
    return jnp.stack([_far_bias(table, A_TILE, A_TILE), near[:, :, :A_TILE], near[:, :, A_TILE:]])


def _a_sample_bias(table, past, t):
    assert t <= LANES
    _check_far_bucket(_A_SAMPLE_BACK + 1)
    far = _far_bias(table, t, past - _A_SAMPLE_BACK)
    near = _near_bias(table, t, _A_SAMPLE_BACK + LANES, _A_SAMPLE_BACK, None)
    return jnp.concatenate([far, near], axis=2)


def _b_bias(table, tq, w, back):
    def one(bk):
        t = _near_bias(table, tq, w, bk, WINDOW // CHUNK)
        return jnp.stack([t[jnp.array(_B_ORDER_A)], t[jnp.array(_B_ORDER_B)]])
    return jnp.stack([one(0), one(back)])


def _pad_keys(t, total):
    return jnp.pad(t, ((0, 0), (0, total - t.shape[1]), (0, 0)))


def _trunk(x, caches, wts, table, *, tm, tm_post, tf):
    b, s, _ = x.shape
    sample = caches is not None
    n_a = wts["a_w_qkv"].shape[0]
    xm = x.reshape(b * s, D_MODEL)
    b_cfg = _B_SAMPLE if sample else _B_PROMPT
    b_bias = _b_bias(table, **b_cfg)
    if sample:
        a_bias = _a_sample_bias(table, caches["a_kt"].shape[3], s)
    else:
        a_bias = _a_prompt_bias(table)
    ak, av, bk, bv = [], [], [], []
    kt_buf = v_buf = None
    for i in range(DEPTH):
        j = i // 2
        g_mix = wts["norm_mix_g"][i][None, :]
        if i % 2 == 0:
            lam_init = _lambda_init(i)
            if sample:
                q, k, v, kb, vb = _proj_a_sample(xm, g_mix, wts["a_w_qkv"], j, tm)
                ak.append(k.reshape(b, s, A_HEADS, 2, A_DK))
                av.append(v.reshape(b, s, A_HEADS, A_DV))
                o = _attn_a_sample(q.reshape(b, s, A_QK), caches["a_kt"], caches["a_v"],
                                   kb.reshape(b, s, A_QK), vb.reshape(b, s, A_V), a_bias,
                                   wts["a_lambda"], wts["a_subln_g"], j, lam_init=lam_init)
            else:
                q, kt_buf, v_buf, ktb, vb = _proj_a_prompt(
                    xm, g_mix, wts["a_w_qkv"], wts["a_w_kt"], j, kt_buf, v_buf,
                    b=b, s=s, tm=tm, n_a=n_a)
                o = _attn_a_prompt(q.reshape(b, s, A_QK), ktb, vb.reshape(b, s, A_V), a_bias,
                                   wts["a_lambda"], wts["a_subln_g"], j, lam_init=lam_init)
            wo, wo_layer = wts["a_w_o"], j
        else:
            q, k, v, kb, vb = _proj_b(xm, g_mix, wts["b_w_qkv"], j, tm)
            k = k.reshape(b, s, B_KV_HEADS, B_HD)
            v = v.reshape(b, s, B_KV_HEADS, B_HD)
            kb = kb.reshape(b, s, 2 * B_KV)
            vb = vb.reshape(b, s, 2 * B_KV)
            i_off = 0
            if sample:
                ck, cv, ckb, cvb = caches["b"][j]
                bk.append(jnp.concatenate([ck, k], axis=1)[:, s:])
                bv.append(jnp.concatenate([cv, v], axis=1)[:, s:])
                i_off = ckb.shape[1] // b_cfg["tq"]
                total = ckb.shape[1] - b_cfg["back"] + b_cfg["w"]
                kb = _pad_keys(jnp.concatenate([ckb, kb], axis=1), total)
                vb = _pad_keys(jnp.concatenate([cvb, vb], axis=1), total)
            else:
                bk.append(k[:, s - WINDOW:])
                bv.append(v[:, s - WINDOW:])
            o = _attn_b(q.reshape(b, s, B_Q), kb, vb, b_bias, wts["b_sinks"], j, i_off=i_off, **b_cfg)
            wo, wo_layer = wts["b_w_o"], j
        xm = _post(xm, o.reshape(b * s, D_MODEL), wo, wo_layer, wts["norm_mlp_g"],
                   wts["mlp_w_up"], wts["mlp_w_down"], i, wts["final_norm_g"][None, :],
                   tm=tm_post, tf=tf, final=(i == DEPTH - 1))
    if sample:
        a_k, a_v = jnp.stack(ak), jnp.stack(av)
    else:
        a_k = jnp.transpose(kt_buf.reshape(n_a, b, A_HEADS, 2, A_DK, s), (0, 1, 5, 2, 3, 4))
        a_v = v_buf.reshape(n_a, b, s, A_HEADS, A_DV)
    return xm.reshape(b, s, D_MODEL), a_k, a_v, jnp.stack(bk), jnp.stack(bv)


def _swap_halves(t):
    flat = t.reshape(t.shape[:-2] + (B_KV,))
    swapped = t[..., ::-1, :].reshape(t.shape[:-2] + (B_KV,))
    return jnp.concatenate([flat, swapped], axis=-1).astype(BF16)


def kernel(x_prompt, x_sample, cache_a_k, cache_a_v, cache_b_k, cache_b_v, rel_table,
           norm_mix_g, norm_mlp_g, final_norm_g, a_w_qkv, a_lambda, a_subln_g, a_w_o,
           b_w_qkv, b_sinks, b_w_o, mlp_w_up, mlp_w_down):
    wts = dict(norm_mix_g=norm_mix_g, norm_mlp_g=norm_mlp_g[:, None, :], final_norm_g=final_norm_g,
               a_w_qkv=a_w_qkv.astype(BF16),
               a_w_kt=jnp.transpose(a_w_qkv[:, :, A_QK:2 * A_QK], (0, 2, 1)).astype(BF16),
               a_lambda=a_lambda, a_subln_g=a_subln_g[:, None, :],
               a_w_o=a_w_o.astype(BF16), b_w_qkv=b_w_qkv.astype(BF16), b_sinks=b_sinks,
               b_w_o=b_w_o.astype(BF16), mlp_w_up=mlp_w_up.astype(BF16),
               mlp_w_down=mlp_w_down.astype(BF16))
    n_a, db, past = cache_a_k.shape[:3]
    caches = dict(
        a_kt=jnp.transpose(cache_a_k, (0, 1, 3, 4, 5, 2)).reshape(n_a, db, A_QK, past),
        a_v=cache_a_v.reshape(n_a, db, past * A_HEADS, A_DV),
        b=[(cache_b_k[j], cache_b_v[j], _swap_halves(cache_b_k[j]), _swap_halves(cache_b_v[j]))
           for j in range(cache_b_k.shape[0])])
    yp, akp, avp, bkp, bvp = _trunk(x_prompt, None, wts, rel_table, tm=512, tm_post=1024, tf=1024)
    ys, aks, avs, bks, bvs = _trunk(x_sample, caches, wts, rel_table, tm=512, tm_post=512, tf=1024)
    return (yp, ys, akp, avp, bkp, bvp, aks, avs, bks, bvs)
```

```python
import functools
import math

import jax
import jax.numpy as jnp
from jax import lax
from jax.experimental import pallas as pl
from jax.experimental.pallas import tpu as pltpu

BF16 = jnp.bfloat16
F32 = jnp.float32

D_MODEL = 1024
DEPTH = 4
CHUNK = 64
A_HEADS = 8
A_DK = 64
A_DV = 128
A_QK = A_HEADS * 2 * A_DK
A_V = A_HEADS * A_DV
B_Q_HEADS = 16
B_KV_HEADS = 2
B_HD = 64
B_Q = B_Q_HEADS * B_HD
B_KV = B_KV_HEADS * B_HD
WINDOW = 128
NUM_BUCKETS = 32
MAX_DIST = 128
N_MAPS = 16
D_FF = 4 * D_MODEL
EPS = 1e-6
NEG = -1e30

LANES = 128
SUBLANES = 8
HALF = LANES // 2
VMEM_LIMIT = 56 * 1024 * 1024

NT_DIMS = (((1,), (1,)), ((), ()))


def _lambda_init(layer):
    return 0.8 - 0.6 * math.exp(-0.3 * layer)


def _rms(x, g):
    return x * lax.rsqrt(jnp.mean(x * x, axis=-1, keepdims=True) + EPS) * g


def _params(n_axes):
    return pltpu.CompilerParams(
        dimension_semantics=("arbitrary",) * n_axes, vmem_limit_bytes=VMEM_LIMIT)


def _lane_fold_max(s):
    mx = s[:, 0:LANES]
    for gi in range(1, s.shape[1] // LANES):
        mx = jnp.maximum(mx, s[:, gi * LANES:(gi + 1) * LANES])
    return mx


def _row_max(s):
    return jnp.broadcast_to(jnp.max(_lane_fold_max(s), axis=-1, keepdims=True), (s.shape[0], LANES))


def _exp_rows(s, m):
    return jnp.concatenate([jnp.exp(s[:, gi * LANES:(gi + 1) * LANES] - m)
                            for gi in range(s.shape[1] // LANES)], axis=1).astype(BF16)


def _with_ones(v):
    return jnp.concatenate([v, jnp.ones(v.shape, v.dtype)], axis=1)


def _proj_a_prompt_body(x_ref, g_ref, w_ref, wkt_ref, *refs, first):
    q_ref, kt_ref, v_ref, ktb_ref, vb_ref = refs[-5:]
    h = _rms(x_ref[...], g_ref[...]).astype(BF16)
    q = jnp.dot(h, w_ref[:, 0:A_QK], preferred_element_type=F32)
    q_ref[...] = (q * (A_DK ** -0.5)).astype(BF16)
    kt = lax.dot_general(wkt_ref[...], h, NT_DIMS, preferred_element_type=F32)
    ktb_ref[...] = kt.astype(BF16)
    v = jnp.dot(h, w_ref[:, 2 * A_QK:], preferred_element_type=F32)
    vb_ref[...] = v.astype(BF16)
    tm = v.shape[0]
    if first:
        for slot in range(1, kt_ref.shape[0]):
            kt_ref[slot] = jnp.zeros(kt_ref.shape[1:], F32)
            v_ref[slot] = jnp.zeros(v_ref.shape[1:], F32)
        kt_ref, v_ref = kt_ref.at[0], v_ref.at[0]
    kt_ref[...] = kt
    for hh in range(A_HEADS):
        v_ref[pl.ds(hh, tm, stride=A_HEADS), :] = v[:, hh * A_DV:(hh + 1) * A_DV]


def _proj_a_prompt(x, g, w, wkt, layer, kt_prev, v_prev, *, b, s, tm, n_a):
    m = b * s
    per_b = s // tm
    row = lambda i: (i, 0)
    full = lambda i: (0, 0)
    stacked = lambda i: (layer, 0, 0)
    first = kt_prev is None
    assert first == (layer == 0)
    in_specs = [pl.BlockSpec((tm, D_MODEL), row),
                pl.BlockSpec((1, D_MODEL), full),
                pl.BlockSpec((None, D_MODEL, 2 * A_QK + A_V), stacked),
                pl.BlockSpec((None, A_QK, D_MODEL), stacked)]
    operands = [x, g, w, wkt]
    if first:
        kt_spec = pl.BlockSpec((n_a, None, A_QK, tm), lambda i: (0, i // per_b, 0, i % per_b))
        v_spec = pl.BlockSpec((n_a, tm * A_HEADS, A_DV), lambda i: (0, i, 0))
        aliases = {}
    else:
        kt_spec = pl.BlockSpec((None, None, A_QK, tm), lambda i: (layer, i // per_b, 0, i % per_b))
        v_spec = pl.BlockSpec((None, tm * A_HEADS, A_DV), lambda i: (layer, i, 0))
        in_specs += [pl.BlockSpec(memory_space=pl.ANY)] * 2
        operands += [kt_prev, v_prev]
        aliases = {4: 1, 5: 2}
    return pl.pallas_call(
        functools.partial(_proj_a_prompt_body, first=first),
        grid=(m // tm,),
        in_specs=in_specs,
        out_specs=[pl.BlockSpec((tm, A_QK), row),
                   kt_spec,
                   v_spec,
                   pl.BlockSpec((None, A_QK, tm), lambda i: (i // per_b, 0, i % per_b)),
                   pl.BlockSpec((tm, A_V), row)],
        out_shape=[jax.ShapeDtypeStruct((m, A_QK), BF16),
                   jax.ShapeDtypeStruct((n_a, b, A_QK, s), F32),
                   jax.ShapeDtypeStruct((n_a, m * A_HEADS, A_DV), F32),
                   jax.ShapeDtypeStruct((b, A_QK, s), BF16),
                   jax.ShapeDtypeStruct((m, A_V), BF16)],
        input_output_aliases=aliases,
        compiler_params=_params(1),
        name="proj_a_prompt",
    )(*operands)


def _proj_a_sample_body(x_ref, g_ref, w_ref, q_ref, k_ref, v_ref, kb_ref, vb_ref):
    h = _rms(x_ref[...], g_ref[...]).astype(BF16)
    q = jnp.dot(h, w_ref[:, 0:A_QK], preferred_element_type=F32)
    q_ref[...] = (q * (A_DK ** -0.5)).astype(BF16)
    k = jnp.dot(h, w_ref[:, A_QK:2 * A_QK], preferred_element_type=F32)
    k_ref[...] = k
    kb_ref[...] = k.astype(BF16)
    v = jnp.dot(h, w_ref[:, 2 * A_QK:], preferred_element_type=F32)
    v_ref[...] = v
    vb_ref[...] = v.astype(BF16)


def _proj_a_sample(x, g, w, layer, tm):
    m = x.shape[0]
    row = lambda i: (i, 0)
    full = lambda i: (0, 0)
    return pl.pallas_call(
        _proj_a_sample_body,
        grid=(m // tm,),
        in_specs=[pl.BlockSpec((tm, D_MODEL), row),
                  pl.BlockSpec((1, D_MODEL), full),
                  pl.BlockSpec((None, D_MODEL, 2 * A_QK + A_V), lambda i: (layer, 0, 0))],
        out_specs=[pl.BlockSpec((tm, A_QK), row),
                   pl.BlockSpec((tm, A_QK), row),
                   pl.BlockSpec((tm, A_V), row),
                   pl.BlockSpec((tm, A_QK), row),
                   pl.BlockSpec((tm, A_V), row)],
        out_shape=[jax.ShapeDtypeStruct((m, A_QK), BF16),
                   jax.ShapeDtypeStruct((m, A_QK), F32),
                   jax.ShapeDtypeStruct((m, A_V), F32),
                   jax.ShapeDtypeStruct((m, A_QK), BF16),
                   jax.ShapeDtypeStruct((m, A_V), BF16)],
        compiler_params=_params(1),
        name="proj_a_sample",
    )(x, g, w)


def _proj_b_body(x_ref, g_ref, w_ref, q_ref, k_ref, v_ref, kb_ref, vb_ref):
    h = _rms(x_ref[...], g_ref[...]).astype(BF16)
    q = jnp.dot(h, w_ref[:, 0:B_Q], preferred_element_type=F32)
    q_ref[...] = (q * (B_HD ** -0.5)).astype(BF16)
    k = jnp.dot(h, w_ref[:, B_Q:B_Q + B_KV], preferred_element_type=F32)
    k_ref[...] = k
    kb_ref[...] = jnp.concatenate([k, pltpu.roll(k, HALF, axis=1)], axis=1).astype(BF16)
    v = jnp.dot(h, w_ref[:, B_Q + B_KV:], preferred_element_type=F32)
    v_ref[...] = v
    vb_ref[...] = jnp.concatenate([v, pltpu.roll(v, HALF, axis=1)], axis=1).astype(BF16)


def _proj_b(x, g, w, layer, tm):
    m = x.shape[0]
    row = lambda i: (i, 0)
    full = lambda i: (0, 0)
    return pl.pallas_call(
        _proj_b_body,
        grid=(m // tm,),
        in_specs=[pl.BlockSpec((tm, D_MODEL), row),
                  pl.BlockSpec((1, D_MODEL), full),
                  pl.BlockSpec((None, D_MODEL, B_Q + 2 * B_KV), lambda i: (layer, 0, 0))],
        out_specs=[pl.BlockSpec((tm, B_Q), row),
                   pl.BlockSpec((tm, B_KV), row),
                   pl.BlockSpec((tm, B_KV), row),
                   pl.BlockSpec((tm, 2 * B_KV), row),
                   pl.BlockSpec((tm, 2 * B_KV), row)],
        out_shape=[jax.ShapeDtypeStruct((m, B_Q), BF16),
                   jax.ShapeDtypeStruct((m, B_KV), F32),
                   jax.ShapeDtypeStruct((m, B_KV), F32),
                   jax.ShapeDtypeStruct((m, 2 * B_KV), BF16),
                   jax.ShapeDtypeStruct((m, 2 * B_KV), BF16)],
        compiler_params=_params(1),
        name="proj_b",
    )(x, g, w)


def _t5_bucket(rel):
    nb = NUM_BUCKETS // 2
    n = -rel
    ret = jnp.where(n < 0, nb, 0)
    n = jnp.abs(n)
    max_exact = nb // 2
    nf = jnp.maximum(n, 1).astype(jnp.float32)
    large = max_exact + (jnp.log(nf / max_exact) / math.log(MAX_DIST / max_exact)
                         * (nb - max_exact)).astype(jnp.int32)
    large = jnp.minimum(large, nb - 1)
    return ret + jnp.where(n < max_exact, n, large)


def _check_far_bucket(min_dist):
    nb, max_exact = NUM_BUCKETS // 2, NUM_BUCKETS // 4
    ratio = math.log(min_dist / max_exact) / math.log(MAX_DIST / max_exact)
    assert ratio * (nb - max_exact) >= nb - 1 - max_exact + 1e-3, "far keys need one shared bucket"


def _far_bias(table, tq, w):
    return jnp.broadcast_to(table[NUM_BUCKETS // 2 - 1].astype(F32)[:, None, None], (N_MAPS, tq, w))


def _near_bias(table, tq, w, back, band):
    span = tq + w - 1
    rel = jnp.arange(span) - (tq - 1) - back
    vec = table[_t5_bucket(rel)].astype(F32).T
    period = jnp.concatenate([vec, jnp.zeros((N_MAPS, 1), F32)], axis=1)
    flat = jnp.tile(period, (1, tq))
    bias = flat[:, tq - 1:tq - 1 + tq * span].reshape(N_MAPS, tq, span)[:, :, :w]
    r = jnp.arange(tq)[:, None]
    j = jnp.arange(w)[None, :] - back
    kc, qc = j // CHUNK, r // CHUNK
    mask = kc <= qc
    if band is not None:
        mask = mask & (kc >= qc - band)
    return jnp.where(mask[None], bias, NEG)


A_TILE = 256
A_ROWS = 64


def _diff_lambda(lamp_ref, lam_init):
    lp = lamp_ref[...]
    return (jnp.exp(jnp.sum(lp[0:1] * lp[1:2], axis=-1, keepdims=True))
            - jnp.exp(jnp.sum(lp[2:3] * lp[3:4], axis=-1, keepdims=True)) + lam_init)


def _stack_maps(q):
    lane = lax.broadcasted_iota(jnp.int32, q.shape, 1)
    zero = jnp.zeros_like(q)
    return jnp.concatenate([jnp.where(lane < HALF, q, zero), jnp.where(lane >= HALF, q, zero)], axis=0)


def _diff_epilogue(acc, lam, g, lam_init):
    tq = acc.shape[0] // 2
    o_all = acc[:, :A_DV] / acc[:, A_DV:]
    o = o_all[:tq] - lam * o_all[tq:]
    return (_rms(o, g) * (1.0 - lam_init)).astype(BF16)


def _attn_a_prompt_body(lamp_ref, q_ref, kt_ref, v_ref, bias_ref, g_ref, o_ref,
                        qs_ref, s_ref, p_ref, v1_ref, *, nt, lam_init):
    tl = A_TILE
    lam = _diff_lambda(lamp_ref, lam_init)
    v1_ref[:, 0:A_DV] = v_ref[0]
    v1_ref[:, A_DV:] = jnp.ones((nt * tl, A_DV), BF16)
    for i in reversed(range(nt)):
        buf = i % 2
        qs_ref[buf] = _stack_maps(q_ref[0, i * tl:(i + 1) * tl, :])
        n_far = max(i - 1, 0)
        for kt in range(i + 1):
            cols = slice(kt * tl, (kt + 1) * tl)
            s = jnp.dot(qs_ref[buf], kt_ref[0, :, cols], preferred_element_type=F32)
            if kt >= n_far:
                s = s + bias_ref[2 - (i - kt)].reshape(2 * tl, tl)
            s_ref[buf, :, cols] = s
        far_groups, all_groups = n_far * tl // LANES, (i + 1) * tl // LANES
        for r in range(0, 2 * tl, A_ROWS):
            rows = slice(r, r + A_ROWS)
            group = lambda gi: s_ref[buf, rows, gi * LANES:(gi + 1) * LANES]
            mx = group(far_groups)
            for gi in range(far_groups + 1, all_groups):
                mx = jnp.maximum(mx, group(gi))
            if far_groups:
                c = bias_ref[0, r // tl, r % tl:r % tl + A_ROWS, 0:LANES]
                mf = group(0)
                for gi in range(1, far_groups):
                    mf = jnp.maximum(mf, group(gi))
                mx = jnp.maximum(mx, mf + c)
            m = jnp.broadcast_to(jnp.max(mx, axis=-1, keepdims=True), (A_ROWS, LANES))
            m_far = m - c if far_groups else m
            for gi in range(all_groups):
                shift = m_far if gi < far_groups else m
                p_ref[buf, rows, gi * LANES:(gi + 1) * LANES] = jnp.exp(group(gi) - shift).astype(BF16)
        nk = (i + 1) * tl
        acc = jnp.dot(p_ref[buf, :, 0:nk], v1_ref[0:nk, :], preferred_element_type=F32)
        o_ref[0, i * tl:(i + 1) * tl, :] = _diff_epilogue(acc, lam, g_ref[...], lam_init)


def _attn_a_prompt(q, kt, v, bias, lamp, g, layer, *, lam_init):
    b, s, _ = q.shape
    nt = s // A_TILE
    assert s % A_TILE == 0
    seq = pl.BlockSpec((1, s, LANES), lambda bi, h: (bi, 0, h))
    return pl.pallas_call(
        functools.partial(_attn_a_prompt_body, nt=nt, lam_init=lam_init),
        grid=(b, A_HEADS),
        in_specs=[pl.BlockSpec((None, 4, A_DK), lambda bi, h: (layer, 0, 0)),
                  seq,
                  pl.BlockSpec((1, LANES, s), lambda bi, h: (bi, h, 0)),
                  seq,
                  pl.BlockSpec((3, 2, A_TILE, A_TILE), lambda bi, h: (0, h, 0, 0)),
                  pl.BlockSpec((None, 1, A_DV), lambda bi, h: (layer, 0, 0))],
        out_specs=seq,
        out_shape=jax.ShapeDtypeStruct((b, s, A_V), BF16),
        scratch_shapes=[pltpu.VMEM((2, 2 * A_TILE, LANES), BF16),
                        pltpu.VMEM((2, 2 * A_TILE, s), F32),
                        pltpu.VMEM((2, 2 * A_TILE, s), BF16),
                        pltpu.VMEM((s, 2 * A_DV), BF16)],
        compiler_params=_params(2),
        name="attn_a_prompt",
    )(lamp, q, kt, v, bias, g)


def _attn_a_sample_body(lamp_ref, q_ref, ktc_ref, vc_ref, kn_ref, vn_ref, bias_ref, g_ref, o_ref,
                        *, lam_init):
    tq = q_ref.shape[1]
    past = ktc_ref.shape[1]
    lam = _diff_lambda(lamp_ref, lam_init)
    for h in range(A_HEADS):
        cols = slice(h * LANES, (h + 1) * LANES)
        qs = _stack_maps(q_ref[0, :, cols])
        bias = bias_ref[2 * h:2 * h + 2].reshape(2 * tq, past + LANES)
        s_c = jnp.dot(qs, ktc_ref[cols, :].astype(BF16), preferred_element_type=F32)
        s_c = s_c + bias[:, 0:past]
        s_n = lax.dot_general(qs, kn_ref[0, :, cols], NT_DIMS, preferred_element_type=F32)
        s_n = s_n + bias[:, past:past + tq]
        m = jnp.maximum(_row_max(s_c), jnp.broadcast_to(jnp.max(s_n, axis=-1, keepdims=True),
                                                        (2 * tq, LANES)))
        p_c = _exp_rows(s_c, m)
        p_n = jnp.exp(s_n - m[:, 0:tq]).astype(BF16)
        v_c = vc_ref[pl.ds(h, past, stride=A_HEADS), :].astype(BF16)
        acc = (jnp.dot(p_c, _with_ones(v_c), preferred_element_type=F32)
               + jnp.dot(p_n, _with_ones(vn_ref[0, :, cols]), preferred_element_type=F32))
        o_ref[0, :, cols] = _diff_epilogue(acc, lam, g_ref[...], lam_init)


def _attn_a_sample(q, ktc, vc, kn, vn, bias, lamp, g, layer, *, lam_init):
    b, tq, _ = q.shape
    past = ktc.shape[3]
    new = lambda bi: (bi, 0, 0)
    return pl.pallas_call(
        functools.partial(_attn_a_sample_body, lam_init=lam_init),
        grid=(b,),
        in_specs=[pl.BlockSpec((None, 4, A_DK), lambda bi: (layer, 0, 0)),
                  pl.BlockSpec((1, tq, A_QK), new),
                  pl.BlockSpec((None, None, A_QK, past), lambda bi: (layer, bi, 0, 0)),
                  pl.BlockSpec((None, None, past * A_HEADS, A_DV), lambda bi: (layer, bi, 0, 0)),
                  pl.BlockSpec((1, tq, A_QK), new),
                  pl.BlockSpec((1, tq, A_V), new),
                  pl.BlockSpec((N_MAPS, tq, past + LANES), lambda bi: (0, 0, 0)),
                  pl.BlockSpec((None, 1, A_DV), lambda bi: (layer, 0, 0))],
        out_specs=pl.BlockSpec((1, tq, A_V), new),
        out_shape=jax.ShapeDtypeStruct((b, tq, A_V), BF16),
        compiler_params=_params(1),
        name="attn_a_sample",
    )(lamp, q, ktc, vc, kn, vn, bias, g)


_B_ORDER_A = (0, 2, 4, 6, 9, 11, 13, 15)
_B_ORDER_B = (1, 3, 5, 7, 8, 10, 12, 14)


def _attn_b_body(sink_ref, q_ref, k_ref, v_ref, bias_ref, o_ref, *, layer, tq, w, back, i_off):
    q0 = (pl.program_id(1) + i_off) * tq
    near_start = pl.multiple_of(jnp.maximum(q0 - back, 0), CHUNK)
    kk = k_ref[0, pl.ds(near_start, w), :]
    vv = v_ref[0, pl.ds(near_start, w), :]
    lane = lax.broadcasted_iota(jnp.int32, (tq, LANES), 1)
    lo = lane < HALF

    q_even, q_odd = [], []
    for c in range(B_Q_HEADS // 2):
        qc = q_ref[0, :, c * LANES:(c + 1) * LANES]
        zero = jnp.zeros_like(qc)
        q_even.append(jnp.where(lo, qc, zero))
        q_odd.append(jnp.where(lo, zero, qc))
    half = B_Q_HEADS // 4
    q_sets = (q_even[:half] + q_odd[half:], q_odd[:half] + q_even[half:])
    orders = (_B_ORDER_A, _B_ORDER_B)

    outs = []
    for t in range(2):
        qs = jnp.concatenate(q_sets[t], axis=0)
        s = lax.dot_general(qs, kk[:, t * LANES:(t + 1) * LANES], NT_DIMS,
                            preferred_element_type=F32)
        s = s + bias_ref[0, t].reshape((B_Q_HEADS // 2) * tq, w)
        sink = jnp.concatenate([jnp.full((tq, LANES), sink_ref[layer, h], F32) for h in orders[t]],
                               axis=0)
        m = jnp.maximum(_row_max(s), sink)
        p = _exp_rows(s, m)
        acc = jnp.dot(p, _with_ones(vv[:, t * LANES:(t + 1) * LANES]), preferred_element_type=F32)
        outs.append(acc[:, :LANES] / (acc[:, LANES:] + jnp.exp(sink - m)))

    for c in range(B_Q_HEADS // 2):
        oa = outs[0][c * tq:(c + 1) * tq]
        ob = outs[1][c * tq:(c + 1) * tq]
        oc = jnp.where(lo, oa, ob) if c < half else jnp.where(lo, ob, oa)
        o_ref[0, :, c * LANES:(c + 1) * LANES] = oc.astype(BF16)


def _attn_b(q, k, v, bias, sinks, layer, *, tq, w, back, i_off):
    b, sq, _ = q.shape
    sk = k.shape[1]
    nq = sq // tq
    assert (nq - 1 + i_off) * tq - back + w <= sk
    body = functools.partial(_attn_b_body, layer=layer, tq=tq, w=w, back=back, i_off=i_off)
    variant = (lambda i: jnp.minimum(i + i_off, 1))
    return pl.pallas_call(
        body,
        grid=(b, nq),
        in_specs=[pl.BlockSpec(memory_space=pltpu.SMEM),
                  pl.BlockSpec((1, tq, B_Q), lambda bi, i: (bi, i, 0)),
                  pl.BlockSpec((1, sk, 2 * B_KV), lambda bi, i: (bi, 0, 0)),
                  pl.BlockSpec((1, sk, 2 * B_KV), lambda bi, i: (bi, 0, 0)),
                  pl.BlockSpec((1, 2, B_Q_HEADS // 2, tq, w), lambda bi, i: (variant(i), 0, 0, 0, 0))],
        out_specs=pl.BlockSpec((1, tq, B_Q), lambda bi, i: (bi, i, 0)),
        out_shape=jax.ShapeDtypeStruct((b, sq, B_Q), BF16),
        compiler_params=_params(2),
        name="attn_b",
    )(sinks, q, k, v, bias)


def _post_body(x_ref, o_ref, wo_ref, g_ref, wu_ref, wd_ref, gf_ref, y_ref, hn_ref, *, final):
    f = pl.program_id(1)

    @pl.when(f == 0)
    def _():
        x1 = x_ref[...] + jnp.dot(o_ref[...], wo_ref[...], preferred_element_type=F32)
        y_ref[...] = x1
        hn_ref[...] = _rms(x1, g_ref[...]).astype(BF16)

    u = jnp.dot(hn_ref[...], wu_ref[...], preferred_element_type=F32)
    a = jnp.square(jnp.maximum(u, 0.0)).astype(BF16)
    y_ref[...] += jnp.dot(a, wd_ref[...], preferred_element_type=F32)

    if final:
        @pl.when(f == pl.num_programs(1) - 1)
        def _():
            y_ref[...] = _rms(y_ref[...], gf_ref[...])


def _post(x, o, wo, wo_layer, g, wu, wd, layer, gf, *, tm, tf, final):
    m = x.shape[0]
    row = lambda i, f: (i, 0)
    full = lambda i, f: (0, 0)
    return pl.pallas_call(
        functools.partial(_post_body, final=final),
        grid=(m // tm, D_FF // tf),
        in_specs=[pl.BlockSpec((tm, D_MODEL), row),
                  pl.BlockSpec((tm, D_MODEL), row),
                  pl.BlockSpec((None, D_MODEL, D_MODEL), lambda i, f: (wo_layer, 0, 0)),
                  pl.BlockSpec((None, 1, D_MODEL), lambda i, f: (layer, 0, 0)),
                  pl.BlockSpec((None, D_MODEL, tf), lambda i, f: (layer, 0, f)),
                  pl.BlockSpec((None, tf, D_MODEL), lambda i, f: (layer, f, 0)),
                  pl.BlockSpec((1, D_MODEL), full)],
        out_specs=pl.BlockSpec((tm, D_MODEL), row),
        out_shape=jax.ShapeDtypeStruct((m, D_MODEL), F32),
        scratch_shapes=[pltpu.VMEM((tm, D_MODEL), BF16)],
        compiler_params=_params(2),
        name="post",
    )(x, o, wo, g, wu, wd, gf)


_B_PROMPT = dict(tq=128, w=256, back=128)
_B_SAMPLE = dict(tq=64, w=256, back=128)
_A_SAMPLE_BACK = 128


def _a_prompt_bias(table):
    _check_far_bucket(A_TILE + 1)
    near = _near_bias(table, A_TILE, 2 * A_TILE, A_TILE, None)
    return jnp.stack([_far_bias(table, A_TILE, A_TILE), near[:, :, :A_TILE], near[:, :, A_TILE:]])


def _a_sample_bias(table, past, t):
    assert t <= LANES
    _check_far_bucket(_A_SAMPLE_BACK + 1)
    far = _far_bias(table, t, past - _A_SAMPLE_BACK)
    near = _near_bias(table, t, _A_SAMPLE_BACK + LANES, _A_SAMPLE_BACK, None)
    return jnp.concatenate([far, near], axis=2)


def _b_bias(table, tq, w, back):
    def one(bk):
        t = _near_bias(table, tq, w, bk, WINDOW // CHUNK)
        return jnp.stack([t[jnp.array(_B_ORDER_A)], t[jnp.array(_B_ORDER_B)]])
    return jnp.stack([one(0), one(back)])


def _pad_keys(t, total):
    return jnp.pad(t, ((0, 0), (0, total - t.shape[1]), (0, 0)))


def _trunk(x, caches, wts, table, *, tm, tm_post, tf):
    b, s, _ = x.shape
    sample = caches is not None
    n_a = wts["a_w_qkv"].shape[0]
    xm = x.reshape(b * s, D_MODEL)
    b_cfg = _B_SAMPLE if sample else _B_PROMPT
    b_bias = _b_bias(table, **b_cfg)
    if sample:
        a_bias = _a_sample_bias(table, caches["a_kt"].shape[3], s)
    else:
        a_bias = _a_prompt_bias(table)
    ak, av, bk, bv = [], [], [], []
    kt_buf = v_buf = None
    for i in range(DEPTH):
        j = i // 2
        g_mix = wts["norm_mix_g"][i][None, :]
        if i % 2 == 0:
            lam_init = _lambda_init(i)
            if sample:
                q, k, v, kb, vb = _proj_a_sample(xm, g_mix, wts["a_w_qkv"], j, tm)
                ak.append(k.reshape(b, s, A_HEADS, 2, A_DK))
                av.append(v.reshape(b, s, A_HEADS, A_DV))
                o = _attn_a_sample(q.reshape(b, s, A_QK), caches["a_kt"], caches["a_v"],
                                   kb.reshape(b, s, A_QK), vb.reshape(b, s, A_V), a_bias,
                                   wts["a_lambda"], wts["a_subln_g"], j, lam_init=lam_init)
            else:
                q, kt_buf, v_buf, ktb, vb = _proj_a_prompt(
                    xm, g_mix, wts["a_w_qkv"], wts["a_w_kt"], j, kt_buf, v_buf,
                    b=b, s=s, tm=tm, n_a=n_a)
                o = _attn_a_prompt(q.reshape(b, s, A_QK), ktb, vb.reshape(b, s, A_V), a_bias,
                                   wts["a_lambda"], wts["a_subln_g"], j, lam_init=lam_init)
            wo, wo_layer = wts["a_w_o"], j
        else:
            q, k, v, kb, vb = _proj_b(xm, g_mix, wts["b_w_qkv"], j, tm)
            k = k.reshape(b, s, B_KV_HEADS, B_HD)
            v = v.reshape(b, s, B_KV_HEADS, B_HD)
            kb = kb.reshape(b, s, 2 * B_KV)
            vb = vb.reshape(b, s, 2 * B_KV)
            i_off = 0
            if sample:
                ck, cv, ckb, cvb = caches["b"][j]
                bk.append(jnp.concatenate([ck, k], axis=1)[:, s:])
                bv.append(jnp.concatenate([cv, v], axis=1)[:, s:])
                i_off = ckb.shape[1] // b_cfg["tq"]
                total = ckb.shape[1] - b_cfg["back"] + b_cfg["w"]
                kb = _pad_keys(jnp.concatenate([ckb, kb], axis=1), total)
                vb = _pad_keys(jnp.concatenate([cvb, vb], axis=1), total)
            else:
                bk.append(k[:, s - WINDOW:])
                bv.append(v[:, s - WINDOW:])
            o = _attn_b(q.reshape(b, s, B_Q), kb, vb, b_bias, wts["b_sinks"], j, i_off=i_off, **b_cfg)
            wo, wo_layer = wts["b_w_o"], j
        xm = _post(xm, o.reshape(b * s, D_MODEL), wo, wo_layer, wts["norm_mlp_g"],
                   wts["mlp_w_up"], wts["mlp_w_down"], i, wts["final_norm_g"][None, :],
                   tm=tm_post, tf=tf, final=(i == DEPTH - 1))
    if sample:
        a_k, a_v = jnp.stack(ak), jnp.stack(av)
    else:
        a_k = jnp.transpose(kt_buf.reshape(n_a, b, A_HEADS, 2, A_DK, s), (0, 1, 5, 2, 3, 4))
        a_v = v_buf.reshape(n_a, b, s, A_HEADS, A_DV)
    return xm.reshape(b, s, D_MODEL), a_k, a_v, jnp.stack(bk), jnp.stack(bv)


def _swap_halves(t):
    flat = t.reshape(t.shape[:-2] + (B_KV,))
    swapped = t[..., ::-1, :].reshape(t.shape[:-2] + (B_KV,))
    return jnp.concatenate([flat, swapped], axis=-1).astype(BF16)


def kernel(x_prompt, x_sample, cache_a_k, cache_a_v, cache_b_k, cache_b_v, rel_table,
           norm_mix_g, norm_mlp_g, final_norm_g, a_w_qkv, a_lambda, a_subln_g, a_w_o,
           b_w_qkv, b_sinks, b_w_o, mlp_w_up, mlp_w_down):
    wts = dict(norm_mix_g=norm_mix_g, norm_mlp_g=norm_mlp_g[:, None, :], final_norm_g=final_norm_g,
               a_w_qkv=a_w_qkv.astype(BF16),
               a_w_kt=jnp.transpose(a_w_qkv[:, :, A_QK:2 * A_QK], (0, 2, 1)).astype(BF16),
               a_lambda=a_lambda, a_subln_g=a_subln_g[:, None, :],
               a_w_o=a_w_o.astype(BF16), b_w_qkv=b_w_qkv.astype(BF16), b_sinks=b_sinks,
               b_w_o=b_w_o.astype(BF16), mlp_w_up=mlp_w_up.astype(BF16),
               mlp_w_down=mlp_w_down.astype(BF16))
    n_a, db, past = cache_a_k.shape[:3]
    caches = dict(
        a_kt=jnp.transpose(cache_a_k, (0, 1, 3, 4, 5, 2)).reshape(n_a, db, A_QK, past),
        a_v=cache_a_v.reshape(n_a, db, past * A_HEADS, A_DV),
        b=[(cache_b_k[j], cache_b_v[j], _swap_halves(cache_b_k[j]), _swap_halves(cache_b_v[j]))
           for j in range(cache_b_k.shape[0])])
    yp, akp, avp, bkp, bvp = _trunk(x_prompt, None, wts, rel_table, tm=512, tm_post=1024, tf=1024)
    ys, aks, avs, bks, bvs = _trunk(x_sample, caches, wts, rel_table, tm=512, tm_post=512, tf=1024)
    return (yp, ys, akp, avp, bkp, bvp, aks, avs, bks, bvs)
```

```python
import functools
import math

import jax
import jax.numpy as jnp
from jax import lax
from jax.experimental import pallas as pl
from jax.experimental.pallas import tpu as pltpu

BF16 = jnp.bfloat16
F32 = jnp.float32

D_MODEL = 1024
DEPTH = 4
CHUNK = 64
A_HEADS = 8
A_DK = 64
A_DV = 128
A_QK = A_HEADS * 2 * A_DK
A_V = A_HEADS * A_DV
B_Q_HEADS = 16
B_KV_HEADS = 2
B_HD = 64
B_Q = B_Q_HEADS * B_HD
B_KV = B_KV_HEADS * B_HD
WINDOW = 128
NUM_BUCKETS = 32
MAX_DIST = 128
N_MAPS = 16
D_FF = 4 * D_MODEL
EPS = 1e-6
NEG = -1e30

LANES = 128
SUBLANES = 8
HALF = LANES // 2
VMEM_LIMIT = 56 * 1024 * 1024

NT_DIMS = (((1,), (1,)), ((), ()))


def _lambda_init(layer):
    return 0.8 - 0.6 * math.exp(-0.3 * layer)


def _rms(x, g):
    return x * lax.rsqrt(jnp.mean(x * x, axis=-1, keepdims=True) + EPS) * g


def _params(n_axes):
    return pltpu.CompilerParams(
        dimension_semantics=("arbitrary",) * n_axes, vmem_limit_bytes=VMEM_LIMIT)


def _lane_fold_max(s):
    mx = s[:, 0:LANES]
    for gi in range(1, s.shape[1] // LANES):
        mx = jnp.maximum(mx, s[:, gi * LANES:(gi + 1) * LANES])
    return mx


def _row_max(s):
    return jnp.broadcast_to(jnp.max(_lane_fold_max(s), axis=-1, keepdims=True), (s.shape[0], LANES))


def _exp_rows(s, m):
    return jnp.concatenate([jnp.exp(s[:, gi * LANES:(gi + 1) * LANES] - m)
                            for gi in range(s.shape[1] // LANES)], axis=1).astype(BF16)


def _with_ones(v):
    return jnp.concatenate([v, jnp.ones(v.shape, v.dtype)], axis=1)


def _proj_a_prompt_body(x_ref, g_ref, w_ref, wkt_ref, *refs, first):
    q_ref, kt_ref, v_ref, ktb_ref, vb_ref = refs[-5:]
    h = _rms(x_ref[...], g_ref[...]).astype(BF16)
    q = jnp.dot(h, w_ref[:, 0:A_QK], preferred_element_type=F32)
    q_ref[...] = (q * (A_DK ** -0.5)).astype(BF16)
    kt = lax.dot_general(wkt_ref[...], h, NT_DIMS, preferred_element_type=F32)
    ktb_ref[...] = kt.astype(BF16)
    v = jnp.dot(h, w_ref[:, 2 * A_QK:], preferred_element_type=F32)
    vb_ref[...] = v.astype(BF16)
    tm = v.shape[0]
    if first:
        for slot in range(1, kt_ref.shape[0]):
            kt_ref[slot] = jnp.zeros(kt_ref.shape[1:], F32)
            v_ref[slot] = jnp.zeros(v_ref.shape[1:], F32)
        kt_ref, v_ref = kt_ref.at[0], v_ref.at[0]
    kt_ref[...] = kt
    for hh in range(A_HEADS):
        v_ref[pl.ds(hh, tm, stride=A_HEADS), :] = v[:, hh * A_DV:(hh + 1) * A_DV]


def _proj_a_prompt(x, g, w, wkt, layer, kt_prev, v_prev, *, b, s, tm, n_a):
    m = b * s
    per_b = s // tm
    row = lambda i: (i, 0)
    full = lambda i: (0, 0)
    stacked = lambda i: (layer, 0, 0)
    first = kt_prev is None
    assert first == (layer == 0)
    in_specs = [pl.BlockSpec((tm, D_MODEL), row),
                pl.BlockSpec((1, D_MODEL), full),
                pl.BlockSpec((None, D_MODEL, 2 * A_QK + A_V), stacked),
                pl.BlockSpec((None, A_QK, D_MODEL), stacked)]
    operands = [x, g, w, wkt]
    if first:
        kt_spec = pl.BlockSpec((n_a, None, A_QK, tm), lambda i: (0, i // per_b, 0, i % per_b))
        v_spec = pl.BlockSpec((n_a, tm * A_HEADS, A_DV), lambda i: (0, i, 0))
        aliases = {}
    else:
        kt_spec = pl.BlockSpec((None, None, A_QK, tm), lambda i: (layer, i // per_b, 0, i % per_b))
        v_spec = pl.BlockSpec((None, tm * A_HEADS, A_DV), lambda i: (layer, i, 0))
        in_specs += [pl.BlockSpec(memory_space=pl.ANY)] * 2
        operands += [kt_prev, v_prev]
        aliases = {4: 1, 5: 2}
    return pl.pallas_call(
        functools.partial(_proj_a_prompt_body, first=first),
        grid=(m // tm,),
        in_specs=in_specs,
        out_specs=[pl.BlockSpec((tm, A_QK), row),
                   kt_spec,
                   v_spec,
                   pl.BlockSpec((None, A_QK, tm), lambda i: (i // per_b, 0, i % per_b)),
                   pl.BlockSpec((tm, A_V), row)],
        out_shape=[jax.ShapeDtypeStruct((m, A_QK), BF16),
                   jax.ShapeDtypeStruct((n_a, b, A_QK, s), F32),
                   jax.ShapeDtypeStruct((n_a, m * A_HEADS, A_DV), F32),
                   jax.ShapeDtypeStruct((b, A_QK, s), BF16),
                   jax.ShapeDtypeStruct((m, A_V), BF16)],
        input_output_aliases=aliases,
        compiler_params=_params(1),
        name="proj_a_prompt",
    )(*operands)


def _proj_a_sample_body(x_ref, g_ref, w_ref, q_ref, k_ref, v_ref, kb_ref, vb_ref):
    h = _rms(x_ref[...], g_ref[...]).astype(BF16)
    q = jnp.dot(h, w_ref[:, 0:A_QK], preferred_element_type=F32)
    q_ref[...] = (q * (A_DK ** -0.5)).astype(BF16)
    k = jnp.dot(h, w_ref[:, A_QK:2 * A_QK], preferred_element_type=F32)
    k_ref[...] = k
    kb_ref[...] = k.astype(BF16)
    v = jnp.dot(h, w_ref[:, 2 * A_QK:], preferred_element_type=F32)
    v_ref[...] = v
    vb_ref[...] = v.astype(BF16)


def _proj_a_sample(x, g, w, layer, tm):
    m = x.shape[0]
    row = lambda i: (i, 0)
    full = lambda i: (0, 0)
    return pl.pallas_call(
        _proj_a_sample_body,
        grid=(m // tm,),
        in_specs=[pl.BlockSpec((tm, D_MODEL), row),
                  pl.BlockSpec((1, D_MODEL), full),
                  pl.BlockSpec((None, D_MODEL, 2 * A_QK + A_V), lambda i: (layer, 0, 0))],
        out_specs=[pl.BlockSpec((tm, A_QK), row),
                   pl.BlockSpec((tm, A_QK), row),
                   pl.BlockSpec((tm, A_V), row),
                   pl.BlockSpec((tm, A_QK), row),
                   pl.BlockSpec((tm, A_V), row)],
        out_shape=[jax.ShapeDtypeStruct((m, A_QK), BF16),
                   jax.ShapeDtypeStruct((m, A_QK), F32),
                   jax.ShapeDtypeStruct((m, A_V), F32),
                   jax.ShapeDtypeStruct((m, A_QK), BF16),
                   jax.ShapeDtypeStruct((m, A_V), BF16)],
        compiler_params=_params(1),
        name="proj_a_sample",
    )(x, g, w)


def _proj_b_body(x_ref, g_ref, w_ref, q_ref, k_ref, v_ref, kb_ref, vb_ref):
    h = _rms(x_ref[...], g_ref[...]).astype(BF16)
    q = jnp.dot(h, w_ref[:, 0:B_Q], preferred_element_type=F32)
    q_ref[...] = (q * (B_HD ** -0.5)).astype(BF16)
    kv = jnp.dot(h, w_ref[:, B_Q:], preferred_element_type=F32)
    k, v = kv[:, 0:B_KV], kv[:, B_KV:]
    k_ref[...] = k
    kb_ref[...] = jnp.concatenate([k, pltpu.roll(k, HALF, axis=1)], axis=1).astype(BF16)
    v_ref[...] = v
    vb_ref[...] = jnp.concatenate([v, pltpu.roll(v, HALF, axis=1)], axis=1).astype(BF16)


def _proj_b(x, g, w, layer, tm):
    m = x.shape[0]
    row = lambda i: (i, 0)
    full = lambda i: (0, 0)
    return pl.pallas_call(
        _proj_b_body,
        grid=(m // tm,),
        in_specs=[pl.BlockSpec((tm, D_MODEL), row),
                  pl.BlockSpec((1, D_MODEL), full),
                  pl.BlockSpec((None, D_MODEL, B_Q + 2 * B_KV), lambda i: (layer, 0, 0))],
        out_specs=[pl.BlockSpec((tm, B_Q), row),
                   pl.BlockSpec((tm, B_KV), row),
                   pl.BlockSpec((tm, B_KV), row),
                   pl.BlockSpec((tm, 2 * B_KV), row),
                   pl.BlockSpec((tm, 2 * B_KV), row)],
        out_shape=[jax.ShapeDtypeStruct((m, B_Q), BF16),
                   jax.ShapeDtypeStruct((m, B_KV), F32),
                   jax.ShapeDtypeStruct((m, B_KV), F32),
                   jax.ShapeDtypeStruct((m, 2 * B_KV), BF16),
                   jax.ShapeDtypeStruct((m, 2 * B_KV), BF16)],
        compiler_params=_params(1),
        name="proj_b",
    )(x, g, w)


def _t5_bucket(rel):
    nb = NUM_BUCKETS // 2
    n = -rel
    ret = jnp.where(n < 0, nb, 0)
    n = jnp.abs(n)
    max_exact = nb // 2
    nf = jnp.maximum(n, 1).astype(jnp.float32)
    large = max_exact + (jnp.log(nf / max_exact) / math.log(MAX_DIST / max_exact)
                         * (nb - max_exact)).astype(jnp.int32)
    large = jnp.minimum(large, nb - 1)
    return ret + jnp.where(n < max_exact, n, large)


def _check_far_bucket(min_dist):
    nb, max_exact = NUM_BUCKETS // 2, NUM_BUCKETS // 4
    ratio = math.log(min_dist / max_exact) / math.log(MAX_DIST / max_exact)
    assert ratio * (nb - max_exact) >= nb - 1 - max_exact + 1e-3, "far keys need one shared bucket"


def _far_bias(table, tq, w):
    return jnp.broadcast_to(table[NUM_BUCKETS // 2 - 1].astype(F32)[:, None, None], (N_MAPS, tq, w))


def _near_bias(table, tq, w, back, band):
    r = jnp.arange(tq)[:, None]
    j = jnp.arange(w)[None, :] - back
    onehot = (_t5_bucket(j - r)[None] == jnp.arange(NUM_BUCKETS)[:, None, None]).astype(F32)
    onehot = lax.optimization_barrier(onehot)
    bias = jnp.einsum("bm,brj->mrj", table.astype(F32), onehot, precision=lax.Precision.HIGHEST)
    kc, qc = j // CHUNK, r // CHUNK
    mask = kc <= qc
    if band is not None:
        mask = mask & (kc >= qc - band)
    return jnp.where(mask[None], bias, NEG)


A_TILE = 256
A_ROWS = 64


def _diff_lambda(lamp_ref, lam_init):
    lp = lamp_ref[...]
    return (jnp.exp(jnp.sum(lp[0:1] * lp[1:2], axis=-1, keepdims=True))
            - jnp.exp(jnp.sum(lp[2:3] * lp[3:4], axis=-1, keepdims=True)) + lam_init)


def _stack_maps(q):
    lane = lax.broadcasted_iota(jnp.int32, q.shape, 1)
    zero = jnp.zeros_like(q)
    return jnp.concatenate([jnp.where(lane < HALF, q, zero), jnp.where(lane >= HALF, q, zero)], axis=0)


def _diff_epilogue(acc, lam, g, lam_init):
    tq = acc.shape[0] // 2
    o_all = acc[:, :A_DV] / acc[:, A_DV:]
    o = o_all[:tq] - lam * o_all[tq:]
    return (_rms(o, g) * (1.0 - lam_init)).astype(BF16)


def _attn_a_prompt_body(lamp_ref, q_ref, kt_ref, v_ref, bias_ref, g_ref, o_ref,
                        s_ref, p_ref, v1_ref, *, nt, lam_init):
    tl = A_TILE
    lam = _diff_lambda(lamp_ref, lam_init)
    v1_ref[:, 0:A_DV] = v_ref[0]
    v1_ref[:, A_DV:] = jnp.ones((nt * tl, A_DV), BF16)
    for i in reversed(range(nt)):
        buf = i % 2
        qs = _stack_maps(q_ref[0, i * tl:(i + 1) * tl, :])
        n_far = max(i - 1, 0)
        for kt in range(i + 1):
            cols = slice(kt * tl, (kt + 1) * tl)
            s = jnp.dot(qs, kt_ref[0, :, cols], preferred_element_type=F32)
            if kt >= n_far:
                s = s + bias_ref[2 - (i - kt)].reshape(2 * tl, tl)
            s_ref[buf, :, cols] = s
        far_groups, all_groups = n_far * tl // LANES, (i + 1) * tl // LANES
        for r in range(0, 2 * tl, A_ROWS):
            rows = slice(r, r + A_ROWS)
            group = lambda gi: s_ref[buf, rows, gi * LANES:(gi + 1) * LANES]
            mx = group(far_groups)
            for gi in range(far_groups + 1, all_groups):
                mx = jnp.maximum(mx, group(gi))
            if far_groups:
                c = bias_ref[0, r // tl, r % tl:r % tl + A_ROWS, 0:LANES]
                mf = group(0)
                for gi in range(1, far_groups):
                    mf = jnp.maximum(mf, group(gi))
                mx = jnp.maximum(mx, mf + c)
            m = jnp.broadcast_to(jnp.max(mx, axis=-1, keepdims=True), (A_ROWS, LANES))
            m_far = m - c if far_groups else m
            for gi in range(all_groups):
                shift = m_far if gi < far_groups else m
                p_ref[buf, rows, gi * LANES:(gi + 1) * LANES] = jnp.exp(group(gi) - shift).astype(BF16)
        nk = (i + 1) * tl
        acc = jnp.dot(p_ref[buf, :, 0:nk], v1_ref[0:nk, :], preferred_element_type=F32)
        o_ref[0, i * tl:(i + 1) * tl, :] = _diff_epilogue(acc, lam, g_ref[...], lam_init)


def _attn_a_prompt(q, kt, v, bias, lamp, g, layer, *, lam_init):
    b, s, _ = q.shape
    nt = s // A_TILE
    assert s % A_TILE == 0
    seq = pl.BlockSpec((1, s, LANES), lambda h, bi: (bi, 0, h))
    return pl.pallas_call(
        functools.partial(_attn_a_prompt_body, nt=nt, lam_init=lam_init),
        grid=(A_HEADS, b),
        in_specs=[pl.BlockSpec((None, 4, A_DK), lambda h, bi: (layer, 0, 0)),
                  seq,
                  pl.BlockSpec((1, LANES, s), lambda h, bi: (bi, h, 0)),
                  seq,
                  pl.BlockSpec((3, 2, A_TILE, A_TILE), lambda h, bi: (0, h, 0, 0)),
                  pl.BlockSpec((None, 1, A_DV), lambda h, bi: (layer, 0, 0))],
        out_specs=seq,
        out_shape=jax.ShapeDtypeStruct((b, s, A_V), BF16),
        scratch_shapes=[pltpu.VMEM((2, 2 * A_TILE, s), F32),
                        pltpu.VMEM((2, 2 * A_TILE, s), BF16),
                        pltpu.VMEM((s, 2 * A_DV), BF16)],
        compiler_params=_params(2),
        name="attn_a_prompt",
    )(lamp, q, kt, v, bias, g)


def _attn_a_sample_body(lamp_ref, q_ref, ktc_ref, vc_ref, kn_ref, vn_ref, bias_ref, g_ref, o_ref,
                        *, lam_init):
    tq = q_ref.shape[1]
    past = ktc_ref.shape[1]
    lam = _diff_lambda(lamp_ref, lam_init)
    for h in range(A_HEADS):
        cols = slice(h * LANES, (h + 1) * LANES)
        qs = _stack_maps(q_ref[0, :, cols])
        bias = bias_ref[2 * h:2 * h + 2].reshape(2 * tq, past + LANES)
        s_c = jnp.dot(qs, ktc_ref[cols, :].astype(BF16), preferred_element_type=F32)
        s_c = s_c + bias[:, 0:past]
        s_n = lax.dot_general(qs, kn_ref[0, :, cols], NT_DIMS, preferred_element_type=F32)
        s_n = s_n + bias[:, past:past + tq]
        m = jnp.maximum(_row_max(s_c), jnp.broadcast_to(jnp.max(s_n, axis=-1, keepdims=True),
                                                        (2 * tq, LANES)))
        p_c = _exp_rows(s_c, m)
        p_n = jnp.exp(s_n - m[:, 0:tq]).astype(BF16)
        v_c = vc_ref[pl.ds(h, past, stride=A_HEADS), :].astype(BF16)
        acc = (jnp.dot(p_c, _with_ones(v_c), preferred_element_type=F32)
               + jnp.dot(p_n, _with_ones(vn_ref[0, :, cols]), preferred_element_type=F32))
        o_ref[0, :, cols] = _diff_epilogue(acc, lam, g_ref[...], lam_init)


def _attn_a_sample(q, ktc, vc, kn, vn, bias, lamp, g, layer, *, lam_init):
    b, tq, _ = q.shape
    past = ktc.shape[3]
    new = lambda bi: (bi, 0, 0)
    return pl.pallas_call(
        functools.partial(_attn_a_sample_body, lam_init=lam_init),
        grid=(b,),
        in_specs=[pl.BlockSpec((None, 4, A_DK), lambda bi: (layer, 0, 0)),
                  pl.BlockSpec((1, tq, A_QK), new),
                  pl.BlockSpec((None, None, A_QK, past), lambda bi: (layer, bi, 0, 0)),
                  pl.BlockSpec((None, None, past * A_HEADS, A_DV), lambda bi: (layer, bi, 0, 0)),
                  pl.BlockSpec((1, tq, A_QK), new),
                  pl.BlockSpec((1, tq, A_V), new),
                  pl.BlockSpec((N_MAPS, tq, past + LANES), lambda bi: (0, 0, 0)),
                  pl.BlockSpec((None, 1, A_DV), lambda bi: (layer, 0, 0))],
        out_specs=pl.BlockSpec((1, tq, A_V), new),
        out_shape=jax.ShapeDtypeStruct((b, tq, A_V), BF16),
        compiler_params=_params(1),
        name="attn_a_sample",
    )(lamp, q, ktc, vc, kn, vn, bias, g)


_B_ORDER_A = (0, 2, 4, 6, 9, 11, 13, 15)
_B_ORDER_B = (1, 3, 5, 7, 8, 10, 12, 14)


def _attn_b_body(sink_ref, q_ref, k_ref, v_ref, bias_ref, o_ref, *, layer, tq, w, back, i_off):
    q0 = (pl.program_id(1) + i_off) * tq
    near_start = pl.multiple_of(jnp.maximum(q0 - back, 0), CHUNK)
    kk = k_ref[0, pl.ds(near_start, w), :]
    vv = v_ref[0, pl.ds(near_start, w), :]
    lane = lax.broadcasted_iota(jnp.int32, (tq, LANES), 1)
    lo = lane < HALF

    q_even, q_odd = [], []
    for c in range(B_Q_HEADS // 2):
        qc = q_ref[0, :, c * LANES:(c + 1) * LANES]
        zero = jnp.zeros_like(qc)
        q_even.append(jnp.where(lo, qc, zero))
        q_odd.append(jnp.where(lo, zero, qc))
    half = B_Q_HEADS // 4
    q_sets = (q_even[:half] + q_odd[half:], q_odd[:half] + q_even[half:])
    orders = (_B_ORDER_A, _B_ORDER_B)

    outs = []
    for t in range(2):
        qs = jnp.concatenate(q_sets[t], axis=0)
        s = lax.dot_general(qs, kk[:, t * LANES:(t + 1) * LANES], NT_DIMS,
                            preferred_element_type=F32)
        s = s + bias_ref[0, t].reshape((B_Q_HEADS // 2) * tq, w)
        sink = jnp.concatenate([jnp.full((tq, LANES), sink_ref[layer, h], F32) for h in orders[t]],
                               axis=0)
        m = jnp.maximum(_row_max(s), sink)
        p = _exp_rows(s, m)
        acc = jnp.dot(p, _with_ones(vv[:, t * LANES:(t + 1) * LANES]), preferred_element_type=F32)
        outs.append(acc[:, :LANES] / (acc[:, LANES:] + jnp.exp(sink - m)))

    for c in range(B_Q_HEADS // 2):
        oa = outs[0][c * tq:(c + 1) * tq]
        ob = outs[1][c * tq:(c + 1) * tq]
        oc = jnp.where(lo, oa, ob) if c < half else jnp.where(lo, ob, oa)
        o_ref[0, :, c * LANES:(c + 1) * LANES] = oc.astype(BF16)


def _attn_b(q, k, v, bias, sinks, layer, *, tq, w, back, i_off):
    b, sq, _ = q.shape
    sk = k.shape[1]
    nq = sq // tq
    assert (nq - 1 + i_off) * tq - back + w <= sk
    body = functools.partial(_attn_b_body, layer=layer, tq=tq, w=w, back=back, i_off=i_off)
    variant = (lambda i: jnp.minimum(i + i_off, 1))
    return pl.pallas_call(
        body,
        grid=(b, nq),
        in_specs=[pl.BlockSpec(memory_space=pltpu.SMEM),
                  pl.BlockSpec((1, tq, B_Q), lambda bi, i: (bi, i, 0)),
                  pl.BlockSpec((1, sk, 2 * B_KV), lambda bi, i: (bi, 0, 0)),
                  pl.BlockSpec((1, sk, 2 * B_KV), lambda bi, i: (bi, 0, 0)),
                  pl.BlockSpec((1, 2, B_Q_HEADS // 2, tq, w), lambda bi, i: (variant(i), 0, 0, 0, 0))],
        out_specs=pl.BlockSpec((1, tq, B_Q), lambda bi, i: (bi, i, 0)),
        out_shape=jax.ShapeDtypeStruct((b, sq, B_Q), BF16),
        compiler_params=_params(2),
        name="attn_b",
    )(sinks, q, k, v, bias)


def _post_body(x_ref, o_ref, wo_ref, g_ref, wu_ref, wd_ref, gf_ref, y_ref, hn_ref, *, final):
    f = pl.program_id(1)

    @pl.when(f == 0)
    def _():
        x1 = x_ref[...] + jnp.dot(o_ref[...], wo_ref[...], preferred_element_type=F32)
        y_ref[...] = x1
        hn_ref[...] = _rms(x1, g_ref[...]).astype(BF16)

    u = jnp.dot(hn_ref[...], wu_ref[...], preferred_element_type=F32)
    a = jnp.square(jnp.maximum(u, 0.0)).astype(BF16)
    y_ref[...] += jnp.dot(a, wd_ref[...], preferred_element_type=F32)

    if final:
        @pl.when(f == pl.num_programs(1) - 1)
        def _():
            y_ref[...] = _rms(y_ref[...], gf_ref[...])


def _post(x, o, wo, wo_layer, g, wu, wd, layer, gf, *, tm, tf, final):
    m = x.shape[0]
    row = lambda i, f: (i, 0)
    full = lambda i, f: (0, 0)
    return pl.pallas_call(
        functools.partial(_post_body, final=final),
        grid=(m // tm, D_FF // tf),
        in_specs=[pl.BlockSpec((tm, D_MODEL), row),
                  pl.BlockSpec((tm, D_MODEL), row),
                  pl.BlockSpec((None, D_MODEL, D_MODEL), lambda i, f: (wo_layer, 0, 0)),
                  pl.BlockSpec((None, 1, D_MODEL), lambda i, f: (layer, 0, 0)),
                  pl.BlockSpec((None, D_MODEL, tf), lambda i, f: (layer, 0, f)),
                  pl.BlockSpec((None, tf, D_MODEL), lambda i, f: (layer, f, 0)),
                  pl.BlockSpec((1, D_MODEL), full)],
        out_specs=pl.BlockSpec((tm, D_MODEL), row),
        out_shape=jax.ShapeDtypeStruct((m, D_MODEL), F32),
        scratch_shapes=[pltpu.VMEM((tm, D_MODEL), BF16)],
        compiler_params=_params(2),
        name="post",
    )(x, o, wo, g, wu, wd, gf)


_B_PROMPT = dict(tq=128, w=256, back=128)
_B_SAMPLE = dict(tq=64, w=256, back=128)
_A_SAMPLE_BACK = 128


def _a_prompt_bias(table):
    _check_far_bucket(A_TILE + 1)
    near = _near_bias(table, A_TILE, 2 * A_TILE, A_TILE, None)
    return jnp.stack([_far_bias(table, A_TILE, A_TILE), near[:, :, :A_TILE], near[:, :, A_TILE:]])


def _a_sample_bias(table, past, t):
    assert t <= LANES
    _check_far_bucket(_A_SAMPLE_BACK + 1)
    far = _far_bias(table, t, past - _A_SAMPLE_BACK)
    near = _near_bias(table, t, _A_SAMPLE_BACK + LANES, _A_SAMPLE_BACK, None)
    return jnp.concatenate([far, near], axis=2)


def _b_bias(table, tq, w, back):
    def one(bk):
        t = _near_bias(table, tq, w, bk, WINDOW // CHUNK)
        return jnp.stack([jnp.stack([t[h] for h in order]) for order in (_B_ORDER_A, _B_ORDER_B)])
    return jnp.stack([one(0), one(back)])


def _pad_keys(t, total):
    return jnp.pad(t, ((0, 0), (0, total - t.shape[1]), (0, 0)))


def _trunk(x, caches, wts, table, *, tm, tm_post, tf):
    b, s, _ = x.shape
    sample = caches is not None
    n_a = wts["a_w_qkv"].shape[0]
    xm = x.reshape(b * s, D_MODEL)
    b_cfg = _B_SAMPLE if sample else _B_PROMPT
    b_bias = _b_bias(table, **b_cfg)
    if sample:
        a_bias = _a_sample_bias(table, caches["a_kt"].shape[3], s)
    else:
        a_bias = _a_prompt_bias(table)
    ak, av, bk, bv = [], [], [], []
    kt_buf = v_buf = None
    for i in range(DEPTH):
        j = i // 2
        g_mix = wts["norm_mix_g"][i][None, :]
        if i % 2 == 0:
            lam_init = _lambda_init(i)
            if sample:
                q, k, v, kb, vb = _proj_a_sample(xm, g_mix, wts["a_w_qkv"], j, tm)
                ak.append(k.reshape(b, s, A_HEADS, 2, A_DK))
                av.append(v.reshape(b, s, A_HEADS, A_DV))
                o = _attn_a_sample(q.reshape(b, s, A_QK), caches["a_kt"], caches["a_v"],
                                   kb.reshape(b, s, A_QK), vb.reshape(b, s, A_V), a_bias,
                                   wts["a_lambda"], wts["a_subln_g"], j, lam_init=lam_init)
            else:
                q, kt_buf, v_buf, ktb, vb = _proj_a_prompt(
                    xm, g_mix, wts["a_w_qkv"], wts["a_w_kt"], j, kt_buf, v_buf,
                    b=b, s=s, tm=tm, n_a=n_a)
                o = _attn_a_prompt(q.reshape(b, s, A_QK), ktb, vb.reshape(b, s, A_V), a_bias,
                                   wts["a_lambda"], wts["a_subln_g"], j, lam_init=lam_init)
            wo, wo_layer = wts["a_w_o"], j
        else:
            q, k, v, kb, vb = _proj_b(xm, g_mix, wts["b_w_qkv"], j, tm)
            k = k.reshape(b, s, B_KV_HEADS, B_HD)
            v = v.reshape(b, s, B_KV_HEADS, B_HD)
            kb = kb.reshape(b, s, 2 * B_KV)
            vb = vb.reshape(b, s, 2 * B_KV)
            i_off = 0
            if sample:
                ck, cv, ckb, cvb = caches["b"][j]
                bk.append(jnp.concatenate([ck, k], axis=1)[:, s:])
                bv.append(jnp.concatenate([cv, v], axis=1)[:, s:])
                i_off = ckb.shape[1] // b_cfg["tq"]
                total = ckb.shape[1] - b_cfg["back"] + b_cfg["w"]
                kb = _pad_keys(jnp.concatenate([ckb, kb], axis=1), total)
                vb = _pad_keys(jnp.concatenate([cvb, vb], axis=1), total)
            else:
                bk.append(k[:, s - WINDOW:])
                bv.append(v[:, s - WINDOW:])
            o = _attn_b(q.reshape(b, s, B_Q), kb, vb, b_bias, wts["b_sinks"], j, i_off=i_off, **b_cfg)
            wo, wo_layer = wts["b_w_o"], j
        xm = _post(xm, o.reshape(b * s, D_MODEL), wo, wo_layer, wts["norm_mlp_g"],
                   wts["mlp_w_up"], wts["mlp_w_down"], i, wts["final_norm_g"][None, :],
                   tm=tm_post, tf=tf, final=(i == DEPTH - 1))
    if sample:
        a_k, a_v = jnp.stack(ak), jnp.stack(av)
    else:
        a_k = jnp.transpose(kt_buf.reshape(n_a, b, A_HEADS, 2, A_DK, s), (0, 1, 5, 2, 3, 4))
        a_v = v_buf.reshape(n_a, b, s, A_HEADS, A_DV)
    return xm.reshape(b, s, D_MODEL), a_k, a_v, jnp.stack(bk), jnp.stack(bv)


def _swap_halves(t):
    flat = t.reshape(t.shape[:-2] + (B_KV,))
    swapped = t[..., ::-1, :].reshape(t.shape[:-2] + (B_KV,))
    return jnp.concatenate([flat, swapped], axis=-1).astype(BF16)


def kernel(x_prompt, x_sample, cache_a_k, cache_a_v, cache_b_k, cache_b_v, rel_table,
           norm_mix_g, norm_mlp_g, final_norm_g, a_w_qkv, a_lambda, a_subln_g, a_w_o,
           b_w_qkv, b_sinks, b_w_o, mlp_w_up, mlp_w_down):
    wts = dict(norm_mix_g=norm_mix_g, norm_mlp_g=norm_mlp_g[:, None, :], final_norm_g=final_norm_g,
               a_w_qkv=a_w_qkv.astype(BF16),
               a_w_kt=jnp.transpose(a_w_qkv[:, :, A_QK:2 * A_QK], (0, 2, 1)).astype(BF16),
               a_lambda=a_lambda, a_subln_g=a_subln_g[:, None, :],
               a_w_o=a_w_o.astype(BF16), b_w_qkv=b_w_qkv.astype(BF16), b_sinks=b_sinks,
               b_w_o=b_w_o.astype(BF16), mlp_w_up=mlp_w_up.astype(BF16),
               mlp_w_down=mlp_w_down.astype(BF16))
    n_a, db, past = cache_a_k.shape[:3]
    caches = dict(
        a_kt=jnp.transpose(cache_a_k, (0, 1, 3, 4, 5, 2)).reshape(n_a, db, A_QK, past),
        a_v=cache_a_v.reshape(n_a, db, past * A_HEADS, A_DV),
        b=[(cache_b_k[j], cache_b_v[j], _swap_halves(cache_b_k[j]), _swap_halves(cache_b_v[j]))
           for j in range(cache_b_k.shape[0])])
    yp, akp, avp, bkp, bvp = _trunk(x_prompt, None, wts, rel_table, tm=512, tm_post=1024, tf=1024)
    ys, aks, avs, bks, bvs = _trunk(x_sample, caches, wts, rel_table, tm=512, tm_post=512, tf=1024)
    return (yp, ys, akp, avp, bkp, bvp, aks, avs, bks, bvs)
```

```python
import functools
import math

import jax
import jax.numpy as jnp
from jax import lax
from jax.experimental import pallas as pl
from jax.experimental.pallas import tpu as pltpu

BF16 = jnp.bfloat16
F32 = jnp.float32

D_MODEL = 1024
DEPTH = 4
CHUNK = 64
A_HEADS = 8
A_DK = 64
A_DV = 128
A_QK = A_HEADS * 2 * A_DK
A_V = A_HEADS * A_DV
B_Q_HEADS = 16
B_KV_HEADS = 2
B_HD = 64
B_Q = B_Q_HEADS * B_HD
B_KV = B_KV_HEADS * B_HD
WINDOW = 128
NUM_BUCKETS = 32
MAX_DIST = 128
N_MAPS = 16
D_FF = 4 * D_MODEL
EPS = 1e-6
NEG = -1e30

LANES = 128
SUBLANES = 8
HALF = LANES // 2
VMEM_LIMIT = 56 * 1024 * 1024

NT_DIMS = (((1,), (1,)), ((), ()))

LOG2E = math.log2(math.e)


def _lambda_init(layer):
    return 0.8 - 0.6 * math.exp(-0.3 * layer)


def _rms(x, g):
    return x * lax.rsqrt(jnp.mean(x * x, axis=-1, keepdims=True) + EPS) * g


def _params(n_axes):
    return pltpu.CompilerParams(
        dimension_semantics=("arbitrary",) * n_axes, vmem_limit_bytes=VMEM_LIMIT)


def _lane_fold_max(s):
    mx = s[:, 0:LANES]
    for gi in range(1, s.shape[1] // LANES):
        mx = jnp.maximum(mx, s[:, gi * LANES:(gi + 1) * LANES])
    return mx


def _row_max(s):
    return jnp.broadcast_to(jnp.max(_lane_fold_max(s), axis=-1, keepdims=True), (s.shape[0], LANES))


def _exp_rows(s, m):
    return jnp.concatenate([jnp.exp2(s[:, gi * LANES:(gi + 1) * LANES] - m)
                            for gi in range(s.shape[1] // LANES)], axis=1).astype(BF16)


def _with_ones(v):
    return jnp.concatenate([v, jnp.ones(v.shape, v.dtype)], axis=1)


def _proj_a_prompt_body(x_ref, g_ref, w_ref, wkt_ref, *refs, first):
    q_ref, kt_ref, v_ref, ktb_ref, vb_ref = refs[-5:]
    h = _rms(x_ref[...], g_ref[...]).astype(BF16)
    q = jnp.dot(h, w_ref[:, 0:A_QK], preferred_element_type=F32)
    q_ref[...] = (q * (A_DK ** -0.5 * LOG2E)).astype(BF16)
    kt = lax.dot_general(wkt_ref[...], h, NT_DIMS, preferred_element_type=F32)
    ktb_ref[...] = kt.astype(BF16)
    v = jnp.dot(h, w_ref[:, 2 * A_QK:], preferred_element_type=F32)
    vb_ref[...] = v.astype(BF16)
    tm = v.shape[0]
    if first:
        for slot in range(1, kt_ref.shape[0]):
            kt_ref[slot] = jnp.zeros(kt_ref.shape[1:], F32)
            v_ref[slot] = jnp.zeros(v_ref.shape[1:], F32)
        kt_ref, v_ref = kt_ref.at[0], v_ref.at[0]
    kt_ref[...] = kt
    for hh in range(A_HEADS):
        v_ref[pl.ds(hh, tm, stride=A_HEADS), :] = v[:, hh * A_DV:(hh + 1) * A_DV]


def _proj_a_prompt(x, g, w, wkt, layer, kt_prev, v_prev, *, b, s, tm, n_a):
    m = b * s
    per_b = s // tm
    row = lambda i: (i, 0)
    full = lambda i: (0, 0)
    stacked = lambda i: (layer, 0, 0)
    first = kt_prev is None
    assert first == (layer == 0)
    in_specs = [pl.BlockSpec((tm, D_MODEL), row),
                pl.BlockSpec((1, D_MODEL), full),
                pl.BlockSpec((None, D_MODEL, 2 * A_QK + A_V), stacked),
                pl.BlockSpec((None, A_QK, D_MODEL), stacked)]
    operands = [x, g, w, wkt]
    if first:
        kt_spec = pl.BlockSpec((n_a, None, A_QK, tm), lambda i: (0, i // per_b, 0, i % per_b))
        v_spec = pl.BlockSpec((n_a, tm * A_HEADS, A_DV), lambda i: (0, i, 0))
        aliases = {}
    else:
        kt_spec = pl.BlockSpec((None, None, A_QK, tm), lambda i: (layer, i // per_b, 0, i % per_b))
        v_spec = pl.BlockSpec((None, tm * A_HEADS, A_DV), lambda i: (layer, i, 0))
        in_specs += [pl.BlockSpec(memory_space=pl.ANY)] * 2
        operands += [kt_prev, v_prev]
        aliases = {4: 1, 5: 2}
    return pl.pallas_call(
        functools.partial(_proj_a_prompt_body, first=first),
        grid=(m // tm,),
        in_specs=in_specs,
        out_specs=[pl.BlockSpec((tm, A_QK), row),
                   kt_spec,
                   v_spec,
                   pl.BlockSpec((None, A_QK, tm), lambda i: (i // per_b, 0, i % per_b)),
                   pl.BlockSpec((tm, A_V), row)],
        out_shape=[jax.ShapeDtypeStruct((m, A_QK), BF16),
                   jax.ShapeDtypeStruct((n_a, b, A_QK, s), F32),
                   jax.ShapeDtypeStruct((n_a, m * A_HEADS, A_DV), F32),
                   jax.ShapeDtypeStruct((b, A_QK, s), BF16),
                   jax.ShapeDtypeStruct((m, A_V), BF16)],
        input_output_aliases=aliases,
        compiler_params=_params(1),
        name="proj_a_prompt",
    )(*operands)


def _proj_a_sample_body(x_ref, g_ref, w_ref, q_ref, k_ref, v_ref, kb_ref, vb_ref):
    h = _rms(x_ref[...], g_ref[...]).astype(BF16)
    q = jnp.dot(h, w_ref[:, 0:A_QK], preferred_element_type=F32)
    q_ref[...] = (q * (A_DK ** -0.5 * LOG2E)).astype(BF16)
    k = jnp.dot(h, w_ref[:, A_QK:2 * A_QK], preferred_element_type=F32)
    k_ref[...] = k
    kb_ref[...] = k.astype(BF16)
    v = jnp.dot(h, w_ref[:, 2 * A_QK:], preferred_element_type=F32)
    v_ref[...] = v
    vb_ref[...] = v.astype(BF16)


def _proj_a_sample(x, g, w, layer, tm):
    m = x.shape[0]
    row = lambda i: (i, 0)
    full = lambda i: (0, 0)
    return pl.pallas_call(
        _proj_a_sample_body,
        grid=(m // tm,),
        in_specs=[pl.BlockSpec((tm, D_MODEL), row),
                  pl.BlockSpec((1, D_MODEL), full),
                  pl.BlockSpec((None, D_MODEL, 2 * A_QK + A_V), lambda i: (layer, 0, 0))],
        out_specs=[pl.BlockSpec((tm, A_QK), row),
                   pl.BlockSpec((tm, A_QK), row),
                   pl.BlockSpec((tm, A_V), row),
                   pl.BlockSpec((tm, A_QK), row),
                   pl.BlockSpec((tm, A_V), row)],
        out_shape=[jax.ShapeDtypeStruct((m, A_QK), BF16),
                   jax.ShapeDtypeStruct((m, A_QK), F32),
                   jax.ShapeDtypeStruct((m, A_V), F32),
                   jax.ShapeDtypeStruct((m, A_QK), BF16),
                   jax.ShapeDtypeStruct((m, A_V), BF16)],
        compiler_params=_params(1),
        name="proj_a_sample",
    )(x, g, w)


def _proj_b_body(x_ref, g_ref, w_ref, q_ref, k_ref, v_ref, kb_ref, vb_ref):
    h = _rms(x_ref[...], g_ref[...]).astype(BF16)
    q = jnp.dot(h, w_ref[:, 0:B_Q], preferred_element_type=F32)
    q_ref[...] = (q * (B_HD ** -0.5 * LOG2E)).astype(BF16)
    kv = jnp.dot(h, w_ref[:, B_Q:], preferred_element_type=F32)
    k, v = kv[:, 0:B_KV], kv[:, B_KV:]
    k_ref[...] = k
    kb_ref[...] = jnp.concatenate([k, pltpu.roll(k, HALF, axis=1)], axis=1).astype(BF16)
    v_ref[...] = v
    vb_ref[...] = jnp.concatenate([v, pltpu.roll(v, HALF, axis=1)], axis=1).astype(BF16)


def _proj_b(x, g, w, layer, tm):
    m = x.shape[0]
    row = lambda i: (i, 0)
    full = lambda i: (0, 0)
    return pl.pallas_call(
        _proj_b_body,
        grid=(m // tm,),
        in_specs=[pl.BlockSpec((tm, D_MODEL), row),
                  pl.BlockSpec((1, D_MODEL), full),
                  pl.BlockSpec((None, D_MODEL, B_Q + 2 * B_KV), lambda i: (layer, 0, 0))],
        out_specs=[pl.BlockSpec((tm, B_Q), row),
                   pl.BlockSpec((tm, B_KV), row),
                   pl.BlockSpec((tm, B_KV), row),
                   pl.BlockSpec((tm, 2 * B_KV), row),
                   pl.BlockSpec((tm, 2 * B_KV), row)],
        out_shape=[jax.ShapeDtypeStruct((m, B_Q), BF16),
                   jax.ShapeDtypeStruct((m, B_KV), F32),
                   jax.ShapeDtypeStruct((m, B_KV), F32),
                   jax.ShapeDtypeStruct((m, 2 * B_KV), BF16),
                   jax.ShapeDtypeStruct((m, 2 * B_KV), BF16)],
        compiler_params=_params(1),
        name="proj_b",
    )(x, g, w)


def _t5_bucket(rel):
    nb = NUM_BUCKETS // 2
    n = -rel
    ret = jnp.where(n < 0, nb, 0)
    n = jnp.abs(n)
    max_exact = nb // 2
    nf = jnp.maximum(n, 1).astype(jnp.float32)
    large = max_exact + (jnp.log(nf / max_exact) / math.log(MAX_DIST / max_exact)
                         * (nb - max_exact)).astype(jnp.int32)
    large = jnp.minimum(large, nb - 1)
    return ret + jnp.where(n < max_exact, n, large)


def _check_far_bucket(min_dist):
    nb, max_exact = NUM_BUCKETS // 2, NUM_BUCKETS // 4
    ratio = math.log(min_dist / max_exact) / math.log(MAX_DIST / max_exact)
    assert ratio * (nb - max_exact) >= nb - 1 - max_exact + 1e-3, "far keys need one shared bucket"


def _far_bias(table, tq, w):
    far = table[NUM_BUCKETS // 2 - 1].astype(F32) * LOG2E
    return jnp.broadcast_to(far[:, None, None], (N_MAPS, tq, w))


def _near_bias(table, tq, w, back, band):
    r = jnp.arange(tq)[:, None]
    j = jnp.arange(w)[None, :] - back
    onehot = (_t5_bucket(j - r)[None] == jnp.arange(NUM_BUCKETS)[:, None, None]).astype(F32)
    onehot = lax.optimization_barrier(onehot)
    bias = jnp.einsum("bm,brj->mrj", table.astype(F32), onehot, precision=lax.Precision.HIGHEST)
    kc, qc = j // CHUNK, r // CHUNK
    mask = kc <= qc
    if band is not None:
        mask = mask & (kc >= qc - band)
    return jnp.where(mask[None], bias * LOG2E, NEG)


A_TILE = 256
A_ROWS = 64
A_HEADS_PER_STEP = 1


def _diff_lambda(lamp_ref, lam_init):
    lp = lamp_ref[...]
    return (jnp.exp(jnp.sum(lp[0:1] * lp[1:2], axis=-1, keepdims=True))
            - jnp.exp(jnp.sum(lp[2:3] * lp[3:4], axis=-1, keepdims=True)) + lam_init)


def _stack_maps(q):
    lane = lax.broadcasted_iota(jnp.int32, q.shape, 1)
    zero = jnp.zeros_like(q)
    return jnp.concatenate([jnp.where(lane < HALF, q, zero), jnp.where(lane >= HALF, q, zero)], axis=0)


def _diff_epilogue(acc, lam, g, lam_init):
    tq = acc.shape[0] // 2
    o_all = acc[:, :A_DV] / acc[:, A_DV:]
    o = o_all[:tq] - lam * o_all[tq:]
    return (_rms(o, g) * (1.0 - lam_init)).astype(BF16)


def _attn_a_tile(i, q, kt_ref, bias_ref, s_ref, m_ref, v1_ref):
    tl = A_TILE
    qs = _stack_maps(q)
    n_far = max(i - 1, 0)
    for kt in range(i + 1):
        cols = slice(kt * tl, (kt + 1) * tl)
        s = jnp.dot(qs, kt_ref[:, cols], preferred_element_type=F32)
        if kt >= n_far:
            s = s + bias_ref[2 - (i - kt)].reshape(2 * tl, tl)
        s_ref[:, cols] = s
    far_groups, all_groups = n_far * tl // LANES, (i + 1) * tl // LANES
    for r in range(0, 2 * tl, A_ROWS):
        rows = slice(r, r + A_ROWS)
        group = lambda gi: s_ref[rows, gi * LANES:(gi + 1) * LANES]
        mx = group(far_groups)
        for gi in range(far_groups + 1, all_groups):
            mx = jnp.maximum(mx, group(gi))
        if far_groups:
            c = bias_ref[0, r // tl, r % tl:r % tl + A_ROWS, 0:LANES]
            mf = group(0)
            for gi in range(1, far_groups):
                mf = jnp.maximum(mf, group(gi))
            mx = jnp.maximum(mx, mf + c)
        m = jnp.broadcast_to(jnp.max(mx, axis=-1, keepdims=True), (A_ROWS, LANES))
        m_ref[1, rows, :] = m
        if far_groups:
            m_ref[0, rows, :] = m - c
    acc = None
    for kt in range(i + 1):
        cols = slice(kt * tl, (kt + 1) * tl)
        p = _exp_rows(s_ref[:, cols], m_ref[1 if kt >= n_far else 0])
        part = jnp.dot(p, v1_ref[cols, :], preferred_element_type=F32)
        acc = part if acc is None else acc + part
    return acc


def _attn_a_prompt_body(lamp_ref, q_ref, kt_ref, v_ref, bias_ref, g_ref, o_ref,
                        s_ref, m_ref, v1_ref, *, nt, nh, lam_init):
    tl = A_TILE
    lam = _diff_lambda(lamp_ref, lam_init)
    for hh in range(nh):
        v1_ref[hh, :, 0:A_DV] = v_ref[0, :, hh * A_DV:(hh + 1) * A_DV]
        v1_ref[hh, :, A_DV:] = jnp.ones((nt * tl, A_DV), BF16)
    for i in reversed(range(nt)):
        rows = slice(i * tl, (i + 1) * tl)
        for hh in range(nh):
            lanes = slice(hh * LANES, (hh + 1) * LANES)
            acc = _attn_a_tile(i, q_ref[0, rows, lanes], kt_ref.at[0, lanes, :],
                               bias_ref.at[:, 2 * hh:2 * hh + 2], s_ref.at[hh, i % 2],
                               m_ref.at[hh, i % 2], v1_ref.at[hh])
            o_ref[0, rows, lanes] = _diff_epilogue(acc, lam, g_ref[...], lam_init)


def _attn_a_prompt(q, kt, v, bias, lamp, g, layer, *, lam_init):
    b, s, _ = q.shape
    nt = s // A_TILE
    nh = A_HEADS_PER_STEP
    assert s % A_TILE == 0 and A_HEADS % nh == 0
    seq = pl.BlockSpec((1, s, nh * LANES), lambda h, bi: (bi, 0, h))
    return pl.pallas_call(
        functools.partial(_attn_a_prompt_body, nt=nt, nh=nh, lam_init=lam_init),
        grid=(A_HEADS // nh, b),
        in_specs=[pl.BlockSpec((None, 4, A_DK), lambda h, bi: (layer, 0, 0)),
                  seq,
                  pl.BlockSpec((1, nh * LANES, s), lambda h, bi: (bi, h, 0)),
                  seq,
                  pl.BlockSpec((3, 2 * nh, A_TILE, A_TILE), lambda h, bi: (0, h, 0, 0)),
                  pl.BlockSpec((None, 1, A_DV), lambda h, bi: (layer, 0, 0))],
        out_specs=seq,
        out_shape=jax.ShapeDtypeStruct((b, s, A_V), BF16),
        scratch_shapes=[pltpu.VMEM((nh, 2, 2 * A_TILE, s), F32),
                        pltpu.VMEM((nh, 2, 2, 2 * A_TILE, LANES), F32),
                        pltpu.VMEM((nh, s, 2 * A_DV), BF16)],
        compiler_params=_params(2),
        name="attn_a_prompt",
    )(lamp, q, kt, v, bias, g)


def _attn_a_sample_body(lamp_ref, q_ref, ktc_ref, vc_ref, kn_ref, vn_ref, bias_ref, g_ref, o_ref,
                        *, lam_init):
    tq = q_ref.shape[1]
    past = ktc_ref.shape[1]
    lam = _diff_lambda(lamp_ref, lam_init)
    for h in range(A_HEADS):
        cols = slice(h * LANES, (h + 1) * LANES)
        qs = _stack_maps(q_ref[0, :, cols])
        bias = bias_ref[2 * h:2 * h + 2].reshape(2 * tq, past + LANES)
        s_c = jnp.dot(qs, ktc_ref[cols, :].astype(BF16), preferred_element_type=F32)
        s_c = s_c + bias[:, 0:past]
        s_n = lax.dot_general(qs, kn_ref[0, :, cols], NT_DIMS, preferred_element_type=F32)
        s_n = s_n + bias[:, past:past + tq]
        m = jnp.maximum(_row_max(s_c), jnp.broadcast_to(jnp.max(s_n, axis=-1, keepdims=True),
                                                        (2 * tq, LANES)))
        p_c = _exp_rows(s_c, m)
        p_n = jnp.exp2(s_n - m[:, 0:tq]).astype(BF16)
        v_c = vc_ref[pl.ds(h, past, stride=A_HEADS), :].astype(BF16)
        acc = (jnp.dot(p_c, _with_ones(v_c), preferred_element_type=F32)
               + jnp.dot(p_n, _with_ones(vn_ref[0, :, cols]), preferred_element_type=F32))
        o_ref[0, :, cols] = _diff_epilogue(acc, lam, g_ref[...], lam_init)


def _attn_a_sample(q, ktc, vc, kn, vn, bias, lamp, g, layer, *, lam_init):
    b, tq, _ = q.shape
    past = ktc.shape[3]
    new = lambda bi: (bi, 0, 0)
    return pl.pallas_call(
        functools.partial(_attn_a_sample_body, lam_init=lam_init),
        grid=(b,),
        in_specs=[pl.BlockSpec((None, 4, A_DK), lambda bi: (layer, 0, 0)),
                  pl.BlockSpec((1, tq, A_QK), new),
                  pl.BlockSpec((None, None, A_QK, past), lambda bi: (layer, bi, 0, 0)),
                  pl.BlockSpec((None, None, past * A_HEADS, A_DV), lambda bi: (layer, bi, 0, 0)),
                  pl.BlockSpec((1, tq, A_QK), new),
                  pl.BlockSpec((1, tq, A_V), new),
                  pl.BlockSpec((N_MAPS, tq, past + LANES), lambda bi: (0, 0, 0)),
                  pl.BlockSpec((None, 1, A_DV), lambda bi: (layer, 0, 0))],
        out_specs=pl.BlockSpec((1, tq, A_V), new),
        out_shape=jax.ShapeDtypeStruct((b, tq, A_V), BF16),
        compiler_params=_params(1),
        name="attn_a_sample",
    )(lamp, q, ktc, vc, kn, vn, bias, g)


_B_ORDER_A = (0, 2, 4, 6, 9, 11, 13, 15)
_B_ORDER_B = (1, 3, 5, 7, 8, 10, 12, 14)


def _attn_b_body(sink_ref, q_ref, k_ref, v_ref, bias_ref, o_ref, *, layer, tq, w, back, i_off, nsub):
    lane = lax.broadcasted_iota(jnp.int32, (tq, LANES), 1)
    lo = lane < HALF
    half = B_Q_HEADS // 4
    orders = (_B_ORDER_A, _B_ORDER_B)
    sinks = [jnp.concatenate([jnp.full((tq, LANES), sink_ref[layer, h] * LOG2E, F32)
                              for h in orders[t]], axis=0) for t in range(2)]
    for u in range(nsub):
        tile = pl.program_id(1) * nsub + u + i_off
        variant = 1 if (u > 0 or i_off > 0) else jnp.minimum(tile, 1)
        near_start = pl.multiple_of(jnp.maximum(tile * tq - back, 0), CHUNK)
        kk = k_ref[0, pl.ds(near_start, w), :]
        vv = v_ref[0, pl.ds(near_start, w), :]
        rows = slice(u * tq, (u + 1) * tq)

        q_even, q_odd = [], []
        for c in range(B_Q_HEADS // 2):
            qc = q_ref[0, rows, c * LANES:(c + 1) * LANES]
            zero = jnp.zeros_like(qc)
            q_even.append(jnp.where(lo, qc, zero))
            q_odd.append(jnp.where(lo, zero, qc))
        q_sets = (q_even[:half] + q_odd[half:], q_odd[:half] + q_even[half:])

        outs = []
        for t in range(2):
            qs = jnp.concatenate(q_sets[t], axis=0)
            s = lax.dot_general(qs, kk[:, t * LANES:(t + 1) * LANES], NT_DIMS,
                                preferred_element_type=F32)
            s = s + bias_ref[variant, t].reshape((B_Q_HEADS // 2) * tq, w)
            m = jnp.maximum(_row_max(s), sinks[t])
            p = _exp_rows(s, m)
            acc = jnp.dot(p, _with_ones(vv[:, t * LANES:(t + 1) * LANES]),
                          preferred_element_type=F32)
            outs.append(acc[:, :LANES] / (acc[:, LANES:] + jnp.exp2(sinks[t] - m)))

        for c in range(B_Q_HEADS // 2):
            oa = outs[0][c * tq:(c + 1) * tq]
            ob = outs[1][c * tq:(c + 1) * tq]
            oc = jnp.where(lo, oa, ob) if c < half else jnp.where(lo, ob, oa)
            o_ref[0, rows, c * LANES:(c + 1) * LANES] = oc.astype(BF16)


def _attn_b(q, k, v, bias, sinks, layer, *, tq, w, back, i_off, nsub):
    b, sq, _ = q.shape
    sk = k.shape[1]
    rows = nsub * tq
    assert sq % rows == 0 and (sq // tq - 1 + i_off) * tq - back + w <= sk
    body = functools.partial(_attn_b_body, layer=layer, tq=tq, w=w, back=back, i_off=i_off,
                             nsub=nsub)
    return pl.pallas_call(
        body,
        grid=(b, sq // rows),
        in_specs=[pl.BlockSpec(memory_space=pltpu.SMEM),
                  pl.BlockSpec((1, rows, B_Q), lambda bi, i: (bi, i, 0)),
                  pl.BlockSpec((1, sk, 2 * B_KV), lambda bi, i: (bi, 0, 0)),
                  pl.BlockSpec((1, sk, 2 * B_KV), lambda bi, i: (bi, 0, 0)),
                  pl.BlockSpec((2, 2, B_Q_HEADS // 2, tq, w), lambda bi, i: (0, 0, 0, 0, 0))],
        out_specs=pl.BlockSpec((1, rows, B_Q), lambda bi, i: (bi, i, 0)),
        out_shape=jax.ShapeDtypeStruct((b, sq, B_Q), BF16),
        compiler_params=_params(2),
        name="attn_b",
    )(sinks, q, k, v, bias)


def _post_body(x_ref, o_ref, wo_ref, g_ref, wu_ref, wd_ref, gf_ref, y_ref, hn_ref, *, final):
    f = pl.program_id(1)

    @pl.when(f == 0)
    def _():
        x1 = x_ref[...] + jnp.dot(o_ref[...], wo_ref[...], preferred_element_type=F32)
        y_ref[...] = x1
        hn_ref[...] = _rms(x1, g_ref[...]).astype(BF16)

    u = jnp.dot(hn_ref[...], wu_ref[...], preferred_element_type=F32)
    a = jnp.square(jnp.maximum(u, 0.0)).astype(BF16)
    y_ref[...] += jnp.dot(a, wd_ref[...], preferred_element_type=F32)

    if final:
        @pl.when(f == pl.num_programs(1) - 1)
        def _():
            y_ref[...] = _rms(y_ref[...], gf_ref[...])


def _post(x, o, wo, wo_layer, g, wu, wd, layer, gf, *, tm, tf, final):
    m = x.shape[0]
    row = lambda i, f: (i, 0)
    full = lambda i, f: (0, 0)
    return pl.pallas_call(
        functools.partial(_post_body, final=final),
        grid=(m // tm, D_FF // tf),
        in_specs=[pl.BlockSpec((tm, D_MODEL), row),
                  pl.BlockSpec((tm, D_MODEL), row),
                  pl.BlockSpec((None, D_MODEL, D_MODEL), lambda i, f: (wo_layer, 0, 0)),
                  pl.BlockSpec((None, 1, D_MODEL), lambda i, f: (layer, 0, 0)),
                  pl.BlockSpec((None, D_MODEL, tf), lambda i, f: (layer, 0, f)),
                  pl.BlockSpec((None, tf, D_MODEL), lambda i, f: (layer, f, 0)),
                  pl.BlockSpec((1, D_MODEL), full)],
        out_specs=pl.BlockSpec((tm, D_MODEL), row),
        out_shape=jax.ShapeDtypeStruct((m, D_MODEL), F32),
        scratch_shapes=[pltpu.VMEM((tm, D_MODEL), BF16)],
        compiler_params=_params(2),
        name="post",
    )(x, o, wo, g, wu, wd, gf)


_B_PROMPT = dict(tq=128, w=256, back=128)
_B_SAMPLE = dict(tq=64, w=256, back=128)
B_TILES_PER_STEP = 4
_A_SAMPLE_BACK = 128


def _a_prompt_bias(table):
    _check_far_bucket(A_TILE + 1)
    near = _near_bias(table, A_TILE, 2 * A_TILE, A_TILE, None)
    return jnp.stack([_far_bias(table, A_TILE, A_TILE), near[:, :, :A_TILE], near[:, :, A_TILE:]])


def _a_sample_bias(table, past, t):
    assert t <= LANES
    _check_far_bucket(_A_SAMPLE_BACK + 1)
    far = _far_bias(table, t, past - _A_SAMPLE_BACK)
    near = _near_bias(table, t, _A_SAMPLE_BACK + LANES, _A_SAMPLE_BACK, None)
    return jnp.concatenate([far, near], axis=2)


def _b_bias(table, tq, w, back):
    def one(bk):
        t = _near_bias(table, tq, w, bk, WINDOW // CHUNK)
        return jnp.stack([jnp.stack([t[h] for h in order]) for order in (_B_ORDER_A, _B_ORDER_B)])
    return jnp.stack([one(0), one(back)])


def _pad_keys(t, total):
    return jnp.pad(t, ((0, 0), (0, total - t.shape[1]), (0, 0)))


def _trunk(x, caches, wts, table, *, tm, tm_post, tf):
    b, s, _ = x.shape
    sample = caches is not None
    n_a = wts["a_w_qkv"].shape[0]
    xm = x.reshape(b * s, D_MODEL)
    b_cfg = _B_SAMPLE if sample else _B_PROMPT
    b_bias = _b_bias(table, **b_cfg)
    if sample:
        a_bias = _a_sample_bias(table, caches["a_kt"].shape[3], s)
    else:
        a_bias = _a_prompt_bias(table)
    ak, av, bk, bv = [], [], [], []
    kt_buf = v_buf = None
    for i in range(DEPTH):
        j = i // 2
        g_mix = wts["norm_mix_g"][i][None, :]
        if i % 2 == 0:
            lam_init = _lambda_init(i)
            if sample:
                q, k, v, kb, vb = _proj_a_sample(xm, g_mix, wts["a_w_qkv"], j, tm)
                ak.append(k.reshape(b, s, A_HEADS, 2, A_DK))
                av.append(v.reshape(b, s, A_HEADS, A_DV))
                o = _attn_a_sample(q.reshape(b, s, A_QK), caches["a_kt"], caches["a_v"],
                                   kb.reshape(b, s, A_QK), vb.reshape(b, s, A_V), a_bias,
                                   wts["a_lambda"], wts["a_subln_g"], j, lam_init=lam_init)
            else:
                q, kt_buf, v_buf, ktb, vb = _proj_a_prompt(
                    xm, g_mix, wts["a_w_qkv"], wts["a_w_kt"], j, kt_buf, v_buf,
                    b=b, s=s, tm=tm, n_a=n_a)
                o = _attn_a_prompt(q.reshape(b, s, A_QK), ktb, vb.reshape(b, s, A_V), a_bias,
                                   wts["a_lambda"], wts["a_subln_g"], j, lam_init=lam_init)
            wo, wo_layer = wts["a_w_o"], j
        else:
            q, k, v, kb, vb = _proj_b(xm, g_mix, wts["b_w_qkv"], j, tm)
            k = k.reshape(b, s, B_KV_HEADS, B_HD)
            v = v.reshape(b, s, B_KV_HEADS, B_HD)
            kb = kb.reshape(b, s, 2 * B_KV)
            vb = vb.reshape(b, s, 2 * B_KV)
            i_off = 0
            if sample:
                ck, cv, ckb, cvb = caches["b"][j]
                bk.append(jnp.concatenate([ck, k], axis=1)[:, s:])
                bv.append(jnp.concatenate([cv, v], axis=1)[:, s:])
                i_off = ckb.shape[1] // b_cfg["tq"]
                total = ckb.shape[1] - b_cfg["back"] + b_cfg["w"]
                kb = _pad_keys(jnp.concatenate([ckb, kb], axis=1), total)
                vb = _pad_keys(jnp.concatenate([cvb, vb], axis=1), total)
            else:
                bk.append(k[:, s - WINDOW:])
                bv.append(v[:, s - WINDOW:])
            o = _attn_b(q.reshape(b, s, B_Q), kb, vb, b_bias, wts["b_sinks"], j, i_off=i_off,
                        nsub=1 if sample else B_TILES_PER_STEP, **b_cfg)
            wo, wo_layer = wts["b_w_o"], j
        xm = _post(xm, o.reshape(b * s, D_MODEL), wo, wo_layer, wts["norm_mlp_g"],
                   wts["mlp_w_up"], wts["mlp_w_down"], i, wts["final_norm_g"][None, :],
                   tm=tm_post, tf=tf, final=(i == DEPTH - 1))
    if sample:
        a_k, a_v = jnp.stack(ak), jnp.stack(av)
    else:
        a_k = jnp.transpose(kt_buf.reshape(n_a, b, A_HEADS, 2, A_DK, s), (0, 1, 5, 2, 3, 4))
        a_v = v_buf.reshape(n_a, b, s, A_HEADS, A_DV)
    return xm.reshape(b, s, D_MODEL), a_k, a_v, jnp.stack(bk), jnp.stack(bv)


def _swap_halves(t):
    flat = t.reshape(t.shape[:-2] + (B_KV,))
    swapped = t[..., ::-1, :].reshape(t.shape[:-2] + (B_KV,))
    return jnp.concatenate([flat, swapped], axis=-1).astype(BF16)


def kernel(x_prompt, x_sample, cache_a_k, cache_a_v, cache_b_k, cache_b_v, rel_table,
           norm_mix_g, norm_mlp_g, final_norm_g, a_w_qkv, a_lambda, a_subln_g, a_w_o,
           b_w_qkv, b_sinks, b_w_o, mlp_w_up, mlp_w_down):
    wts = dict(norm_mix_g=norm_mix_g, norm_mlp_g=norm_mlp_g[:, None, :], final_norm_g=final_norm_g,
               a_w_qkv=a_w_qkv.astype(BF16),
               a_w_kt=jnp.transpose(a_w_qkv[:, :, A_QK:2 * A_QK], (0, 2, 1)).astype(BF16),
               a_lambda=a_lambda, a_subln_g=a_subln_g[:, None, :],
               a_w_o=a_w_o.astype(BF16), b_w_qkv=b_w_qkv.astype(BF16), b_sinks=b_sinks,
               b_w_o=b_w_o.astype(BF16), mlp_w_up=mlp_w_up.astype(BF16),
               mlp_w_down=mlp_w_down.astype(BF16))
    n_a, db, past = cache_a_k.shape[:3]
    caches = dict(
        a_kt=jnp.transpose(cache_a_k, (0, 1, 3, 4, 5, 2)).reshape(n_a, db, A_QK, past),
        a_v=cache_a_v.reshape(n_a, db, past * A_HEADS, A_DV),
        b=[(cache_b_k[j], cache_b_v[j], _swap_halves(cache_b_k[j]), _swap_halves(cache_b_v[j]))
           for j in range(cache_b_k.shape[0])])
    yp, akp, avp, bkp, bvp = _trunk(x_prompt, None, wts, rel_table, tm=512, tm_post=1024, tf=1024)
    ys, aks, avs, bks, bvs = _trunk(x_sample, caches, wts, rel_table, tm=512, tm_post=512, tf=1024)
    return (yp, ys, akp, avp, bkp, bvp, aks, avs, bks, bvs)
```

```python
import functools
import math

import jax
import jax.numpy as jnp
from jax import lax
from jax.experimental import pallas as pl
from jax.experimental.pallas import tpu as pltpu

BF16 = jnp.bfloat16
F32 = jnp.float32

D_MODEL = 1024
DEPTH = 4
CHUNK = 64
A_HEADS = 8
A_DK = 64
A_DV = 128
A_QK = A_HEADS * 2 * A_DK
A_V = A_HEADS * A_DV
B_Q_HEADS = 16
B_KV_HEADS = 2
B_HD = 64
B_Q = B_Q_HEADS * B_HD
B_KV = B_KV_HEADS * B_HD
WINDOW = 128
NUM_BUCKETS = 32
MAX_DIST = 128
N_MAPS = 16
D_FF = 4 * D_MODEL
EPS = 1e-6
NEG = -1e30

LANES = 128
SUBLANES = 8
HALF = LANES // 2
VMEM_LIMIT = 56 * 1024 * 1024

NT_DIMS = (((1,), (1,)), ((), ()))

LOG2E = math.log2(math.e)


def _lambda_init(layer):
    return 0.8 - 0.6 * math.exp(-0.3 * layer)


def _rms(x, g):
    return x * lax.rsqrt(jnp.mean(x * x, axis=-1, keepdims=True) + EPS) * g


def _params(n_axes):
    return pltpu.CompilerParams(
        dimension_semantics=("arbitrary",) * n_axes, vmem_limit_bytes=VMEM_LIMIT)


def _lane_fold_max(s):
    mx = s[:, 0:LANES]
    for gi in range(1, s.shape[1] // LANES):
        mx = jnp.maximum(mx, s[:, gi * LANES:(gi + 1) * LANES])
    return mx


def _row_max(s):
    return jnp.broadcast_to(jnp.max(_lane_fold_max(s), axis=-1, keepdims=True), (s.shape[0], LANES))


def _exp_rows(s, m):
    return jnp.concatenate([jnp.exp2(s[:, gi * LANES:(gi + 1) * LANES] - m)
                            for gi in range(s.shape[1] // LANES)], axis=1).astype(BF16)


def _with_ones(v):
    return jnp.concatenate([v, jnp.ones(v.shape, v.dtype)], axis=1)


def _proj_a_prompt_body(x_ref, g_ref, w_ref, wkt_ref, *refs, first):
    q_ref, kt_ref, v_ref, ktb_ref, vb_ref = refs[-5:]
    h = _rms(x_ref[...], g_ref[...]).astype(BF16)
    q = jnp.dot(h, w_ref[:, 0:A_QK], preferred_element_type=F32)
    q_ref[...] = (q * (A_DK ** -0.5 * LOG2E)).astype(BF16)
    kt = lax.dot_general(wkt_ref[...], h, NT_DIMS, preferred_element_type=F32)
    ktb_ref[...] = kt.astype(BF16)
    v = jnp.dot(h, w_ref[:, 2 * A_QK:], preferred_element_type=F32)
    vb_ref[...] = v.astype(BF16)
    tm = v.shape[0]
    if first:
        for slot in range(1, kt_ref.shape[0]):
            kt_ref[slot] = jnp.zeros(kt_ref.shape[1:], F32)
            v_ref[slot] = jnp.zeros(v_ref.shape[1:], F32)
        kt_ref, v_ref = kt_ref.at[0], v_ref.at[0]
    kt_ref[...] = kt
    for hh in range(A_HEADS):
        v_ref[pl.ds(hh, tm, stride=A_HEADS), :] = v[:, hh * A_DV:(hh + 1) * A_DV]


def _proj_a_prompt(x, g, w, wkt, layer, kt_prev, v_prev, *, b, s, tm, n_a):
    m = b * s
    per_b = s // tm
    row = lambda i: (i, 0)
    full = lambda i: (0, 0)
    stacked = lambda i: (layer, 0, 0)
    first = kt_prev is None
    assert first == (layer == 0)
    in_specs = [pl.BlockSpec((tm, D_MODEL), row),
                pl.BlockSpec((1, D_MODEL), full),
                pl.BlockSpec((None, D_MODEL, 2 * A_QK + A_V), stacked),
                pl.BlockSpec((None, A_QK, D_MODEL), stacked)]
    operands = [x, g, w, wkt]
    if first:
        kt_spec = pl.BlockSpec((n_a, None, A_QK, tm), lambda i: (0, i // per_b, 0, i % per_b))
        v_spec = pl.BlockSpec((n_a, tm * A_HEADS, A_DV), lambda i: (0, i, 0))
        aliases = {}
    else:
        kt_spec = pl.BlockSpec((None, None, A_QK, tm), lambda i: (layer, i // per_b, 0, i % per_b))
        v_spec = pl.BlockSpec((None, tm * A_HEADS, A_DV), lambda i: (layer, i, 0))
        in_specs += [pl.BlockSpec(memory_space=pl.ANY)] * 2
        operands += [kt_prev, v_prev]
        aliases = {4: 1, 5: 2}
    return pl.pallas_call(
        functools.partial(_proj_a_prompt_body, first=first),
        grid=(m // tm,),
        in_specs=in_specs,
        out_specs=[pl.BlockSpec((tm, A_QK), row),
                   kt_spec,
                   v_spec,
                   pl.BlockSpec((None, A_QK, tm), lambda i: (i // per_b, 0, i % per_b)),
                   pl.BlockSpec((tm, A_V), row)],
        out_shape=[jax.ShapeDtypeStruct((m, A_QK), BF16),
                   jax.ShapeDtypeStruct((n_a, b, A_QK, s), F32),
                   jax.ShapeDtypeStruct((n_a, m * A_HEADS, A_DV), F32),
                   jax.ShapeDtypeStruct((b, A_QK, s), BF16),
                   jax.ShapeDtypeStruct((m, A_V), BF16)],
        input_output_aliases=aliases,
        compiler_params=_params(1),
        name="proj_a_prompt",
    )(*operands)


def _proj_a_sample_body(x_ref, g_ref, w_ref, q_ref, k_ref, v_ref, kb_ref, vb_ref):
    h = _rms(x_ref[...], g_ref[...]).astype(BF16)
    q = jnp.dot(h, w_ref[:, 0:A_QK], preferred_element_type=F32)
    q_ref[...] = (q * (A_DK ** -0.5 * LOG2E)).astype(BF16)
    k = jnp.dot(h, w_ref[:, A_QK:2 * A_QK], preferred_element_type=F32)
    k_ref[...] = k
    kb_ref[...] = k.astype(BF16)
    v = jnp.dot(h, w_ref[:, 2 * A_QK:], preferred_element_type=F32)
    v_ref[...] = v
    vb_ref[...] = v.astype(BF16)


def _proj_a_sample(x, g, w, layer, tm):
    m = x.shape[0]
    row = lambda i: (i, 0)
    full = lambda i: (0, 0)
    return pl.pallas_call(
        _proj_a_sample_body,
        grid=(m // tm,),
        in_specs=[pl.BlockSpec((tm, D_MODEL), row),
                  pl.BlockSpec((1, D_MODEL), full),
                  pl.BlockSpec((None, D_MODEL, 2 * A_QK + A_V), lambda i: (layer, 0, 0))],
        out_specs=[pl.BlockSpec((tm, A_QK), row),
                   pl.BlockSpec((tm, A_QK), row),
                   pl.BlockSpec((tm, A_V), row),
                   pl.BlockSpec((tm, A_QK), row),
                   pl.BlockSpec((tm, A_V), row)],
        out_shape=[jax.ShapeDtypeStruct((m, A_QK), BF16),
                   jax.ShapeDtypeStruct((m, A_QK), F32),
                   jax.ShapeDtypeStruct((m, A_V), F32),
                   jax.ShapeDtypeStruct((m, A_QK), BF16),
                   jax.ShapeDtypeStruct((m, A_V), BF16)],
        compiler_params=_params(1),
        name="proj_a_sample",
    )(x, g, w)


def _proj_b_body(x_ref, g_ref, w_ref, q_ref, k_ref, v_ref, kb_ref, vb_ref):
    h = _rms(x_ref[...], g_ref[...]).astype(BF16)
    q = jnp.dot(h, w_ref[:, 0:B_Q], preferred_element_type=F32)
    q_ref[...] = (q * (B_HD ** -0.5 * LOG2E)).astype(BF16)
    kv = jnp.dot(h, w_ref[:, B_Q:], preferred_element_type=F32)
    k, v = kv[:, 0:B_KV], kv[:, B_KV:]
    k_ref[...] = k
    kb_ref[...] = jnp.concatenate([k, pltpu.roll(k, HALF, axis=1)], axis=1).astype(BF16)
    v_ref[...] = v
    vb_ref[...] = jnp.concatenate([v, pltpu.roll(v, HALF, axis=1)], axis=1).astype(BF16)


def _proj_b(x, g, w, layer, tm):
    m = x.shape[0]
    row = lambda i: (i, 0)
    full = lambda i: (0, 0)
    return pl.pallas_call(
        _proj_b_body,
        grid=(m // tm,),
        in_specs=[pl.BlockSpec((tm, D_MODEL), row),
                  pl.BlockSpec((1, D_MODEL), full),
                  pl.BlockSpec((None, D_MODEL, B_Q + 2 * B_KV), lambda i: (layer, 0, 0))],
        out_specs=[pl.BlockSpec((tm, B_Q), row),
                   pl.BlockSpec((tm, B_KV), row),
                   pl.BlockSpec((tm, B_KV), row),
                   pl.BlockSpec((tm, 2 * B_KV), row),
                   pl.BlockSpec((tm, 2 * B_KV), row)],
        out_shape=[jax.ShapeDtypeStruct((m, B_Q), BF16),
                   jax.ShapeDtypeStruct((m, B_KV), F32),
                   jax.ShapeDtypeStruct((m, B_KV), F32),
                   jax.ShapeDtypeStruct((m, 2 * B_KV), BF16),
                   jax.ShapeDtypeStruct((m, 2 * B_KV), BF16)],
        compiler_params=_params(1),
        name="proj_b",
    )(x, g, w)


def _t5_bucket(rel):
    nb = NUM_BUCKETS // 2
    n = -rel
    ret = jnp.where(n < 0, nb, 0)
    n = jnp.abs(n)
    max_exact = nb // 2
    nf = jnp.maximum(n, 1).astype(jnp.float32)
    large = max_exact + (jnp.log(nf / max_exact) / math.log(MAX_DIST / max_exact)
                         * (nb - max_exact)).astype(jnp.int32)
    large = jnp.minimum(large, nb - 1)
    return ret + jnp.where(n < max_exact, n, large)


def _check_far_bucket(min_dist):
    nb, max_exact = NUM_BUCKETS // 2, NUM_BUCKETS // 4
    ratio = math.log(min_dist / max_exact) / math.log(MAX_DIST / max_exact)
    assert ratio * (nb - max_exact) >= nb - 1 - max_exact + 1e-3, "far keys need one shared bucket"


def _far_bias(table, tq, w):
    far = table[NUM_BUCKETS // 2 - 1].astype(F32) * LOG2E
    return jnp.broadcast_to(far[:, None, None], (N_MAPS, tq, w))


def _near_bias(table, tq, w, back, band):
    r = jnp.arange(tq)[:, None]
    j = jnp.arange(w)[None, :] - back
    onehot = (_t5_bucket(j - r)[None] == jnp.arange(NUM_BUCKETS)[:, None, None]).astype(F32)
    onehot = lax.optimization_barrier(onehot)
    bias = jnp.einsum("bm,brj->mrj", table.astype(F32), onehot, precision=lax.Precision.HIGHEST)
    kc, qc = j // CHUNK, r // CHUNK
    mask = kc <= qc
    if band is not None:
        mask = mask & (kc >= qc - band)
    return jnp.where(mask[None], bias * LOG2E, NEG)


A_TILE = 256
A_ROWS = 64
A_HEADS_PER_STEP = 1


def _diff_lambda(lamp_ref, lam_init):
    lp = lamp_ref[...]
    return (jnp.exp(jnp.sum(lp[0:1] * lp[1:2], axis=-1, keepdims=True))
            - jnp.exp(jnp.sum(lp[2:3] * lp[3:4], axis=-1, keepdims=True)) + lam_init)


def _stack_maps(q):
    lane = lax.broadcasted_iota(jnp.int32, q.shape, 1)
    zero = jnp.zeros_like(q)
    return jnp.concatenate([jnp.where(lane < HALF, q, zero), jnp.where(lane >= HALF, q, zero)], axis=0)


def _diff_epilogue(acc, lam, g, lam_init):
    tq = acc.shape[0] // 2
    o_all = acc[:, :A_DV] / acc[:, A_DV:]
    o = o_all[:tq] - lam * o_all[tq:]
    return (_rms(o, g) * (1.0 - lam_init)).astype(BF16)


def _attn_a_scores(i, q, kt_ref, bias_ref, s_ref, m_ref):
    tl = A_TILE
    qs = _stack_maps(q)
    n_far = max(i - 1, 0)
    for kt in range(i + 1):
        cols = slice(kt * tl, (kt + 1) * tl)
        s = jnp.dot(qs, kt_ref[:, cols], preferred_element_type=F32)
        near = kt >= n_far
        if near:
            s = s + bias_ref[2 - (i - kt)].reshape(2 * tl, tl)
        s_ref[:, cols] = s
        fold = _lane_fold_max(s)
        first = kt == (n_far if near else 0)
        m_ref[int(near)] = fold if first else jnp.maximum(m_ref[int(near)], fold)
        yield
    for r in range(0, 2 * tl, A_ROWS):
        rows = slice(r, r + A_ROWS)
        mx = m_ref[1, rows, :]
        if n_far:
            c = bias_ref[0, r // tl, r % tl:r % tl + A_ROWS, 0:LANES]
            mx = jnp.maximum(mx, m_ref[0, rows, :] + c)
        m = jnp.broadcast_to(jnp.max(mx, axis=-1, keepdims=True), (A_ROWS, LANES))
        m_ref[1, rows, :] = m
        if n_far:
            m_ref[0, rows, :] = m - c


def _attn_a_numerators(i, s_ref, m_ref, v1_ref, out):
    tl = A_TILE
    n_far = max(i - 1, 0)
    acc = None
    for kt in range(i + 1):
        cols = slice(kt * tl, (kt + 1) * tl)
        p = _exp_rows(s_ref[:, cols], m_ref[1 if kt >= n_far else 0])
        part = jnp.dot(p, v1_ref[cols, :], preferred_element_type=F32)
        acc = part if acc is None else acc + part
        yield
    out.append(acc)


def _interleave(*steps):
    steps = list(steps)
    while steps:
        for gen in list(steps):
            if next(gen, StopIteration) is StopIteration:
                steps.remove(gen)


def _attn_a_prompt_body(lamp_ref, q_ref, kt_ref, v_ref, bias_ref, g_ref, o_ref,
                        s_ref, m_ref, v1_ref, *, nt, nh, lam_init):
    tl = A_TILE
    lam = _diff_lambda(lamp_ref, lam_init)
    for hh in range(nh):
        v1_ref[hh, :, 0:A_DV] = v_ref[0, :, hh * A_DV:(hh + 1) * A_DV]
        v1_ref[hh, :, A_DV:] = jnp.ones((nt * tl, A_DV), BF16)
    work = [(i, hh) for i in reversed(range(nt)) for hh in range(nh)]

    def scores(i, hh):
        lanes = slice(hh * LANES, (hh + 1) * LANES)
        return _attn_a_scores(i, q_ref[0, i * tl:(i + 1) * tl, lanes], kt_ref.at[0, lanes, :],
                              bias_ref.at[:, 2 * hh:2 * hh + 2], s_ref.at[hh, i % 2],
                              m_ref.at[hh, i % 2])

    _interleave(scores(*work[0]))
    for n, (i, hh) in enumerate(work):
        out = []
        steps = [_attn_a_numerators(i, s_ref.at[hh, i % 2], m_ref.at[hh, i % 2], v1_ref.at[hh], out)]
        if n + 1 < len(work):
            steps.insert(0, scores(*work[n + 1]))
        _interleave(*steps)
        acc = out[0]
        o_ref[0, i * tl:(i + 1) * tl, hh * LANES:(hh + 1) * LANES] = _diff_epilogue(
            acc, lam, g_ref[...], lam_init)


def _attn_a_prompt(q, kt, v, bias, lamp, g, layer, *, lam_init):
    b, s, _ = q.shape
    nt = s // A_TILE
    nh = A_HEADS_PER_STEP
    assert s % A_TILE == 0 and A_HEADS % nh == 0
    seq = pl.BlockSpec((1, s, nh * LANES), lambda h, bi: (bi, 0, h))
    return pl.pallas_call(
        functools.partial(_attn_a_prompt_body, nt=nt, nh=nh, lam_init=lam_init),
        grid=(A_HEADS // nh, b),
        in_specs=[pl.BlockSpec((None, 4, A_DK), lambda h, bi: (layer, 0, 0)),
                  seq,
                  pl.BlockSpec((1, nh * LANES, s), lambda h, bi: (bi, h, 0)),
                  seq,
                  pl.BlockSpec((3, 2 * nh, A_TILE, A_TILE), lambda h, bi: (0, h, 0, 0)),
                  pl.BlockSpec((None, 1, A_DV), lambda h, bi: (layer, 0, 0))],
        out_specs=seq,
        out_shape=jax.ShapeDtypeStruct((b, s, A_V), BF16),
        scratch_shapes=[pltpu.VMEM((nh, 2, 2 * A_TILE, s), F32),
                        pltpu.VMEM((nh, 2, 2, 2 * A_TILE, LANES), F32),
                        pltpu.VMEM((nh, s, 2 * A_DV), BF16)],
        compiler_params=_params(2),
        name="attn_a_prompt",
    )(lamp, q, kt, v, bias, g)


def _attn_a_sample_body(lamp_ref, q_ref, ktc_ref, vc_ref, kn_ref, vn_ref, bias_ref, g_ref, o_ref,
                        *, lam_init):
    tq = q_ref.shape[1]
    past = ktc_ref.shape[1]
    lam = _diff_lambda(lamp_ref, lam_init)
    for h in range(A_HEADS):
        cols = slice(h * LANES, (h + 1) * LANES)
        qs = _stack_maps(q_ref[0, :, cols])
        bias = bias_ref[2 * h:2 * h + 2].reshape(2 * tq, past + LANES)
        s_c = jnp.dot(qs, ktc_ref[cols, :].astype(BF16), preferred_element_type=F32)
        s_c = s_c + bias[:, 0:past]
        s_n = lax.dot_general(qs, kn_ref[0, :, cols], NT_DIMS, preferred_element_type=F32)
        s_n = s_n + bias[:, past:past + tq]
        m = jnp.maximum(_row_max(s_c), jnp.broadcast_to(jnp.max(s_n, axis=-1, keepdims=True),
                                                        (2 * tq, LANES)))
        p_c = _exp_rows(s_c, m)
        p_n = jnp.exp2(s_n - m[:, 0:tq]).astype(BF16)
        v_c = vc_ref[pl.ds(h, past, stride=A_HEADS), :].astype(BF16)
        acc = (jnp.dot(p_c, _with_ones(v_c), preferred_element_type=F32)
               + jnp.dot(p_n, _with_ones(vn_ref[0, :, cols]), preferred_element_type=F32))
        o_ref[0, :, cols] = _diff_epilogue(acc, lam, g_ref[...], lam_init)


def _attn_a_sample(q, ktc, vc, kn, vn, bias, lamp, g, layer, *, lam_init):
    b, tq, _ = q.shape
    past = ktc.shape[3]
    new = lambda bi: (bi, 0, 0)
    return pl.pallas_call(
        functools.partial(_attn_a_sample_body, lam_init=lam_init),
        grid=(b,),
        in_specs=[pl.BlockSpec((None, 4, A_DK), lambda bi: (layer, 0, 0)),
                  pl.BlockSpec((1, tq, A_QK), new),
                  pl.BlockSpec((None, None, A_QK, past), lambda bi: (layer, bi, 0, 0)),
                  pl.BlockSpec((None, None, past * A_HEADS, A_DV), lambda bi: (layer, bi, 0, 0)),
                  pl.BlockSpec((1, tq, A_QK), new),
                  pl.BlockSpec((1, tq, A_V), new),
                  pl.BlockSpec((N_MAPS, tq, past + LANES), lambda bi: (0, 0, 0)),
                  pl.BlockSpec((None, 1, A_DV), lambda bi: (layer, 0, 0))],
        out_specs=pl.BlockSpec((1, tq, A_V), new),
        out_shape=jax.ShapeDtypeStruct((b, tq, A_V), BF16),
        compiler_params=_params(1),
        name="attn_a_sample",
    )(lamp, q, ktc, vc, kn, vn, bias, g)


_B_ORDER_A = (0, 2, 4, 6, 9, 11, 13, 15)
_B_ORDER_B = (1, 3, 5, 7, 8, 10, 12, 14)


def _attn_b_body(sink_ref, q_ref, k_ref, v_ref, bias_ref, o_ref, *, layer, tq, w, back, i_off, nsub):
    lane = lax.broadcasted_iota(jnp.int32, (tq, LANES), 1)
    lo = lane < HALF
    half = B_Q_HEADS // 4
    orders = (_B_ORDER_A, _B_ORDER_B)
    sinks = [jnp.concatenate([jnp.full((tq, LANES), sink_ref[layer, h] * LOG2E, F32)
                              for h in orders[t]], axis=0) for t in range(2)]
    outs = {}

    def chain(u, t):
        tile = pl.program_id(1) * nsub + u + i_off
        variant = 1 if (u > 0 or i_off > 0) else jnp.minimum(tile, 1)
        near_start = pl.multiple_of(jnp.maximum(tile * tq - back, 0), CHUNK)
        group = slice(t * LANES, (t + 1) * LANES)
        pieces = []
        for c in range(B_Q_HEADS // 2):
            qc = q_ref[0, u * tq:(u + 1) * tq, c * LANES:(c + 1) * LANES]
            keep_lo = (c < half) == (t == 0)
            pieces.append(jnp.where(lo == keep_lo, qc, jnp.zeros_like(qc)))
        qs = jnp.concatenate(pieces, axis=0)
        s = lax.dot_general(qs, k_ref[0, pl.ds(near_start, w), group], NT_DIMS,
                            preferred_element_type=F32)
        s = s + bias_ref[variant, t].reshape((B_Q_HEADS // 2) * tq, w)
        m = jnp.maximum(_row_max(s), sinks[t])
        p = _exp_rows(s, m)
        acc = jnp.dot(p, _with_ones(v_ref[0, pl.ds(near_start, w), group]),
                      preferred_element_type=F32)
        outs[u, t] = acc[:, :LANES] / (acc[:, LANES:] + jnp.exp2(sinks[t] - m))

    for u in range(nsub):
        for t in range(2):
            chain(u, t)

    for u in range(nsub):
        for c in range(B_Q_HEADS // 2):
            oa = outs[u, 0][c * tq:(c + 1) * tq]
            ob = outs[u, 1][c * tq:(c + 1) * tq]
            oc = jnp.where(lo, oa, ob) if c < half else jnp.where(lo, ob, oa)
            o_ref[0, u * tq:(u + 1) * tq, c * LANES:(c + 1) * LANES] = oc.astype(BF16)


def _attn_b(q, k, v, bias, sinks, layer, *, tq, w, back, i_off, nsub):
    b, sq, _ = q.shape
    sk = k.shape[1]
    rows = nsub * tq
    assert sq % rows == 0 and (sq // tq - 1 + i_off) * tq - back + w <= sk
    body = functools.partial(_attn_b_body, layer=layer, tq=tq, w=w, back=back, i_off=i_off,
                             nsub=nsub)
    return pl.pallas_call(
        body,
        grid=(b, sq // rows),
        in_specs=[pl.BlockSpec(memory_space=pltpu.SMEM),
                  pl.BlockSpec((1, rows, B_Q), lambda bi, i: (bi, i, 0)),
                  pl.BlockSpec((1, sk, 2 * B_KV), lambda bi, i: (bi, 0, 0)),
                  pl.BlockSpec((1, sk, 2 * B_KV), lambda bi, i: (bi, 0, 0)),
                  pl.BlockSpec((2, 2, B_Q_HEADS // 2, tq, w), lambda bi, i: (0, 0, 0, 0, 0))],
        out_specs=pl.BlockSpec((1, rows, B_Q), lambda bi, i: (bi, i, 0)),
        out_shape=jax.ShapeDtypeStruct((b, sq, B_Q), BF16),
        compiler_params=_params(2),
        name="attn_b",
    )(sinks, q, k, v, bias)


def _post_body(x_ref, o_ref, wo_ref, g_ref, wu_ref, wd_ref, gf_ref, y_ref, hn_ref, *, final):
    f = pl.program_id(1)

    @pl.when(f == 0)
    def _():
        x1 = x_ref[...] + jnp.dot(o_ref[...], wo_ref[...], preferred_element_type=F32)
        y_ref[...] = x1
        hn_ref[...] = _rms(x1, g_ref[...]).astype(BF16)

    u = jnp.dot(hn_ref[...], wu_ref[...], preferred_element_type=F32)
    a = jnp.square(jnp.maximum(u, 0.0)).astype(BF16)
    y_ref[...] += jnp.dot(a, wd_ref[...], preferred_element_type=F32)

    if final:
        @pl.when(f == pl.num_programs(1) - 1)
        def _():
            y_ref[...] = _rms(y_ref[...], gf_ref[...])


def _post(x, o, wo, wo_layer, g, wu, wd, layer, gf, *, tm, tf, final):
    m = x.shape[0]
    row = lambda i, f: (i, 0)
    full = lambda i, f: (0, 0)
    return pl.pallas_call(
        functools.partial(_post_body, final=final),
        grid=(m // tm, D_FF // tf),
        in_specs=[pl.BlockSpec((tm, D_MODEL), row),
                  pl.BlockSpec((tm, D_MODEL), row),
                  pl.BlockSpec((None, D_MODEL, D_MODEL), lambda i, f: (wo_layer, 0, 0)),
                  pl.BlockSpec((None, 1, D_MODEL), lambda i, f: (layer, 0, 0)),
                  pl.BlockSpec((None, D_MODEL, tf), lambda i, f: (layer, 0, f)),
                  pl.BlockSpec((None, tf, D_MODEL), lambda i, f: (layer, f, 0)),
                  pl.BlockSpec((1, D_MODEL), full)],
        out_specs=pl.BlockSpec((tm, D_MODEL), row),
        out_shape=jax.ShapeDtypeStruct((m, D_MODEL), F32),
        scratch_shapes=[pltpu.VMEM((tm, D_MODEL), BF16)],
        compiler_params=_params(2),
        name="post",
    )(x, o, wo, g, wu, wd, gf)


_B_PROMPT = dict(tq=128, w=256, back=128)
_B_SAMPLE = dict(tq=64, w=256, back=128)
B_TILES_PER_STEP = 8
_A_SAMPLE_BACK = 128


def _a_prompt_bias(table):
    _check_far_bucket(A_TILE + 1)
    near = _near_bias(table, A_TILE, 2 * A_TILE, A_TILE, None)
    return jnp.stack([_far_bias(table, A_TILE, A_TILE), near[:, :, :A_TILE], near[:, :, A_TILE:]])


def _a_sample_bias(table, past, t):
    assert t <= LANES
    _check_far_bucket(_A_SAMPLE_BACK + 1)
    far = _far_bias(table, t, past - _A_SAMPLE_BACK)
    near = _near_bias(table, t, _A_SAMPLE_BACK + LANES, _A_SAMPLE_BACK, None)
    return jnp.concatenate([far, near], axis=2)


def _b_bias(table, tq, w, back):
    def one(bk):
        t = _near_bias(table, tq, w, bk, WINDOW // CHUNK)
        return jnp.stack([jnp.stack([t[h] for h in order]) for order in (_B_ORDER_A, _B_ORDER_B)])
    return jnp.stack([one(0), one(back)])


def _pad_keys(t, total):
    return jnp.pad(t, ((0, 0), (0, total - t.shape[1]), (0, 0)))


def _trunk(x, caches, wts, table, *, tm, tm_post, tf):
    b, s, _ = x.shape
    sample = caches is not None
    n_a = wts["a_w_qkv"].shape[0]
    xm = x.reshape(b * s, D_MODEL)
    b_cfg = _B_SAMPLE if sample else _B_PROMPT
    b_bias = _b_bias(table, **b_cfg)
    if sample:
        a_bias = _a_sample_bias(table, caches["a_kt"].shape[3], s)
    else:
        a_bias = _a_prompt_bias(table)
    ak, av, bk, bv = [], [], [], []
    kt_buf = v_buf = None
    for i in range(DEPTH):
        j = i // 2
        g_mix = wts["norm_mix_g"][i][None, :]
        if i % 2 == 0:
            lam_init = _lambda_init(i)
            if sample:
                q, k, v, kb, vb = _proj_a_sample(xm, g_mix, wts["a_w_qkv"], j, tm)
                ak.append(k.reshape(b, s, A_HEADS, 2, A_DK))
                av.append(v.reshape(b, s, A_HEADS, A_DV))
                o = _attn_a_sample(q.reshape(b, s, A_QK), caches["a_kt"], caches["a_v"],
                                   kb.reshape(b, s, A_QK), vb.reshape(b, s, A_V), a_bias,
                                   wts["a_lambda"], wts["a_subln_g"], j, lam_init=lam_init)
            else:
                q, kt_buf, v_buf, ktb, vb = _proj_a_prompt(
                    xm, g_mix, wts["a_w_qkv"], wts["a_w_kt"], j, kt_buf, v_buf,
                    b=b, s=s, tm=tm, n_a=n_a)
                o = _attn_a_prompt(q.reshape(b, s, A_QK), ktb, vb.reshape(b, s, A_V), a_bias,
                                   wts["a_lambda"], wts["a_subln_g"], j, lam_init=lam_init)
            wo, wo_layer = wts["a_w_o"], j
        else:
            q, k, v, kb, vb = _proj_b(xm, g_mix, wts["b_w_qkv"], j, tm)
            k = k.reshape(b, s, B_KV_HEADS, B_HD)
            v = v.reshape(b, s, B_KV_HEADS, B_HD)
            kb = kb.reshape(b, s, 2 * B_KV)
            vb = vb.reshape(b, s, 2 * B_KV)
            i_off = 0
            if sample:
                ck, cv, ckb, cvb = caches["b"][j]
                bk.append(jnp.concatenate([ck, k], axis=1)[:, s:])
                bv.append(jnp.concatenate([cv, v], axis=1)[:, s:])
                i_off = ckb.shape[1] // b_cfg["tq"]
                total = ckb.shape[1] - b_cfg["back"] + b_cfg["w"]
                kb = _pad_keys(jnp.concatenate([ckb, kb], axis=1), total)
                vb = _pad_keys(jnp.concatenate([cvb, vb], axis=1), total)
            else:
                bk.append(k[:, s - WINDOW:])
                bv.append(v[:, s - WINDOW:])
            o = _attn_b(q.reshape(b, s, B_Q), kb, vb, b_bias, wts["b_sinks"], j, i_off=i_off,
                        nsub=1 if sample else B_TILES_PER_STEP, **b_cfg)
            wo, wo_layer = wts["b_w_o"], j
        xm = _post(xm, o.reshape(b * s, D_MODEL), wo, wo_layer, wts["norm_mlp_g"],
                   wts["mlp_w_up"], wts["mlp_w_down"], i, wts["final_norm_g"][None, :],
                   tm=tm_post, tf=tf, final=(i == DEPTH - 1))
    if sample:
        a_k, a_v = jnp.stack(ak), jnp.stack(av)
    else:
        a_k = jnp.transpose(kt_buf.reshape(n_a, b, A_HEADS, 2, A_DK, s), (0, 1, 5, 2, 3, 4))
        a_v = v_buf.reshape(n_a, b, s, A_HEADS, A_DV)
    return xm.reshape(b, s, D_MODEL), a_k, a_v, jnp.stack(bk), jnp.stack(bv)


def _swap_halves(t):
    flat = t.reshape(t.shape[:-2] + (B_KV,))
    swapped = t[..., ::-1, :].reshape(t.shape[:-2] + (B_KV,))
    return jnp.concatenate([flat, swapped], axis=-1).astype(BF16)


def kernel(x_prompt, x_sample, cache_a_k, cache_a_v, cache_b_k, cache_b_v, rel_table,
           norm_mix_g, norm_mlp_g, final_norm_g, a_w_qkv, a_lambda, a_subln_g, a_w_o,
           b_w_qkv, b_sinks, b_w_o, mlp_w_up, mlp_w_down):
    wts = dict(norm_mix_g=norm_mix_g, norm_mlp_g=norm_mlp_g[:, None, :], final_norm_g=final_norm_g,
               a_w_qkv=a_w_qkv.astype(BF16),
               a_w_kt=jnp.transpose(a_w_qkv[:, :, A_QK:2 * A_QK], (0, 2, 1)).astype(BF16),
               a_lambda=a_lambda, a_subln_g=a_subln_g[:, None, :],
               a_w_o=a_w_o.astype(BF16), b_w_qkv=b_w_qkv.astype(BF16), b_sinks=b_sinks,
               b_w_o=b_w_o.astype(BF16), mlp_w_up=mlp_w_up.astype(BF16),
               mlp_w_down=mlp_w_down.astype(BF16))
    n_a, db, past = cache_a_k.shape[:3]
    caches = dict(
        a_kt=jnp.transpose(cache_a_k, (0, 1, 3, 4, 5, 2)).reshape(n_a, db, A_QK, past),
        a_v=cache_a_v.reshape(n_a, db, past * A_HEADS, A_DV),
        b=[(cache_b_k[j], cache_b_v[j], _swap_halves(cache_b_k[j]), _swap_halves(cache_b_v[j]))
           for j in range(cache_b_k.shape[0])])
    yp, akp, avp, bkp, bvp = _trunk(x_prompt, None, wts, rel_table, tm=512, tm_post=1024, tf=1024)
    ys, aks, avs, bks, bvs = _trunk(x_sample, caches, wts, rel_table, tm=512, tm_post=512, tf=1024)
    return (yp, ys, akp, avp, bkp, bvp, aks, avs, bks, bvs)
```

```python
import functools
import math

import jax
import jax.numpy as jnp
from jax import lax
from jax.experimental import pallas as pl
from jax.experimental.pallas import tpu as pltpu

BF16 = jnp.bfloat16
F32 = jnp.float32

D_MODEL = 1024
DEPTH = 4
CHUNK = 64
A_HEADS = 8
A_DK = 64
A_DV = 128
A_QK = A_HEADS * 2 * A_DK
A_V = A_HEADS * A_DV
B_Q_HEADS = 16
B_KV_HEADS = 2
B_HD = 64
B_Q = B_Q_HEADS * B_HD
B_KV = B_KV_HEADS * B_HD
WINDOW = 128
NUM_BUCKETS = 32
MAX_DIST = 128
N_MAPS = 16
D_FF = 4 * D_MODEL
EPS = 1e-6
NEG = -1e30

LANES = 128
SUBLANES = 8
HALF = LANES // 2
VMEM_LIMIT = 56 * 1024 * 1024

NT_DIMS = (((1,), (1,)), ((), ()))

LOG2E = math.log2(math.e)


def _lambda_init(layer):
    return 0.8 - 0.6 * math.exp(-0.3 * layer)


def _rms(x, g):
    return x * lax.rsqrt(jnp.mean(x * x, axis=-1, keepdims=True) + EPS) * g


def _params(n_axes):
    return pltpu.CompilerParams(
        dimension_semantics=("arbitrary",) * n_axes, vmem_limit_bytes=VMEM_LIMIT)


def _lane_fold_max(s):
    mx = s[:, 0:LANES]
    for gi in range(1, s.shape[1] // LANES):
        mx = jnp.maximum(mx, s[:, gi * LANES:(gi + 1) * LANES])
    return mx


def _row_max(s):
    return jnp.broadcast_to(jnp.max(_lane_fold_max(s), axis=-1, keepdims=True), (s.shape[0], LANES))


def _exp_rows(s, m):
    return jnp.concatenate([jnp.exp2(s[:, gi * LANES:(gi + 1) * LANES] - m)
                            for gi in range(s.shape[1] // LANES)], axis=1).astype(BF16)


def _with_ones(v):
    return jnp.concatenate([v, jnp.ones(v.shape, v.dtype)], axis=1)


def _proj_a_prompt_body(x_ref, g_ref, w_ref, wkt_ref, *refs, first):
    q_ref, kt_ref, v_ref, ktb_ref, vb_ref = refs[-5:]
    h = _rms(x_ref[...], g_ref[...]).astype(BF16)
    tm = h.shape[0]
    if first:
        for slot in range(1, kt_ref.shape[0]):
            kt_ref[slot] = jnp.zeros(kt_ref.shape[1:], F32)
            v_ref[slot] = jnp.zeros(v_ref.shape[1:], F32)
        kt_ref, v_ref = kt_ref.at[0], v_ref.at[0]
    v = jnp.dot(h, w_ref[:, 2 * A_QK:], preferred_element_type=F32)
    vb_ref[...] = v.astype(BF16)
    for hh in range(A_HEADS):
        v_ref[pl.ds(hh, tm, stride=A_HEADS), :] = v[:, hh * A_DV:(hh + 1) * A_DV]
    kt = lax.dot_general(wkt_ref[...], h, NT_DIMS, preferred_element_type=F32)
    ktb_ref[...] = kt.astype(BF16)
    kt_ref[...] = kt
    q = jnp.dot(h, w_ref[:, 0:A_QK], preferred_element_type=F32)
    q_ref[...] = (q * (A_DK ** -0.5 * LOG2E)).astype(BF16)


def _proj_a_prompt(x, g, w, wkt, layer, kt_prev, v_prev, *, b, s, tm, n_a):
    m = b * s
    per_b = s // tm
    row = lambda i: (i, 0)
    full = lambda i: (0, 0)
    stacked = lambda i: (layer, 0, 0)
    first = kt_prev is None
    assert first == (layer == 0)
    in_specs = [pl.BlockSpec((tm, D_MODEL), row),
                pl.BlockSpec((1, D_MODEL), full),
                pl.BlockSpec((None, D_MODEL, 2 * A_QK + A_V), stacked),
                pl.BlockSpec((None, A_QK, D_MODEL), stacked)]
    operands = [x, g, w, wkt]
    if first:
        kt_spec = pl.BlockSpec((n_a, None, A_QK, tm), lambda i: (0, i // per_b, 0, i % per_b))
        v_spec = pl.BlockSpec((n_a, tm * A_HEADS, A_DV), lambda i: (0, i, 0))
        aliases = {}
    else:
        kt_spec = pl.BlockSpec((None, None, A_QK, tm), lambda i: (layer, i // per_b, 0, i % per_b))
        v_spec = pl.BlockSpec((None, tm * A_HEADS, A_DV), lambda i: (layer, i, 0))
        in_specs += [pl.BlockSpec(memory_space=pl.ANY)] * 2
        operands += [kt_prev, v_prev]
        aliases = {4: 1, 5: 2}
    return pl.pallas_call(
        functools.partial(_proj_a_prompt_body, first=first),
        grid=(m // tm,),
        in_specs=in_specs,
        out_specs=[pl.BlockSpec((tm, A_QK), row),
                   kt_spec,
                   v_spec,
                   pl.BlockSpec((None, A_QK, tm), lambda i: (i // per_b, 0, i % per_b)),
                   pl.BlockSpec((tm, A_V), row)],
        out_shape=[jax.ShapeDtypeStruct((m, A_QK), BF16),
                   jax.ShapeDtypeStruct((n_a, b, A_QK, s), F32),
                   jax.ShapeDtypeStruct((n_a, m * A_HEADS, A_DV), F32),
                   jax.ShapeDtypeStruct((b, A_QK, s), BF16),
                   jax.ShapeDtypeStruct((m, A_V), BF16)],
        input_output_aliases=aliases,
        compiler_params=_params(1),
        name="proj_a_prompt",
    )(*operands)


def _proj_a_sample_body(x_ref, g_ref, w_ref, q_ref, k_ref, v_ref, kb_ref, vb_ref):
    h = _rms(x_ref[...], g_ref[...]).astype(BF16)
    q = jnp.dot(h, w_ref[:, 0:A_QK], preferred_element_type=F32)
    q_ref[...] = (q * (A_DK ** -0.5 * LOG2E)).astype(BF16)
    k = jnp.dot(h, w_ref[:, A_QK:2 * A_QK], preferred_element_type=F32)
    k_ref[...] = k
    kb_ref[...] = k.astype(BF16)
    v = jnp.dot(h, w_ref[:, 2 * A_QK:], preferred_element_type=F32)
    v_ref[...] = v
    vb_ref[...] = v.astype(BF16)


def _proj_a_sample(x, g, w, layer, tm):
    m = x.shape[0]
    row = lambda i: (i, 0)
    full = lambda i: (0, 0)
    return pl.pallas_call(
        _proj_a_sample_body,
        grid=(m // tm,),
        in_specs=[pl.BlockSpec((tm, D_MODEL), row),
                  pl.BlockSpec((1, D_MODEL), full),
                  pl.BlockSpec((None, D_MODEL, 2 * A_QK + A_V), lambda i: (layer, 0, 0))],
        out_specs=[pl.BlockSpec((tm, A_QK), row),
                   pl.BlockSpec((tm, A_QK), row),
                   pl.BlockSpec((tm, A_V), row),
                   pl.BlockSpec((tm, A_QK), row),
                   pl.BlockSpec((tm, A_V), row)],
        out_shape=[jax.ShapeDtypeStruct((m, A_QK), BF16),
                   jax.ShapeDtypeStruct((m, A_QK), F32),
                   jax.ShapeDtypeStruct((m, A_V), F32),
                   jax.ShapeDtypeStruct((m, A_QK), BF16),
                   jax.ShapeDtypeStruct((m, A_V), BF16)],
        compiler_params=_params(1),
        name="proj_a_sample",
    )(x, g, w)


def _proj_b_body(x_ref, g_ref, w_ref, q_ref, k_ref, v_ref, kb_ref, vb_ref, *, tail):
    h = _rms(x_ref[...], g_ref[...]).astype(BF16)
    kv = jnp.dot(h, w_ref[:, B_Q:], preferred_element_type=F32)
    k, v = kv[:, 0:B_KV], kv[:, B_KV:]
    kb_ref[...] = jnp.concatenate([k, pltpu.roll(k, HALF, axis=1)], axis=1).astype(BF16)
    vb_ref[...] = jnp.concatenate([v, pltpu.roll(v, HALF, axis=1)], axis=1).astype(BF16)
    keep = k.shape[0] if tail is None else tail
    k_ref[...] = k[k.shape[0] - keep:]
    v_ref[...] = v[v.shape[0] - keep:]
    q = jnp.dot(h, w_ref[:, 0:B_Q], preferred_element_type=F32)
    q_ref[...] = (q * (B_HD ** -0.5 * LOG2E)).astype(BF16)


def _proj_b(x, g, w, layer, tm, *, seq=None, tail=None):
    m = x.shape[0]
    row = lambda i: (i, 0)
    full = lambda i: (0, 0)
    if tail is None:
        kv_spec = pl.BlockSpec((tm, B_KV), row)
        kv_shape = jax.ShapeDtypeStruct((m, B_KV), F32)
    else:
        assert seq % tm == 0 and tail <= tm
        per_seq = seq // tm
        kv_spec = pl.BlockSpec((None, tail, B_KV), lambda i: (i // per_seq, 0, 0))
        kv_shape = jax.ShapeDtypeStruct((m // seq, tail, B_KV), F32)
    return pl.pallas_call(
        functools.partial(_proj_b_body, tail=tail),
        grid=(m // tm,),
        in_specs=[pl.BlockSpec((tm, D_MODEL), row),
                  pl.BlockSpec((1, D_MODEL), full),
                  pl.BlockSpec((None, D_MODEL, B_Q + 2 * B_KV), lambda i: (layer, 0, 0))],
        out_specs=[pl.BlockSpec((tm, B_Q), row),
                   kv_spec,
                   kv_spec,
                   pl.BlockSpec((tm, 2 * B_KV), row),
                   pl.BlockSpec((tm, 2 * B_KV), row)],
        out_shape=[jax.ShapeDtypeStruct((m, B_Q), BF16),
                   kv_shape,
                   kv_shape,
                   jax.ShapeDtypeStruct((m, 2 * B_KV), BF16),
                   jax.ShapeDtypeStruct((m, 2 * B_KV), BF16)],
        compiler_params=_params(1),
        name="proj_b",
    )(x, g, w)


def _t5_bucket(rel):
    nb = NUM_BUCKETS // 2
    n = -rel
    ret = jnp.where(n < 0, nb, 0)
    n = jnp.abs(n)
    max_exact = nb // 2
    nf = jnp.maximum(n, 1).astype(jnp.float32)
    large = max_exact + (jnp.log(nf / max_exact) / math.log(MAX_DIST / max_exact)
                         * (nb - max_exact)).astype(jnp.int32)
    large = jnp.minimum(large, nb - 1)
    return ret + jnp.where(n < max_exact, n, large)


def _check_far_bucket(min_dist):
    nb, max_exact = NUM_BUCKETS // 2, NUM_BUCKETS // 4
    ratio = math.log(min_dist / max_exact) / math.log(MAX_DIST / max_exact)
    assert ratio * (nb - max_exact) >= nb - 1 - max_exact + 1e-3, "far keys need one shared bucket"


def _far_bias(table, tq, w):
    far = table[NUM_BUCKETS // 2 - 1].astype(F32) * LOG2E
    return jnp.broadcast_to(far[:, None, None], (N_MAPS, tq, w))


def _near_bias(table, tq, w, back, band):
    r = jnp.arange(tq)[:, None]
    j = jnp.arange(w)[None, :] - back
    onehot = (_t5_bucket(j - r)[None] == jnp.arange(NUM_BUCKETS)[:, None, None]).astype(F32)
    onehot = lax.optimization_barrier(onehot)
    bias = jnp.einsum("bm,brj->mrj", table.astype(F32), onehot, precision=lax.Precision.HIGHEST)
    kc, qc = j // CHUNK, r // CHUNK
    mask = kc <= qc
    if band is not None:
        mask = mask & (kc >= qc - band)
    return jnp.where(mask[None], bias * LOG2E, NEG)


A_TILE = 256
A_ROWS = 64
A_HEADS_PER_STEP = 1


def _diff_lambda(lamp_ref, lam_init):
    lp = lamp_ref[...]
    return (jnp.exp(jnp.sum(lp[0:1] * lp[1:2], axis=-1, keepdims=True))
            - jnp.exp(jnp.sum(lp[2:3] * lp[3:4], axis=-1, keepdims=True)) + lam_init)


def _stack_maps(q):
    lane = lax.broadcasted_iota(jnp.int32, q.shape, 1)
    zero = jnp.zeros_like(q)
    return jnp.concatenate([jnp.where(lane < HALF, q, zero), jnp.where(lane >= HALF, q, zero)], axis=0)


def _diff_epilogue(acc, lam, g, lam_init):
    tq = acc.shape[0] // 2
    o_all = acc[:, :A_DV] / acc[:, A_DV:]
    o = o_all[:tq] - lam * o_all[tq:]
    return (_rms(o, g) * (1.0 - lam_init)).astype(BF16)


def _attn_a_scores(i, q, kt_ref, bias_ref, s_ref, m_ref):
    tl = A_TILE
    qs = _stack_maps(q)
    n_far = max(i - 1, 0)
    for kt in range(i + 1):
        cols = slice(kt * tl, (kt + 1) * tl)
        s = jnp.dot(qs, kt_ref[:, cols], preferred_element_type=F32)
        near = kt >= n_far
        if near:
            s = s + bias_ref[2 - (i - kt)].reshape(2 * tl, tl)
        s_ref[:, cols] = s
        fold = _lane_fold_max(s)
        first = kt == (n_far if near else 0)
        m_ref[int(near)] = fold if first else jnp.maximum(m_ref[int(near)], fold)
        yield
    for r in range(0, 2 * tl, A_ROWS):
        rows = slice(r, r + A_ROWS)
        mx = m_ref[1, rows, :]
        if n_far:
            c = bias_ref[0, r // tl, r % tl:r % tl + A_ROWS, 0:LANES]
            mx = jnp.maximum(mx, m_ref[0, rows, :] + c)
        m = jnp.broadcast_to(jnp.max(mx, axis=-1, keepdims=True), (A_ROWS, LANES))
        m_ref[1, rows, :] = m
        if n_far:
            m_ref[0, rows, :] = m - c


def _attn_a_numerators(i, s_ref, m_ref, v1_ref, out):
    tl = A_TILE
    n_far = max(i - 1, 0)
    acc = None
    for kt in range(i + 1):
        cols = slice(kt * tl, (kt + 1) * tl)
        p = _exp_rows(s_ref[:, cols], m_ref[1 if kt >= n_far else 0])
        part = jnp.dot(p, v1_ref[cols, :], preferred_element_type=F32)
        acc = part if acc is None else acc + part
        yield
    out.append(acc)


def _interleave(*steps):
    steps = list(steps)
    while steps:
        for gen in list(steps):
            if next(gen, StopIteration) is StopIteration:
                steps.remove(gen)


def _attn_a_prompt_body(lamp_ref, q_ref, kt_ref, v_ref, bias_ref, g_ref, o_ref,
                        s_ref, m_ref, v1_ref, *, nt, nh, lam_init):
    tl = A_TILE
    lam = _diff_lambda(lamp_ref, lam_init)
    for hh in range(nh):
        v1_ref[hh, :, 0:A_DV] = v_ref[0, :, hh * A_DV:(hh + 1) * A_DV]
        v1_ref[hh, :, A_DV:] = jnp.ones((nt * tl, A_DV), BF16)
    work = [(i, hh) for i in reversed(range(nt)) for hh in range(nh)]

    def scores(i, hh):
        lanes = slice(hh * LANES, (hh + 1) * LANES)
        return _attn_a_scores(i, q_ref[0, i * tl:(i + 1) * tl, lanes], kt_ref.at[0, lanes, :],
                              bias_ref.at[:, 2 * hh:2 * hh + 2], s_ref.at[hh, i % 2],
                              m_ref.at[hh, i % 2])

    _interleave(scores(*work[0]))
    for n, (i, hh) in enumerate(work):
        out = []
        steps = [_attn_a_numerators(i, s_ref.at[hh, i % 2], m_ref.at[hh, i % 2], v1_ref.at[hh], out)]
        if n + 1 < len(work):
            steps.insert(0, scores(*work[n + 1]))
        _interleave(*steps)
        acc = out[0]
        o_ref[0, i * tl:(i + 1) * tl, hh * LANES:(hh + 1) * LANES] = _diff_epilogue(
            acc, lam, g_ref[...], lam_init)


def _attn_a_prompt(q, kt, v, bias, lamp, g, layer, *, lam_init):
    b, s, _ = q.shape
    nt = s // A_TILE
    nh = A_HEADS_PER_STEP
    assert s % A_TILE == 0 and A_HEADS % nh == 0
    seq = pl.BlockSpec((1, s, nh * LANES), lambda h, bi: (bi, 0, h))
    return pl.pallas_call(
        functools.partial(_attn_a_prompt_body, nt=nt, nh=nh, lam_init=lam_init),
        grid=(A_HEADS // nh, b),
        in_specs=[pl.BlockSpec((None, 4, A_DK), lambda h, bi: (layer, 0, 0)),
                  seq,
                  pl.BlockSpec((1, nh * LANES, s), lambda h, bi: (bi, h, 0)),
                  seq,
                  pl.BlockSpec((3, 2 * nh, A_TILE, A_TILE), lambda h, bi: (0, h, 0, 0)),
                  pl.BlockSpec((None, 1, A_DV), lambda h, bi: (layer, 0, 0))],
        out_specs=seq,
        out_shape=jax.ShapeDtypeStruct((b, s, A_V), BF16),
        scratch_shapes=[pltpu.VMEM((nh, 2, 2 * A_TILE, s), F32),
                        pltpu.VMEM((nh, 2, 2, 2 * A_TILE, LANES), F32),
                        pltpu.VMEM((nh, s, 2 * A_DV), BF16)],
        compiler_params=_params(2),
        name="attn_a_prompt",
    )(lamp, q, kt, v, bias, g)


def _attn_a_sample_body(lamp_ref, q_ref, ktc_ref, vc_ref, kn_ref, vn_ref, bias_ref, g_ref, o_ref,
                        *, lam_init):
    tq = q_ref.shape[1]
    past = ktc_ref.shape[1]
    lam = _diff_lambda(lamp_ref, lam_init)
    for h in range(A_HEADS):
        cols = slice(h * LANES, (h + 1) * LANES)
        qs = _stack_maps(q_ref[0, :, cols])
        bias = bias_ref[2 * h:2 * h + 2].reshape(2 * tq, past + LANES)
        s_c = jnp.dot(qs, ktc_ref[cols, :].astype(BF16), preferred_element_type=F32)
        s_c = s_c + bias[:, 0:past]
        s_n = lax.dot_general(qs, kn_ref[0, :, cols], NT_DIMS, preferred_element_type=F32)
        s_n = s_n + bias[:, past:past + tq]
        m = jnp.maximum(_row_max(s_c), jnp.broadcast_to(jnp.max(s_n, axis=-1, keepdims=True),
                                                        (2 * tq, LANES)))
        p_c = _exp_rows(s_c, m)
        p_n = jnp.exp2(s_n - m[:, 0:tq]).astype(BF16)
        v_c = vc_ref[pl.ds(h, past, stride=A_HEADS), :].astype(BF16)
        acc = (jnp.dot(p_c, _with_ones(v_c), preferred_element_type=F32)
               + jnp.dot(p_n, _with_ones(vn_ref[0, :, cols]), preferred_element_type=F32))
        o_ref[0, :, cols] = _diff_epilogue(acc, lam, g_ref[...], lam_init)


def _attn_a_sample(q, ktc, vc, kn, vn, bias, lamp, g, layer, *, lam_init):
    b, tq, _ = q.shape
    past = ktc.shape[3]
    new = lambda bi: (bi, 0, 0)
    return pl.pallas_call(
        functools.partial(_attn_a_sample_body, lam_init=lam_init),
        grid=(b,),
        in_specs=[pl.BlockSpec((None, 4, A_DK), lambda bi: (layer, 0, 0)),
                  pl.BlockSpec((1, tq, A_QK), new),
                  pl.BlockSpec((None, None, A_QK, past), lambda bi: (layer, bi, 0, 0)),
                  pl.BlockSpec((None, None, past * A_HEADS, A_DV), lambda bi: (layer, bi, 0, 0)),
                  pl.BlockSpec((1, tq, A_QK), new),
                  pl.BlockSpec((1, tq, A_V), new),
                  pl.BlockSpec((N_MAPS, tq, past + LANES), lambda bi: (0, 0, 0)),
                  pl.BlockSpec((None, 1, A_DV), lambda bi: (layer, 0, 0))],
        out_specs=pl.BlockSpec((1, tq, A_V), new),
        out_shape=jax.ShapeDtypeStruct((b, tq, A_V), BF16),
        compiler_params=_params(1),
        name="attn_a_sample",
    )(lamp, q, ktc, vc, kn, vn, bias, g)


_B_ORDER_A = (0, 2, 4, 6, 9, 11, 13, 15)
_B_ORDER_B = (1, 3, 5, 7, 8, 10, 12, 14)


def _attn_b_body(sink_ref, q_ref, k_ref, v_ref, bias_ref, o_ref, *, layer, tq, w, back, i_off, nsub):
    lane = lax.broadcasted_iota(jnp.int32, (tq, LANES), 1)
    lo = lane < HALF
    half = B_Q_HEADS // 4
    orders = (_B_ORDER_A, _B_ORDER_B)
    sinks = [jnp.concatenate([jnp.full((tq, LANES), sink_ref[layer, h] * LOG2E, F32)
                              for h in orders[t]], axis=0) for t in range(2)]
    outs = {}

    def chain(u, t):
        tile = pl.program_id(1) * nsub + u + i_off
        variant = 1 if (u > 0 or i_off > 0) else jnp.minimum(tile, 1)
        near_start = pl.multiple_of(jnp.maximum(tile * tq - back, 0), CHUNK)
        group = slice(t * LANES, (t + 1) * LANES)
        pieces = []
        for c in range(B_Q_HEADS // 2):
            qc = q_ref[0, u * tq:(u + 1) * tq, c * LANES:(c + 1) * LANES]
            keep_lo = (c < half) == (t == 0)
            pieces.append(jnp.where(lo == keep_lo, qc, jnp.zeros_like(qc)))
        qs = jnp.concatenate(pieces, axis=0)
        s = lax.dot_general(qs, k_ref[0, pl.ds(near_start, w), group], NT_DIMS,
                            preferred_element_type=F32)
        s = s + bias_ref[variant, t].reshape((B_Q_HEADS // 2) * tq, w)
        m = jnp.maximum(_row_max(s), sinks[t])
        p = _exp_rows(s, m)
        acc = jnp.dot(p, _with_ones(v_ref[0, pl.ds(near_start, w), group]),
                      preferred_element_type=F32)
        outs[u, t] = acc[:, :LANES] / (acc[:, LANES:] + jnp.exp2(sinks[t] - m))

    for u in range(nsub):
        for t in range(2):
            chain(u, t)

    for u in range(nsub):
        for c in range(B_Q_HEADS // 2):
            oa = outs[u, 0][c * tq:(c + 1) * tq]
            ob = outs[u, 1][c * tq:(c + 1) * tq]
            oc = jnp.where(lo, oa, ob) if c < half else jnp.where(lo, ob, oa)
            o_ref[0, u * tq:(u + 1) * tq, c * LANES:(c + 1) * LANES] = oc.astype(BF16)


def _attn_b(q, k, v, bias, sinks, layer, *, tq, w, back, i_off, nsub):
    b, sq, _ = q.shape
    sk = k.shape[1]
    rows = nsub * tq
    assert sq % rows == 0 and (sq // tq - 1 + i_off) * tq - back + w <= sk
    body = functools.partial(_attn_b_body, layer=layer, tq=tq, w=w, back=back, i_off=i_off,
                             nsub=nsub)
    return pl.pallas_call(
        body,
        grid=(b, sq // rows),
        in_specs=[pl.BlockSpec(memory_space=pltpu.SMEM),
                  pl.BlockSpec((1, rows, B_Q), lambda bi, i: (bi, i, 0)),
                  pl.BlockSpec((1, sk, 2 * B_KV), lambda bi, i: (bi, 0, 0)),
                  pl.BlockSpec((1, sk, 2 * B_KV), lambda bi, i: (bi, 0, 0)),
                  pl.BlockSpec((2, 2, B_Q_HEADS // 2, tq, w), lambda bi, i: (0, 0, 0, 0, 0))],
        out_specs=pl.BlockSpec((1, rows, B_Q), lambda bi, i: (bi, i, 0)),
        out_shape=jax.ShapeDtypeStruct((b, sq, B_Q), BF16),
        compiler_params=_params(2),
        name="attn_b",
    )(sinks, q, k, v, bias)


def _post_body(x_ref, o_ref, wo_ref, g_ref, wu_ref, wd_ref, gf_ref, y_ref, hn_ref, *, final):
    f = pl.program_id(1)

    @pl.when(f == 0)
    def _():
        x1 = x_ref[...] + jnp.dot(o_ref[...], wo_ref[...], preferred_element_type=F32)
        y_ref[...] = x1
        hn_ref[...] = _rms(x1, g_ref[...]).astype(BF16)

    u = jnp.dot(hn_ref[...], wu_ref[...], preferred_element_type=F32)
    a = jnp.square(jnp.maximum(u, 0.0)).astype(BF16)
    y_ref[...] += jnp.dot(a, wd_ref[...], preferred_element_type=F32)

    if final:
        @pl.when(f == pl.num_programs(1) - 1)
        def _():
            y_ref[...] = _rms(y_ref[...], gf_ref[...])


def _post(x, o, wo, wo_layer, g, wu, wd, layer, gf, *, tm, tf, final):
    m = x.shape[0]
    row = lambda i, f: (i, 0)
    full = lambda i, f: (0, 0)
    return pl.pallas_call(
        functools.partial(_post_body, final=final),
        grid=(m // tm, D_FF // tf),
        in_specs=[pl.BlockSpec((tm, D_MODEL), row),
                  pl.BlockSpec((tm, D_MODEL), row),
                  pl.BlockSpec((None, D_MODEL, D_MODEL), lambda i, f: (wo_layer, 0, 0)),
                  pl.BlockSpec((None, 1, D_MODEL), lambda i, f: (layer, 0, 0)),
                  pl.BlockSpec((None, D_MODEL, tf), lambda i, f: (layer, 0, f)),
                  pl.BlockSpec((None, tf, D_MODEL), lambda i, f: (layer, f, 0)),
                  pl.BlockSpec((1, D_MODEL), full)],
        out_specs=pl.BlockSpec((tm, D_MODEL), row),
        out_shape=jax.ShapeDtypeStruct((m, D_MODEL), F32),
        scratch_shapes=[pltpu.VMEM((tm, D_MODEL), BF16)],
        compiler_params=_params(2),
        name="post",
    )(x, o, wo, g, wu, wd, gf)


_B_PROMPT = dict(tq=128, w=256, back=128)
_B_SAMPLE = dict(tq=64, w=256, back=128)
B_TILES_PER_STEP = 8
_A_SAMPLE_BACK = 128


def _a_prompt_bias(table):
    _check_far_bucket(A_TILE + 1)
    near = _near_bias(table, A_TILE, 2 * A_TILE, A_TILE, None)
    return jnp.stack([_far_bias(table, A_TILE, A_TILE), near[:, :, :A_TILE], near[:, :, A_TILE:]])


def _a_sample_bias(table, past, t):
    assert t <= LANES
    _check_far_bucket(_A_SAMPLE_BACK + 1)
    far = _far_bias(table, t, past - _A_SAMPLE_BACK)
    near = _near_bias(table, t, _A_SAMPLE_BACK + LANES, _A_SAMPLE_BACK, None)
    return jnp.concatenate([far, near], axis=2)


def _b_bias(table, tq, w, back):
    def one(bk):
        t = _near_bias(table, tq, w, bk, WINDOW // CHUNK)
        return jnp.stack([jnp.stack([t[h] for h in order]) for order in (_B_ORDER_A, _B_ORDER_B)])
    return jnp.stack([one(0), one(back)])


def _pad_keys(t, total):
    return jnp.pad(t, ((0, 0), (0, total - t.shape[1]), (0, 0)))


def _trunk(x, caches, wts, table, *, tm, tm_post, tf):
    b, s, _ = x.shape
    sample = caches is not None
    n_a = wts["a_w_qkv"].shape[0]
    xm = x.reshape(b * s, D_MODEL)
    b_cfg = _B_SAMPLE if sample else _B_PROMPT
    b_bias = _b_bias(table, **b_cfg)
    if sample:
        a_bias = _a_sample_bias(table, caches["a_kt"].shape[3], s)
    else:
        a_bias = _a_prompt_bias(table)
    ak, av, bk, bv = [], [], [], []
    kt_buf = v_buf = None
    for i in range(DEPTH):
        j = i // 2
        g_mix = wts["norm_mix_g"][i][None, :]
        if i % 2 == 0:
            lam_init = _lambda_init(i)
            if sample:
                q, k, v, kb, vb = _proj_a_sample(xm, g_mix, wts["a_w_qkv"], j, tm)
                ak.append(k.reshape(b, s, A_HEADS, 2, A_DK))
                av.append(v.reshape(b, s, A_HEADS, A_DV))
                o = _attn_a_sample(q.reshape(b, s, A_QK), caches["a_kt"], caches["a_v"],
                                   kb.reshape(b, s, A_QK), vb.reshape(b, s, A_V), a_bias,
                                   wts["a_lambda"], wts["a_subln_g"], j, lam_init=lam_init)
            else:
                q, kt_buf, v_buf, ktb, vb = _proj_a_prompt(
                    xm, g_mix, wts["a_w_qkv"], wts["a_w_kt"], j, kt_buf, v_buf,
                    b=b, s=s, tm=tm, n_a=n_a)
                o = _attn_a_prompt(q.reshape(b, s, A_QK), ktb, vb.reshape(b, s, A_V), a_bias,
                                   wts["a_lambda"], wts["a_subln_g"], j, lam_init=lam_init)
            wo, wo_layer = wts["a_w_o"], j
        else:
            window = {} if sample else dict(seq=s, tail=WINDOW)
            q, k, v, kb, vb = _proj_b(xm, g_mix, wts["b_w_qkv"], j, tm, **window)
            k = k.reshape(b, -1, B_KV_HEADS, B_HD)
            v = v.reshape(b, -1, B_KV_HEADS, B_HD)
            kb = kb.reshape(b, s, 2 * B_KV)
            vb = vb.reshape(b, s, 2 * B_KV)
            i_off = 0
            if sample:
                ck, cv, ckb, cvb = caches["b"][j]
                bk.append(jnp.concatenate([ck, k], axis=1)[:, s:])
                bv.append(jnp.concatenate([cv, v], axis=1)[:, s:])
                i_off = ckb.shape[1] // b_cfg["tq"]
                total = ckb.shape[1] - b_cfg["back"] + b_cfg["w"]
                kb = _pad_keys(jnp.concatenate([ckb, kb], axis=1), total)
                vb = _pad_keys(jnp.concatenate([cvb, vb], axis=1), total)
            else:
                bk.append(k)
                bv.append(v)
            o = _attn_b(q.reshape(b, s, B_Q), kb, vb, b_bias, wts["b_sinks"], j, i_off=i_off,
                        nsub=1 if sample else B_TILES_PER_STEP, **b_cfg)
            wo, wo_layer = wts["b_w_o"], j
        xm = _post(xm, o.reshape(b * s, D_MODEL), wo, wo_layer, wts["norm_mlp_g"],
                   wts["mlp_w_up"], wts["mlp_w_down"], i, wts["final_norm_g"][None, :],
                   tm=tm_post, tf=tf, final=(i == DEPTH - 1))
    if sample:
        a_k, a_v = jnp.stack(ak), jnp.stack(av)
    else:
        a_k = jnp.transpose(kt_buf.reshape(n_a, b, A_HEADS, 2, A_DK, s), (0, 1, 5, 2, 3, 4))
        a_v = v_buf.reshape(n_a, b, s, A_HEADS, A_DV)
    return xm.reshape(b, s, D_MODEL), a_k, a_v, jnp.stack(bk), jnp.stack(bv)


def _swap_halves(t):
    flat = t.reshape(t.shape[:-2] + (B_KV,))
    swapped = t[..., ::-1, :].reshape(t.shape[:-2] + (B_KV,))
    return jnp.concatenate([flat, swapped], axis=-1).astype(BF16)


def kernel(x_prompt, x_sample, cache_a_k, cache_a_v, cache_b_k, cache_b_v, rel_table,
           norm_mix_g, norm_mlp_g, final_norm_g, a_w_qkv, a_lambda, a_subln_g, a_w_o,
           b_w_qkv, b_sinks, b_w_o, mlp_w_up, mlp_w_down):
    wts = dict(norm_mix_g=norm_mix_g, norm_mlp_g=norm_mlp_g[:, None, :], final_norm_g=final_norm_g,
               a_w_qkv=a_w_qkv.astype(BF16),
               a_w_kt=jnp.transpose(a_w_qkv[:, :, A_QK:2 * A_QK], (0, 2, 1)).astype(BF16),
               a_lambda=a_lambda, a_subln_g=a_subln_g[:, None, :],
               a_w_o=a_w_o.astype(BF16), b_w_qkv=b_w_qkv.astype(BF16), b_sinks=b_sinks,
               b_w_o=b_w_o.astype(BF16), mlp_w_up=mlp_w_up.astype(BF16),
               mlp_w_down=mlp_w_down.astype(BF16))
    n_a, db, past = cache_a_k.shape[:3]
    caches = dict(
        a_kt=jnp.transpose(cache_a_k, (0, 1, 3, 4, 5, 2)).reshape(n_a, db, A_QK, past),
        a_v=cache_a_v.reshape(n_a, db, past * A_HEADS, A_DV),
        b=[(cache_b_k[j], cache_b_v[j], _swap_halves(cache_b_k[j]), _swap_halves(cache_b_v[j]))
           for j in range(cache_b_k.shape[0])])
    yp, akp, avp, bkp, bvp = _trunk(x_prompt, None, wts, rel_table, tm=512, tm_post=1024, tf=1024)
    ys, aks, avs, bks, bvs = _trunk(x_sample, caches, wts, rel_table, tm=512, tm_post=512, tf=1024)
    return (yp, ys, akp, avp, bkp, bvp, aks, avs, bks, bvs)
```

```python
import functools
import math

import jax
import jax.numpy as jnp
from jax import lax
from jax.experimental import pallas as pl
from jax.experimental.pallas import tpu as pltpu

BF16 = jnp.bfloat16
F32 = jnp.float32

D_MODEL = 1024
DEPTH = 4
CHUNK = 64
A_HEADS = 8
A_DK = 64
A_DV = 128
A_QK = A_HEADS * 2 * A_DK
A_V = A_HEADS * A_DV
B_Q_HEADS = 16
B_KV_HEADS = 2
B_HD = 64
B_Q = B_Q_HEADS * B_HD
B_KV = B_KV_HEADS * B_HD
WINDOW = 128
NUM_BUCKETS = 32
MAX_DIST = 128
N_MAPS = 16
D_FF = 4 * D_MODEL
EPS = 1e-6
NEG = -1e30

LANES = 128
SUBLANES = 8
HALF = LANES // 2
VMEM_LIMIT = 56 * 1024 * 1024

NT_DIMS = (((1,), (1,)), ((), ()))

LOG2E = math.log2(math.e)


def _lambda_init(layer):
    return 0.8 - 0.6 * math.exp(-0.3 * layer)


def _rms(x, g):
    return x * lax.rsqrt(jnp.mean(x * x, axis=-1, keepdims=True) + EPS) * g


def _params(n_axes):
    return pltpu.CompilerParams(
        dimension_semantics=("arbitrary",) * n_axes, vmem_limit_bytes=VMEM_LIMIT)


def _lane_fold_max(s):
    mx = s[:, 0:LANES]
    for gi in range(1, s.shape[1] // LANES):
        mx = jnp.maximum(mx, s[:, gi * LANES:(gi + 1) * LANES])
    return mx


def _row_max(s):
    return jnp.broadcast_to(jnp.max(_lane_fold_max(s), axis=-1, keepdims=True), (s.shape[0], LANES))


def _exp_rows(s, m):
    return jnp.concatenate([jnp.exp2(s[:, gi * LANES:(gi + 1) * LANES] - m)
                            for gi in range(s.shape[1] // LANES)], axis=1).astype(BF16)


def _with_ones(v):
    return jnp.concatenate([v, jnp.ones(v.shape, v.dtype)], axis=1)


def _proj_a_prompt_body(x_ref, g_ref, w_ref, wkt_ref, *refs, first):
    q_ref, kt_ref, v_ref, ktb_ref, vb_ref = refs[-5:]
    h = _rms(x_ref[...], g_ref[...]).astype(BF16)
    tm = h.shape[0]
    if first:
        for slot in range(1, kt_ref.shape[0]):
            kt_ref[slot] = jnp.zeros(kt_ref.shape[1:], F32)
            v_ref[slot] = jnp.zeros(v_ref.shape[1:], F32)
        kt_ref, v_ref = kt_ref.at[0], v_ref.at[0]
    v = jnp.dot(h, w_ref[:, 2 * A_QK:], preferred_element_type=F32)
    vb_ref[...] = v.astype(BF16)
    for hh in range(A_HEADS):
        v_ref[pl.ds(hh, tm, stride=A_HEADS), :] = v[:, hh * A_DV:(hh + 1) * A_DV]
    kt = lax.dot_general(wkt_ref[...], h, NT_DIMS, preferred_element_type=F32)
    ktb_ref[...] = kt.astype(BF16)
    kt_ref[...] = kt
    q = jnp.dot(h, w_ref[:, 0:A_QK], preferred_element_type=F32)
    q_ref[...] = (q * (A_DK ** -0.5 * LOG2E)).astype(BF16)


def _proj_a_prompt(x, g, w, wkt, w_layer, layer, kt_prev, v_prev, *, b, s, tm, n_a):
    m = b * s
    per_b = s // tm
    row = lambda i: (i, 0)
    full = lambda i: (0, 0)
    stacked = lambda i: (w_layer, 0, 0)
    first = kt_prev is None
    assert first == (layer == 0)
    in_specs = [pl.BlockSpec((tm, D_MODEL), row),
                pl.BlockSpec((1, D_MODEL), full),
                pl.BlockSpec((None, D_MODEL, 2 * A_QK + A_V), stacked),
                pl.BlockSpec((None, A_QK, D_MODEL), stacked)]
    operands = [x, g, w, wkt]
    if first:
        kt_spec = pl.BlockSpec((n_a, None, A_QK, tm), lambda i: (0, i // per_b, 0, i % per_b))
        v_spec = pl.BlockSpec((n_a, tm * A_HEADS, A_DV), lambda i: (0, i, 0))
        aliases = {}
    else:
        kt_spec = pl.BlockSpec((None, None, A_QK, tm), lambda i: (layer, i // per_b, 0, i % per_b))
        v_spec = pl.BlockSpec((None, tm * A_HEADS, A_DV), lambda i: (layer, i, 0))
        in_specs += [pl.BlockSpec(memory_space=pl.ANY)] * 2
        operands += [kt_prev, v_prev]
        aliases = {4: 1, 5: 2}
    return pl.pallas_call(
        functools.partial(_proj_a_prompt_body, first=first),
        grid=(m // tm,),
        in_specs=in_specs,
        out_specs=[pl.BlockSpec((tm, A_QK), row),
                   kt_spec,
                   v_spec,
                   pl.BlockSpec((None, A_QK, tm), lambda i: (i // per_b, 0, i % per_b)),
                   pl.BlockSpec((tm, A_V), row)],
        out_shape=[jax.ShapeDtypeStruct((m, A_QK), BF16),
                   jax.ShapeDtypeStruct((n_a, b, A_QK, s), F32),
                   jax.ShapeDtypeStruct((n_a, m * A_HEADS, A_DV), F32),
                   jax.ShapeDtypeStruct((b, A_QK, s), BF16),
                   jax.ShapeDtypeStruct((m, A_V), BF16)],
        input_output_aliases=aliases,
        compiler_params=_params(1),
        name="proj_a_prompt",
    )(*operands)


def _proj_a_sample_body(x_ref, g_ref, w_ref, q_ref, k_ref, v_ref, kb_ref, vb_ref, wb_ref, wkt_ref):
    wb_ref[...] = w_ref[...].astype(BF16)
    wkt_ref[...] = w_ref[:, A_QK:2 * A_QK].T.astype(BF16)
    h = _rms(x_ref[...], g_ref[...]).astype(BF16)
    q = jnp.dot(h, wb_ref[:, 0:A_QK], preferred_element_type=F32)
    q_ref[...] = (q * (A_DK ** -0.5 * LOG2E)).astype(BF16)
    k = jnp.dot(h, wb_ref[:, A_QK:2 * A_QK], preferred_element_type=F32)
    k_ref[...] = k
    kb_ref[...] = k.astype(BF16)
    v = jnp.dot(h, wb_ref[:, 2 * A_QK:], preferred_element_type=F32)
    v_ref[...] = v
    vb_ref[...] = v.astype(BF16)


def _proj_a_sample(x, g, w, layer):
    m = x.shape[0]
    n_out = 2 * A_QK + A_V
    full = lambda i: (0, 0)
    full3 = lambda i: (0, 0, 0)
    return pl.pallas_call(
        _proj_a_sample_body,
        grid=(1,),
        in_specs=[pl.BlockSpec((m, D_MODEL), full),
                  pl.BlockSpec((1, D_MODEL), full),
                  pl.BlockSpec((None, D_MODEL, n_out), lambda i: (layer, 0, 0))],
        out_specs=[pl.BlockSpec((m, A_QK), full),
                   pl.BlockSpec((m, A_QK), full),
                   pl.BlockSpec((m, A_V), full),
                   pl.BlockSpec((m, A_QK), full),
                   pl.BlockSpec((m, A_V), full),
                   pl.BlockSpec((None, D_MODEL, n_out), full3),
                   pl.BlockSpec((None, A_QK, D_MODEL), full3)],
        out_shape=[jax.ShapeDtypeStruct((m, A_QK), BF16),
                   jax.ShapeDtypeStruct((m, A_QK), F32),
                   jax.ShapeDtypeStruct((m, A_V), F32),
                   jax.ShapeDtypeStruct((m, A_QK), BF16),
                   jax.ShapeDtypeStruct((m, A_V), BF16),
                   jax.ShapeDtypeStruct((1, D_MODEL, n_out), BF16),
                   jax.ShapeDtypeStruct((1, A_QK, D_MODEL), BF16)],
        compiler_params=_params(1),
        name="proj_a_sample",
    )(x, g, w)


def _proj_b_body(x_ref, g_ref, w_ref, q_ref, k_ref, v_ref, kb_ref, vb_ref, *wb_ref, tail):
    if wb_ref:
        wb_ref[0][...] = w_ref[...].astype(BF16)
        w_ref = wb_ref[0]
    h = _rms(x_ref[...], g_ref[...]).astype(BF16)
    kv = jnp.dot(h, w_ref[:, B_Q:], preferred_element_type=F32)
    k, v = kv[:, 0:B_KV], kv[:, B_KV:]
    kb_ref[...] = jnp.concatenate([k, pltpu.roll(k, HALF, axis=1)], axis=1).astype(BF16)
    vb_ref[...] = jnp.concatenate([v, pltpu.roll(v, HALF, axis=1)], axis=1).astype(BF16)
    keep = k.shape[0] if tail is None else tail
    k_ref[...] = k[k.shape[0] - keep:]
    v_ref[...] = v[v.shape[0] - keep:]
    q = jnp.dot(h, w_ref[:, 0:B_Q], preferred_element_type=F32)
    q_ref[...] = (q * (B_HD ** -0.5 * LOG2E)).astype(BF16)


def _proj_b(x, g, w, layer, tm, *, seq=None, tail=None, cast_weights=False):
    m = x.shape[0]
    assert not cast_weights or m == tm
    n_out = B_Q + 2 * B_KV
    row = lambda i: (i, 0)
    full = lambda i: (0, 0)
    wb_spec = [pl.BlockSpec((None, D_MODEL, n_out), lambda i: (0, 0, 0))] if cast_weights else []
    wb_shape = [jax.ShapeDtypeStruct((1, D_MODEL, n_out), BF16)] if cast_weights else []
    if tail is None:
        kv_spec = pl.BlockSpec((tm, B_KV), row)
        kv_shape = jax.ShapeDtypeStruct((m, B_KV), F32)
    else:
        assert seq % tm == 0 and tail <= tm
        per_seq = seq // tm
        kv_spec = pl.BlockSpec((None, tail, B_KV), lambda i: (i // per_seq, 0, 0))
        kv_shape = jax.ShapeDtypeStruct((m // seq, tail, B_KV), F32)
    return pl.pallas_call(
        functools.partial(_proj_b_body, tail=tail),
        grid=(m // tm,),
        in_specs=[pl.BlockSpec((tm, D_MODEL), row),
                  pl.BlockSpec((1, D_MODEL), full),
                  pl.BlockSpec((None, D_MODEL, B_Q + 2 * B_KV), lambda i: (layer, 0, 0))],
        out_specs=[pl.BlockSpec((tm, B_Q), row),
                   kv_spec,
                   kv_spec,
                   pl.BlockSpec((tm, 2 * B_KV), row),
                   pl.BlockSpec((tm, 2 * B_KV), row)] + wb_spec,
        out_shape=[jax.ShapeDtypeStruct((m, B_Q), BF16),
                   kv_shape,
                   kv_shape,
                   jax.ShapeDtypeStruct((m, 2 * B_KV), BF16),
                   jax.ShapeDtypeStruct((m, 2 * B_KV), BF16)] + wb_shape,
        compiler_params=_params(1),
        name="proj_b",
    )(x, g, w)


def _t5_bucket(rel):
    nb = NUM_BUCKETS // 2
    n = -rel
    ret = jnp.where(n < 0, nb, 0)
    n = jnp.abs(n)
    max_exact = nb // 2
    nf = jnp.maximum(n, 1).astype(jnp.float32)
    large = max_exact + (jnp.log(nf / max_exact) / math.log(MAX_DIST / max_exact)
                         * (nb - max_exact)).astype(jnp.int32)
    large = jnp.minimum(large, nb - 1)
    return ret + jnp.where(n < max_exact, n, large)


def _check_far_bucket(min_dist):
    nb, max_exact = NUM_BUCKETS // 2, NUM_BUCKETS // 4
    ratio = math.log(min_dist / max_exact) / math.log(MAX_DIST / max_exact)
    assert ratio * (nb - max_exact) >= nb - 1 - max_exact + 1e-3, "far keys need one shared bucket"


def _far_bias(table, tq, w):
    far = table[NUM_BUCKETS // 2 - 1].astype(F32) * LOG2E
    return jnp.broadcast_to(far[:, None, None], (N_MAPS, tq, w))


def _near_bias(table, tq, w, back, band):
    r = jnp.arange(tq)[:, None]
    j = jnp.arange(w)[None, :] - back
    onehot = (_t5_bucket(j - r)[None] == jnp.arange(NUM_BUCKETS)[:, None, None]).astype(F32)
    onehot = lax.optimization_barrier(onehot)
    bias = jnp.einsum("bm,brj->mrj", table.astype(F32), onehot, precision=lax.Precision.HIGHEST)
    kc, qc = j // CHUNK, r // CHUNK
    mask = kc <= qc
    if band is not None:
        mask = mask & (kc >= qc - band)
    return jnp.where(mask[None], bias * LOG2E, NEG)


A_TILE = 256
A_ROWS = 64
A_HEADS_PER_STEP = 1


def _diff_lambda(lamp_ref, lam_init):
    lp = lamp_ref[...]
    return (jnp.exp(jnp.sum(lp[0:1] * lp[1:2], axis=-1, keepdims=True))
            - jnp.exp(jnp.sum(lp[2:3] * lp[3:4], axis=-1, keepdims=True)) + lam_init)


def _stack_maps(q):
    lane = lax.broadcasted_iota(jnp.int32, q.shape, 1)
    zero = jnp.zeros_like(q)
    return jnp.concatenate([jnp.where(lane < HALF, q, zero), jnp.where(lane >= HALF, q, zero)], axis=0)


def _diff_epilogue(acc, lam, g, lam_init):
    tq = acc.shape[0] // 2
    o_all = acc[:, :A_DV] / acc[:, A_DV:]
    o = o_all[:tq] - lam * o_all[tq:]
    return (_rms(o, g) * (1.0 - lam_init)).astype(BF16)


def _attn_a_scores(i, q, kt_ref, bias_ref, s_ref, m_ref):
    tl = A_TILE
    qs = _stack_maps(q)
    n_far = max(i - 1, 0)
    for kt in range(i + 1):
        cols = slice(kt * tl, (kt + 1) * tl)
        s = jnp.dot(qs, kt_ref[:, cols], preferred_element_type=F32)
        near = kt >= n_far
        if near:
            s = s + bias_ref[2 - (i - kt)].reshape(2 * tl, tl)
        s_ref[:, cols] = s
        fold = _lane_fold_max(s)
        first = kt == (n_far if near else 0)
        m_ref[int(near)] = fold if first else jnp.maximum(m_ref[int(near)], fold)
        yield
    for r in range(0, 2 * tl, A_ROWS):
        rows = slice(r, r + A_ROWS)
        mx = m_ref[1, rows, :]
        if n_far:
            c = bias_ref[0, r // tl, r % tl:r % tl + A_ROWS, 0:LANES]
            mx = jnp.maximum(mx, m_ref[0, rows, :] + c)
        m = jnp.broadcast_to(jnp.max(mx, axis=-1, keepdims=True), (A_ROWS, LANES))
        m_ref[1, rows, :] = m
        if n_far:
            m_ref[0, rows, :] = m - c


def _attn_a_numerators(i, s_ref, m_ref, v1_ref, out):
    tl = A_TILE
    n_far = max(i - 1, 0)
    acc = None
    for kt in range(i + 1):
        cols = slice(kt * tl, (kt + 1) * tl)
        p = _exp_rows(s_ref[:, cols], m_ref[1 if kt >= n_far else 0])
        part = jnp.dot(p, v1_ref[cols, :], preferred_element_type=F32)
        acc = part if acc is None else acc + part
        yield
    out.append(acc)


def _interleave(*steps):
    steps = list(steps)
    while steps:
        for gen in list(steps):
            if next(gen, StopIteration) is StopIteration:
                steps.remove(gen)


def _attn_a_prompt_body(lamp_ref, q_ref, kt_ref, v_ref, bias_ref, g_ref, o_ref,
                        s_ref, m_ref, v1_ref, *, nt, nh, lam_init):
    tl = A_TILE
    lam = _diff_lambda(lamp_ref, lam_init)
    for hh in range(nh):
        v1_ref[hh, :, 0:A_DV] = v_ref[0, :, hh * A_DV:(hh + 1) * A_DV]
        v1_ref[hh, :, A_DV:] = jnp.ones((nt * tl, A_DV), BF16)
    work = [(i, hh) for i in reversed(range(nt)) for hh in range(nh)]

    def scores(i, hh):
        lanes = slice(hh * LANES, (hh + 1) * LANES)
        return _attn_a_scores(i, q_ref[0, i * tl:(i + 1) * tl, lanes], kt_ref.at[0, lanes, :],
                              bias_ref.at[:, 2 * hh:2 * hh + 2], s_ref.at[hh, i % 2],
                              m_ref.at[hh, i % 2])

    _interleave(scores(*work[0]))
    for n, (i, hh) in enumerate(work):
        out = []
        steps = [_attn_a_numerators(i, s_ref.at[hh, i % 2], m_ref.at[hh, i % 2], v1_ref.at[hh], out)]
        if n + 1 < len(work):
            steps.insert(0, scores(*work[n + 1]))
        _interleave(*steps)
        acc = out[0]
        o_ref[0, i * tl:(i + 1) * tl, hh * LANES:(hh + 1) * LANES] = _diff_epilogue(
            acc, lam, g_ref[...], lam_init)


def _attn_a_prompt(q, kt, v, bias, lamp, g, layer, *, lam_init):
    b, s, _ = q.shape
    nt = s // A_TILE
    nh = A_HEADS_PER_STEP
    assert s % A_TILE == 0 and A_HEADS % nh == 0
    seq = pl.BlockSpec((1, s, nh * LANES), lambda h, bi: (bi, 0, h))
    return pl.pallas_call(
        functools.partial(_attn_a_prompt_body, nt=nt, nh=nh, lam_init=lam_init),
        grid=(A_HEADS // nh, b),
        in_specs=[pl.BlockSpec((None, 4, A_DK), lambda h, bi: (layer, 0, 0)),
                  seq,
                  pl.BlockSpec((1, nh * LANES, s), lambda h, bi: (bi, h, 0)),
                  seq,
                  pl.BlockSpec((3, 2 * nh, A_TILE, A_TILE), lambda h, bi: (0, h, 0, 0)),
                  pl.BlockSpec((None, 1, A_DV), lambda h, bi: (layer, 0, 0))],
        out_specs=seq,
        out_shape=jax.ShapeDtypeStruct((b, s, A_V), BF16),
        scratch_shapes=[pltpu.VMEM((nh, 2, 2 * A_TILE, s), F32),
                        pltpu.VMEM((nh, 2, 2, 2 * A_TILE, LANES), F32),
                        pltpu.VMEM((nh, s, 2 * A_DV), BF16)],
        compiler_params=_params(2),
        name="attn_a_prompt",
    )(lamp, q, kt, v, bias, g)


def _attn_a_sample_body(lamp_ref, q_ref, ktc_ref, vc_ref, kn_ref, vn_ref, bias_ref, g_ref, o_ref,
                        *, lam_init):
    tq = q_ref.shape[1]
    past = ktc_ref.shape[1]
    lam = _diff_lambda(lamp_ref, lam_init)
    for h in range(A_HEADS):
        cols = slice(h * LANES, (h + 1) * LANES)
        qs = _stack_maps(q_ref[0, :, cols])
        bias = bias_ref[2 * h:2 * h + 2].reshape(2 * tq, past + LANES)
        s_c = jnp.dot(qs, ktc_ref[cols, :].astype(BF16), preferred_element_type=F32)
        s_c = s_c + bias[:, 0:past]
        s_n = lax.dot_general(qs, kn_ref[0, :, cols], NT_DIMS, preferred_element_type=F32)
        s_n = s_n + bias[:, past:past + tq]
        m = jnp.maximum(_row_max(s_c), jnp.broadcast_to(jnp.max(s_n, axis=-1, keepdims=True),
                                                        (2 * tq, LANES)))
        p_c = _exp_rows(s_c, m)
        p_n = jnp.exp2(s_n - m[:, 0:tq]).astype(BF16)
        v_c = vc_ref[pl.ds(h, past, stride=A_HEADS), :].astype(BF16)
        acc = (jnp.dot(p_c, _with_ones(v_c), preferred_element_type=F32)
               + jnp.dot(p_n, _with_ones(vn_ref[0, :, cols]), preferred_element_type=F32))
        o_ref[0, :, cols] = _diff_epilogue(acc, lam, g_ref[...], lam_init)


def _attn_a_sample(q, ktc, vc, kn, vn, bias, lamp, g, layer, *, lam_init):
    b, tq, _ = q.shape
    past = ktc.shape[3]
    new = lambda bi: (bi, 0, 0)
    return pl.pallas_call(
        functools.partial(_attn_a_sample_body, lam_init=lam_init),
        grid=(b,),
        in_specs=[pl.BlockSpec((None, 4, A_DK), lambda bi: (layer, 0, 0)),
                  pl.BlockSpec((1, tq, A_QK), new),
                  pl.BlockSpec((None, None, A_QK, past), lambda bi: (layer, bi, 0, 0)),
                  pl.BlockSpec((None, None, past * A_HEADS, A_DV), lambda bi: (layer, bi, 0, 0)),
                  pl.BlockSpec((1, tq, A_QK), new),
                  pl.BlockSpec((1, tq, A_V), new),
                  pl.BlockSpec((N_MAPS, tq, past + LANES), lambda bi: (0, 0, 0)),
                  pl.BlockSpec((None, 1, A_DV), lambda bi: (layer, 0, 0))],
        out_specs=pl.BlockSpec((1, tq, A_V), new),
        out_shape=jax.ShapeDtypeStruct((b, tq, A_V), BF16),
        compiler_params=_params(1),
        name="attn_a_sample",
    )(lamp, q, ktc, vc, kn, vn, bias, g)


_B_ORDER_A = (0, 2, 4, 6, 9, 11, 13, 15)
_B_ORDER_B = (1, 3, 5, 7, 8, 10, 12, 14)


def _attn_b_body(sink_ref, q_ref, k_ref, v_ref, bias_ref, o_ref, *, layer, tq, w, back, i_off, nsub):
    lane = lax.broadcasted_iota(jnp.int32, (tq, LANES), 1)
    lo = lane < HALF
    half = B_Q_HEADS // 4
    orders = (_B_ORDER_A, _B_ORDER_B)
    sinks = [jnp.concatenate([jnp.full((tq, LANES), sink_ref[layer, h] * LOG2E, F32)
                              for h in orders[t]], axis=0) for t in range(2)]
    outs = {}

    def chain(u, t):
        tile = pl.program_id(1) * nsub + u + i_off
        variant = 1 if (u > 0 or i_off > 0) else jnp.minimum(tile, 1)
        near_start = pl.multiple_of(jnp.maximum(tile * tq - back, 0), CHUNK)
        group = slice(t * LANES, (t + 1) * LANES)
        pieces = []
        for c in range(B_Q_HEADS // 2):
            qc = q_ref[0, u * tq:(u + 1) * tq, c * LANES:(c + 1) * LANES]
            keep_lo = (c < half) == (t == 0)
            pieces.append(jnp.where(lo == keep_lo, qc, jnp.zeros_like(qc)))
        qs = jnp.concatenate(pieces, axis=0)
        s = lax.dot_general(qs, k_ref[0, pl.ds(near_start, w), group], NT_DIMS,
                            preferred_element_type=F32)
        s = s + bias_ref[variant, t].reshape((B_Q_HEADS // 2) * tq, w)
        m = jnp.maximum(_row_max(s), sinks[t])
        p = _exp_rows(s, m)
        acc = jnp.dot(p, _with_ones(v_ref[0, pl.ds(near_start, w), group]),
                      preferred_element_type=F32)
        outs[u, t] = acc[:, :LANES] / (acc[:, LANES:] + jnp.exp2(sinks[t] - m))

    for u in range(nsub):
        for t in range(2):
            chain(u, t)

    for u in range(nsub):
        for c in range(B_Q_HEADS // 2):
            oa = outs[u, 0][c * tq:(c + 1) * tq]
            ob = outs[u, 1][c * tq:(c + 1) * tq]
            oc = jnp.where(lo, oa, ob) if c < half else jnp.where(lo, ob, oa)
            o_ref[0, u * tq:(u + 1) * tq, c * LANES:(c + 1) * LANES] = oc.astype(BF16)


def _attn_b(q, k, v, bias, sinks, layer, *, tq, w, back, i_off, nsub):
    b, sq, _ = q.shape
    sk = k.shape[1]
    rows = nsub * tq
    assert sq % rows == 0 and (sq // tq - 1 + i_off) * tq - back + w <= sk
    body = functools.partial(_attn_b_body, layer=layer, tq=tq, w=w, back=back, i_off=i_off,
                             nsub=nsub)
    return pl.pallas_call(
        body,
        grid=(b, sq // rows),
        in_specs=[pl.BlockSpec(memory_space=pltpu.SMEM),
                  pl.BlockSpec((1, rows, B_Q), lambda bi, i: (bi, i, 0)),
                  pl.BlockSpec((1, sk, 2 * B_KV), lambda bi, i: (bi, 0, 0)),
                  pl.BlockSpec((1, sk, 2 * B_KV), lambda bi, i: (bi, 0, 0)),
                  pl.BlockSpec((2, 2, B_Q_HEADS // 2, tq, w), lambda bi, i: (0, 0, 0, 0, 0))],
        out_specs=pl.BlockSpec((1, rows, B_Q), lambda bi, i: (bi, i, 0)),
        out_shape=jax.ShapeDtypeStruct((b, sq, B_Q), BF16),
        compiler_params=_params(2),
        name="attn_b",
    )(sinks, q, k, v, bias)


def _post_body(x_ref, o_ref, wo_ref, g_ref, wu_ref, wd_ref, gf_ref, y_ref, *refs, final, cast_weights):
    hn_ref = refs[-1]
    f = pl.program_id(1)
    if cast_weights:
        wob_ref, wub_ref, wdb_ref = refs[:3]
        wu = wu_ref[...].astype(BF16)
        wd = wd_ref[...].astype(BF16)
        wub_ref[...] = wu
        wdb_ref[...] = wd
    else:
        wu, wd = wu_ref[...], wd_ref[...]

    @pl.when(f == 0)
    def _():
        wo = wo_ref[...]
        if cast_weights:
            wo = wo.astype(BF16)
            wob_ref[...] = wo
        x1 = x_ref[...] + jnp.dot(o_ref[...], wo, preferred_element_type=F32)
        y_ref[...] = x1
        hn_ref[...] = _rms(x1, g_ref[...]).astype(BF16)

    u = jnp.dot(hn_ref[...], wu, preferred_element_type=F32)
    a = jnp.square(jnp.maximum(u, 0.0)).astype(BF16)
    y_ref[...] += jnp.dot(a, wd, preferred_element_type=F32)

    if final:
        @pl.when(f == pl.num_programs(1) - 1)
        def _():
            y_ref[...] = _rms(y_ref[...], gf_ref[...])


def _post(x, o, wo, wo_layer, g, g_layer, wu, wd, w_layer, gf, *, tm, tf, final, cast_weights):
    m = x.shape[0]
    assert not cast_weights or m == tm
    row = lambda i, f: (i, 0)
    full = lambda i, f: (0, 0)
    out_specs = [pl.BlockSpec((tm, D_MODEL), row)]
    out_shape = [jax.ShapeDtypeStruct((m, D_MODEL), F32)]
    if cast_weights:
        out_specs += [pl.BlockSpec((None, D_MODEL, D_MODEL), lambda i, f: (0, 0, 0)),
                      pl.BlockSpec((None, D_MODEL, tf), lambda i, f: (0, 0, f)),
                      pl.BlockSpec((None, tf, D_MODEL), lambda i, f: (0, f, 0))]
        out_shape += [jax.ShapeDtypeStruct((1, D_MODEL, D_MODEL), BF16),
                      jax.ShapeDtypeStruct((1, D_MODEL, D_FF), BF16),
                      jax.ShapeDtypeStruct((1, D_FF, D_MODEL), BF16)]
    outs = pl.pallas_call(
        functools.partial(_post_body, final=final, cast_weights=cast_weights),
        grid=(m // tm, D_FF // tf),
        in_specs=[pl.BlockSpec((tm, D_MODEL), row),
                  pl.BlockSpec((tm, D_MODEL), row),
                  pl.BlockSpec((None, D_MODEL, D_MODEL), lambda i, f: (wo_layer, 0, 0)),
                  pl.BlockSpec((None, 1, D_MODEL), lambda i, f: (g_layer, 0, 0)),
                  pl.BlockSpec((None, D_MODEL, tf), lambda i, f: (w_layer, 0, f)),
                  pl.BlockSpec((None, tf, D_MODEL), lambda i, f: (w_layer, f, 0)),
                  pl.BlockSpec((1, D_MODEL), full)],
        out_specs=out_specs,
        out_shape=out_shape,
        scratch_shapes=[pltpu.VMEM((tm, D_MODEL), BF16)],
        compiler_params=_params(2),
        name="post",
    )(x, o, wo, g, wu, wd, gf)
    return outs if cast_weights else outs[0]


_B_PROMPT = dict(tq=128, w=256, back=128)
_B_SAMPLE = dict(tq=64, w=256, back=128)
B_TILES_PER_STEP = 8
_A_SAMPLE_BACK = 128


def _a_prompt_bias(table):
    _check_far_bucket(A_TILE + 1)
    near = _near_bias(table, A_TILE, 2 * A_TILE, A_TILE, None)
    return jnp.stack([_far_bias(table, A_TILE, A_TILE), near[:, :, :A_TILE], near[:, :, A_TILE:]])


def _a_sample_bias(table, past, t):
    assert t <= LANES
    _check_far_bucket(_A_SAMPLE_BACK + 1)
    far = _far_bias(table, t, past - _A_SAMPLE_BACK)
    near = _near_bias(table, t, _A_SAMPLE_BACK + LANES, _A_SAMPLE_BACK, None)
    return jnp.concatenate([far, near], axis=2)


def _b_bias(table, tq, w, back):
    def one(bk):
        t = _near_bias(table, tq, w, bk, WINDOW // CHUNK)
        return jnp.stack([jnp.stack([t[h] for h in order]) for order in (_B_ORDER_A, _B_ORDER_B)])
    return jnp.stack([one(0), one(back)])


def _pad_keys(t, total):
    return jnp.pad(t, ((0, 0), (0, total - t.shape[1]), (0, 0)))


def _trunk(x, caches, wts, table, bf16_weights, *, tm, tm_post, tf):
    b, s, _ = x.shape
    sample = caches is not None
    n_a = wts["a_w_qkv"].shape[0]
    xm = x.reshape(b * s, D_MODEL)
    b_cfg = _B_SAMPLE if sample else _B_PROMPT
    b_bias = _b_bias(table, **b_cfg)
    if sample:
        a_bias = _a_sample_bias(table, caches["a_kt"].shape[3], s)
    else:
        a_bias = _a_prompt_bias(table)
    ak, av, bk, bv = [], [], [], []
    kt_buf = v_buf = None
    for i in range(DEPTH):
        j = i // 2
        g_mix = wts["norm_mix_g"][i][None, :]
        if sample:
            bf16_weights.append({})
        cast = bf16_weights[i]
        if i % 2 == 0:
            lam_init = _lambda_init(i)
            if sample:
                q, k, v, kb, vb, cast["qkv"], cast["kt"] = _proj_a_sample(xm, g_mix, wts["a_w_qkv"], j)
                ak.append(k.reshape(b, s, A_HEADS, 2, A_DK))
                av.append(v.reshape(b, s, A_HEADS, A_DV))
                o = _attn_a_sample(q.reshape(b, s, A_QK), caches["a_kt"], caches["a_v"],
                                   kb.reshape(b, s, A_QK), vb.reshape(b, s, A_V), a_bias,
                                   wts["a_lambda"], wts["a_subln_g"], j, lam_init=lam_init)
            else:
                q, kt_buf, v_buf, ktb, vb = _proj_a_prompt(
                    xm, g_mix, cast["qkv"], cast["kt"], 0, j, kt_buf, v_buf,
                    b=b, s=s, tm=tm, n_a=n_a)
                o = _attn_a_prompt(q.reshape(b, s, A_QK), ktb, vb.reshape(b, s, A_V), a_bias,
                                   wts["a_lambda"], wts["a_subln_g"], j, lam_init=lam_init)
            wo, wo_layer = wts["a_w_o"], j
        else:
            if sample:
                q, k, v, kb, vb, cast["qkv"] = _proj_b(xm, g_mix, wts["b_w_qkv"], j, tm,
                                                       cast_weights=True)
            else:
                q, k, v, kb, vb = _proj_b(xm, g_mix, cast["qkv"], 0, tm, seq=s, tail=WINDOW)
            k = k.reshape(b, -1, B_KV_HEADS, B_HD)
            v = v.reshape(b, -1, B_KV_HEADS, B_HD)
            kb = kb.reshape(b, s, 2 * B_KV)
            vb = vb.reshape(b, s, 2 * B_KV)
            i_off = 0
            if sample:
                ck, cv, ckb, cvb = caches["b"][j]
                bk.append(jnp.concatenate([ck, k], axis=1)[:, s:])
                bv.append(jnp.concatenate([cv, v], axis=1)[:, s:])
                i_off = ckb.shape[1] // b_cfg["tq"]
                total = ckb.shape[1] - b_cfg["back"] + b_cfg["w"]
                kb = _pad_keys(jnp.concatenate([ckb, kb], axis=1), total)
                vb = _pad_keys(jnp.concatenate([cvb, vb], axis=1), total)
            else:
                bk.append(k)
                bv.append(v)
            o = _attn_b(q.reshape(b, s, B_Q), kb, vb, b_bias, wts["b_sinks"], j, i_off=i_off,
                        nsub=1 if sample else B_TILES_PER_STEP, **b_cfg)
            wo, wo_layer = wts["b_w_o"], j
        post = functools.partial(_post, xm, o.reshape(b * s, D_MODEL), tm=tm_post, tf=tf,
                                 final=(i == DEPTH - 1), cast_weights=sample)
        gf = wts["final_norm_g"][None, :]
        if sample:
            xm, cast["wo"], cast["wu"], cast["wd"] = post(
                wo, wo_layer, wts["norm_mlp_g"], i, wts["mlp_w_up"], wts["mlp_w_down"], i, gf)
        else:
            xm = post(cast["wo"], 0, wts["norm_mlp_g"], i, cast["wu"], cast["wd"], 0, gf)
    if sample:
        a_k, a_v = jnp.stack(ak), jnp.stack(av)
    else:
        a_k = jnp.transpose(kt_buf.reshape(n_a, b, A_HEADS, 2, A_DK, s), (0, 1, 5, 2, 3, 4))
        a_v = v_buf.reshape(n_a, b, s, A_HEADS, A_DV)
    return xm.reshape(b, s, D_MODEL), a_k, a_v, jnp.stack(bk), jnp.stack(bv)


def _swap_halves(t):
    flat = t.reshape(t.shape[:-2] + (B_KV,))
    swapped = t[..., ::-1, :].reshape(t.shape[:-2] + (B_KV,))
    return jnp.concatenate([flat, swapped], axis=-1).astype(BF16)


def kernel(x_prompt, x_sample, cache_a_k, cache_a_v, cache_b_k, cache_b_v, rel_table,
           norm_mix_g, norm_mlp_g, final_norm_g, a_w_qkv, a_lambda, a_subln_g, a_w_o,
           b_w_qkv, b_sinks, b_w_o, mlp_w_up, mlp_w_down):
    wts = dict(norm_mix_g=norm_mix_g, norm_mlp_g=norm_mlp_g[:, None, :], final_norm_g=final_norm_g,
               a_w_qkv=a_w_qkv, a_lambda=a_lambda, a_subln_g=a_subln_g[:, None, :],
               a_w_o=a_w_o, b_w_qkv=b_w_qkv, b_sinks=b_sinks,
               b_w_o=b_w_o, mlp_w_up=mlp_w_up, mlp_w_down=mlp_w_down)
    n_a, db, past = cache_a_k.shape[:3]
    caches = dict(
        a_kt=jnp.transpose(cache_a_k, (0, 1, 3, 4, 5, 2)).reshape(n_a, db, A_QK, past),
        a_v=cache_a_v.reshape(n_a, db, past * A_HEADS, A_DV),
        b=[(cache_b_k[j], cache_b_v[j], _swap_halves(cache_b_k[j]), _swap_halves(cache_b_v[j]))
           for j in range(cache_b_k.shape[0])])
    bf16_weights = []
    ys, aks, avs, bks, bvs = _trunk(x_sample, caches, wts, rel_table, bf16_weights,
                                    tm=512, tm_post=512, tf=1024)
    yp, akp, avp, bkp, bvp = _trunk(x_prompt, None, wts, rel_table, bf16_weights,
                                    tm=512, tm_post=1024, tf=1024)
    return (yp, ys, akp, avp, bkp, bvp, aks, avs, bks, bvs)
```

```python
import functools
import math

import jax
import jax.numpy as jnp
from jax import lax
from jax.experimental import pallas as pl
from jax.experimental.pallas import tpu as pltpu

BF16 = jnp.bfloat16
F32 = jnp.float32

D_MODEL = 1024
DEPTH = 4
CHUNK = 64
A_HEADS = 8
A_DK = 64
A_DV = 128
A_QK = A_HEADS * 2 * A_DK
A_V = A_HEADS * A_DV
B_Q_HEADS = 16
B_KV_HEADS = 2
B_HD = 64
B_Q = B_Q_HEADS * B_HD
B_KV = B_KV_HEADS * B_HD
WINDOW = 128
NUM_BUCKETS = 32
MAX_DIST = 128
N_MAPS = 16
D_FF = 4 * D_MODEL
EPS = 1e-6
NEG = -1e30

LANES = 128
SUBLANES = 8
HALF = LANES // 2
VMEM_LIMIT = 56 * 1024 * 1024

NT_DIMS = (((1,), (1,)), ((), ()))

LOG2E = math.log2(math.e)


def _lambda_init(layer):
    return 0.8 - 0.6 * math.exp(-0.3 * layer)


def _rms(x, g):
    return x * lax.rsqrt(jnp.mean(x * x, axis=-1, keepdims=True) + EPS) * g


def _params(n_axes):
    return pltpu.CompilerParams(
        dimension_semantics=("arbitrary",) * n_axes, vmem_limit_bytes=VMEM_LIMIT)


def _lane_fold_max(s):
    mx = s[:, 0:LANES]
    for gi in range(1, s.shape[1] // LANES):
        mx = jnp.maximum(mx, s[:, gi * LANES:(gi + 1) * LANES])
    return mx


def _row_max(s):
    return jnp.broadcast_to(jnp.max(_lane_fold_max(s), axis=-1, keepdims=True), (s.shape[0], LANES))


def _exp_rows(s, m):
    return jnp.concatenate([jnp.exp2(s[:, gi * LANES:(gi + 1) * LANES] - m)
                            for gi in range(s.shape[1] // LANES)], axis=1).astype(BF16)


def _with_ones(v):
    return jnp.concatenate([v, jnp.ones(v.shape, v.dtype)], axis=1)


def _proj_a_prompt_body(x_ref, g_ref, w_ref, wkt_ref, *refs, first):
    q_ref, kt_ref, v_ref, ktb_ref, vb_ref = refs[-5:]
    h = _rms(x_ref[...], g_ref[...]).astype(BF16)
    tm = h.shape[0]
    if first:
        for slot in range(1, kt_ref.shape[0]):
            kt_ref[slot] = jnp.zeros(kt_ref.shape[1:], F32)
            v_ref[slot] = jnp.zeros(v_ref.shape[1:], F32)
        kt_ref, v_ref = kt_ref.at[0], v_ref.at[0]
    v = jnp.dot(h, w_ref[:, 2 * A_QK:], preferred_element_type=F32)
    vb_ref[...] = v.astype(BF16)
    for hh in range(A_HEADS):
        v_ref[pl.ds(hh, tm, stride=A_HEADS), :] = v[:, hh * A_DV:(hh + 1) * A_DV]
    kt = lax.dot_general(wkt_ref[...], h, NT_DIMS, preferred_element_type=F32)
    ktb_ref[...] = kt.astype(BF16)
    kt_ref[...] = kt
    q = jnp.dot(h, w_ref[:, 0:A_QK], preferred_element_type=F32)
    q_ref[...] = (q * (A_DK ** -0.5 * LOG2E)).astype(BF16)


def _proj_a_prompt(x, g, w, wkt, w_layer, layer, kt_prev, v_prev, *, b, s, tm, n_a):
    m = b * s
    per_b = s // tm
    row = lambda i: (i, 0)
    full = lambda i: (0, 0)
    stacked = lambda i: (w_layer, 0, 0)
    first = kt_prev is None
    assert first == (layer == 0)
    in_specs = [pl.BlockSpec((tm, D_MODEL), row),
                pl.BlockSpec((1, D_MODEL), full),
                pl.BlockSpec((None, D_MODEL, 2 * A_QK + A_V), stacked),
                pl.BlockSpec((None, A_QK, D_MODEL), stacked)]
    operands = [x, g, w, wkt]
    if first:
        kt_spec = pl.BlockSpec((n_a, None, A_QK, tm), lambda i: (0, i // per_b, 0, i % per_b))
        v_spec = pl.BlockSpec((n_a, tm * A_HEADS, A_DV), lambda i: (0, i, 0))
        aliases = {}
    else:
        kt_spec = pl.BlockSpec((None, None, A_QK, tm), lambda i: (layer, i // per_b, 0, i % per_b))
        v_spec = pl.BlockSpec((None, tm * A_HEADS, A_DV), lambda i: (layer, i, 0))
        in_specs += [pl.BlockSpec(memory_space=pl.ANY)] * 2
        operands += [kt_prev, v_prev]
        aliases = {4: 1, 5: 2}
    return pl.pallas_call(
        functools.partial(_proj_a_prompt_body, first=first),
        grid=(m // tm,),
        in_specs=in_specs,
        out_specs=[pl.BlockSpec((tm, A_QK), row),
                   kt_spec,
                   v_spec,
                   pl.BlockSpec((None, A_QK, tm), lambda i: (i // per_b, 0, i % per_b)),
                   pl.BlockSpec((tm, A_V), row)],
        out_shape=[jax.ShapeDtypeStruct((m, A_QK), BF16),
                   jax.ShapeDtypeStruct((n_a, b, A_QK, s), F32),
                   jax.ShapeDtypeStruct((n_a, m * A_HEADS, A_DV), F32),
                   jax.ShapeDtypeStruct((b, A_QK, s), BF16),
                   jax.ShapeDtypeStruct((m, A_V), BF16)],
        input_output_aliases=aliases,
        compiler_params=_params(1),
        name="proj_a_prompt",
    )(*operands)


def _proj_a_sample_body(x_ref, g_ref, w_ref, q_ref, k_ref, v_ref, kb_ref, vb_ref, wb_ref, wkt_ref):
    wb_ref[...] = w_ref[...].astype(BF16)
    wkt_ref[...] = w_ref[:, A_QK:2 * A_QK].T.astype(BF16)
    h = _rms(x_ref[...], g_ref[...]).astype(BF16)
    q = jnp.dot(h, wb_ref[:, 0:A_QK], preferred_element_type=F32)
    q_ref[...] = (q * (A_DK ** -0.5 * LOG2E)).astype(BF16)
    k = jnp.dot(h, wb_ref[:, A_QK:2 * A_QK], preferred_element_type=F32)
    k_ref[...] = k
    kb_ref[...] = k.astype(BF16)
    v = jnp.dot(h, wb_ref[:, 2 * A_QK:], preferred_element_type=F32)
    v_ref[...] = v
    vb_ref[...] = v.astype(BF16)


def _proj_a_sample(x, g, w, layer):
    m = x.shape[0]
    n_out = 2 * A_QK + A_V
    full = lambda i: (0, 0)
    full3 = lambda i: (0, 0, 0)
    return pl.pallas_call(
        _proj_a_sample_body,
        grid=(1,),
        in_specs=[pl.BlockSpec((m, D_MODEL), full),
                  pl.BlockSpec((1, D_MODEL), full),
                  pl.BlockSpec((None, D_MODEL, n_out), lambda i: (layer, 0, 0))],
        out_specs=[pl.BlockSpec((m, A_QK), full),
                   pl.BlockSpec((m, A_QK), full),
                   pl.BlockSpec((m, A_V), full),
                   pl.BlockSpec((m, A_QK), full),
                   pl.BlockSpec((m, A_V), full),
                   pl.BlockSpec((None, D_MODEL, n_out), full3),
                   pl.BlockSpec((None, A_QK, D_MODEL), full3)],
        out_shape=[jax.ShapeDtypeStruct((m, A_QK), BF16),
                   jax.ShapeDtypeStruct((m, A_QK), F32),
                   jax.ShapeDtypeStruct((m, A_V), F32),
                   jax.ShapeDtypeStruct((m, A_QK), BF16),
                   jax.ShapeDtypeStruct((m, A_V), BF16),
                   jax.ShapeDtypeStruct((1, D_MODEL, n_out), BF16),
                   jax.ShapeDtypeStruct((1, A_QK, D_MODEL), BF16)],
        compiler_params=_params(1),
        name="proj_a_sample",
    )(x, g, w)


def _proj_b_body(x_ref, g_ref, w_ref, q_ref, k_ref, v_ref, kb_ref, vb_ref, *wb_ref, tail):
    if wb_ref:
        wb_ref[0][...] = w_ref[...].astype(BF16)
        w_ref = wb_ref[0]
    h = _rms(x_ref[...], g_ref[...]).astype(BF16)
    kv = jnp.dot(h, w_ref[:, B_Q:], preferred_element_type=F32)
    k, v = kv[:, 0:B_KV], kv[:, B_KV:]
    kb_ref[...] = jnp.concatenate([k, pltpu.roll(k, HALF, axis=1)], axis=1).astype(BF16)
    vb_ref[...] = jnp.concatenate([v, pltpu.roll(v, HALF, axis=1)], axis=1).astype(BF16)
    keep = k.shape[0] if tail is None else tail
    k_ref[...] = k[k.shape[0] - keep:]
    v_ref[...] = v[v.shape[0] - keep:]
    q = jnp.dot(h, w_ref[:, 0:B_Q], preferred_element_type=F32)
    q_ref[...] = (q * (B_HD ** -0.5 * LOG2E)).astype(BF16)


def _proj_b(x, g, w, layer, tm, *, seq=None, tail=None, cast_weights=False):
    m = x.shape[0]
    assert not cast_weights or m == tm
    n_out = B_Q + 2 * B_KV
    row = lambda i: (i, 0)
    full = lambda i: (0, 0)
    wb_spec = [pl.BlockSpec((None, D_MODEL, n_out), lambda i: (0, 0, 0))] if cast_weights else []
    wb_shape = [jax.ShapeDtypeStruct((1, D_MODEL, n_out), BF16)] if cast_weights else []
    if tail is None:
        kv_spec = pl.BlockSpec((tm, B_KV), row)
        kv_shape = jax.ShapeDtypeStruct((m, B_KV), F32)
    else:
        assert seq % tm == 0 and tail <= tm
        per_seq = seq // tm
        kv_spec = pl.BlockSpec((None, tail, B_KV), lambda i: (i // per_seq, 0, 0))
        kv_shape = jax.ShapeDtypeStruct((m // seq, tail, B_KV), F32)
    return pl.pallas_call(
        functools.partial(_proj_b_body, tail=tail),
        grid=(m // tm,),
        in_specs=[pl.BlockSpec((tm, D_MODEL), row),
                  pl.BlockSpec((1, D_MODEL), full),
                  pl.BlockSpec((None, D_MODEL, B_Q + 2 * B_KV), lambda i: (layer, 0, 0))],
        out_specs=[pl.BlockSpec((tm, B_Q), row),
                   kv_spec,
                   kv_spec,
                   pl.BlockSpec((tm, 2 * B_KV), row),
                   pl.BlockSpec((tm, 2 * B_KV), row)] + wb_spec,
        out_shape=[jax.ShapeDtypeStruct((m, B_Q), BF16),
                   kv_shape,
                   kv_shape,
                   jax.ShapeDtypeStruct((m, 2 * B_KV), BF16),
                   jax.ShapeDtypeStruct((m, 2 * B_KV), BF16)] + wb_shape,
        compiler_params=_params(1),
        name="proj_b",
    )(x, g, w)


def _t5_bucket(rel):
    nb = NUM_BUCKETS // 2
    n = -rel
    ret = jnp.where(n < 0, nb, 0)
    n = jnp.abs(n)
    max_exact = nb // 2
    nf = jnp.maximum(n, 1).astype(jnp.float32)
    large = max_exact + (jnp.log(nf / max_exact) / math.log(MAX_DIST / max_exact)
                         * (nb - max_exact)).astype(jnp.int32)
    large = jnp.minimum(large, nb - 1)
    return ret + jnp.where(n < max_exact, n, large)


def _check_far_bucket(min_dist):
    nb, max_exact = NUM_BUCKETS // 2, NUM_BUCKETS // 4
    ratio = math.log(min_dist / max_exact) / math.log(MAX_DIST / max_exact)
    assert ratio * (nb - max_exact) >= nb - 1 - max_exact + 1e-3, "far keys need one shared bucket"


def _far_bias(table, tq, w):
    far = table[NUM_BUCKETS // 2 - 1].astype(F32) * LOG2E
    return jnp.broadcast_to(far[:, None, None], (N_MAPS, tq, w))


def _near_bias(table, tq, w, back, band):
    r = jnp.arange(tq)[:, None]
    j = jnp.arange(w)[None, :] - back
    onehot = (_t5_bucket(j - r)[None] == jnp.arange(NUM_BUCKETS)[:, None, None]).astype(F32)
    onehot = lax.optimization_barrier(onehot)
    bias = jnp.einsum("bm,brj->mrj", table.astype(F32), onehot, precision=lax.Precision.HIGHEST)
    kc, qc = j // CHUNK, r // CHUNK
    mask = kc <= qc
    if band is not None:
        mask = mask & (kc >= qc - band)
    return jnp.where(mask[None], bias * LOG2E, NEG)


A_TILE = 256
A_ROWS = 64
A_HEADS_PER_STEP = 1


def _diff_lambda(lamp_ref, lam_init):
    lp = lamp_ref[...]
    return (jnp.exp(jnp.sum(lp[0:1] * lp[1:2], axis=-1, keepdims=True))
            - jnp.exp(jnp.sum(lp[2:3] * lp[3:4], axis=-1, keepdims=True)) + lam_init)


def _stack_maps(q):
    lane = lax.broadcasted_iota(jnp.int32, q.shape, 1)
    zero = jnp.zeros_like(q)
    return jnp.concatenate([jnp.where(lane < HALF, q, zero), jnp.where(lane >= HALF, q, zero)], axis=0)


def _diff_epilogue(acc, lam, g, lam_init):
    tq = acc.shape[0] // 2
    o_all = acc[:, :A_DV] / acc[:, A_DV:]
    o = o_all[:tq] - lam * o_all[tq:]
    return (_rms(o, g) * (1.0 - lam_init)).astype(BF16)


def _attn_a_scores(i, q, kt_ref, bias_ref, s_ref, m_ref):
    tl = A_TILE
    qs = _stack_maps(q)
    n_far = max(i - 1, 0)
    for kt in range(i + 1):
        cols = slice(kt * tl, (kt + 1) * tl)
        s = jnp.dot(qs, kt_ref[:, cols], preferred_element_type=F32)
        near = kt >= n_far
        if near:
            s = s + bias_ref[2 - (i - kt)].reshape(2 * tl, tl)
        s_ref[:, cols] = s
        fold = _lane_fold_max(s)
        first = kt == (n_far if near else 0)
        m_ref[int(near)] = fold if first else jnp.maximum(m_ref[int(near)], fold)
        yield
    for r in range(0, 2 * tl, A_ROWS):
        rows = slice(r, r + A_ROWS)
        mx = m_ref[1, rows, :]
        if n_far:
            c = bias_ref[0, r // tl, r % tl:r % tl + A_ROWS, 0:LANES]
            mx = jnp.maximum(mx, m_ref[0, rows, :] + c)
        m = jnp.broadcast_to(jnp.max(mx, axis=-1, keepdims=True), (A_ROWS, LANES))
        m_ref[1, rows, :] = m
        if n_far:
            m_ref[0, rows, :] = m - c


def _attn_a_numerators(i, s_ref, m_ref, v1_ref, out):
    tl = A_TILE
    n_far = max(i - 1, 0)
    acc = None
    for kt in range(i + 1):
        cols = slice(kt * tl, (kt + 1) * tl)
        p = _exp_rows(s_ref[:, cols], m_ref[1 if kt >= n_far else 0])
        part = jnp.dot(p, v1_ref[cols, :], preferred_element_type=F32)
        acc = part if acc is None else acc + part
        yield
    out.append(acc)


def _interleave(*steps):
    steps = list(steps)
    while steps:
        for gen in list(steps):
            if next(gen, StopIteration) is StopIteration:
                steps.remove(gen)


def _attn_a_prompt_body(lamp_ref, q_ref, kt_ref, v_ref, bias_ref, g_ref, o_ref,
                        s_ref, m_ref, v1_ref, *, nt, nh, lam_init):
    tl = A_TILE
    lam = _diff_lambda(lamp_ref, lam_init)
    for hh in range(nh):
        v1_ref[hh, :, 0:A_DV] = v_ref[0, :, hh * A_DV:(hh + 1) * A_DV]
        v1_ref[hh, :, A_DV:] = jnp.ones((nt * tl, A_DV), BF16)
    work = [(i, hh) for i in reversed(range(nt)) for hh in range(nh)]

    def scores(i, hh):
        lanes = slice(hh * LANES, (hh + 1) * LANES)
        return _attn_a_scores(i, q_ref[0, i * tl:(i + 1) * tl, lanes], kt_ref.at[0, lanes, :],
                              bias_ref.at[:, 2 * hh:2 * hh + 2], s_ref.at[hh, i % 2],
                              m_ref.at[hh, i % 2])

    _interleave(scores(*work[0]))
    for n, (i, hh) in enumerate(work):
        out = []
        steps = [_attn_a_numerators(i, s_ref.at[hh, i % 2], m_ref.at[hh, i % 2], v1_ref.at[hh], out)]
        if n + 1 < len(work):
            steps.insert(0, scores(*work[n + 1]))
        _interleave(*steps)
        acc = out[0]
        o_ref[0, i * tl:(i + 1) * tl, hh * LANES:(hh + 1) * LANES] = _diff_epilogue(
            acc, lam, g_ref[...], lam_init)


def _attn_a_prompt(q, kt, v, bias, lamp, g, layer, *, lam_init):
    b, s, _ = q.shape
    nt = s // A_TILE
    nh = A_HEADS_PER_STEP
    assert s % A_TILE == 0 and A_HEADS % nh == 0
    seq = pl.BlockSpec((1, s, nh * LANES), lambda h, bi: (bi, 0, h))
    return pl.pallas_call(
        functools.partial(_attn_a_prompt_body, nt=nt, nh=nh, lam_init=lam_init),
        grid=(A_HEADS // nh, b),
        in_specs=[pl.BlockSpec((None, 4, A_DK), lambda h, bi: (layer, 0, 0)),
                  seq,
                  pl.BlockSpec((1, nh * LANES, s), lambda h, bi: (bi, h, 0)),
                  seq,
                  pl.BlockSpec((3, 2 * nh, A_TILE, A_TILE), lambda h, bi: (0, h, 0, 0)),
                  pl.BlockSpec((None, 1, A_DV), lambda h, bi: (layer, 0, 0))],
        out_specs=seq,
        out_shape=jax.ShapeDtypeStruct((b, s, A_V), BF16),
        scratch_shapes=[pltpu.VMEM((nh, 2, 2 * A_TILE, s), F32),
                        pltpu.VMEM((nh, 2, 2, 2 * A_TILE, LANES), F32),
                        pltpu.VMEM((nh, s, 2 * A_DV), BF16)],
        compiler_params=_params(2),
        name="attn_a_prompt",
    )(lamp, q, kt, v, bias, g)


def _attn_a_sample_body(lamp_ref, q_ref, ktc_ref, vc_ref, kn_ref, vn_ref, bias_ref, g_ref, o_ref,
                        *, lam_init):
    tq = q_ref.shape[1]
    past = ktc_ref.shape[1]
    lam = _diff_lambda(lamp_ref, lam_init)
    for h in range(A_HEADS):
        cols = slice(h * LANES, (h + 1) * LANES)
        qs = _stack_maps(q_ref[0, :, cols])
        bias = bias_ref[2 * h:2 * h + 2].reshape(2 * tq, past + LANES)
        s_c = jnp.dot(qs, ktc_ref[cols, :].astype(BF16), preferred_element_type=F32)
        s_c = s_c + bias[:, 0:past]
        s_n = lax.dot_general(qs, kn_ref[0, :, cols], NT_DIMS, preferred_element_type=F32)
        s_n = s_n + bias[:, past:past + tq]
        m = jnp.maximum(_row_max(s_c), jnp.broadcast_to(jnp.max(s_n, axis=-1, keepdims=True),
                                                        (2 * tq, LANES)))
        p_c = _exp_rows(s_c, m)
        p_n = jnp.exp2(s_n - m[:, 0:tq]).astype(BF16)
        v_c = vc_ref[pl.ds(h, past, stride=A_HEADS), :].astype(BF16)
        acc = (jnp.dot(p_c, _with_ones(v_c), preferred_element_type=F32)
               + jnp.dot(p_n, _with_ones(vn_ref[0, :, cols]), preferred_element_type=F32))
        o_ref[0, :, cols] = _diff_epilogue(acc, lam, g_ref[...], lam_init)


def _attn_a_sample(q, ktc, vc, kn, vn, bias, lamp, g, layer, *, lam_init):
    b, tq, _ = q.shape
    past = ktc.shape[3]
    new = lambda bi: (bi, 0, 0)
    return pl.pallas_call(
        functools.partial(_attn_a_sample_body, lam_init=lam_init),
        grid=(b,),
        in_specs=[pl.BlockSpec((None, 4, A_DK), lambda bi: (layer, 0, 0)),
                  pl.BlockSpec((1, tq, A_QK), new),
                  pl.BlockSpec((None, None, A_QK, past), lambda bi: (layer, bi, 0, 0)),
                  pl.BlockSpec((None, None, past * A_HEADS, A_DV), lambda bi: (layer, bi, 0, 0)),
                  pl.BlockSpec((1, tq, A_QK), new),
                  pl.BlockSpec((1, tq, A_V), new),
                  pl.BlockSpec((N_MAPS, tq, past + LANES), lambda bi: (0, 0, 0)),
                  pl.BlockSpec((None, 1, A_DV), lambda bi: (layer, 0, 0))],
        out_specs=pl.BlockSpec((1, tq, A_V), new),
        out_shape=jax.ShapeDtypeStruct((b, tq, A_V), BF16),
        compiler_params=_params(1),
        name="attn_a_sample",
    )(lamp, q, ktc, vc, kn, vn, bias, g)


_B_ORDER_A = (0, 2, 4, 6, 9, 11, 13, 15)
_B_ORDER_B = (1, 3, 5, 7, 8, 10, 12, 14)


def _attn_b_body(sink_ref, q_ref, k_ref, v_ref, bias_ref, o_ref, *, layer, tq, w, back, i_off, nsub):
    lane = lax.broadcasted_iota(jnp.int32, (tq, LANES), 1)
    lo = lane < HALF
    half = B_Q_HEADS // 4
    orders = (_B_ORDER_A, _B_ORDER_B)
    sinks = [jnp.concatenate([jnp.full((tq, LANES), sink_ref[layer, h] * LOG2E, F32)
                              for h in orders[t]], axis=0) for t in range(2)]
    outs = {}

    def chain(u, t):
        tile = pl.program_id(1) * nsub + u + i_off
        variant = 1 if (u > 0 or i_off > 0) else jnp.minimum(tile, 1)
        near_start = pl.multiple_of(jnp.maximum(tile * tq - back, 0), CHUNK)
        group = slice(t * LANES, (t + 1) * LANES)
        pieces = []
        for c in range(B_Q_HEADS // 2):
            qc = q_ref[0, u * tq:(u + 1) * tq, c * LANES:(c + 1) * LANES]
            keep_lo = (c < half) == (t == 0)
            pieces.append(jnp.where(lo == keep_lo, qc, jnp.zeros_like(qc)))
        qs = jnp.concatenate(pieces, axis=0)
        s = lax.dot_general(qs, k_ref[0, pl.ds(near_start, w), group], NT_DIMS,
                            preferred_element_type=F32)
        s = s + bias_ref[variant, t].reshape((B_Q_HEADS // 2) * tq, w)
        m = jnp.maximum(_row_max(s), sinks[t])
        p = _exp_rows(s, m)
        acc = jnp.dot(p, _with_ones(v_ref[0, pl.ds(near_start, w), group]),
                      preferred_element_type=F32)
        outs[u, t] = acc[:, :LANES] / (acc[:, LANES:] + jnp.exp2(sinks[t] - m))

    for u in range(nsub):
        for t in range(2):
            chain(u, t)

    for u in range(nsub):
        for c in range(B_Q_HEADS // 2):
            oa = outs[u, 0][c * tq:(c + 1) * tq]
            ob = outs[u, 1][c * tq:(c + 1) * tq]
            oc = jnp.where(lo, oa, ob) if c < half else jnp.where(lo, ob, oa)
            o_ref[0, u * tq:(u + 1) * tq, c * LANES:(c + 1) * LANES] = oc.astype(BF16)


def _attn_b(q, k, v, bias, sinks, layer, *, tq, w, back, i_off, nsub):
    b, sq, _ = q.shape
    sk = k.shape[1]
    rows = nsub * tq
    assert sq % rows == 0 and (sq // tq - 1 + i_off) * tq - back + w <= sk
    body = functools.partial(_attn_b_body, layer=layer, tq=tq, w=w, back=back, i_off=i_off,
                             nsub=nsub)
    return pl.pallas_call(
        body,
        grid=(b, sq // rows),
        in_specs=[pl.BlockSpec(memory_space=pltpu.SMEM),
                  pl.BlockSpec((1, rows, B_Q), lambda bi, i: (bi, i, 0)),
                  pl.BlockSpec((1, sk, 2 * B_KV), lambda bi, i: (bi, 0, 0)),
                  pl.BlockSpec((1, sk, 2 * B_KV), lambda bi, i: (bi, 0, 0)),
                  pl.BlockSpec((2, 2, B_Q_HEADS // 2, tq, w), lambda bi, i: (0, 0, 0, 0, 0))],
        out_specs=pl.BlockSpec((1, rows, B_Q), lambda bi, i: (bi, i, 0)),
        out_shape=jax.ShapeDtypeStruct((b, sq, B_Q), BF16),
        compiler_params=_params(2),
        name="attn_b",
    )(sinks, q, k, v, bias)


def _post_body(x_ref, o_ref, wo_ref, g_ref, wu_ref, wd_ref, gf_ref, y_ref, *refs, final, cast_weights):
    hn_ref = refs[-1]
    f = pl.program_id(1)
    if cast_weights:
        wob_ref, wub_ref, wdb_ref = refs[:3]

    @pl.when(f == 0)
    def _():
        if cast_weights:
            wob_ref[...] = wo_ref[...].astype(BF16)
        wo = (wob_ref if cast_weights else wo_ref)[...]
        x1 = x_ref[...] + jnp.dot(o_ref[...], wo, preferred_element_type=F32)
        y_ref[...] = x1
        hn_ref[...] = _rms(x1, g_ref[...]).astype(BF16)

    if cast_weights:
        wub_ref[...] = wu_ref[...].astype(BF16)
        wdb_ref[...] = wd_ref[...].astype(BF16)
        wu_ref, wd_ref = wub_ref, wdb_ref
    u = jnp.dot(hn_ref[...], wu_ref[...], preferred_element_type=F32)
    a = jnp.square(jnp.maximum(u, 0.0)).astype(BF16)
    y_ref[...] += jnp.dot(a, wd_ref[...], preferred_element_type=F32)

    if final:
        @pl.when(f == pl.num_programs(1) - 1)
        def _():
            y_ref[...] = _rms(y_ref[...], gf_ref[...])


def _post(x, o, wo, wo_layer, g, g_layer, wu, wd, w_layer, gf, *, tm, tf, final, cast_weights):
    m = x.shape[0]
    assert not cast_weights or m == tm
    row = lambda i, f: (i, 0)
    full = lambda i, f: (0, 0)
    out_specs = [pl.BlockSpec((tm, D_MODEL), row)]
    out_shape = [jax.ShapeDtypeStruct((m, D_MODEL), F32)]
    if cast_weights:
        out_specs += [pl.BlockSpec((None, D_MODEL, D_MODEL), lambda i, f: (0, 0, 0)),
                      pl.BlockSpec((None, D_MODEL, tf), lambda i, f: (0, 0, f)),
                      pl.BlockSpec((None, tf, D_MODEL), lambda i, f: (0, f, 0))]
        out_shape += [jax.ShapeDtypeStruct((1, D_MODEL, D_MODEL), BF16),
                      jax.ShapeDtypeStruct((1, D_MODEL, D_FF), BF16),
                      jax.ShapeDtypeStruct((1, D_FF, D_MODEL), BF16)]
    outs = pl.pallas_call(
        functools.partial(_post_body, final=final, cast_weights=cast_weights),
        grid=(m // tm, D_FF // tf),
        in_specs=[pl.BlockSpec((tm, D_MODEL), row),
                  pl.BlockSpec((tm, D_MODEL), row),
                  pl.BlockSpec((None, D_MODEL, D_MODEL), lambda i, f: (wo_layer, 0, 0)),
                  pl.BlockSpec((None, 1, D_MODEL), lambda i, f: (g_layer, 0, 0)),
                  pl.BlockSpec((None, D_MODEL, tf), lambda i, f: (w_layer, 0, f)),
                  pl.BlockSpec((None, tf, D_MODEL), lambda i, f: (w_layer, f, 0)),
                  pl.BlockSpec((1, D_MODEL), full)],
        out_specs=out_specs,
        out_shape=out_shape,
        scratch_shapes=[pltpu.VMEM((tm, D_MODEL), BF16)],
        compiler_params=_params(2),
        name="post",
    )(x, o, wo, g, wu, wd, gf)
    return outs if cast_weights else outs[0]


_B_PROMPT = dict(tq=128, w=256, back=128)
_B_SAMPLE = dict(tq=64, w=256, back=128)
B_TILES_PER_STEP = 8
_A_SAMPLE_BACK = 128


def _a_prompt_bias(table):
    _check_far_bucket(A_TILE + 1)
    near = _near_bias(table, A_TILE, 2 * A_TILE, A_TILE, None)
    return jnp.stack([_far_bias(table, A_TILE, A_TILE), near[:, :, :A_TILE], near[:, :, A_TILE:]])


def _a_sample_bias(table, past, t):
    assert t <= LANES
    _check_far_bucket(_A_SAMPLE_BACK + 1)
    far = _far_bias(table, t, past - _A_SAMPLE_BACK)
    near = _near_bias(table, t, _A_SAMPLE_BACK + LANES, _A_SAMPLE_BACK, None)
    return jnp.concatenate([far, near], axis=2)


def _b_bias(table, tq, w, back):
    def one(bk):
        t = _near_bias(table, tq, w, bk, WINDOW // CHUNK)
        return jnp.stack([jnp.stack([t[h] for h in order]) for order in (_B_ORDER_A, _B_ORDER_B)])
    return jnp.stack([one(0), one(back)])


def _pad_keys(t, total):
    return jnp.pad(t, ((0, 0), (0, total - t.shape[1]), (0, 0)))


def _trunk(x, caches, wts, table, bf16_weights, *, tm, tm_post, tf):
    b, s, _ = x.shape
    sample = caches is not None
    n_a = wts["a_w_qkv"].shape[0]
    xm = x.reshape(b * s, D_MODEL)
    b_cfg = _B_SAMPLE if sample else _B_PROMPT
    b_bias = _b_bias(table, **b_cfg)
    if sample:
        a_bias = _a_sample_bias(table, caches["a_kt"].shape[3], s)
    else:
        a_bias = _a_prompt_bias(table)
    ak, av, bk, bv = [], [], [], []
    kt_buf = v_buf = None
    for i in range(DEPTH):
        j = i // 2
        g_mix = wts["norm_mix_g"][i][None, :]
        if sample:
            bf16_weights.append({})
        cast = bf16_weights[i]
        if i % 2 == 0:
            lam_init = _lambda_init(i)
            if sample:
                q, k, v, kb, vb, cast["qkv"], cast["kt"] = _proj_a_sample(xm, g_mix, wts["a_w_qkv"], j)
                ak.append(k.reshape(b, s, A_HEADS, 2, A_DK))
                av.append(v.reshape(b, s, A_HEADS, A_DV))
                o = _attn_a_sample(q.reshape(b, s, A_QK), caches["a_kt"], caches["a_v"],
                                   kb.reshape(b, s, A_QK), vb.reshape(b, s, A_V), a_bias,
                                   wts["a_lambda"], wts["a_subln_g"], j, lam_init=lam_init)
            else:
                q, kt_buf, v_buf, ktb, vb = _proj_a_prompt(
                    xm, g_mix, cast["qkv"], cast["kt"], 0, j, kt_buf, v_buf,
                    b=b, s=s, tm=tm, n_a=n_a)
                o = _attn_a_prompt(q.reshape(b, s, A_QK), ktb, vb.reshape(b, s, A_V), a_bias,
                                   wts["a_lambda"], wts["a_subln_g"], j, lam_init=lam_init)
            wo, wo_layer = wts["a_w_o"], j
        else:
            if sample:
                q, k, v, kb, vb, cast["qkv"] = _proj_b(xm, g_mix, wts["b_w_qkv"], j, tm,
                                                       cast_weights=True)
            else:
                q, k, v, kb, vb = _proj_b(xm, g_mix, cast["qkv"], 0, tm, seq=s, tail=WINDOW)
            k = k.reshape(b, -1, B_KV_HEADS, B_HD)
            v = v.reshape(b, -1, B_KV_HEADS, B_HD)
            kb = kb.reshape(b, s, 2 * B_KV)
            vb = vb.reshape(b, s, 2 * B_KV)
            i_off = 0
            if sample:
                ck, cv, ckb, cvb = caches["b"][j]
                bk.append(jnp.concatenate([ck, k], axis=1)[:, s:])
                bv.append(jnp.concatenate([cv, v], axis=1)[:, s:])
                i_off = ckb.shape[1] // b_cfg["tq"]
                total = ckb.shape[1] - b_cfg["back"] + b_cfg["w"]
                kb = _pad_keys(jnp.concatenate([ckb, kb], axis=1), total)
                vb = _pad_keys(jnp.concatenate([cvb, vb], axis=1), total)
            else:
                bk.append(k)
                bv.append(v)
            o = _attn_b(q.reshape(b, s, B_Q), kb, vb, b_bias, wts["b_sinks"], j, i_off=i_off,
                        nsub=1 if sample else B_TILES_PER_STEP, **b_cfg)
            wo, wo_layer = wts["b_w_o"], j
        post = functools.partial(_post, xm, o.reshape(b * s, D_MODEL), tm=tm_post, tf=tf,
                                 final=(i == DEPTH - 1), cast_weights=sample)
        gf = wts["final_norm_g"][None, :]
        if sample:
            xm, cast["wo"], cast["wu"], cast["wd"] = post(
                wo, wo_layer, wts["norm_mlp_g"], i, wts["mlp_w_up"], wts["mlp_w_down"], i, gf)
        else:
            xm = post(cast["wo"], 0, wts["norm_mlp_g"], i, cast["wu"], cast["wd"], 0, gf)
    if sample:
        a_k, a_v = jnp.stack(ak), jnp.stack(av)
    else:
        a_k = jnp.transpose(kt_buf.reshape(n_a, b, A_HEADS, 2, A_DK, s), (0, 1, 5, 2, 3, 4))
        a_v = v_buf.reshape(n_a, b, s, A_HEADS, A_DV)
    return xm.reshape(b, s, D_MODEL), a_k, a_v, jnp.stack(bk), jnp.stack(bv)


def _swap_halves(t):
    flat = t.reshape(t.shape[:-2] + (B_KV,))
    swapped = t[..., ::-1, :].reshape(t.shape[:-2] + (B_KV,))
    return jnp.concatenate([flat, swapped], axis=-1).astype(BF16)


def kernel(x_prompt, x_sample, cache_a_k, cache_a_v, cache_b_k, cache_b_v, rel_table,
           norm_mix_g, norm_mlp_g, final_norm_g, a_w_qkv, a_lambda, a_subln_g, a_w_o,
           b_w_qkv, b_sinks, b_w_o, mlp_w_up, mlp_w_down):
    wts = dict(norm_mix_g=norm_mix_g, norm_mlp_g=norm_mlp_g[:, None, :], final_norm_g=final_norm_g,
               a_w_qkv=a_w_qkv, a_lambda=a_lambda, a_subln_g=a_subln_g[:, None, :],
               a_w_o=a_w_o, b_w_qkv=b_w_qkv, b_sinks=b_sinks,
               b_w_o=b_w_o, mlp_w_up=mlp_w_up, mlp_w_down=mlp_w_down)
    n_a, db, past = cache_a_k.shape[:3]
    caches = dict(
        a_kt=jnp.transpose(cache_a_k, (0, 1, 3, 4, 5, 2)).reshape(n_a, db, A_QK, past),
        a_v=cache_a_v.reshape(n_a, db, past * A_HEADS, A_DV),
        b=[(cache_b_k[j], cache_b_v[j], _swap_halves(cache_b_k[j]), _swap_halves(cache_b_v[j]))
           for j in range(cache_b_k.shape[0])])
    bf16_weights = []
    ys, aks, avs, bks, bvs = _trunk(x_sample, caches, wts, rel_table, bf16_weights,
                                    tm=512, tm_post=512, tf=1024)
    yp, akp, avp, bkp, bvp = _trunk(x_prompt, None, wts, rel_table, bf16_weights,
                                    tm=512, tm_post=1024, tf=1024)
    return (yp, ys, akp, avp, bkp, bvp, aks, avs, bks, bvs)
```

```python
import functools
import math

import jax
import jax.numpy as jnp
from jax import lax
from jax.experimental import pallas as pl
from jax.experimental.pallas import tpu as pltpu

BF16 = jnp.bfloat16
F32 = jnp.float32

D_MODEL = 1024
DEPTH = 4
CHUNK = 64
A_HEADS = 8
A_DK = 64
A_DV = 128
A_QK = A_HEADS * 2 * A_DK
A_V = A_HEADS * A_DV
B_Q_HEADS = 16
B_KV_HEADS = 2
B_HD = 64
B_Q = B_Q_HEADS * B_HD
B_KV = B_KV_HEADS * B_HD
WINDOW = 128
NUM_BUCKETS = 32
MAX_DIST = 128
N_MAPS = 16
D_FF = 4 * D_MODEL
EPS = 1e-6
NEG = -1e30

LANES = 128
SUBLANES = 8
HALF = LANES // 2
VMEM_LIMIT = 56 * 1024 * 1024

NT_DIMS = (((1,), (1,)), ((), ()))

LOG2E = math.log2(math.e)


def _lambda_init(layer):
    return 0.8 - 0.6 * math.exp(-0.3 * layer)


def _rms(x, g):
    return x * lax.rsqrt(jnp.mean(x * x, axis=-1, keepdims=True) + EPS) * g


def _params(n_axes):
    return pltpu.CompilerParams(
        dimension_semantics=("arbitrary",) * n_axes, vmem_limit_bytes=VMEM_LIMIT)


def _lane_fold_max(s):
    mx = s[:, 0:LANES]
    for gi in range(1, s.shape[1] // LANES):
        mx = jnp.maximum(mx, s[:, gi * LANES:(gi + 1) * LANES])
    return mx


def _row_max(s):
    return jnp.broadcast_to(jnp.max(_lane_fold_max(s), axis=-1, keepdims=True), (s.shape[0], LANES))


def _exp_rows(s, m):
    return jnp.concatenate([jnp.exp2(s[:, gi * LANES:(gi + 1) * LANES] - m)
                            for gi in range(s.shape[1] // LANES)], axis=1).astype(BF16)


def _with_ones(v):
    return jnp.concatenate([v, jnp.ones(v.shape, v.dtype)], axis=1)


def _proj_a_prompt_body(x_ref, g_ref, w_ref, wkt_ref, *refs, first):
    q_ref, kt_ref, v_ref, ktb_ref, vb_ref = refs[-5:]
    h = _rms(x_ref[...], g_ref[...]).astype(BF16)
    tm = h.shape[0]
    if first:
        for slot in range(1, kt_ref.shape[0]):
            kt_ref[slot] = jnp.zeros(kt_ref.shape[1:], F32)
            v_ref[slot] = jnp.zeros(v_ref.shape[1:], F32)
        kt_ref, v_ref = kt_ref.at[0], v_ref.at[0]
    v = jnp.dot(h, w_ref[:, 2 * A_QK:], preferred_element_type=F32)
    vb_ref[...] = v.astype(BF16)
    for hh in range(A_HEADS):
        v_ref[pl.ds(hh, tm, stride=A_HEADS), :] = v[:, hh * A_DV:(hh + 1) * A_DV]
    kt = lax.dot_general(wkt_ref[...], h, NT_DIMS, preferred_element_type=F32)
    ktb_ref[...] = kt.astype(BF16)
    kt_ref[...] = kt
    q = jnp.dot(h, w_ref[:, 0:A_QK], preferred_element_type=F32)
    q_ref[...] = (q * (A_DK ** -0.5 * LOG2E)).astype(BF16)


def _proj_a_prompt(x, g, w, wkt, w_layer, layer, kt_prev, v_prev, *, b, s, tm, n_a):
    m = b * s
    per_b = s // tm
    row = lambda i: (i, 0)
    full = lambda i: (0, 0)
    stacked = lambda i: (w_layer, 0, 0)
    first = kt_prev is None
    assert first == (layer == 0)
    in_specs = [pl.BlockSpec((tm, D_MODEL), row),
                pl.BlockSpec((1, D_MODEL), full),
                pl.BlockSpec((None, D_MODEL, 2 * A_QK + A_V), stacked),
                pl.BlockSpec((None, A_QK, D_MODEL), stacked)]
    operands = [x, g, w, wkt]
    if first:
        kt_spec = pl.BlockSpec((n_a, None, A_QK, tm), lambda i: (0, i // per_b, 0, i % per_b))
        v_spec = pl.BlockSpec((n_a, tm * A_HEADS, A_DV), lambda i: (0, i, 0))
        aliases = {}
    else:
        kt_spec = pl.BlockSpec((None, None, A_QK, tm), lambda i: (layer, i // per_b, 0, i % per_b))
        v_spec = pl.BlockSpec((None, tm * A_HEADS, A_DV), lambda i: (layer, i, 0))
        in_specs += [pl.BlockSpec(memory_space=pl.ANY)] * 2
        operands += [kt_prev, v_prev]
        aliases = {4: 1, 5: 2}
    return pl.pallas_call(
        functools.partial(_proj_a_prompt_body, first=first),
        grid=(m // tm,),
        in_specs=in_specs,
        out_specs=[pl.BlockSpec((tm, A_QK), row),
                   kt_spec,
                   v_spec,
                   pl.BlockSpec((None, A_QK, tm), lambda i: (i // per_b, 0, i % per_b)),
                   pl.BlockSpec((tm, A_V), row)],
        out_shape=[jax.ShapeDtypeStruct((m, A_QK), BF16),
                   jax.ShapeDtypeStruct((n_a, b, A_QK, s), F32),
                   jax.ShapeDtypeStruct((n_a, m * A_HEADS, A_DV), F32),
                   jax.ShapeDtypeStruct((b, A_QK, s), BF16),
                   jax.ShapeDtypeStruct((m, A_V), BF16)],
        input_output_aliases=aliases,
        compiler_params=_params(1),
        name="proj_a_prompt",
    )(*operands)


def _proj_a_sample_body(x_ref, g_ref, w_ref, q_ref, k_ref, v_ref, kb_ref, vb_ref, wb_ref, wkt_ref):
    wb_ref[...] = w_ref[...].astype(BF16)
    wkt_ref[...] = w_ref[:, A_QK:2 * A_QK].T.astype(BF16)
    h = _rms(x_ref[...], g_ref[...]).astype(BF16)
    q = jnp.dot(h, wb_ref[:, 0:A_QK], preferred_element_type=F32)
    q_ref[...] = (q * (A_DK ** -0.5 * LOG2E)).astype(BF16)
    k = jnp.dot(h, wb_ref[:, A_QK:2 * A_QK], preferred_element_type=F32)
    k_ref[...] = k
    kb_ref[...] = k.astype(BF16)
    v = jnp.dot(h, wb_ref[:, 2 * A_QK:], preferred_element_type=F32)
    v_ref[...] = v
    vb_ref[...] = v.astype(BF16)


def _proj_a_sample(x, g, w, layer):
    m = x.shape[0]
    n_out = 2 * A_QK + A_V
    full = lambda i: (0, 0)
    full3 = lambda i: (0, 0, 0)
    return pl.pallas_call(
        _proj_a_sample_body,
        grid=(1,),
        in_specs=[pl.BlockSpec((m, D_MODEL), full),
                  pl.BlockSpec((1, D_MODEL), full),
                  pl.BlockSpec((None, D_MODEL, n_out), lambda i: (layer, 0, 0))],
        out_specs=[pl.BlockSpec((m, A_QK), full),
                   pl.BlockSpec((m, A_QK), full),
                   pl.BlockSpec((m, A_V), full),
                   pl.BlockSpec((m, A_QK), full),
                   pl.BlockSpec((m, A_V), full),
                   pl.BlockSpec((None, D_MODEL, n_out), full3),
                   pl.BlockSpec((None, A_QK, D_MODEL), full3)],
        out_shape=[jax.ShapeDtypeStruct((m, A_QK), BF16),
                   jax.ShapeDtypeStruct((m, A_QK), F32),
                   jax.ShapeDtypeStruct((m, A_V), F32),
                   jax.ShapeDtypeStruct((m, A_QK), BF16),
                   jax.ShapeDtypeStruct((m, A_V), BF16),
                   jax.ShapeDtypeStruct((1, D_MODEL, n_out), BF16),
                   jax.ShapeDtypeStruct((1, A_QK, D_MODEL), BF16)],
        compiler_params=_params(1),
        name="proj_a_sample",
    )(x, g, w)


def _proj_b_body(x_ref, g_ref, w_ref, q_ref, k_ref, v_ref, kb_ref, vb_ref, *wb_ref, tail):
    if wb_ref:
        wb_ref[0][...] = w_ref[...].astype(BF16)
        w_ref = wb_ref[0]
    h = _rms(x_ref[...], g_ref[...]).astype(BF16)
    kv = jnp.dot(h, w_ref[:, B_Q:], preferred_element_type=F32)
    k, v = kv[:, 0:B_KV], kv[:, B_KV:]
    kb_ref[...] = jnp.concatenate([k, pltpu.roll(k, HALF, axis=1)], axis=1).astype(BF16)
    vb_ref[...] = jnp.concatenate([v, pltpu.roll(v, HALF, axis=1)], axis=1).astype(BF16)
    keep = k.shape[0] if tail is None else tail
    k_ref[...] = k[k.shape[0] - keep:]
    v_ref[...] = v[v.shape[0] - keep:]
    q = jnp.dot(h, w_ref[:, 0:B_Q], preferred_element_type=F32)
    q_ref[...] = (q * (B_HD ** -0.5 * LOG2E)).astype(BF16)


def _proj_b(x, g, w, layer, tm, *, seq=None, tail=None, cast_weights=False):
    m = x.shape[0]
    assert not cast_weights or m == tm
    n_out = B_Q + 2 * B_KV
    row = lambda i: (i, 0)
    full = lambda i: (0, 0)
    wb_spec = [pl.BlockSpec((None, D_MODEL, n_out), lambda i: (0, 0, 0))] if cast_weights else []
    wb_shape = [jax.ShapeDtypeStruct((1, D_MODEL, n_out), BF16)] if cast_weights else []
    if tail is None:
        kv_spec = pl.BlockSpec((tm, B_KV), row)
        kv_shape = jax.ShapeDtypeStruct((m, B_KV), F32)
    else:
        assert seq % tm == 0 and tail <= tm
        per_seq = seq // tm
        kv_spec = pl.BlockSpec((None, tail, B_KV), lambda i: (i // per_seq, 0, 0))
        kv_shape = jax.ShapeDtypeStruct((m // seq, tail, B_KV), F32)
    return pl.pallas_call(
        functools.partial(_proj_b_body, tail=tail),
        grid=(m // tm,),
        in_specs=[pl.BlockSpec((tm, D_MODEL), row),
                  pl.BlockSpec((1, D_MODEL), full),
                  pl.BlockSpec((None, D_MODEL, B_Q + 2 * B_KV), lambda i: (layer, 0, 0))],
        out_specs=[pl.BlockSpec((tm, B_Q), row),
                   kv_spec,
                   kv_spec,
                   pl.BlockSpec((tm, 2 * B_KV), row),
                   pl.BlockSpec((tm, 2 * B_KV), row)] + wb_spec,
        out_shape=[jax.ShapeDtypeStruct((m, B_Q), BF16),
                   kv_shape,
                   kv_shape,
                   jax.ShapeDtypeStruct((m, 2 * B_KV), BF16),
                   jax.ShapeDtypeStruct((m, 2 * B_KV), BF16)] + wb_shape,
        compiler_params=_params(1),
        name="proj_b",
    )(x, g, w)


def _t5_bucket(rel):
    nb = NUM_BUCKETS // 2
    n = -rel
    ret = jnp.where(n < 0, nb, 0)
    n = jnp.abs(n)
    max_exact = nb // 2
    nf = jnp.maximum(n, 1).astype(jnp.float32)
    large = max_exact + (jnp.log(nf / max_exact) / math.log(MAX_DIST / max_exact)
                         * (nb - max_exact)).astype(jnp.int32)
    large = jnp.minimum(large, nb - 1)
    return ret + jnp.where(n < max_exact, n, large)


def _check_far_bucket(min_dist):
    nb, max_exact = NUM_BUCKETS // 2, NUM_BUCKETS // 4
    ratio = math.log(min_dist / max_exact) / math.log(MAX_DIST / max_exact)
    assert ratio * (nb - max_exact) >= nb - 1 - max_exact + 1e-3, "far keys need one shared bucket"


def _far_bias(table, tq, w):
    far = table[NUM_BUCKETS // 2 - 1].astype(F32) * LOG2E
    return jnp.broadcast_to(far[:, None, None], (N_MAPS, tq, w))


def _near_bias(table, tq, w, back, band):
    r = jnp.arange(tq)[:, None]
    j = jnp.arange(w)[None, :] - back
    onehot = (_t5_bucket(j - r)[None] == jnp.arange(NUM_BUCKETS)[:, None, None]).astype(F32)
    onehot = lax.optimization_barrier(onehot)
    bias = jnp.einsum("bm,brj->mrj", table.astype(F32), onehot, precision=lax.Precision.HIGHEST)
    kc, qc = j // CHUNK, r // CHUNK
    mask = kc <= qc
    if band is not None:
        mask = mask & (kc >= qc - band)
    return jnp.where(mask[None], bias * LOG2E, NEG)


A_TILE = 256
A_ROWS = 64
A_HEADS_PER_STEP = 1


def _diff_lambda(lamp_ref, lam_init):
    lp = lamp_ref[...]
    return (jnp.exp(jnp.sum(lp[0:1] * lp[1:2], axis=-1, keepdims=True))
            - jnp.exp(jnp.sum(lp[2:3] * lp[3:4], axis=-1, keepdims=True)) + lam_init)


def _stack_maps(q):
    lane = lax.broadcasted_iota(jnp.int32, q.shape, 1)
    zero = jnp.zeros_like(q)
    return jnp.concatenate([jnp.where(lane < HALF, q, zero), jnp.where(lane >= HALF, q, zero)], axis=0)


def _diff_epilogue(acc, lam, g, lam_init):
    tq = acc.shape[0] // 2
    o_all = acc[:, :A_DV] / acc[:, A_DV:]
    o = o_all[:tq] - lam * o_all[tq:]
    return (_rms(o, g) * (1.0 - lam_init)).astype(BF16)


def _attn_a_scores(i, q, kt_ref, bias_ref, s_ref, m_ref):
    tl = A_TILE
    qs = _stack_maps(q)
    n_far = max(i - 1, 0)
    for kt in range(i + 1):
        cols = slice(kt * tl, (kt + 1) * tl)
        s = jnp.dot(qs, kt_ref[:, cols], preferred_element_type=F32)
        near = kt >= n_far
        if near:
            s = s + bias_ref[2 - (i - kt)].reshape(2 * tl, tl)
        s_ref[:, cols] = s
        fold = _lane_fold_max(s)
        first = kt == (n_far if near else 0)
        m_ref[int(near)] = fold if first else jnp.maximum(m_ref[int(near)], fold)
        yield
    for r in range(0, 2 * tl, A_ROWS):
        rows = slice(r, r + A_ROWS)
        mx = m_ref[1, rows, :]
        if n_far:
            c = bias_ref[0, r // tl, r % tl:r % tl + A_ROWS, 0:LANES]
            mx = jnp.maximum(mx, m_ref[0, rows, :] + c)
        m = jnp.broadcast_to(jnp.max(mx, axis=-1, keepdims=True), (A_ROWS, LANES))
        m_ref[1, rows, :] = m
        if n_far:
            m_ref[0, rows, :] = m - c


def _attn_a_numerators(i, s_ref, m_ref, v1_ref, out):
    tl = A_TILE
    n_far = max(i - 1, 0)
    acc = None
    for kt in range(i + 1):
        cols = slice(kt * tl, (kt + 1) * tl)
        p = _exp_rows(s_ref[:, cols], m_ref[1 if kt >= n_far else 0])
        part = jnp.dot(p, v1_ref[cols, :], preferred_element_type=F32)
        acc = part if acc is None else acc + part
        yield
    out.append(acc)


def _interleave(*steps):
    steps = list(steps)
    while steps:
        for gen in list(steps):
            if next(gen, StopIteration) is StopIteration:
                steps.remove(gen)


def _attn_a_prompt_body(lamp_ref, q_ref, kt_ref, v_ref, bias_ref, g_ref, o_ref,
                        s_ref, m_ref, v1_ref, *, nt, nh, lam_init):
    tl = A_TILE
    lam = _diff_lambda(lamp_ref, lam_init)
    for hh in range(nh):
        v1_ref[hh, :, 0:A_DV] = v_ref[0, :, hh * A_DV:(hh + 1) * A_DV]
        v1_ref[hh, :, A_DV:] = jnp.ones((nt * tl, A_DV), BF16)
    work = [(i, hh) for i in reversed(range(nt)) for hh in range(nh)]

    def scores(i, hh):
        lanes = slice(hh * LANES, (hh + 1) * LANES)
        return _attn_a_scores(i, q_ref[0, i * tl:(i + 1) * tl, lanes], kt_ref.at[0, lanes, :],
                              bias_ref.at[:, 2 * hh:2 * hh + 2], s_ref.at[hh, i % 2],
                              m_ref.at[hh, i % 2])

    _interleave(scores(*work[0]))
    for n, (i, hh) in enumerate(work):
        out = []
        steps = [_attn_a_numerators(i, s_ref.at[hh, i % 2], m_ref.at[hh, i % 2], v1_ref.at[hh], out)]
        if n + 1 < len(work):
            steps.insert(0, scores(*work[n + 1]))
        _interleave(*steps)
        acc = out[0]
        o_ref[0, i * tl:(i + 1) * tl, hh * LANES:(hh + 1) * LANES] = _diff_epilogue(
            acc, lam, g_ref[...], lam_init)


def _attn_a_prompt(q, kt, v, bias, lamp, g, layer, *, lam_init):
    b, s, _ = q.shape
    nt = s // A_TILE
    nh = A_HEADS_PER_STEP
    assert s % A_TILE == 0 and A_HEADS % nh == 0
    seq = pl.BlockSpec((1, s, nh * LANES), lambda h, bi: (bi, 0, h))
    return pl.pallas_call(
        functools.partial(_attn_a_prompt_body, nt=nt, nh=nh, lam_init=lam_init),
        grid=(A_HEADS // nh, b),
        in_specs=[pl.BlockSpec((None, 4, A_DK), lambda h, bi: (layer, 0, 0)),
                  seq,
                  pl.BlockSpec((1, nh * LANES, s), lambda h, bi: (bi, h, 0)),
                  seq,
                  pl.BlockSpec((3, 2 * nh, A_TILE, A_TILE), lambda h, bi: (0, h, 0, 0)),
                  pl.BlockSpec((None, 1, A_DV), lambda h, bi: (layer, 0, 0))],
        out_specs=seq,
        out_shape=jax.ShapeDtypeStruct((b, s, A_V), BF16),
        scratch_shapes=[pltpu.VMEM((nh, 2, 2 * A_TILE, s), F32),
                        pltpu.VMEM((nh, 2, 2, 2 * A_TILE, LANES), F32),
                        pltpu.VMEM((nh, s, 2 * A_DV), BF16)],
        compiler_params=_params(2),
        name="attn_a_prompt",
    )(lamp, q, kt, v, bias, g)


def _attn_a_sample_body(lamp_ref, q_ref, ktc_ref, vc_ref, kn_ref, vn_ref, bias_ref, g_ref, o_ref,
                        *, lam_init):
    tq = q_ref.shape[1]
    past = ktc_ref.shape[1]
    lam = _diff_lambda(lamp_ref, lam_init)
    for h in range(A_HEADS):
        cols = slice(h * LANES, (h + 1) * LANES)
        qs = _stack_maps(q_ref[0, :, cols])
        bias = bias_ref[2 * h:2 * h + 2].reshape(2 * tq, past + LANES)
        s_c = jnp.dot(qs, ktc_ref[cols, :].astype(BF16), preferred_element_type=F32)
        s_c = s_c + bias[:, 0:past]
        s_n = lax.dot_general(qs, kn_ref[0, :, cols], NT_DIMS, preferred_element_type=F32)
        s_n = s_n + bias[:, past:past + tq]
        m = jnp.maximum(_row_max(s_c), jnp.broadcast_to(jnp.max(s_n, axis=-1, keepdims=True),
                                                        (2 * tq, LANES)))
        p_c = _exp_rows(s_c, m)
        p_n = jnp.exp2(s_n - m[:, 0:tq]).astype(BF16)
        v_c = vc_ref[pl.ds(h, past, stride=A_HEADS), :].astype(BF16)
        acc = (jnp.dot(p_c, _with_ones(v_c), preferred_element_type=F32)
               + jnp.dot(p_n, _with_ones(vn_ref[0, :, cols]), preferred_element_type=F32))
        o_ref[0, :, cols] = _diff_epilogue(acc, lam, g_ref[...], lam_init)


def _attn_a_sample(q, ktc, vc, kn, vn, bias, lamp, g, layer, *, lam_init):
    b, tq, _ = q.shape
    past = ktc.shape[3]
    new = lambda bi: (bi, 0, 0)
    return pl.pallas_call(
        functools.partial(_attn_a_sample_body, lam_init=lam_init),
        grid=(b,),
        in_specs=[pl.BlockSpec((None, 4, A_DK), lambda bi: (layer, 0, 0)),
                  pl.BlockSpec((1, tq, A_QK), new),
                  pl.BlockSpec((None, None, A_QK, past), lambda bi: (layer, bi, 0, 0)),
                  pl.BlockSpec((None, None, past * A_HEADS, A_DV), lambda bi: (layer, bi, 0, 0)),
                  pl.BlockSpec((1, tq, A_QK), new),
                  pl.BlockSpec((1, tq, A_V), new),
                  pl.BlockSpec((N_MAPS, tq, past + LANES), lambda bi: (0, 0, 0)),
                  pl.BlockSpec((None, 1, A_DV), lambda bi: (layer, 0, 0))],
        out_specs=pl.BlockSpec((1, tq, A_V), new),
        out_shape=jax.ShapeDtypeStruct((b, tq, A_V), BF16),
        compiler_params=_params(1),
        name="attn_a_sample",
    )(lamp, q, ktc, vc, kn, vn, bias, g)


_B_ORDER_A = (0, 2, 4, 6, 9, 11, 13, 15)
_B_ORDER_B = (1, 3, 5, 7, 8, 10, 12, 14)


def _attn_b_body(sink_ref, q_ref, k_ref, v_ref, bias_ref, o_ref, *, layer, tq, w, back, i_off, nsub):
    lane = lax.broadcasted_iota(jnp.int32, (tq, LANES), 1)
    lo = lane < HALF
    half = B_Q_HEADS // 4
    orders = (_B_ORDER_A, _B_ORDER_B)
    sinks = [jnp.concatenate([jnp.full((tq, LANES), sink_ref[layer, h] * LOG2E, F32)
                              for h in orders[t]], axis=0) for t in range(2)]
    outs = {}

    def chain(u, t):
        tile = pl.program_id(1) * nsub + u + i_off
        variant = 1 if (u > 0 or i_off > 0) else jnp.minimum(tile, 1)
        near_start = pl.multiple_of(jnp.maximum(tile * tq - back, 0), CHUNK)
        group = slice(t * LANES, (t + 1) * LANES)
        pieces = []
        for c in range(B_Q_HEADS // 2):
            qc = q_ref[0, u * tq:(u + 1) * tq, c * LANES:(c + 1) * LANES]
            keep_lo = (c < half) == (t == 0)
            pieces.append(jnp.where(lo == keep_lo, qc, jnp.zeros_like(qc)))
        qs = jnp.concatenate(pieces, axis=0)
        s = lax.dot_general(qs, k_ref[0, pl.ds(near_start, w), group], NT_DIMS,
                            preferred_element_type=F32)
        s = s + bias_ref[variant, t].reshape((B_Q_HEADS // 2) * tq, w)
        m = jnp.maximum(_row_max(s), sinks[t])
        p = _exp_rows(s, m)
        acc = jnp.dot(p, _with_ones(v_ref[0, pl.ds(near_start, w), group]),
                      preferred_element_type=F32)
        outs[u, t] = acc[:, :LANES] / (acc[:, LANES:] + jnp.exp2(sinks[t] - m))

    for u in range(nsub):
        for t in range(2):
            chain(u, t)

    for u in range(nsub):
        for c in range(B_Q_HEADS // 2):
            oa = outs[u, 0][c * tq:(c + 1) * tq]
            ob = outs[u, 1][c * tq:(c + 1) * tq]
            oc = jnp.where(lo, oa, ob) if c < half else jnp.where(lo, ob, oa)
            o_ref[0, u * tq:(u + 1) * tq, c * LANES:(c + 1) * LANES] = oc.astype(BF16)


def _attn_b(q, k, v, bias, sinks, layer, *, tq, w, back, i_off, nsub):
    b, sq, _ = q.shape
    sk = k.shape[1]
    rows = nsub * tq
    assert sq % rows == 0 and (sq // tq - 1 + i_off) * tq - back + w <= sk
    body = functools.partial(_attn_b_body, layer=layer, tq=tq, w=w, back=back, i_off=i_off,
                             nsub=nsub)
    return pl.pallas_call(
        body,
        grid=(b, sq // rows),
        in_specs=[pl.BlockSpec(memory_space=pltpu.SMEM),
                  pl.BlockSpec((1, rows, B_Q), lambda bi, i: (bi, i, 0)),
                  pl.BlockSpec((1, sk, 2 * B_KV), lambda bi, i: (bi, 0, 0)),
                  pl.BlockSpec((1, sk, 2 * B_KV), lambda bi, i: (bi, 0, 0)),
                  pl.BlockSpec((2, 2, B_Q_HEADS // 2, tq, w), lambda bi, i: (0, 0, 0, 0, 0))],
        out_specs=pl.BlockSpec((1, rows, B_Q), lambda bi, i: (bi, i, 0)),
        out_shape=jax.ShapeDtypeStruct((b, sq, B_Q), BF16),
        compiler_params=_params(2),
        name="attn_b",
    )(sinks, q, k, v, bias)


def _post_body(x_ref, o_ref, wo_ref, g_ref, wu_ref, wd_ref, gf_ref, y_ref, *refs, final, cast_weights):
    hn_ref = refs[-1]
    f = pl.program_id(1)
    if cast_weights:
        wob_ref, wub_ref, wdb_ref = refs[:3]

    @pl.when(f == 0)
    def _():
        if cast_weights:
            wob_ref[...] = wo_ref[...].astype(BF16)
        wo = (wob_ref if cast_weights else wo_ref)[...]
        x1 = x_ref[...] + jnp.dot(o_ref[...], wo, preferred_element_type=F32)
        y_ref[...] = x1
        hn_ref[...] = _rms(x1, g_ref[...]).astype(BF16)

    if cast_weights:
        wub_ref[...] = wu_ref[...].astype(BF16)
        wdb_ref[...] = wd_ref[...].astype(BF16)
        wu_ref, wd_ref = wub_ref, wdb_ref
    u = jnp.dot(hn_ref[...], wu_ref[...], preferred_element_type=F32)
    a = jnp.square(jnp.maximum(u, 0.0)).astype(BF16)
    y_ref[...] += jnp.dot(a, wd_ref[...], preferred_element_type=F32)

    if final:
        @pl.when(f == pl.num_programs(1) - 1)
        def _():
            y_ref[...] = _rms(y_ref[...], gf_ref[...])


def _post(x, o, wo, wo_layer, g, g_layer, wu, wd, w_layer, gf, *, tm, tf, final, cast_weights):
    m = x.shape[0]
    assert not cast_weights or m == tm
    row = lambda i, f: (i, 0)
    full = lambda i, f: (0, 0)
    out_specs = [pl.BlockSpec((tm, D_MODEL), row)]
    out_shape = [jax.ShapeDtypeStruct((m, D_MODEL), F32)]
    if cast_weights:
        out_specs += [pl.BlockSpec((None, D_MODEL, D_MODEL), lambda i, f: (0, 0, 0)),
                      pl.BlockSpec((None, D_MODEL, tf), lambda i, f: (0, 0, f)),
                      pl.BlockSpec((None, tf, D_MODEL), lambda i, f: (0, f, 0))]
        out_shape += [jax.ShapeDtypeStruct((1, D_MODEL, D_MODEL), BF16),
                      jax.ShapeDtypeStruct((1, D_MODEL, D_FF), BF16),
                      jax.ShapeDtypeStruct((1, D_FF, D_MODEL), BF16)]
    outs = pl.pallas_call(
        functools.partial(_post_body, final=final, cast_weights=cast_weights),
        grid=(m // tm, D_FF // tf),
        in_specs=[pl.BlockSpec((tm, D_MODEL), row),
                  pl.BlockSpec((tm, D_MODEL), row),
                  pl.BlockSpec((None, D_MODEL, D_MODEL), lambda i, f: (wo_layer, 0, 0)),
                  pl.BlockSpec((None, 1, D_MODEL), lambda i, f: (g_layer, 0, 0)),
                  pl.BlockSpec((None, D_MODEL, tf), lambda i, f: (w_layer, 0, f)),
                  pl.BlockSpec((None, tf, D_MODEL), lambda i, f: (w_layer, f, 0)),
                  pl.BlockSpec((1, D_MODEL), full)],
        out_specs=out_specs,
        out_shape=out_shape,
        scratch_shapes=[pltpu.VMEM((tm, D_MODEL), BF16)],
        compiler_params=_params(2),
        name="post",
    )(x, o, wo, g, wu, wd, gf)
    return outs if cast_weights else outs[0]


def _post_resident_body(x_ref, o_ref, wo_ref, g_ref, wu_ref, wd_ref, gf_ref, y_ref, hn_ref,
                        *, tf, final):
    x1 = x_ref[...] + jnp.dot(o_ref[...], wo_ref[...], preferred_element_type=F32)
    y_ref[...] = x1
    hn_ref[...] = _rms(x1, g_ref[...]).astype(BF16)
    mlp = None
    for f in range(D_FF // tf):
        u = jnp.dot(hn_ref[...], wu_ref[:, f * tf:(f + 1) * tf], preferred_element_type=F32)
        a = jnp.square(jnp.maximum(u, 0.0)).astype(BF16)
        part = jnp.dot(a, wd_ref[f * tf:(f + 1) * tf, :], preferred_element_type=F32)
        mlp = part if mlp is None else mlp + part
    y = y_ref[...] + mlp
    y_ref[...] = _rms(y, gf_ref[...]) if final else y


def _post_resident(x, o, wo, g, g_layer, wu, wd, gf, *, tm, tf, final):
    m = x.shape[0]
    row = lambda i: (i, 0)
    once = dict(pipeline_mode=pl.Buffered(1))
    return pl.pallas_call(
        functools.partial(_post_resident_body, tf=tf, final=final),
        grid=(m // tm,),
        in_specs=[pl.BlockSpec((tm, D_MODEL), row),
                  pl.BlockSpec((tm, D_MODEL), row),
                  pl.BlockSpec((None, D_MODEL, D_MODEL), lambda i: (0, 0, 0), **once),
                  pl.BlockSpec((None, 1, D_MODEL), lambda i: (g_layer, 0, 0)),
                  pl.BlockSpec((None, D_MODEL, D_FF), lambda i: (0, 0, 0), **once),
                  pl.BlockSpec((None, D_FF, D_MODEL), lambda i: (0, 0, 0), **once),
                  pl.BlockSpec((1, D_MODEL), lambda i: (0, 0))],
        out_specs=pl.BlockSpec((tm, D_MODEL), row),
        out_shape=jax.ShapeDtypeStruct((m, D_MODEL), F32),
        scratch_shapes=[pltpu.VMEM((tm, D_MODEL), BF16)],
        compiler_params=_params(1),
        name="post_resident",
    )(x, o, wo, g, wu, wd, gf)


_B_PROMPT = dict(tq=128, w=256, back=128)
_B_SAMPLE = dict(tq=64, w=256, back=128)
B_TILES_PER_STEP = 8
_A_SAMPLE_BACK = 128


def _a_prompt_bias(table):
    _check_far_bucket(A_TILE + 1)
    near = _near_bias(table, A_TILE, 2 * A_TILE, A_TILE, None)
    return jnp.stack([_far_bias(table, A_TILE, A_TILE), near[:, :, :A_TILE], near[:, :, A_TILE:]])


def _a_sample_bias(table, past, t):
    assert t <= LANES
    _check_far_bucket(_A_SAMPLE_BACK + 1)
    far = _far_bias(table, t, past - _A_SAMPLE_BACK)
    near = _near_bias(table, t, _A_SAMPLE_BACK + LANES, _A_SAMPLE_BACK, None)
    return jnp.concatenate([far, near], axis=2)


def _b_bias(table, tq, w, back):
    def one(bk):
        t = _near_bias(table, tq, w, bk, WINDOW // CHUNK)
        return jnp.stack([jnp.stack([t[h] for h in order]) for order in (_B_ORDER_A, _B_ORDER_B)])
    return jnp.stack([one(0), one(back)])


def _pad_keys(t, total):
    return jnp.pad(t, ((0, 0), (0, total - t.shape[1]), (0, 0)))


def _trunk(x, caches, wts, table, bf16_weights, *, tm, tm_post, tf):
    b, s, _ = x.shape
    sample = caches is not None
    n_a = wts["a_w_qkv"].shape[0]
    xm = x.reshape(b * s, D_MODEL)
    b_cfg = _B_SAMPLE if sample else _B_PROMPT
    b_bias = _b_bias(table, **b_cfg)
    if sample:
        a_bias = _a_sample_bias(table, caches["a_kt"].shape[3], s)
    else:
        a_bias = _a_prompt_bias(table)
    ak, av, bk, bv = [], [], [], []
    kt_buf = v_buf = None
    for i in range(DEPTH):
        j = i // 2
        g_mix = wts["norm_mix_g"][i][None, :]
        if sample:
            bf16_weights.append({})
        cast = bf16_weights[i]
        if i % 2 == 0:
            lam_init = _lambda_init(i)
            if sample:
                q, k, v, kb, vb, cast["qkv"], cast["kt"] = _proj_a_sample(xm, g_mix, wts["a_w_qkv"], j)
                ak.append(k.reshape(b, s, A_HEADS, 2, A_DK))
                av.append(v.reshape(b, s, A_HEADS, A_DV))
                o = _attn_a_sample(q.reshape(b, s, A_QK), caches["a_kt"], caches["a_v"],
                                   kb.reshape(b, s, A_QK), vb.reshape(b, s, A_V), a_bias,
                                   wts["a_lambda"], wts["a_subln_g"], j, lam_init=lam_init)
            else:
                q, kt_buf, v_buf, ktb, vb = _proj_a_prompt(
                    xm, g_mix, cast["qkv"], cast["kt"], 0, j, kt_buf, v_buf,
                    b=b, s=s, tm=tm, n_a=n_a)
                o = _attn_a_prompt(q.reshape(b, s, A_QK), ktb, vb.reshape(b, s, A_V), a_bias,
                                   wts["a_lambda"], wts["a_subln_g"], j, lam_init=lam_init)
            wo, wo_layer = wts["a_w_o"], j
        else:
            if sample:
                q, k, v, kb, vb, cast["qkv"] = _proj_b(xm, g_mix, wts["b_w_qkv"], j, tm,
                                                       cast_weights=True)
            else:
                q, k, v, kb, vb = _proj_b(xm, g_mix, cast["qkv"], 0, tm, seq=s, tail=WINDOW)
            k = k.reshape(b, -1, B_KV_HEADS, B_HD)
            v = v.reshape(b, -1, B_KV_HEADS, B_HD)
            kb = kb.reshape(b, s, 2 * B_KV)
            vb = vb.reshape(b, s, 2 * B_KV)
            i_off = 0
            if sample:
                ck, cv, ckb, cvb = caches["b"][j]
                bk.append(jnp.concatenate([ck, k], axis=1)[:, s:])
                bv.append(jnp.concatenate([cv, v], axis=1)[:, s:])
                i_off = ckb.shape[1] // b_cfg["tq"]
                total = ckb.shape[1] - b_cfg["back"] + b_cfg["w"]
                kb = _pad_keys(jnp.concatenate([ckb, kb], axis=1), total)
                vb = _pad_keys(jnp.concatenate([cvb, vb], axis=1), total)
            else:
                bk.append(k)
                bv.append(v)
            o = _attn_b(q.reshape(b, s, B_Q), kb, vb, b_bias, wts["b_sinks"], j, i_off=i_off,
                        nsub=1 if sample else B_TILES_PER_STEP, **b_cfg)
            wo, wo_layer = wts["b_w_o"], j
        gf = wts["final_norm_g"][None, :]
        tiling = dict(tm=tm_post, tf=tf, final=(i == DEPTH - 1))
        if sample:
            xm, cast["wo"], cast["wu"], cast["wd"] = _post(
                xm, o.reshape(b * s, D_MODEL), wo, wo_layer, wts["norm_mlp_g"], i,
                wts["mlp_w_up"], wts["mlp_w_down"], i, gf, cast_weights=True, **tiling)
        else:
            xm = _post_resident(xm, o.reshape(b * s, D_MODEL), cast["wo"], wts["norm_mlp_g"], i,
                                cast["wu"], cast["wd"], gf, **tiling)
    if sample:
        a_k, a_v = jnp.stack(ak), jnp.stack(av)
    else:
        a_k = jnp.transpose(kt_buf.reshape(n_a, b, A_HEADS, 2, A_DK, s), (0, 1, 5, 2, 3, 4))
        a_v = v_buf.reshape(n_a, b, s, A_HEADS, A_DV)
    return xm.reshape(b, s, D_MODEL), a_k, a_v, jnp.stack(bk), jnp.stack(bv)


def _swap_halves(t):
    flat = t.reshape(t.shape[:-2] + (B_KV,))
    swapped = t[..., ::-1, :].reshape(t.shape[:-2] + (B_KV,))
    return jnp.concatenate([flat, swapped], axis=-1).astype(BF16)


def kernel(x_prompt, x_sample, cache_a_k, cache_a_v, cache_b_k, cache_b_v, rel_table,
           norm_mix_g, norm_mlp_g, final_norm_g, a_w_qkv, a_lambda, a_subln_g, a_w_o,
           b_w_qkv, b_sinks, b_w_o, mlp_w_up, mlp_w_down):
    wts = dict(norm_mix_g=norm_mix_g, norm_mlp_g=norm_mlp_g[:, None, :], final_norm_g=final_norm_g,
               a_w_qkv=a_w_qkv, a_lambda=a_lambda, a_subln_g=a_subln_g[:, None, :],
               a_w_o=a_w_o, b_w_qkv=b_w_qkv, b_sinks=b_sinks,
               b_w_o=b_w_o, mlp_w_up=mlp_w_up, mlp_w_down=mlp_w_down)
    n_a, db, past = cache_a_k.shape[:3]
    caches = dict(
        a_kt=jnp.transpose(cache_a_k, (0, 1, 3, 4, 5, 2)).reshape(n_a, db, A_QK, past),
        a_v=cache_a_v.reshape(n_a, db, past * A_HEADS, A_DV),
        b=[(cache_b_k[j], cache_b_v[j], _swap_halves(cache_b_k[j]), _swap_halves(cache_b_v[j]))
           for j in range(cache_b_k.shape[0])])
    bf16_weights = []
    ys, aks, avs, bks, bvs = _trunk(x_sample, caches, wts, rel_table, bf16_weights,
                                    tm=512, tm_post=512, tf=1024)
    yp, akp, avp, bkp, bvp = _trunk(x_prompt, None, wts, rel_table, bf16_weights,
                                    tm=512, tm_post=1024, tf=1024)
    return (yp, ys, akp, avp, bkp, bvp, aks, avs, bks, bvs)
```

```python
import functools
import math

import jax
import jax.numpy as jnp
from jax import lax
from jax.experimental import pallas as pl
from jax.experimental.pallas import tpu as pltpu

BF16 = jnp.bfloat16
F32 = jnp.float32

D_MODEL = 1024
DEPTH = 4
CHUNK = 64
A_HEADS = 8
A_DK = 64
A_DV = 128
A_QK = A_HEADS * 2 * A_DK
A_V = A_HEADS * A_DV
B_Q_HEADS = 16
B_KV_HEADS = 2
B_HD = 64
B_Q = B_Q_HEADS * B_HD
B_KV = B_KV_HEADS * B_HD
WINDOW = 128
NUM_BUCKETS = 32
MAX_DIST = 128
N_MAPS = 16
D_FF = 4 * D_MODEL
EPS = 1e-6
NEG = -1e30

LANES = 128
HALF = LANES // 2
VMEM_LIMIT = 56 * 1024 * 1024

NT_DIMS = (((1,), (1,)), ((), ()))

LOG2E = math.log2(math.e)


def _lambda_init(layer):
    return 0.8 - 0.6 * math.exp(-0.3 * layer)


def _rms(x, g):
    return x * lax.rsqrt(jnp.mean(x * x, axis=-1, keepdims=True) + EPS) * g


def _params(n_axes):
    return pltpu.CompilerParams(
        dimension_semantics=("arbitrary",) * n_axes, vmem_limit_bytes=VMEM_LIMIT)


def _lane_fold_max(s):
    mx = s[:, 0:LANES]
    for gi in range(1, s.shape[1] // LANES):
        mx = jnp.maximum(mx, s[:, gi * LANES:(gi + 1) * LANES])
    return mx


def _row_max(s):
    return jnp.broadcast_to(jnp.max(_lane_fold_max(s), axis=-1, keepdims=True), (s.shape[0], LANES))


def _exp_rows(s, m):
    return jnp.concatenate([jnp.exp2(s[:, gi * LANES:(gi + 1) * LANES] - m)
                            for gi in range(s.shape[1] // LANES)], axis=1).astype(BF16)


def _with_ones(v):
    return jnp.concatenate([v, jnp.ones(v.shape, v.dtype)], axis=1)


def _proj_a_prompt_body(x_ref, g_ref, w_ref, wkt_ref, *refs, first):
    q_ref, kt_ref, v_ref, ktb_ref, vb_ref = refs[-5:]
    h = _rms(x_ref[...], g_ref[...]).astype(BF16)
    tm = h.shape[0]
    if first:
        for slot in range(1, kt_ref.shape[0]):
            kt_ref[slot] = jnp.zeros(kt_ref.shape[1:], F32)
            v_ref[slot] = jnp.zeros(v_ref.shape[1:], F32)
        kt_ref, v_ref = kt_ref.at[0], v_ref.at[0]
    v = jnp.dot(h, w_ref[:, 2 * A_QK:], preferred_element_type=F32)
    vb_ref[...] = v.astype(BF16)
    for hh in range(A_HEADS):
        v_ref[pl.ds(hh, tm, stride=A_HEADS), :] = v[:, hh * A_DV:(hh + 1) * A_DV]
    kt = lax.dot_general(wkt_ref[...], h, NT_DIMS, preferred_element_type=F32)
    ktb_ref[...] = kt.astype(BF16)
    kt_ref[...] = kt
    q = jnp.dot(h, w_ref[:, 0:A_QK], preferred_element_type=F32)
    q_ref[...] = (q * (A_DK ** -0.5 * LOG2E)).astype(BF16)


def _proj_a_prompt(x, g, w, wkt, w_layer, layer, kt_prev, v_prev, *, b, s, tm, n_a):
    m = b * s
    per_b = s // tm
    row = lambda i: (i, 0)
    full = lambda i: (0, 0)
    stacked = lambda i: (w_layer, 0, 0)
    first = kt_prev is None
    assert first == (layer == 0)
    in_specs = [pl.BlockSpec((tm, D_MODEL), row),
                pl.BlockSpec((1, D_MODEL), full),
                pl.BlockSpec((None, D_MODEL, 2 * A_QK + A_V), stacked),
                pl.BlockSpec((None, A_QK, D_MODEL), stacked)]
    operands = [x, g, w, wkt]
    if first:
        kt_spec = pl.BlockSpec((n_a, None, A_QK, tm), lambda i: (0, i // per_b, 0, i % per_b))
        v_spec = pl.BlockSpec((n_a, tm * A_HEADS, A_DV), lambda i: (0, i, 0))
        aliases = {}
    else:
        kt_spec = pl.BlockSpec((None, None, A_QK, tm), lambda i: (layer, i // per_b, 0, i % per_b))
        v_spec = pl.BlockSpec((None, tm * A_HEADS, A_DV), lambda i: (layer, i, 0))
        in_specs += [pl.BlockSpec(memory_space=pl.ANY)] * 2
        operands += [kt_prev, v_prev]
        aliases = {4: 1, 5: 2}
    return pl.pallas_call(
        functools.partial(_proj_a_prompt_body, first=first),
        grid=(m // tm,),
        in_specs=in_specs,
        out_specs=[pl.BlockSpec((tm, A_QK), row),
                   kt_spec,
                   v_spec,
                   pl.BlockSpec((None, A_QK, tm), lambda i: (i // per_b, 0, i % per_b)),
                   pl.BlockSpec((tm, A_V), row)],
        out_shape=[jax.ShapeDtypeStruct((m, A_QK), BF16),
                   jax.ShapeDtypeStruct((n_a, b, A_QK, s), F32),
                   jax.ShapeDtypeStruct((n_a, m * A_HEADS, A_DV), F32),
                   jax.ShapeDtypeStruct((b, A_QK, s), BF16),
                   jax.ShapeDtypeStruct((m, A_V), BF16)],
        input_output_aliases=aliases,
        compiler_params=_params(1),
        name="proj_a_prompt",
    )(*operands)


def _proj_a_sample_body(x_ref, g_ref, w_ref, q_ref, k_ref, v_ref, kb_ref, vb_ref, wb_ref, wkt_ref):
    wb_ref[...] = w_ref[...].astype(BF16)
    wkt_ref[...] = w_ref[:, A_QK:2 * A_QK].T.astype(BF16)
    h = _rms(x_ref[...], g_ref[...]).astype(BF16)
    q = jnp.dot(h, wb_ref[:, 0:A_QK], preferred_element_type=F32)
    q_ref[...] = (q * (A_DK ** -0.5 * LOG2E)).astype(BF16)
    k = jnp.dot(h, wb_ref[:, A_QK:2 * A_QK], preferred_element_type=F32)
    k_ref[...] = k
    kb_ref[...] = k.astype(BF16)
    v = jnp.dot(h, wb_ref[:, 2 * A_QK:], preferred_element_type=F32)
    v_ref[...] = v
    vb_ref[...] = v.astype(BF16)


def _proj_a_sample(x, g, w, layer):
    m = x.shape[0]
    n_out = 2 * A_QK + A_V
    full = lambda i: (0, 0)
    full3 = lambda i: (0, 0, 0)
    return pl.pallas_call(
        _proj_a_sample_body,
        grid=(1,),
        in_specs=[pl.BlockSpec((m, D_MODEL), full),
                  pl.BlockSpec((1, D_MODEL), full),
                  pl.BlockSpec((None, D_MODEL, n_out), lambda i: (layer, 0, 0))],
        out_specs=[pl.BlockSpec((m, A_QK), full),
                   pl.BlockSpec((m, A_QK), full),
                   pl.BlockSpec((m, A_V), full),
                   pl.BlockSpec((m, A_QK), full),
                   pl.BlockSpec((m, A_V), full),
                   pl.BlockSpec((None, D_MODEL, n_out), full3),
                   pl.BlockSpec((None, A_QK, D_MODEL), full3)],
        out_shape=[jax.ShapeDtypeStruct((m, A_QK), BF16),
                   jax.ShapeDtypeStruct((m, A_QK), F32),
                   jax.ShapeDtypeStruct((m, A_V), F32),
                   jax.ShapeDtypeStruct((m, A_QK), BF16),
                   jax.ShapeDtypeStruct((m, A_V), BF16),
                   jax.ShapeDtypeStruct((1, D_MODEL, n_out), BF16),
                   jax.ShapeDtypeStruct((1, A_QK, D_MODEL), BF16)],
        compiler_params=_params(1),
        name="proj_a_sample",
    )(x, g, w)


def _proj_b_body(x_ref, g_ref, w_ref, q_ref, k_ref, v_ref, kb_ref, vb_ref, *wb_ref, tail):
    if wb_ref:
        wb_ref[0][...] = w_ref[...].astype(BF16)
        w_ref = wb_ref[0]
    h = _rms(x_ref[...], g_ref[...]).astype(BF16)
    kv = jnp.dot(h, w_ref[:, B_Q:], preferred_element_type=F32)
    k, v = kv[:, 0:B_KV], kv[:, B_KV:]
    kb_ref[...] = jnp.concatenate([k, pltpu.roll(k, HALF, axis=1)], axis=1).astype(BF16)
    vb_ref[...] = jnp.concatenate([v, pltpu.roll(v, HALF, axis=1)], axis=1).astype(BF16)
    keep = k.shape[0] if tail is None else tail
    k_ref[...] = k[k.shape[0] - keep:]
    v_ref[...] = v[v.shape[0] - keep:]
    q = jnp.dot(h, w_ref[:, 0:B_Q], preferred_element_type=F32)
    q_ref[...] = (q * (B_HD ** -0.5 * LOG2E)).astype(BF16)


def _proj_b(x, g, w, layer, tm, *, seq=None, tail=None, cast_weights=False):
    m = x.shape[0]
    assert not cast_weights or m == tm
    n_out = B_Q + 2 * B_KV
    row = lambda i: (i, 0)
    full = lambda i: (0, 0)
    wb_spec = [pl.BlockSpec((None, D_MODEL, n_out), lambda i: (0, 0, 0))] if cast_weights else []
    wb_shape = [jax.ShapeDtypeStruct((1, D_MODEL, n_out), BF16)] if cast_weights else []
    if tail is None:
        kv_spec = pl.BlockSpec((tm, B_KV), row)
        kv_shape = jax.ShapeDtypeStruct((m, B_KV), F32)
    else:
        assert seq % tm == 0 and tail <= tm
        per_seq = seq // tm
        kv_spec = pl.BlockSpec((None, tail, B_KV), lambda i: (i // per_seq, 0, 0))
        kv_shape = jax.ShapeDtypeStruct((m // seq, tail, B_KV), F32)
    return pl.pallas_call(
        functools.partial(_proj_b_body, tail=tail),
        grid=(m // tm,),
        in_specs=[pl.BlockSpec((tm, D_MODEL), row),
                  pl.BlockSpec((1, D_MODEL), full),
                  pl.BlockSpec((None, D_MODEL, B_Q + 2 * B_KV), lambda i: (layer, 0, 0))],
        out_specs=[pl.BlockSpec((tm, B_Q), row),
                   kv_spec,
                   kv_spec,
                   pl.BlockSpec((tm, 2 * B_KV), row),
                   pl.BlockSpec((tm, 2 * B_KV), row)] + wb_spec,
        out_shape=[jax.ShapeDtypeStruct((m, B_Q), BF16),
                   kv_shape,
                   kv_shape,
                   jax.ShapeDtypeStruct((m, 2 * B_KV), BF16),
                   jax.ShapeDtypeStruct((m, 2 * B_KV), BF16)] + wb_shape,
        compiler_params=_params(1),
        name="proj_b",
    )(x, g, w)


def _t5_bucket(rel):
    nb = NUM_BUCKETS // 2
    n = -rel
    ret = jnp.where(n < 0, nb, 0)
    n = jnp.abs(n)
    max_exact = nb // 2
    nf = jnp.maximum(n, 1).astype(jnp.float32)
    large = max_exact + (jnp.log(nf / max_exact) / math.log(MAX_DIST / max_exact)
                         * (nb - max_exact)).astype(jnp.int32)
    large = jnp.minimum(large, nb - 1)
    return ret + jnp.where(n < max_exact, n, large)


def _check_far_bucket(min_dist):
    nb, max_exact = NUM_BUCKETS // 2, NUM_BUCKETS // 4
    ratio = math.log(min_dist / max_exact) / math.log(MAX_DIST / max_exact)
    assert ratio * (nb - max_exact) >= nb - 1 - max_exact + 1e-3, "far keys need one shared bucket"


def _far_bias(table, tq, w):
    far = table[NUM_BUCKETS // 2 - 1].astype(F32) * LOG2E
    return jnp.broadcast_to(far[:, None, None], (N_MAPS, tq, w))


def _near_bias(table, tq, w, back, band):
    r = jnp.arange(tq)[:, None]
    j = jnp.arange(w)[None, :] - back
    onehot = (_t5_bucket(j - r)[None] == jnp.arange(NUM_BUCKETS)[:, None, None]).astype(F32)
    onehot = lax.optimization_barrier(onehot)
    bias = jnp.einsum("bm,brj->mrj", table.astype(F32), onehot, precision=lax.Precision.HIGHEST)
    kc, qc = j // CHUNK, r // CHUNK
    mask = kc <= qc
    if band is not None:
        mask = mask & (kc >= qc - band)
    return jnp.where(mask[None], bias * LOG2E, NEG)


A_TILE = 256
A_ROWS = 64
A_HEADS_PER_STEP = 1


def _diff_lambda(lamp_ref, lam_init):
    lp = lamp_ref[...]
    return (jnp.exp(jnp.sum(lp[0:1] * lp[1:2], axis=-1, keepdims=True))
            - jnp.exp(jnp.sum(lp[2:3] * lp[3:4], axis=-1, keepdims=True)) + lam_init)


def _stack_maps(q):
    lane = lax.broadcasted_iota(jnp.int32, q.shape, 1)
    zero = jnp.zeros_like(q)
    return jnp.concatenate([jnp.where(lane < HALF, q, zero), jnp.where(lane >= HALF, q, zero)], axis=0)


def _diff_epilogue(acc, lam, g, lam_init):
    tq = acc.shape[0] // 2
    o_all = acc[:, :A_DV] / acc[:, A_DV:]
    o = o_all[:tq] - lam * o_all[tq:]
    return (_rms(o, g) * (1.0 - lam_init)).astype(BF16)


def _attn_a_scores(i, q, kt_ref, bias_ref, s_ref, m_ref):
    tl = A_TILE
    qs = _stack_maps(q)
    n_far = max(i - 1, 0)
    for kt in range(i + 1):
        cols = slice(kt * tl, (kt + 1) * tl)
        s = jnp.dot(qs, kt_ref[:, cols], preferred_element_type=F32)
        near = kt >= n_far
        if near:
            s = s + bias_ref[2 - (i - kt)].reshape(2 * tl, tl)
        s_ref[:, cols] = s
        fold = _lane_fold_max(s)
        first = kt == (n_far if near else 0)
        m_ref[int(near)] = fold if first else jnp.maximum(m_ref[int(near)], fold)
        yield
    for r in range(0, 2 * tl, A_ROWS):
        rows = slice(r, r + A_ROWS)
        mx = m_ref[1, rows, :]
        if n_far:
            c = bias_ref[0, r // tl, r % tl:r % tl + A_ROWS, 0:LANES]
            mx = jnp.maximum(mx, m_ref[0, rows, :] + c)
        m = jnp.broadcast_to(jnp.max(mx, axis=-1, keepdims=True), (A_ROWS, LANES))
        m_ref[1, rows, :] = m
        if n_far:
            m_ref[0, rows, :] = m - c


def _attn_a_numerators(i, s_ref, m_ref, v1_ref, out):
    tl = A_TILE
    n_far = max(i - 1, 0)
    acc = None
    for kt in range(i + 1):
        cols = slice(kt * tl, (kt + 1) * tl)
        p = _exp_rows(s_ref[:, cols], m_ref[1 if kt >= n_far else 0])
        part = jnp.dot(p, v1_ref[cols, :], preferred_element_type=F32)
        acc = part if acc is None else acc + part
        yield
    out.append(acc)


def _interleave(*steps):
    steps = list(steps)
    while steps:
        for gen in list(steps):
            if next(gen, StopIteration) is StopIteration:
                steps.remove(gen)


def _attn_a_prompt_body(lamp_ref, q_ref, kt_ref, v_ref, bias_ref, g_ref, o_ref,
                        s_ref, m_ref, v1_ref, *, nt, nh, lam_init):
    tl = A_TILE
    lam = _diff_lambda(lamp_ref, lam_init)
    for hh in range(nh):
        v1_ref[hh, :, 0:A_DV] = v_ref[0, :, hh * A_DV:(hh + 1) * A_DV]
        v1_ref[hh, :, A_DV:] = jnp.ones((nt * tl, A_DV), BF16)
    work = [(i, hh) for i in reversed(range(nt)) for hh in range(nh)]

    def scores(i, hh):
        lanes = slice(hh * LANES, (hh + 1) * LANES)
        return _attn_a_scores(i, q_ref[0, i * tl:(i + 1) * tl, lanes], kt_ref.at[0, lanes, :],
                              bias_ref.at[:, 2 * hh:2 * hh + 2], s_ref.at[hh, i % 2],
                              m_ref.at[hh, i % 2])

    _interleave(scores(*work[0]))
    for n, (i, hh) in enumerate(work):
        out = []
        steps = [_attn_a_numerators(i, s_ref.at[hh, i % 2], m_ref.at[hh, i % 2], v1_ref.at[hh], out)]
        if n + 1 < len(work):
            steps.insert(0, scores(*work[n + 1]))
        _interleave(*steps)
        acc = out[0]
        o_ref[0, i * tl:(i + 1) * tl, hh * LANES:(hh + 1) * LANES] = _diff_epilogue(
            acc, lam, g_ref[...], lam_init)


def _attn_a_prompt(q, kt, v, bias, lamp, g, layer, *, lam_init):
    b, s, _ = q.shape
    nt = s // A_TILE
    nh = A_HEADS_PER_STEP
    assert s % A_TILE == 0 and A_HEADS % nh == 0
    seq = pl.BlockSpec((1, s, nh * LANES), lambda h, bi: (bi, 0, h))
    return pl.pallas_call(
        functools.partial(_attn_a_prompt_body, nt=nt, nh=nh, lam_init=lam_init),
        grid=(A_HEADS // nh, b),
        in_specs=[pl.BlockSpec((None, 4, A_DK), lambda h, bi: (layer, 0, 0)),
                  seq,
                  pl.BlockSpec((1, nh * LANES, s), lambda h, bi: (bi, h, 0)),
                  seq,
                  pl.BlockSpec((3, 2 * nh, A_TILE, A_TILE), lambda h, bi: (0, h, 0, 0)),
                  pl.BlockSpec((None, 1, A_DV), lambda h, bi: (layer, 0, 0))],
        out_specs=seq,
        out_shape=jax.ShapeDtypeStruct((b, s, A_V), BF16),
        scratch_shapes=[pltpu.VMEM((nh, 2, 2 * A_TILE, s), F32),
                        pltpu.VMEM((nh, 2, 2, 2 * A_TILE, LANES), F32),
                        pltpu.VMEM((nh, s, 2 * A_DV), BF16)],
        compiler_params=_params(2),
        name="attn_a_prompt",
    )(lamp, q, kt, v, bias, g)


def _attn_a_sample_body(lamp_ref, q_ref, ktc_ref, vc_ref, kn_ref, vn_ref, bias_ref, g_ref, o_ref,
                        *, lam_init):
    tq = q_ref.shape[1]
    past = ktc_ref.shape[1]
    lam = _diff_lambda(lamp_ref, lam_init)
    for h in range(A_HEADS):
        cols = slice(h * LANES, (h + 1) * LANES)
        qs = _stack_maps(q_ref[0, :, cols])
        bias = bias_ref[2 * h:2 * h + 2].reshape(2 * tq, past + LANES)
        s_c = jnp.dot(qs, ktc_ref[cols, :].astype(BF16), preferred_element_type=F32)
        s_c = s_c + bias[:, 0:past]
        s_n = lax.dot_general(qs, kn_ref[0, :, cols], NT_DIMS, preferred_element_type=F32)
        s_n = s_n + bias[:, past:past + tq]
        m = jnp.maximum(_row_max(s_c), jnp.broadcast_to(jnp.max(s_n, axis=-1, keepdims=True),
                                                        (2 * tq, LANES)))
        p_c = _exp_rows(s_c, m)
        p_n = jnp.exp2(s_n - m[:, 0:tq]).astype(BF16)
        v_c = vc_ref[pl.ds(h, past, stride=A_HEADS), :].astype(BF16)
        acc = (jnp.dot(p_c, _with_ones(v_c), preferred_element_type=F32)
               + jnp.dot(p_n, _with_ones(vn_ref[0, :, cols]), preferred_element_type=F32))
        o_ref[0, :, cols] = _diff_epilogue(acc, lam, g_ref[...], lam_init)


def _attn_a_sample(q, ktc, vc, kn, vn, bias, lamp, g, layer, *, lam_init):
    b, tq, _ = q.shape
    past = ktc.shape[3]
    new = lambda bi: (bi, 0, 0)
    return pl.pallas_call(
        functools.partial(_attn_a_sample_body, lam_init=lam_init),
        grid=(b,),
        in_specs=[pl.BlockSpec((None, 4, A_DK), lambda bi: (layer, 0, 0)),
                  pl.BlockSpec((1, tq, A_QK), new),
                  pl.BlockSpec((None, None, A_QK, past), lambda bi: (layer, bi, 0, 0)),
                  pl.BlockSpec((None, None, past * A_HEADS, A_DV), lambda bi: (layer, bi, 0, 0)),
                  pl.BlockSpec((1, tq, A_QK), new),
                  pl.BlockSpec((1, tq, A_V), new),
                  pl.BlockSpec((N_MAPS, tq, past + LANES), lambda bi: (0, 0, 0)),
                  pl.BlockSpec((None, 1, A_DV), lambda bi: (layer, 0, 0))],
        out_specs=pl.BlockSpec((1, tq, A_V), new),
        out_shape=jax.ShapeDtypeStruct((b, tq, A_V), BF16),
        compiler_params=_params(1),
        name="attn_a_sample",
    )(lamp, q, ktc, vc, kn, vn, bias, g)


_B_ORDER_A = (0, 2, 4, 6, 9, 11, 13, 15)
_B_ORDER_B = (1, 3, 5, 7, 8, 10, 12, 14)


def _attn_b_body(sink_ref, q_ref, k_ref, v_ref, bias_ref, o_ref, *, layer, tq, w, back, i_off, nsub):
    lane = lax.broadcasted_iota(jnp.int32, (tq, LANES), 1)
    lo = lane < HALF
    half = B_Q_HEADS // 4
    orders = (_B_ORDER_A, _B_ORDER_B)
    sinks = [jnp.concatenate([jnp.full((tq, LANES), sink_ref[layer, h] * LOG2E, F32)
                              for h in orders[t]], axis=0) for t in range(2)]
    outs = {}

    def chain(u, t):
        tile = pl.program_id(1) * nsub + u + i_off
        variant = 1 if (u > 0 or i_off > 0) else jnp.minimum(tile, 1)
        near_start = pl.multiple_of(jnp.maximum(tile * tq - back, 0), CHUNK)
        group = slice(t * LANES, (t + 1) * LANES)
        pieces = []
        for c in range(B_Q_HEADS // 2):
            qc = q_ref[0, u * tq:(u + 1) * tq, c * LANES:(c + 1) * LANES]
            keep_lo = (c < half) == (t == 0)
            pieces.append(jnp.where(lo == keep_lo, qc, jnp.zeros_like(qc)))
        qs = jnp.concatenate(pieces, axis=0)
        s = lax.dot_general(qs, k_ref[0, pl.ds(near_start, w), group], NT_DIMS,
                            preferred_element_type=F32)
        s = s + bias_ref[variant, t].reshape((B_Q_HEADS // 2) * tq, w)
        m = jnp.maximum(_row_max(s), sinks[t])
        p = _exp_rows(s, m)
        acc = jnp.dot(p, _with_ones(v_ref[0, pl.ds(near_start, w), group]),
                      preferred_element_type=F32)
        outs[u, t] = acc[:, :LANES] / (acc[:, LANES:] + jnp.exp2(sinks[t] - m))

    for u in range(nsub):
        for t in range(2):
            chain(u, t)

    for u in range(nsub):
        for c in range(B_Q_HEADS // 2):
            oa = outs[u, 0][c * tq:(c + 1) * tq]
            ob = outs[u, 1][c * tq:(c + 1) * tq]
            oc = jnp.where(lo, oa, ob) if c < half else jnp.where(lo, ob, oa)
            o_ref[0, u * tq:(u + 1) * tq, c * LANES:(c + 1) * LANES] = oc.astype(BF16)


def _attn_b(q, k, v, bias, sinks, layer, *, tq, w, back, i_off, nsub):
    b, sq, _ = q.shape
    sk = k.shape[1]
    rows = nsub * tq
    assert sq % rows == 0 and (sq // tq - 1 + i_off) * tq - back + w <= sk
    body = functools.partial(_attn_b_body, layer=layer, tq=tq, w=w, back=back, i_off=i_off,
                             nsub=nsub)
    return pl.pallas_call(
        body,
        grid=(b, sq // rows),
        in_specs=[pl.BlockSpec(memory_space=pltpu.SMEM),
                  pl.BlockSpec((1, rows, B_Q), lambda bi, i: (bi, i, 0)),
                  pl.BlockSpec((1, sk, 2 * B_KV), lambda bi, i: (bi, 0, 0)),
                  pl.BlockSpec((1, sk, 2 * B_KV), lambda bi, i: (bi, 0, 0)),
                  pl.BlockSpec((2, 2, B_Q_HEADS // 2, tq, w), lambda bi, i: (0, 0, 0, 0, 0))],
        out_specs=pl.BlockSpec((1, rows, B_Q), lambda bi, i: (bi, i, 0)),
        out_shape=jax.ShapeDtypeStruct((b, sq, B_Q), BF16),
        compiler_params=_params(2),
        name="attn_b",
    )(sinks, q, k, v, bias)


def _post_body(x_ref, o_ref, wo_ref, g_ref, wu_ref, wd_ref, gf_ref, y_ref, *refs, final, cast_weights):
    hn_ref = refs[-1]
    f = pl.program_id(1)
    if cast_weights:
        wob_ref, wub_ref, wdb_ref = refs[:3]

    @pl.when(f == 0)
    def _():
        if cast_weights:
            wob_ref[...] = wo_ref[...].astype(BF16)
        wo = (wob_ref if cast_weights else wo_ref)[...]
        x1 = x_ref[...] + jnp.dot(o_ref[...], wo, preferred_element_type=F32)
        y_ref[...] = x1
        hn_ref[...] = _rms(x1, g_ref[...]).astype(BF16)

    if cast_weights:
        wub_ref[...] = wu_ref[...].astype(BF16)
        wdb_ref[...] = wd_ref[...].astype(BF16)
        wu_ref, wd_ref = wub_ref, wdb_ref
    u = jnp.dot(hn_ref[...], wu_ref[...], preferred_element_type=F32)
    a = jnp.square(jnp.maximum(u, 0.0)).astype(BF16)
    y_ref[...] += jnp.dot(a, wd_ref[...], preferred_element_type=F32)

    if final:
        @pl.when(f == pl.num_programs(1) - 1)
        def _():
            y_ref[...] = _rms(y_ref[...], gf_ref[...])


def _post(x, o, wo, wo_layer, g, g_layer, wu, wd, w_layer, gf, *, tm, tf, final, cast_weights):
    m = x.shape[0]
    assert not cast_weights or m == tm
    row = lambda i, f: (i, 0)
    full = lambda i, f: (0, 0)
    out_specs = [pl.BlockSpec((tm, D_MODEL), row)]
    out_shape = [jax.ShapeDtypeStruct((m, D_MODEL), F32)]
    if cast_weights:
        out_specs += [pl.BlockSpec((None, D_MODEL, D_MODEL), lambda i, f: (0, 0, 0)),
                      pl.BlockSpec((None, D_MODEL, tf), lambda i, f: (0, 0, f)),
                      pl.BlockSpec((None, tf, D_MODEL), lambda i, f: (0, f, 0))]
        out_shape += [jax.ShapeDtypeStruct((1, D_MODEL, D_MODEL), BF16),
                      jax.ShapeDtypeStruct((1, D_MODEL, D_FF), BF16),
                      jax.ShapeDtypeStruct((1, D_FF, D_MODEL), BF16)]
    outs = pl.pallas_call(
        functools.partial(_post_body, final=final, cast_weights=cast_weights),
        grid=(m // tm, D_FF // tf),
        in_specs=[pl.BlockSpec((tm, D_MODEL), row),
                  pl.BlockSpec((tm, D_MODEL), row),
                  pl.BlockSpec((None, D_MODEL, D_MODEL), lambda i, f: (wo_layer, 0, 0)),
                  pl.BlockSpec((None, 1, D_MODEL), lambda i, f: (g_layer, 0, 0)),
                  pl.BlockSpec((None, D_MODEL, tf), lambda i, f: (w_layer, 0, f)),
                  pl.BlockSpec((None, tf, D_MODEL), lambda i, f: (w_layer, f, 0)),
                  pl.BlockSpec((1, D_MODEL), full)],
        out_specs=out_specs,
        out_shape=out_shape,
        scratch_shapes=[pltpu.VMEM((tm, D_MODEL), BF16)],
        compiler_params=_params(2),
        name="post",
    )(x, o, wo, g, wu, wd, gf)
    return outs if cast_weights else outs[0]


def _post_resident_body(x_ref, o_ref, wo_ref, g_ref, wu_ref, wd_ref, gf_ref, y_ref, hn_ref,
                        *, tf, final):
    x1 = x_ref[...] + jnp.dot(o_ref[...], wo_ref[...], preferred_element_type=F32)
    y_ref[...] = x1
    hn_ref[...] = _rms(x1, g_ref[...]).astype(BF16)
    mlp = None
    for f in range(D_FF // tf):
        u = jnp.dot(hn_ref[...], wu_ref[:, f * tf:(f + 1) * tf], preferred_element_type=F32)
        a = jnp.square(jnp.maximum(u, 0.0)).astype(BF16)
        part = jnp.dot(a, wd_ref[f * tf:(f + 1) * tf, :], preferred_element_type=F32)
        mlp = part if mlp is None else mlp + part
    y = y_ref[...] + mlp
    y_ref[...] = _rms(y, gf_ref[...]) if final else y


def _post_resident(x, o, wo, g, g_layer, wu, wd, gf, *, tm, tf, final):
    m = x.shape[0]
    row = lambda i: (i, 0)
    once = dict(pipeline_mode=pl.Buffered(1))
    return pl.pallas_call(
        functools.partial(_post_resident_body, tf=tf, final=final),
        grid=(m // tm,),
        in_specs=[pl.BlockSpec((tm, D_MODEL), row),
                  pl.BlockSpec((tm, D_MODEL), row),
                  pl.BlockSpec((None, D_MODEL, D_MODEL), lambda i: (0, 0, 0), **once),
                  pl.BlockSpec((None, 1, D_MODEL), lambda i: (g_layer, 0, 0)),
                  pl.BlockSpec((None, D_MODEL, D_FF), lambda i: (0, 0, 0), **once),
                  pl.BlockSpec((None, D_FF, D_MODEL), lambda i: (0, 0, 0), **once),
                  pl.BlockSpec((1, D_MODEL), lambda i: (0, 0))],
        out_specs=pl.BlockSpec((tm, D_MODEL), row),
        out_shape=jax.ShapeDtypeStruct((m, D_MODEL), F32),
        scratch_shapes=[pltpu.VMEM((tm, D_MODEL), BF16)],
        compiler_params=_params(1),
        name="post_resident",
    )(x, o, wo, g, wu, wd, gf)


_B_PROMPT = dict(tq=128, w=256, back=128)
_B_SAMPLE = dict(tq=64, w=256, back=128)
B_TILES_PER_STEP = 8
_A_SAMPLE_BACK = 128


def _a_prompt_bias(table):
    _check_far_bucket(A_TILE + 1)
    near = _near_bias(table, A_TILE, 2 * A_TILE, A_TILE, None)
    return jnp.stack([_far_bias(table, A_TILE, A_TILE), near[:, :, :A_TILE], near[:, :, A_TILE:]])


def _a_sample_bias(table, past, t):
    assert t <= LANES
    _check_far_bucket(_A_SAMPLE_BACK + 1)
    far = _far_bias(table, t, past - _A_SAMPLE_BACK)
    near = _near_bias(table, t, _A_SAMPLE_BACK + LANES, _A_SAMPLE_BACK, None)
    return jnp.concatenate([far, near], axis=2)


def _b_bias(table, tq, w, back):
    def one(bk):
        t = _near_bias(table, tq, w, bk, WINDOW // CHUNK)
        return jnp.stack([jnp.stack([t[h] for h in order]) for order in (_B_ORDER_A, _B_ORDER_B)])
    return jnp.stack([one(0), one(back)])


def _pad_keys(t, total):
    return jnp.pad(t, ((0, 0), (0, total - t.shape[1]), (0, 0)))


def _trunk(x, caches, wts, table, bf16_weights, *, tm, tm_post, tf):
    b, s, _ = x.shape
    sample = caches is not None
    n_a = wts["a_w_qkv"].shape[0]
    xm = x.reshape(b * s, D_MODEL)
    b_cfg = _B_SAMPLE if sample else _B_PROMPT
    b_bias = _b_bias(table, **b_cfg)
    if sample:
        a_bias = _a_sample_bias(table, caches["a_kt"].shape[3], s)
    else:
        a_bias = _a_prompt_bias(table)
    ak, av, bk, bv = [], [], [], []
    kt_buf = v_buf = None
    for i in range(DEPTH):
        j = i // 2
        g_mix = wts["norm_mix_g"][i][None, :]
        if sample:
            bf16_weights.append({})
        cast = bf16_weights[i]
        if i % 2 == 0:
            lam_init = _lambda_init(i)
            if sample:
                q, k, v, kb, vb, cast["qkv"], cast["kt"] = _proj_a_sample(xm, g_mix, wts["a_w_qkv"], j)
                ak.append(k.reshape(b, s, A_HEADS, 2, A_DK))
                av.append(v.reshape(b, s, A_HEADS, A_DV))
                o = _attn_a_sample(q.reshape(b, s, A_QK), caches["a_kt"], caches["a_v"],
                                   kb.reshape(b, s, A_QK), vb.reshape(b, s, A_V), a_bias,
                                   wts["a_lambda"], wts["a_subln_g"], j, lam_init=lam_init)
            else:
                q, kt_buf, v_buf, ktb, vb = _proj_a_prompt(
                    xm, g_mix, cast["qkv"], cast["kt"], 0, j, kt_buf, v_buf,
                    b=b, s=s, tm=tm, n_a=n_a)
                o = _attn_a_prompt(q.reshape(b, s, A_QK), ktb, vb.reshape(b, s, A_V), a_bias,
                                   wts["a_lambda"], wts["a_subln_g"], j, lam_init=lam_init)
            wo, wo_layer = wts["a_w_o"], j
        else:
            if sample:
                q, k, v, kb, vb, cast["qkv"] = _proj_b(xm, g_mix, wts["b_w_qkv"], j, tm,
                                                       cast_weights=True)
            else:
                q, k, v, kb, vb = _proj_b(xm, g_mix, cast["qkv"], 0, tm, seq=s, tail=WINDOW)
            k = k.reshape(b, -1, B_KV_HEADS, B_HD)
            v = v.reshape(b, -1, B_KV_HEADS, B_HD)
            kb = kb.reshape(b, s, 2 * B_KV)
            vb = vb.reshape(b, s, 2 * B_KV)
            i_off = 0
            if sample:
                ck, cv, ckb, cvb = caches["b"][j]
                bk.append(jnp.concatenate([ck, k], axis=1)[:, s:])
                bv.append(jnp.concatenate([cv, v], axis=1)[:, s:])
                i_off = ckb.shape[1] // b_cfg["tq"]
                total = ckb.shape[1] - b_cfg["back"] + b_cfg["w"]
                kb = _pad_keys(jnp.concatenate([ckb, kb], axis=1), total)
                vb = _pad_keys(jnp.concatenate([cvb, vb], axis=1), total)
            else:
                bk.append(k)
                bv.append(v)
            o = _attn_b(q.reshape(b, s, B_Q), kb, vb, b_bias, wts["b_sinks"], j, i_off=i_off,
                        nsub=1 if sample else B_TILES_PER_STEP, **b_cfg)
            wo, wo_layer = wts["b_w_o"], j
        gf = wts["final_norm_g"][None, :]
        tiling = dict(tm=tm_post, tf=tf, final=(i == DEPTH - 1))
        if sample:
            xm, cast["wo"], cast["wu"], cast["wd"] = _post(
                xm, o.reshape(b * s, D_MODEL), wo, wo_layer, wts["norm_mlp_g"], i,
                wts["mlp_w_up"], wts["mlp_w_down"], i, gf, cast_weights=True, **tiling)
        else:
            xm = _post_resident(xm, o.reshape(b * s, D_MODEL), cast["wo"], wts["norm_mlp_g"], i,
                                cast["wu"], cast["wd"], gf, **tiling)
    if sample:
        a_k, a_v = jnp.stack(ak), jnp.stack(av)
    else:
        a_k = jnp.transpose(kt_buf.reshape(n_a, b, A_HEADS, 2, A_DK, s), (0, 1, 5, 2, 3, 4))
        a_v = v_buf.reshape(n_a, b, s, A_HEADS, A_DV)
    return xm.reshape(b, s, D_MODEL), a_k, a_v, jnp.stack(bk), jnp.stack(bv)


def _swap_halves(t):
    flat = t.reshape(t.shape[:-2] + (B_KV,))
    swapped = t[..., ::-1, :].reshape(t.shape[:-2] + (B_KV,))
    return jnp.concatenate([flat, swapped], axis=-1).astype(BF16)


def kernel(x_prompt, x_sample, cache_a_k, cache_a_v, cache_b_k, cache_b_v, rel_table,
           norm_mix_g, norm_mlp_g, final_norm_g, a_w_qkv, a_lambda, a_subln_g, a_w_o,
           b_w_qkv, b_sinks, b_w_o, mlp_w_up, mlp_w_down):
    wts = dict(norm_mix_g=norm_mix_g, norm_mlp_g=norm_mlp_g[:, None, :], final_norm_g=final_norm_g,
               a_w_qkv=a_w_qkv, a_lambda=a_lambda, a_subln_g=a_subln_g[:, None, :],
               a_w_o=a_w_o, b_w_qkv=b_w_qkv, b_sinks=b_sinks,
               b_w_o=b_w_o, mlp_w_up=mlp_w_up, mlp_w_down=mlp_w_down)
    n_a, db, past = cache_a_k.shape[:3]
    caches = dict(
        a_kt=jnp.transpose(cache_a_k, (0, 1, 3, 4, 5, 2)).reshape(n_a, db, A_QK, past),
        a_v=cache_a_v.reshape(n_a, db, past * A_HEADS, A_DV),
        b=[(cache_b_k[j], cache_b_v[j], _swap_halves(cache_b_k[j]), _swap_halves(cache_b_v[j]))
           for j in range(cache_b_k.shape[0])])
    bf16_weights = []
    ys, aks, avs, bks, bvs = _trunk(x_sample, caches, wts, rel_table, bf16_weights,
                                    tm=512, tm_post=512, tf=1024)
    yp, akp, avp, bkp, bvp = _trunk(x_prompt, None, wts, rel_table, bf16_weights,
                                    tm=512, tm_post=1024, tf=1024)
    return (yp, ys, akp, avp, bkp, bvp, aks, avs, bks, bvs)
```

```python
import functools
import math

import jax
import jax.numpy as jnp
from jax import lax
from jax.experimental import pallas as pl
from jax.experimental.pallas import tpu as pltpu

BF16 = jnp.bfloat16
F32 = jnp.float32

D_MODEL = 1024
DEPTH = 4
CHUNK = 64
A_HEADS = 8
A_DK = 64
A_DV = 128
A_QK = A_HEADS * 2 * A_DK
A_V = A_HEADS * A_DV
B_Q_HEADS = 16
B_KV_HEADS = 2
B_HD = 64
B_Q = B_Q_HEADS * B_HD
B_KV = B_KV_HEADS * B_HD
WINDOW = 128
NUM_BUCKETS = 32
MAX_DIST = 128
N_MAPS = 16
D_FF = 4 * D_MODEL
EPS = 1e-6
NEG = -1e30

LANES = 128
HALF = LANES // 2
VMEM_LIMIT = 56 * 1024 * 1024

NT_DIMS = (((1,), (1,)), ((), ()))

LOG2E = math.log2(math.e)


def _lambda_init(layer):
    return 0.8 - 0.6 * math.exp(-0.3 * layer)


def _rms(x, g):
    return x * lax.rsqrt(jnp.mean(x * x, axis=-1, keepdims=True) + EPS) * g


def _params(n_axes):
    return pltpu.CompilerParams(
        dimension_semantics=("arbitrary",) * n_axes, vmem_limit_bytes=VMEM_LIMIT)


def _lane_fold_max(s):
    mx = s[:, 0:LANES]
    for gi in range(1, s.shape[1] // LANES):
        mx = jnp.maximum(mx, s[:, gi * LANES:(gi + 1) * LANES])
    return mx


def _row_max(s):
    return jnp.broadcast_to(jnp.max(_lane_fold_max(s), axis=-1, keepdims=True), (s.shape[0], LANES))


def _exp_rows(s, m):
    return jnp.concatenate([jnp.exp2(s[:, gi * LANES:(gi + 1) * LANES] - m)
                            for gi in range(s.shape[1] // LANES)], axis=1).astype(BF16)


def _with_ones(v):
    return jnp.concatenate([v, jnp.ones(v.shape, v.dtype)], axis=1)


def _proj_a_prompt_body(x_ref, g_ref, w_ref, wkt_ref, *refs, first):
    q_ref, kt_ref, v_ref, vb_ref = refs[-4:]
    h = _rms(x_ref[...], g_ref[...]).astype(BF16)
    tm = h.shape[0]
    if first:
        for slot in range(1, kt_ref.shape[0]):
            kt_ref[slot] = jnp.zeros(kt_ref.shape[1:], F32)
            v_ref[slot] = jnp.zeros(v_ref.shape[1:], F32)
        kt_ref, v_ref = kt_ref.at[0], v_ref.at[0]
    v = jnp.dot(h, w_ref[:, 2 * A_QK:], preferred_element_type=F32)
    vb_ref[...] = v.astype(BF16)
    for hh in range(A_HEADS):
        v_ref[pl.ds(hh, tm, stride=A_HEADS), :] = v[:, hh * A_DV:(hh + 1) * A_DV]
    kt_ref[...] = lax.dot_general(wkt_ref[...], h, NT_DIMS, preferred_element_type=F32)
    q = jnp.dot(h, w_ref[:, 0:A_QK], preferred_element_type=F32)
    q_ref[...] = (q * (A_DK ** -0.5 * LOG2E)).astype(BF16)


def _proj_a_prompt(x, g, w, wkt, w_layer, layer, kt_prev, v_prev, *, b, s, tm, n_a):
    m = b * s
    per_b = s // tm
    row = lambda i: (i, 0)
    full = lambda i: (0, 0)
    stacked = lambda i: (w_layer, 0, 0)
    first = kt_prev is None
    assert first == (layer == 0)
    in_specs = [pl.BlockSpec((tm, D_MODEL), row),
                pl.BlockSpec((1, D_MODEL), full),
                pl.BlockSpec((None, D_MODEL, 2 * A_QK + A_V), stacked),
                pl.BlockSpec((None, A_QK, D_MODEL), stacked)]
    operands = [x, g, w, wkt]
    if first:
        kt_spec = pl.BlockSpec((n_a, None, A_QK, tm), lambda i: (0, i // per_b, 0, i % per_b))
        v_spec = pl.BlockSpec((n_a, tm * A_HEADS, A_DV), lambda i: (0, i, 0))
        aliases = {}
    else:
        kt_spec = pl.BlockSpec((None, None, A_QK, tm), lambda i: (layer, i // per_b, 0, i % per_b))
        v_spec = pl.BlockSpec((None, tm * A_HEADS, A_DV), lambda i: (layer, i, 0))
        in_specs += [pl.BlockSpec(memory_space=pl.ANY)] * 2
        operands += [kt_prev, v_prev]
        aliases = {4: 1, 5: 2}
    return pl.pallas_call(
        functools.partial(_proj_a_prompt_body, first=first),
        grid=(m // tm,),
        in_specs=in_specs,
        out_specs=[pl.BlockSpec((tm, A_QK), row),
                   kt_spec,
                   v_spec,
                   pl.BlockSpec((tm, A_V), row)],
        out_shape=[jax.ShapeDtypeStruct((m, A_QK), BF16),
                   jax.ShapeDtypeStruct((n_a, b, A_QK, s), F32),
                   jax.ShapeDtypeStruct((n_a, m * A_HEADS, A_DV), F32),
                   jax.ShapeDtypeStruct((m, A_V), BF16)],
        input_output_aliases=aliases,
        compiler_params=_params(1),
        name="proj_a_prompt",
    )(*operands)


def _proj_a_sample_body(x_ref, g_ref, w_ref, q_ref, k_ref, v_ref, kb_ref, vb_ref, wb_ref, wkt_ref):
    wb_ref[...] = w_ref[...].astype(BF16)
    wkt_ref[...] = w_ref[:, A_QK:2 * A_QK].T.astype(BF16)
    h = _rms(x_ref[...], g_ref[...]).astype(BF16)
    q = jnp.dot(h, wb_ref[:, 0:A_QK], preferred_element_type=F32)
    q_ref[...] = (q * (A_DK ** -0.5 * LOG2E)).astype(BF16)
    k = jnp.dot(h, wb_ref[:, A_QK:2 * A_QK], preferred_element_type=F32)
    k_ref[...] = k
    kb_ref[...] = k.astype(BF16)
    v = jnp.dot(h, wb_ref[:, 2 * A_QK:], preferred_element_type=F32)
    v_ref[...] = v
    vb_ref[...] = v.astype(BF16)


def _proj_a_sample(x, g, w, layer):
    m = x.shape[0]
    n_out = 2 * A_QK + A_V
    full = lambda i: (0, 0)
    full3 = lambda i: (0, 0, 0)
    return pl.pallas_call(
        _proj_a_sample_body,
        grid=(1,),
        in_specs=[pl.BlockSpec((m, D_MODEL), full),
                  pl.BlockSpec((1, D_MODEL), full),
                  pl.BlockSpec((None, D_MODEL, n_out), lambda i: (layer, 0, 0))],
        out_specs=[pl.BlockSpec((m, A_QK), full),
                   pl.BlockSpec((m, A_QK), full),
                   pl.BlockSpec((m, A_V), full),
                   pl.BlockSpec((m, A_QK), full),
                   pl.BlockSpec((m, A_V), full),
                   pl.BlockSpec((None, D_MODEL, n_out), full3),
                   pl.BlockSpec((None, A_QK, D_MODEL), full3)],
        out_shape=[jax.ShapeDtypeStruct((m, A_QK), BF16),
                   jax.ShapeDtypeStruct((m, A_QK), F32),
                   jax.ShapeDtypeStruct((m, A_V), F32),
                   jax.ShapeDtypeStruct((m, A_QK), BF16),
                   jax.ShapeDtypeStruct((m, A_V), BF16),
                   jax.ShapeDtypeStruct((1, D_MODEL, n_out), BF16),
                   jax.ShapeDtypeStruct((1, A_QK, D_MODEL), BF16)],
        compiler_params=_params(1),
        name="proj_a_sample",
    )(x, g, w)


def _proj_b_body(x_ref, g_ref, w_ref, q_ref, k_ref, v_ref, kb_ref, vb_ref, *wb_ref, tail):
    if wb_ref:
        wb_ref[0][...] = w_ref[...].astype(BF16)
        w_ref = wb_ref[0]
    h = _rms(x_ref[...], g_ref[...]).astype(BF16)
    kv = jnp.dot(h, w_ref[:, B_Q:], preferred_element_type=F32)
    k, v = kv[:, 0:B_KV], kv[:, B_KV:]
    kb_ref[...] = jnp.concatenate([k, pltpu.roll(k, HALF, axis=1)], axis=1).astype(BF16)
    vb_ref[...] = jnp.concatenate([v, pltpu.roll(v, HALF, axis=1)], axis=1).astype(BF16)
    keep = k.shape[0] if tail is None else tail
    k_ref[...] = k[k.shape[0] - keep:]
    v_ref[...] = v[v.shape[0] - keep:]
    q = jnp.dot(h, w_ref[:, 0:B_Q], preferred_element_type=F32)
    q_ref[...] = (q * (B_HD ** -0.5 * LOG2E)).astype(BF16)


def _proj_b(x, g, w, layer, tm, *, seq=None, tail=None, cast_weights=False):
    m = x.shape[0]
    assert not cast_weights or m == tm
    n_out = B_Q + 2 * B_KV
    row = lambda i: (i, 0)
    full = lambda i: (0, 0)
    wb_spec = [pl.BlockSpec((None, D_MODEL, n_out), lambda i: (0, 0, 0))] if cast_weights else []
    wb_shape = [jax.ShapeDtypeStruct((1, D_MODEL, n_out), BF16)] if cast_weights else []
    if tail is None:
        kv_spec = pl.BlockSpec((tm, B_KV), row)
        kv_shape = jax.ShapeDtypeStruct((m, B_KV), F32)
    else:
        assert seq % tm == 0 and tail <= tm
        per_seq = seq // tm
        kv_spec = pl.BlockSpec((None, tail, B_KV), lambda i: (i // per_seq, 0, 0))
        kv_shape = jax.ShapeDtypeStruct((m // seq, tail, B_KV), F32)
    return pl.pallas_call(
        functools.partial(_proj_b_body, tail=tail),
        grid=(m // tm,),
        in_specs=[pl.BlockSpec((tm, D_MODEL), row),
                  pl.BlockSpec((1, D_MODEL), full),
                  pl.BlockSpec((None, D_MODEL, B_Q + 2 * B_KV), lambda i: (layer, 0, 0))],
        out_specs=[pl.BlockSpec((tm, B_Q), row),
                   kv_spec,
                   kv_spec,
                   pl.BlockSpec((tm, 2 * B_KV), row),
                   pl.BlockSpec((tm, 2 * B_KV), row)] + wb_spec,
        out_shape=[jax.ShapeDtypeStruct((m, B_Q), BF16),
                   kv_shape,
                   kv_shape,
                   jax.ShapeDtypeStruct((m, 2 * B_KV), BF16),
                   jax.ShapeDtypeStruct((m, 2 * B_KV), BF16)] + wb_shape,
        compiler_params=_params(1),
        name="proj_b",
    )(x, g, w)


def _t5_bucket(rel):
    nb = NUM_BUCKETS // 2
    n = -rel
    ret = jnp.where(n < 0, nb, 0)
    n = jnp.abs(n)
    max_exact = nb // 2
    nf = jnp.maximum(n, 1).astype(jnp.float32)
    large = max_exact + (jnp.log(nf / max_exact) / math.log(MAX_DIST / max_exact)
                         * (nb - max_exact)).astype(jnp.int32)
    large = jnp.minimum(large, nb - 1)
    return ret + jnp.where(n < max_exact, n, large)


def _check_far_bucket(min_dist):
    nb, max_exact = NUM_BUCKETS // 2, NUM_BUCKETS // 4
    ratio = math.log(min_dist / max_exact) / math.log(MAX_DIST / max_exact)
    assert ratio * (nb - max_exact) >= nb - 1 - max_exact + 1e-3, "far keys need one shared bucket"


def _far_bias(table, tq, w):
    far = table[NUM_BUCKETS // 2 - 1].astype(F32) * LOG2E
    return jnp.broadcast_to(far[:, None, None], (N_MAPS, tq, w))


def _near_bias(table, tq, w, back, band):
    r = jnp.arange(tq)[:, None]
    j = jnp.arange(w)[None, :] - back
    onehot = (_t5_bucket(j - r)[None] == jnp.arange(NUM_BUCKETS)[:, None, None]).astype(F32)
    onehot = lax.optimization_barrier(onehot)
    bias = jnp.einsum("bm,brj->mrj", table.astype(F32), onehot, precision=lax.Precision.HIGHEST)
    kc, qc = j // CHUNK, r // CHUNK
    mask = kc <= qc
    if band is not None:
        mask = mask & (kc >= qc - band)
    return jnp.where(mask[None], bias * LOG2E, NEG)


A_TILE = 256
A_ROWS = 64
A_HEADS_PER_STEP = 1


def _diff_lambda(lamp_ref, lam_init):
    lp = lamp_ref[...]
    return (jnp.exp(jnp.sum(lp[0:1] * lp[1:2], axis=-1, keepdims=True))
            - jnp.exp(jnp.sum(lp[2:3] * lp[3:4], axis=-1, keepdims=True)) + lam_init)


def _stack_maps(q):
    lane = lax.broadcasted_iota(jnp.int32, q.shape, 1)
    zero = jnp.zeros_like(q)
    return jnp.concatenate([jnp.where(lane < HALF, q, zero), jnp.where(lane >= HALF, q, zero)], axis=0)


def _diff_epilogue(acc, lam, g, lam_init):
    tq = acc.shape[0] // 2
    o_all = acc[:, :A_DV] / acc[:, A_DV:]
    o = o_all[:tq] - lam * o_all[tq:]
    return (_rms(o, g) * (1.0 - lam_init)).astype(BF16)


def _attn_a_scores(i, q, kt_ref, bias_ref, s_ref, m_ref):
    tl = A_TILE
    qs = _stack_maps(q)
    n_far = max(i - 1, 0)
    for kt in range(i + 1):
        cols = slice(kt * tl, (kt + 1) * tl)
        s = jnp.dot(qs, kt_ref[:, cols], preferred_element_type=F32)
        near = kt >= n_far
        if near:
            s = s + bias_ref[2 - (i - kt)].reshape(2 * tl, tl)
        s_ref[:, cols] = s
        fold = _lane_fold_max(s)
        first = kt == (n_far if near else 0)
        m_ref[int(near)] = fold if first else jnp.maximum(m_ref[int(near)], fold)
        yield
    for r in range(0, 2 * tl, A_ROWS):
        rows = slice(r, r + A_ROWS)
        mx = m_ref[1, rows, :]
        if n_far:
            c = bias_ref[0, r // tl, r % tl:r % tl + A_ROWS, 0:LANES]
            mx = jnp.maximum(mx, m_ref[0, rows, :] + c)
        m = jnp.broadcast_to(jnp.max(mx, axis=-1, keepdims=True), (A_ROWS, LANES))
        m_ref[1, rows, :] = m
        if n_far:
            m_ref[0, rows, :] = m - c


def _attn_a_numerators(i, s_ref, m_ref, v1_ref, out):
    tl = A_TILE
    n_far = max(i - 1, 0)
    acc = None
    for kt in range(i + 1):
        cols = slice(kt * tl, (kt + 1) * tl)
        p = _exp_rows(s_ref[:, cols], m_ref[1 if kt >= n_far else 0])
        part = jnp.dot(p, v1_ref[cols, :], preferred_element_type=F32)
        acc = part if acc is None else acc + part
        yield
    out.append(acc)


def _interleave(*steps):
    steps = list(steps)
    while steps:
        for gen in list(steps):
            if next(gen, StopIteration) is StopIteration:
                steps.remove(gen)


def _attn_a_prompt_body(lamp_ref, q_ref, kt_ref, v_ref, bias_ref, g_ref, o_ref,
                        s_ref, m_ref, v1_ref, ktb_ref, *, nt, nh, lam_init):
    tl = A_TILE
    lam = _diff_lambda(lamp_ref, lam_init)
    ktb_ref[...] = kt_ref[...].astype(BF16)
    for hh in range(nh):
        v1_ref[hh, :, 0:A_DV] = v_ref[0, :, hh * A_DV:(hh + 1) * A_DV]
        v1_ref[hh, :, A_DV:] = jnp.ones((nt * tl, A_DV), BF16)
    work = [(i, hh) for i in reversed(range(nt)) for hh in range(nh)]

    def scores(i, hh):
        lanes = slice(hh * LANES, (hh + 1) * LANES)
        return _attn_a_scores(i, q_ref[0, i * tl:(i + 1) * tl, lanes], ktb_ref.at[lanes, :],
                              bias_ref.at[:, 2 * hh:2 * hh + 2], s_ref.at[hh, i % 2],
                              m_ref.at[hh, i % 2])

    _interleave(scores(*work[0]))
    for n, (i, hh) in enumerate(work):
        out = []
        steps = [_attn_a_numerators(i, s_ref.at[hh, i % 2], m_ref.at[hh, i % 2], v1_ref.at[hh], out)]
        if n + 1 < len(work):
            steps.insert(0, scores(*work[n + 1]))
        _interleave(*steps)
        acc = out[0]
        o_ref[0, i * tl:(i + 1) * tl, hh * LANES:(hh + 1) * LANES] = _diff_epilogue(
            acc, lam, g_ref[...], lam_init)


def _attn_a_prompt(q, kt, v, bias, lamp, g, layer, *, lam_init):
    b, s, _ = q.shape
    nt = s // A_TILE
    nh = A_HEADS_PER_STEP
    assert s % A_TILE == 0 and A_HEADS % nh == 0
    seq = pl.BlockSpec((1, s, nh * LANES), lambda h, bi: (bi, 0, h))
    return pl.pallas_call(
        functools.partial(_attn_a_prompt_body, nt=nt, nh=nh, lam_init=lam_init),
        grid=(A_HEADS // nh, b),
        in_specs=[pl.BlockSpec((None, 4, A_DK), lambda h, bi: (layer, 0, 0)),
                  seq,
                  pl.BlockSpec((None, None, nh * LANES, s), lambda h, bi: (layer, bi, h, 0)),
                  seq,
                  pl.BlockSpec((3, 2 * nh, A_TILE, A_TILE), lambda h, bi: (0, h, 0, 0)),
                  pl.BlockSpec((None, 1, A_DV), lambda h, bi: (layer, 0, 0))],
        out_specs=seq,
        out_shape=jax.ShapeDtypeStruct((b, s, A_V), BF16),
        scratch_shapes=[pltpu.VMEM((nh, 2, 2 * A_TILE, s), F32),
                        pltpu.VMEM((nh, 2, 2, 2 * A_TILE, LANES), F32),
                        pltpu.VMEM((nh, s, 2 * A_DV), BF16),
                        pltpu.VMEM((nh * LANES, s), BF16)],
        compiler_params=_params(2),
        name="attn_a_prompt",
    )(lamp, q, kt, v, bias, g)


def _attn_a_sample_body(lamp_ref, q_ref, ktc_ref, vc_ref, kn_ref, vn_ref, bias_ref, g_ref, o_ref,
                        *, lam_init):
    tq = q_ref.shape[1]
    past = ktc_ref.shape[1]
    lam = _diff_lambda(lamp_ref, lam_init)
    for h in range(A_HEADS):
        cols = slice(h * LANES, (h + 1) * LANES)
        qs = _stack_maps(q_ref[0, :, cols])
        bias = bias_ref[2 * h:2 * h + 2].reshape(2 * tq, past + LANES)
        s_c = jnp.dot(qs, ktc_ref[cols, :].astype(BF16), preferred_element_type=F32)
        s_c = s_c + bias[:, 0:past]
        s_n = lax.dot_general(qs, kn_ref[0, :, cols], NT_DIMS, preferred_element_type=F32)
        s_n = s_n + bias[:, past:past + tq]
        m = jnp.maximum(_row_max(s_c), jnp.broadcast_to(jnp.max(s_n, axis=-1, keepdims=True),
                                                        (2 * tq, LANES)))
        p_c = _exp_rows(s_c, m)
        p_n = jnp.exp2(s_n - m[:, 0:tq]).astype(BF16)
        v_c = vc_ref[pl.ds(h, past, stride=A_HEADS), :].astype(BF16)
        acc = (jnp.dot(p_c, _with_ones(v_c), preferred_element_type=F32)
               + jnp.dot(p_n, _with_ones(vn_ref[0, :, cols]), preferred_element_type=F32))
        o_ref[0, :, cols] = _diff_epilogue(acc, lam, g_ref[...], lam_init)


def _attn_a_sample(q, ktc, vc, kn, vn, bias, lamp, g, layer, *, lam_init):
    b, tq, _ = q.shape
    past = ktc.shape[3]
    new = lambda bi: (bi, 0, 0)
    return pl.pallas_call(
        functools.partial(_attn_a_sample_body, lam_init=lam_init),
        grid=(b,),
        in_specs=[pl.BlockSpec((None, 4, A_DK), lambda bi: (layer, 0, 0)),
                  pl.BlockSpec((1, tq, A_QK), new),
                  pl.BlockSpec((None, None, A_QK, past), lambda bi: (layer, bi, 0, 0)),
                  pl.BlockSpec((None, None, past * A_HEADS, A_DV), lambda bi: (layer, bi, 0, 0)),
                  pl.BlockSpec((1, tq, A_QK), new),
                  pl.BlockSpec((1, tq, A_V), new),
                  pl.BlockSpec((N_MAPS, tq, past + LANES), lambda bi: (0, 0, 0)),
                  pl.BlockSpec((None, 1, A_DV), lambda bi: (layer, 0, 0))],
        out_specs=pl.BlockSpec((1, tq, A_V), new),
        out_shape=jax.ShapeDtypeStruct((b, tq, A_V), BF16),
        compiler_params=_params(1),
        name="attn_a_sample",
    )(lamp, q, ktc, vc, kn, vn, bias, g)


_B_ORDER_A = (0, 2, 4, 6, 9, 11, 13, 15)
_B_ORDER_B = (1, 3, 5, 7, 8, 10, 12, 14)


def _attn_b_body(sink_ref, q_ref, k_ref, v_ref, bias_ref, o_ref, *, layer, tq, w, back, i_off, nsub):
    lane = lax.broadcasted_iota(jnp.int32, (tq, LANES), 1)
    lo = lane < HALF
    half = B_Q_HEADS // 4
    orders = (_B_ORDER_A, _B_ORDER_B)
    sinks = [jnp.concatenate([jnp.full((tq, LANES), sink_ref[layer, h] * LOG2E, F32)
                              for h in orders[t]], axis=0) for t in range(2)]
    outs = {}

    def chain(u, t):
        tile = pl.program_id(1) * nsub + u + i_off
        variant = 1 if (u > 0 or i_off > 0) else jnp.minimum(tile, 1)
        near_start = pl.multiple_of(jnp.maximum(tile * tq - back, 0), CHUNK)
        group = slice(t * LANES, (t + 1) * LANES)
        pieces = []
        for c in range(B_Q_HEADS // 2):
            qc = q_ref[0, u * tq:(u + 1) * tq, c * LANES:(c + 1) * LANES]
            keep_lo = (c < half) == (t == 0)
            pieces.append(jnp.where(lo == keep_lo, qc, jnp.zeros_like(qc)))
        qs = jnp.concatenate(pieces, axis=0)
        s = lax.dot_general(qs, k_ref[0, pl.ds(near_start, w), group], NT_DIMS,
                            preferred_element_type=F32)
        s = s + bias_ref[variant, t].reshape((B_Q_HEADS // 2) * tq, w)
        m = jnp.maximum(_row_max(s), sinks[t])
        p = _exp_rows(s, m)
        acc = jnp.dot(p, _with_ones(v_ref[0, pl.ds(near_start, w), group]),
                      preferred_element_type=F32)
        outs[u, t] = acc[:, :LANES] / (acc[:, LANES:] + jnp.exp2(sinks[t] - m))

    for u in range(nsub):
        for t in range(2):
            chain(u, t)

    for u in range(nsub):
        for c in range(B_Q_HEADS // 2):
            oa = outs[u, 0][c * tq:(c + 1) * tq]
            ob = outs[u, 1][c * tq:(c + 1) * tq]
            oc = jnp.where(lo, oa, ob) if c < half else jnp.where(lo, ob, oa)
            o_ref[0, u * tq:(u + 1) * tq, c * LANES:(c + 1) * LANES] = oc.astype(BF16)


def _attn_b(q, k, v, bias, sinks, layer, *, tq, w, back, i_off, nsub):
    b, sq, _ = q.shape
    sk = k.shape[1]
    rows = nsub * tq
    assert sq % rows == 0 and (sq // tq - 1 + i_off) * tq - back + w <= sk
    body = functools.partial(_attn_b_body, layer=layer, tq=tq, w=w, back=back, i_off=i_off,
                             nsub=nsub)
    return pl.pallas_call(
        body,
        grid=(b, sq // rows),
        in_specs=[pl.BlockSpec(memory_space=pltpu.SMEM),
                  pl.BlockSpec((1, rows, B_Q), lambda bi, i: (bi, i, 0)),
                  pl.BlockSpec((1, sk, 2 * B_KV), lambda bi, i: (bi, 0, 0)),
                  pl.BlockSpec((1, sk, 2 * B_KV), lambda bi, i: (bi, 0, 0)),
                  pl.BlockSpec((2, 2, B_Q_HEADS // 2, tq, w), lambda bi, i: (0, 0, 0, 0, 0))],
        out_specs=pl.BlockSpec((1, rows, B_Q), lambda bi, i: (bi, i, 0)),
        out_shape=jax.ShapeDtypeStruct((b, sq, B_Q), BF16),
        compiler_params=_params(2),
        name="attn_b",
    )(sinks, q, k, v, bias)


def _post_body(x_ref, o_ref, wo_ref, g_ref, wu_ref, wd_ref, gf_ref, y_ref, *refs, final, cast_weights):
    hn_ref = refs[-1]
    f = pl.program_id(1)
    if cast_weights:
        wob_ref, wub_ref, wdb_ref = refs[:3]

    @pl.when(f == 0)
    def _():
        if cast_weights:
            wob_ref[...] = wo_ref[...].astype(BF16)
        wo = (wob_ref if cast_weights else wo_ref)[...]
        x1 = x_ref[...] + jnp.dot(o_ref[...], wo, preferred_element_type=F32)
        y_ref[...] = x1
        hn_ref[...] = _rms(x1, g_ref[...]).astype(BF16)

    if cast_weights:
        wub_ref[...] = wu_ref[...].astype(BF16)
        wdb_ref[...] = wd_ref[...].astype(BF16)
        wu_ref, wd_ref = wub_ref, wdb_ref
    u = jnp.dot(hn_ref[...], wu_ref[...], preferred_element_type=F32)
    a = jnp.square(jnp.maximum(u, 0.0)).astype(BF16)
    y_ref[...] += jnp.dot(a, wd_ref[...], preferred_element_type=F32)

    if final:
        @pl.when(f == pl.num_programs(1) - 1)
        def _():
            y_ref[...] = _rms(y_ref[...], gf_ref[...])


def _post(x, o, wo, wo_layer, g, g_layer, wu, wd, w_layer, gf, *, tm, tf, final, cast_weights):
    m = x.shape[0]
    assert not cast_weights or m == tm
    row = lambda i, f: (i, 0)
    full = lambda i, f: (0, 0)
    out_specs = [pl.BlockSpec((tm, D_MODEL), row)]
    out_shape = [jax.ShapeDtypeStruct((m, D_MODEL), F32)]
    if cast_weights:
        out_specs += [pl.BlockSpec((None, D_MODEL, D_MODEL), lambda i, f: (0, 0, 0)),
                      pl.BlockSpec((None, D_MODEL, tf), lambda i, f: (0, 0, f)),
                      pl.BlockSpec((None, tf, D_MODEL), lambda i, f: (0, f, 0))]
        out_shape += [jax.ShapeDtypeStruct((1, D_MODEL, D_MODEL), BF16),
                      jax.ShapeDtypeStruct((1, D_MODEL, D_FF), BF16),
                      jax.ShapeDtypeStruct((1, D_FF, D_MODEL), BF16)]
    outs = pl.pallas_call(
        functools.partial(_post_body, final=final, cast_weights=cast_weights),
        grid=(m // tm, D_FF // tf),
        in_specs=[pl.BlockSpec((tm, D_MODEL), row),
                  pl.BlockSpec((tm, D_MODEL), row),
                  pl.BlockSpec((None, D_MODEL, D_MODEL), lambda i, f: (wo_layer, 0, 0)),
                  pl.BlockSpec((None, 1, D_MODEL), lambda i, f: (g_layer, 0, 0)),
                  pl.BlockSpec((None, D_MODEL, tf), lambda i, f: (w_layer, 0, f)),
                  pl.BlockSpec((None, tf, D_MODEL), lambda i, f: (w_layer, f, 0)),
                  pl.BlockSpec((1, D_MODEL), full)],
        out_specs=out_specs,
        out_shape=out_shape,
        scratch_shapes=[pltpu.VMEM((tm, D_MODEL), BF16)],
        compiler_params=_params(2),
        name="post",
    )(x, o, wo, g, wu, wd, gf)
    return outs if cast_weights else outs[0]


def _post_resident_body(x_ref, o_ref, wo_ref, g_ref, wu_ref, wd_ref, gf_ref, y_ref, hn_ref,
                        *, tf, final):
    x1 = x_ref[...] + jnp.dot(o_ref[...], wo_ref[...], preferred_element_type=F32)
    y_ref[...] = x1
    hn_ref[...] = _rms(x1, g_ref[...]).astype(BF16)
    mlp = None
    for f in range(D_FF // tf):
        u = jnp.dot(hn_ref[...], wu_ref[:, f * tf:(f + 1) * tf], preferred_element_type=F32)
        a = jnp.square(jnp.maximum(u, 0.0)).astype(BF16)
        part = jnp.dot(a, wd_ref[f * tf:(f + 1) * tf, :], preferred_element_type=F32)
        mlp = part if mlp is None else mlp + part
    y = y_ref[...] + mlp
    y_ref[...] = _rms(y, gf_ref[...]) if final else y


def _post_resident(x, o, wo, g, g_layer, wu, wd, gf, *, tm, tf, final):
    m = x.shape[0]
    row = lambda i: (i, 0)
    once = dict(pipeline_mode=pl.Buffered(1))
    return pl.pallas_call(
        functools.partial(_post_resident_body, tf=tf, final=final),
        grid=(m // tm,),
        in_specs=[pl.BlockSpec((tm, D_MODEL), row),
                  pl.BlockSpec((tm, D_MODEL), row),
                  pl.BlockSpec((None, D_MODEL, D_MODEL), lambda i: (0, 0, 0), **once),
                  pl.BlockSpec((None, 1, D_MODEL), lambda i: (g_layer, 0, 0)),
                  pl.BlockSpec((None, D_MODEL, D_FF), lambda i: (0, 0, 0), **once),
                  pl.BlockSpec((None, D_FF, D_MODEL), lambda i: (0, 0, 0), **once),
                  pl.BlockSpec((1, D_MODEL), lambda i: (0, 0))],
        out_specs=pl.BlockSpec((tm, D_MODEL), row),
        out_shape=jax.ShapeDtypeStruct((m, D_MODEL), F32),
        scratch_shapes=[pltpu.VMEM((tm, D_MODEL), BF16)],
        compiler_params=_params(1),
        name="post_resident",
    )(x, o, wo, g, wu, wd, gf)


_B_PROMPT = dict(tq=128, w=256, back=128)
_B_SAMPLE = dict(tq=64, w=256, back=128)
B_TILES_PER_STEP = 8
_A_SAMPLE_BACK = 128


def _a_prompt_bias(table):
    _check_far_bucket(A_TILE + 1)
    near = _near_bias(table, A_TILE, 2 * A_TILE, A_TILE, None)
    return jnp.stack([_far_bias(table, A_TILE, A_TILE), near[:, :, :A_TILE], near[:, :, A_TILE:]])


def _a_sample_bias(table, past, t):
    assert t <= LANES
    _check_far_bucket(_A_SAMPLE_BACK + 1)
    far = _far_bias(table, t, past - _A_SAMPLE_BACK)
    near = _near_bias(table, t, _A_SAMPLE_BACK + LANES, _A_SAMPLE_BACK, None)
    return jnp.concatenate([far, near], axis=2)


def _b_bias(table, tq, w, back):
    def one(bk):
        t = _near_bias(table, tq, w, bk, WINDOW // CHUNK)
        return jnp.stack([jnp.stack([t[h] for h in order]) for order in (_B_ORDER_A, _B_ORDER_B)])
    return jnp.stack([one(0), one(back)])


def _pad_keys(t, total):
    return jnp.pad(t, ((0, 0), (0, total - t.shape[1]), (0, 0)))


def _trunk(x, caches, wts, table, bf16_weights, *, tm, tm_post, tf):
    b, s, _ = x.shape
    sample = caches is not None
    n_a = wts["a_w_qkv"].shape[0]
    xm = x.reshape(b * s, D_MODEL)
    b_cfg = _B_SAMPLE if sample else _B_PROMPT
    b_bias = _b_bias(table, **b_cfg)
    if sample:
        a_bias = _a_sample_bias(table, caches["a_kt"].shape[3], s)
    else:
        a_bias = _a_prompt_bias(table)
    ak, av, bk, bv = [], [], [], []
    kt_buf = v_buf = None
    for i in range(DEPTH):
        j = i // 2
        g_mix = wts["norm_mix_g"][i][None, :]
        if sample:
            bf16_weights.append({})
        cast = bf16_weights[i]
        if i % 2 == 0:
            lam_init = _lambda_init(i)
            if sample:
                q, k, v, kb, vb, cast["qkv"], cast["kt"] = _proj_a_sample(xm, g_mix, wts["a_w_qkv"], j)
                ak.append(k.reshape(b, s, A_HEADS, 2, A_DK))
                av.append(v.reshape(b, s, A_HEADS, A_DV))
                o = _attn_a_sample(q.reshape(b, s, A_QK), caches["a_kt"], caches["a_v"],
                                   kb.reshape(b, s, A_QK), vb.reshape(b, s, A_V), a_bias,
                                   wts["a_lambda"], wts["a_subln_g"], j, lam_init=lam_init)
            else:
                q, kt_buf, v_buf, vb = _proj_a_prompt(
                    xm, g_mix, cast["qkv"], cast["kt"], 0, j, kt_buf, v_buf,
                    b=b, s=s, tm=tm, n_a=n_a)
                o = _attn_a_prompt(q.reshape(b, s, A_QK), kt_buf, vb.reshape(b, s, A_V), a_bias,
                                   wts["a_lambda"], wts["a_subln_g"], j, lam_init=lam_init)
            wo, wo_layer = wts["a_w_o"], j
        else:
            if sample:
                q, k, v, kb, vb, cast["qkv"] = _proj_b(xm, g_mix, wts["b_w_qkv"], j, tm,
                                                       cast_weights=True)
            else:
                q, k, v, kb, vb = _proj_b(xm, g_mix, cast["qkv"], 0, tm, seq=s, tail=WINDOW)
            k = k.reshape(b, -1, B_KV_HEADS, B_HD)
            v = v.reshape(b, -1, B_KV_HEADS, B_HD)
            kb = kb.reshape(b, s, 2 * B_KV)
            vb = vb.reshape(b, s, 2 * B_KV)
            i_off = 0
            if sample:
                ck, cv, ckb, cvb = caches["b"][j]
                bk.append(jnp.concatenate([ck, k], axis=1)[:, s:])
                bv.append(jnp.concatenate([cv, v], axis=1)[:, s:])
                i_off = ckb.shape[1] // b_cfg["tq"]
                total = ckb.shape[1] - b_cfg["back"] + b_cfg["w"]
                kb = _pad_keys(jnp.concatenate([ckb, kb], axis=1), total)
                vb = _pad_keys(jnp.concatenate([cvb, vb], axis=1), total)
            else:
                bk.append(k)
                bv.append(v)
            o = _attn_b(q.reshape(b, s, B_Q), kb, vb, b_bias, wts["b_sinks"], j, i_off=i_off,
                        nsub=1 if sample else B_TILES_PER_STEP, **b_cfg)
            wo, wo_layer = wts["b_w_o"], j
        gf = wts["final_norm_g"][None, :]
        tiling = dict(tm=tm_post, tf=tf, final=(i == DEPTH - 1))
        if sample:
            xm, cast["wo"], cast["wu"], cast["wd"] = _post(
                xm, o.reshape(b * s, D_MODEL), wo, wo_layer, wts["norm_mlp_g"], i,
                wts["mlp_w_up"], wts["mlp_w_down"], i, gf, cast_weights=True, **tiling)
        else:
            xm = _post_resident(xm, o.reshape(b * s, D_MODEL), cast["wo"], wts["norm_mlp_g"], i,
                                cast["wu"], cast["wd"], gf, **tiling)
    if sample:
        a_k, a_v = jnp.stack(ak), jnp.stack(av)
    else:
        a_k = jnp.transpose(kt_buf.reshape(n_a, b, A_HEADS, 2, A_DK, s), (0, 1, 5, 2, 3, 4))
        a_v = v_buf.reshape(n_a, b, s, A_HEADS, A_DV)
    return xm.reshape(b, s, D_MODEL), a_k, a_v, jnp.stack(bk), jnp.stack(bv)


def _swap_halves(t):
    flat = t.reshape(t.shape[:-2] + (B_KV,))
    swapped = t[..., ::-1, :].reshape(t.shape[:-2] + (B_KV,))
    return jnp.concatenate([flat, swapped], axis=-1).astype(BF16)


def kernel(x_prompt, x_sample, cache_a_k, cache_a_v, cache_b_k, cache_b_v, rel_table,
           norm_mix_g, norm_mlp_g, final_norm_g, a_w_qkv, a_lambda, a_subln_g, a_w_o,
           b_w_qkv, b_sinks, b_w_o, mlp_w_up, mlp_w_down):
    wts = dict(norm_mix_g=norm_mix_g, norm_mlp_g=norm_mlp_g[:, None, :], final_norm_g=final_norm_g,
               a_w_qkv=a_w_qkv, a_lambda=a_lambda, a_subln_g=a_subln_g[:, None, :],
               a_w_o=a_w_o, b_w_qkv=b_w_qkv, b_sinks=b_sinks,
               b_w_o=b_w_o, mlp_w_up=mlp_w_up, mlp_w_down=mlp_w_down)
    n_a, db, past = cache_a_k.shape[:3]
    caches = dict(
        a_kt=jnp.transpose(cache_a_k, (0, 1, 3, 4, 5, 2)).reshape(n_a, db, A_QK, past),
        a_v=cache_a_v.reshape(n_a, db, past * A_HEADS, A_DV),
        b=[(cache_b_k[j], cache_b_v[j], _swap_halves(cache_b_k[j]), _swap_halves(cache_b_v[j]))
           for j in range(cache_b_k.shape[0])])
    bf16_weights = []
    ys, aks, avs, bks, bvs = _trunk(x_sample, caches, wts, rel_table, bf16_weights,
                                    tm=512, tm_post=512, tf=1024)
    yp, akp, avp, bkp, bvp = _trunk(x_prompt, None, wts, rel_table, bf16_weights,
                                    tm=512, tm_post=1024, tf=1024)
    return (yp, ys, akp, avp, bkp, bvp, aks, avs, bks, bvs)
```

```python
import functools
import math

import jax
import jax.numpy as jnp
from jax import lax
from jax.experimental import pallas as pl
from jax.experimental.pallas import tpu as pltpu

BF16 = jnp.bfloat16
F32 = jnp.float32

D_MODEL = 1024
DEPTH = 4
CHUNK = 64
A_HEADS = 8
A_DK = 64
A_DV = 128
A_QK = A_HEADS * 2 * A_DK
A_V = A_HEADS * A_DV
B_Q_HEADS = 16
B_KV_HEADS = 2
B_HD = 64
B_Q = B_Q_HEADS * B_HD
B_KV = B_KV_HEADS * B_HD
WINDOW = 128
NUM_BUCKETS = 32
MAX_DIST = 128
N_MAPS = 16
D_FF = 4 * D_MODEL
EPS = 1e-6
NEG = -1e30

LANES = 128
HALF = LANES // 2
VMEM_LIMIT = 56 * 1024 * 1024

NT_DIMS = (((1,), (1,)), ((), ()))

LOG2E = math.log2(math.e)


def _lambda_init(layer):
    return 0.8 - 0.6 * math.exp(-0.3 * layer)


def _rms(x, g):
    return x * lax.rsqrt(jnp.mean(x * x, axis=-1, keepdims=True) + EPS) * g


def _params(n_axes):
    return pltpu.CompilerParams(
        dimension_semantics=("arbitrary",) * n_axes, vmem_limit_bytes=VMEM_LIMIT)


def _lane_fold_max(s):
    mx = s[:, 0:LANES]
    for gi in range(1, s.shape[1] // LANES):
        mx = jnp.maximum(mx, s[:, gi * LANES:(gi + 1) * LANES])
    return mx


def _row_max(s):
    return jnp.broadcast_to(jnp.max(_lane_fold_max(s), axis=-1, keepdims=True), (s.shape[0], LANES))


def _exp_rows(s, m):
    return jnp.concatenate([jnp.exp2(s[:, gi * LANES:(gi + 1) * LANES] - m)
                            for gi in range(s.shape[1] // LANES)], axis=1).astype(BF16)


def _with_ones(v):
    return jnp.concatenate([v, jnp.ones(v.shape, v.dtype)], axis=1)


def _proj_a_prompt_body(x_ref, g_ref, w_ref, wkt_ref, *refs, first):
    q_ref, kt_ref, v_ref, vb_ref = refs[-4:]
    h = _rms(x_ref[...], g_ref[...]).astype(BF16)
    tm = h.shape[0]
    if first:
        for slot in range(1, kt_ref.shape[0]):
            kt_ref[slot] = jnp.zeros(kt_ref.shape[1:], F32)
            v_ref[slot] = jnp.zeros(v_ref.shape[1:], F32)
        kt_ref, v_ref = kt_ref.at[0], v_ref.at[0]
    v = jnp.dot(h, w_ref[:, 2 * A_QK:], preferred_element_type=F32)
    vb_ref[...] = v.astype(BF16)
    for hh in range(A_HEADS):
        v_ref[pl.ds(hh, tm, stride=A_HEADS), :] = v[:, hh * A_DV:(hh + 1) * A_DV]
    kt_ref[...] = lax.dot_general(wkt_ref[...], h, NT_DIMS, preferred_element_type=F32)
    q = jnp.dot(h, w_ref[:, 0:A_QK], preferred_element_type=F32)
    q_ref[...] = (q * (A_DK ** -0.5 * LOG2E)).astype(BF16)


def _proj_a_prompt(x, g, w, wkt, w_layer, layer, kt_prev, v_prev, *, b, s, tm, n_a):
    m = b * s
    per_b = s // tm
    row = lambda i: (i, 0)
    full = lambda i: (0, 0)
    stacked = lambda i: (w_layer, 0, 0)
    first = kt_prev is None
    assert first == (layer == 0)
    in_specs = [pl.BlockSpec((tm, D_MODEL), row),
                pl.BlockSpec((1, D_MODEL), full),
                pl.BlockSpec((None, D_MODEL, 2 * A_QK + A_V), stacked),
                pl.BlockSpec((None, A_QK, D_MODEL), stacked)]
    operands = [x, g, w, wkt]
    if first:
        kt_spec = pl.BlockSpec((n_a, None, A_QK, tm), lambda i: (0, i // per_b, 0, i % per_b))
        v_spec = pl.BlockSpec((n_a, tm * A_HEADS, A_DV), lambda i: (0, i, 0))
        aliases = {}
    else:
        kt_spec = pl.BlockSpec((None, None, A_QK, tm), lambda i: (layer, i // per_b, 0, i % per_b))
        v_spec = pl.BlockSpec((None, tm * A_HEADS, A_DV), lambda i: (layer, i, 0))
        in_specs += [pl.BlockSpec(memory_space=pl.ANY)] * 2
        operands += [kt_prev, v_prev]
        aliases = {4: 1, 5: 2}
    return pl.pallas_call(
        functools.partial(_proj_a_prompt_body, first=first),
        grid=(m // tm,),
        in_specs=in_specs,
        out_specs=[pl.BlockSpec((tm, A_QK), row),
                   kt_spec,
                   v_spec,
                   pl.BlockSpec((tm, A_V), row)],
        out_shape=[jax.ShapeDtypeStruct((m, A_QK), BF16),
                   jax.ShapeDtypeStruct((n_a, b, A_QK, s), F32),
                   jax.ShapeDtypeStruct((n_a, m * A_HEADS, A_DV), F32),
                   jax.ShapeDtypeStruct((m, A_V), BF16)],
        input_output_aliases=aliases,
        compiler_params=_params(1),
        name="proj_a_prompt",
    )(*operands)


def _proj_a_sample_body(x_ref, g_ref, w_ref, q_ref, k_ref, v_ref, kb_ref, vb_ref, wb_ref, wkt_ref):
    wb_ref[...] = w_ref[...].astype(BF16)
    wkt_ref[...] = w_ref[:, A_QK:2 * A_QK].T.astype(BF16)
    h = _rms(x_ref[...], g_ref[...]).astype(BF16)
    q = jnp.dot(h, wb_ref[:, 0:A_QK], preferred_element_type=F32)
    q_ref[...] = (q * (A_DK ** -0.5 * LOG2E)).astype(BF16)
    k = jnp.dot(h, wb_ref[:, A_QK:2 * A_QK], preferred_element_type=F32)
    k_ref[...] = k
    kb_ref[...] = k.astype(BF16)
    v = jnp.dot(h, wb_ref[:, 2 * A_QK:], preferred_element_type=F32)
    v_ref[...] = v
    vb_ref[...] = v.astype(BF16)


def _proj_a_sample(x, g, w, layer):
    m = x.shape[0]
    n_out = 2 * A_QK + A_V
    full = lambda i: (0, 0)
    full3 = lambda i: (0, 0, 0)
    return pl.pallas_call(
        _proj_a_sample_body,
        grid=(1,),
        in_specs=[pl.BlockSpec((m, D_MODEL), full),
                  pl.BlockSpec((1, D_MODEL), full),
                  pl.BlockSpec((None, D_MODEL, n_out), lambda i: (layer, 0, 0))],
        out_specs=[pl.BlockSpec((m, A_QK), full),
                   pl.BlockSpec((m, A_QK), full),
                   pl.BlockSpec((m, A_V), full),
                   pl.BlockSpec((m, A_QK), full),
                   pl.BlockSpec((m, A_V), full),
                   pl.BlockSpec((None, D_MODEL, n_out), full3),
                   pl.BlockSpec((None, A_QK, D_MODEL), full3)],
        out_shape=[jax.ShapeDtypeStruct((m, A_QK), BF16),
                   jax.ShapeDtypeStruct((m, A_QK), F32),
                   jax.ShapeDtypeStruct((m, A_V), F32),
                   jax.ShapeDtypeStruct((m, A_QK), BF16),
                   jax.ShapeDtypeStruct((m, A_V), BF16),
                   jax.ShapeDtypeStruct((1, D_MODEL, n_out), BF16),
                   jax.ShapeDtypeStruct((1, A_QK, D_MODEL), BF16)],
        compiler_params=_params(1),
        name="proj_a_sample",
    )(x, g, w)


def _proj_b_body(x_ref, g_ref, w_ref, q_ref, k_ref, v_ref, kb_ref, vb_ref, *wb_ref, tail):
    if wb_ref:
        wb_ref[0][...] = w_ref[...].astype(BF16)
        w_ref = wb_ref[0]
    h = _rms(x_ref[...], g_ref[...]).astype(BF16)
    kv = jnp.dot(h, w_ref[:, B_Q:], preferred_element_type=F32)
    k, v = kv[:, 0:B_KV], kv[:, B_KV:]
    kb_ref[...] = jnp.concatenate([k, pltpu.roll(k, HALF, axis=1)], axis=1).astype(BF16)
    vb_ref[...] = jnp.concatenate([v, pltpu.roll(v, HALF, axis=1)], axis=1).astype(BF16)
    keep = k.shape[0] if tail is None else tail
    k_ref[...] = k[k.shape[0] - keep:]
    v_ref[...] = v[v.shape[0] - keep:]
    q = jnp.dot(h, w_ref[:, 0:B_Q], preferred_element_type=F32)
    q_ref[...] = (q * (B_HD ** -0.5 * LOG2E)).astype(BF16)


def _proj_b(x, g, w, layer, tm, *, seq=None, tail=None, cast_weights=False):
    m = x.shape[0]
    assert not cast_weights or m == tm
    n_out = B_Q + 2 * B_KV
    row = lambda i: (i, 0)
    full = lambda i: (0, 0)
    wb_spec = [pl.BlockSpec((None, D_MODEL, n_out), lambda i: (0, 0, 0))] if cast_weights else []
    wb_shape = [jax.ShapeDtypeStruct((1, D_MODEL, n_out), BF16)] if cast_weights else []
    if tail is None:
        kv_spec = pl.BlockSpec((tm, B_KV), row)
        kv_shape = jax.ShapeDtypeStruct((m, B_KV), F32)
    else:
        assert seq % tm == 0 and tail <= tm
        per_seq = seq // tm
        kv_spec = pl.BlockSpec((None, tail, B_KV), lambda i: (i // per_seq, 0, 0))
        kv_shape = jax.ShapeDtypeStruct((m // seq, tail, B_KV), F32)
    return pl.pallas_call(
        functools.partial(_proj_b_body, tail=tail),
        grid=(m // tm,),
        in_specs=[pl.BlockSpec((tm, D_MODEL), row),
                  pl.BlockSpec((1, D_MODEL), full),
                  pl.BlockSpec((None, D_MODEL, B_Q + 2 * B_KV), lambda i: (layer, 0, 0))],
        out_specs=[pl.BlockSpec((tm, B_Q), row),
                   kv_spec,
                   kv_spec,
                   pl.BlockSpec((tm, 2 * B_KV), row),
                   pl.BlockSpec((tm, 2 * B_KV), row)] + wb_spec,
        out_shape=[jax.ShapeDtypeStruct((m, B_Q), BF16),
                   kv_shape,
                   kv_shape,
                   jax.ShapeDtypeStruct((m, 2 * B_KV), BF16),
                   jax.ShapeDtypeStruct((m, 2 * B_KV), BF16)] + wb_shape,
        compiler_params=_params(1),
        name="proj_b",
    )(x, g, w)


def _t5_bucket(rel):
    nb = NUM_BUCKETS // 2
    n = -rel
    ret = jnp.where(n < 0, nb, 0)
    n = jnp.abs(n)
    max_exact = nb // 2
    nf = jnp.maximum(n, 1).astype(jnp.float32)
    large = max_exact + (jnp.log(nf / max_exact) / math.log(MAX_DIST / max_exact)
                         * (nb - max_exact)).astype(jnp.int32)
    large = jnp.minimum(large, nb - 1)
    return ret + jnp.where(n < max_exact, n, large)


def _check_far_bucket(min_dist):
    nb, max_exact = NUM_BUCKETS // 2, NUM_BUCKETS // 4
    ratio = math.log(min_dist / max_exact) / math.log(MAX_DIST / max_exact)
    assert ratio * (nb - max_exact) >= nb - 1 - max_exact + 1e-3, "far keys need one shared bucket"


def _far_bias(table, tq, w):
    far = table[NUM_BUCKETS // 2 - 1].astype(F32) * LOG2E
    return jnp.broadcast_to(far[:, None, None], (N_MAPS, tq, w))


def _near_bias(table, tq, w, back, band):
    r = jnp.arange(tq)[:, None]
    j = jnp.arange(w)[None, :] - back
    onehot = (_t5_bucket(j - r)[None] == jnp.arange(NUM_BUCKETS)[:, None, None]).astype(F32)
    onehot = lax.optimization_barrier(onehot)
    bias = jnp.einsum("bm,brj->mrj", table.astype(F32), onehot, precision=lax.Precision.HIGHEST)
    kc, qc = j // CHUNK, r // CHUNK
    mask = kc <= qc
    if band is not None:
        mask = mask & (kc >= qc - band)
    return jnp.where(mask[None], bias * LOG2E, NEG)


A_TILE = 256
A_ROWS = 64
A_HEADS_PER_STEP = 1


def _diff_lambda(lamp_ref, lam_init):
    lp = lamp_ref[...]
    return (jnp.exp(jnp.sum(lp[0:1] * lp[1:2], axis=-1, keepdims=True))
            - jnp.exp(jnp.sum(lp[2:3] * lp[3:4], axis=-1, keepdims=True)) + lam_init)


def _stack_maps(q):
    lane = lax.broadcasted_iota(jnp.int32, q.shape, 1)
    zero = jnp.zeros_like(q)
    return jnp.concatenate([jnp.where(lane < HALF, q, zero), jnp.where(lane >= HALF, q, zero)], axis=0)


def _diff_epilogue(acc, lam, g, lam_init):
    tq = acc.shape[0] // 2
    o_all = acc[:, :A_DV] / acc[:, A_DV:]
    o = o_all[:tq] - lam * o_all[tq:]
    return (_rms(o, g) * (1.0 - lam_init)).astype(BF16)


def _attn_a_scores(i, q, kt_ref, bias_ref, s_ref, m_ref):
    tl = A_TILE
    qs = _stack_maps(q)
    n_far = max(i - 1, 0)
    for kt in range(i + 1):
        cols = slice(kt * tl, (kt + 1) * tl)
        s = jnp.dot(qs, kt_ref[:, cols], preferred_element_type=F32)
        near = kt >= n_far
        if near:
            s = s + bias_ref[2 - (i - kt)].reshape(2 * tl, tl)
        s_ref[:, cols] = s
        fold = _lane_fold_max(s)
        first = kt == (n_far if near else 0)
        m_ref[int(near)] = fold if first else jnp.maximum(m_ref[int(near)], fold)
        yield
    for r in range(0, 2 * tl, A_ROWS):
        rows = slice(r, r + A_ROWS)
        mx = m_ref[1, rows, :]
        if n_far:
            c = bias_ref[0, r // tl, r % tl:r % tl + A_ROWS, 0:LANES]
            mx = jnp.maximum(mx, m_ref[0, rows, :] + c)
        m = jnp.broadcast_to(jnp.max(mx, axis=-1, keepdims=True), (A_ROWS, LANES))
        m_ref[1, rows, :] = m
        if n_far:
            m_ref[0, rows, :] = m - c


def _attn_a_numerators(i, s_ref, m_ref, v1_ref, out):
    tl = A_TILE
    n_far = max(i - 1, 0)
    acc = None
    for kt in range(i + 1):
        cols = slice(kt * tl, (kt + 1) * tl)
        p = _exp_rows(s_ref[:, cols], m_ref[1 if kt >= n_far else 0])
        part = jnp.dot(p, v1_ref[cols, :], preferred_element_type=F32)
        acc = part if acc is None else acc + part
        yield
    out.append(acc)


def _interleave(*steps):
    steps = list(steps)
    while steps:
        for gen in list(steps):
            if next(gen, StopIteration) is StopIteration:
                steps.remove(gen)


def _attn_a_prompt_body(lamp_ref, q_ref, kt_ref, v_ref, bias_ref, g_ref, o_ref,
                        s_ref, m_ref, v1_ref, ktb_ref, *, nt, nh, lam_init):
    tl = A_TILE
    lam = _diff_lambda(lamp_ref, lam_init)
    ktb_ref[...] = kt_ref[...].astype(BF16)
    for hh in range(nh):
        v1_ref[hh, :, 0:A_DV] = v_ref[0, :, hh * A_DV:(hh + 1) * A_DV]
        v1_ref[hh, :, A_DV:] = jnp.ones((nt * tl, A_DV), BF16)
    work = [(i, hh) for i in reversed(range(nt)) for hh in range(nh)]

    def scores(i, hh):
        lanes = slice(hh * LANES, (hh + 1) * LANES)
        return _attn_a_scores(i, q_ref[0, i * tl:(i + 1) * tl, lanes], ktb_ref.at[lanes, :],
                              bias_ref.at[:, 2 * hh:2 * hh + 2], s_ref.at[hh, i % 2],
                              m_ref.at[hh, i % 2])

    _interleave(scores(*work[0]))
    for n, (i, hh) in enumerate(work):
        out = []
        steps = [_attn_a_numerators(i, s_ref.at[hh, i % 2], m_ref.at[hh, i % 2], v1_ref.at[hh], out)]
        if n + 1 < len(work):
            steps.insert(0, scores(*work[n + 1]))
        _interleave(*steps)
        acc = out[0]
        o_ref[0, i * tl:(i + 1) * tl, hh * LANES:(hh + 1) * LANES] = _diff_epilogue(
            acc, lam, g_ref[...], lam_init)


def _attn_a_prompt(q, kt, v, bias, lamp, g, layer, *, lam_init):
    b, s, _ = q.shape
    nt = s // A_TILE
    nh = A_HEADS_PER_STEP
    assert s % A_TILE == 0 and A_HEADS % nh == 0
    seq = pl.BlockSpec((1, s, nh * LANES), lambda h, bi: (bi, 0, h))
    return pl.pallas_call(
        functools.partial(_attn_a_prompt_body, nt=nt, nh=nh, lam_init=lam_init),
        grid=(A_HEADS // nh, b),
        in_specs=[pl.BlockSpec((None, 4, A_DK), lambda h, bi: (layer, 0, 0)),
                  seq,
                  pl.BlockSpec((None, None, nh * LANES, s), lambda h, bi: (layer, bi, h, 0)),
                  seq,
                  pl.BlockSpec((3, 2 * nh, A_TILE, A_TILE), lambda h, bi: (0, h, 0, 0)),
                  pl.BlockSpec((None, 1, A_DV), lambda h, bi: (layer, 0, 0))],
        out_specs=seq,
        out_shape=jax.ShapeDtypeStruct((b, s, A_V), BF16),
        scratch_shapes=[pltpu.VMEM((nh, 2, 2 * A_TILE, s), F32),
                        pltpu.VMEM((nh, 2, 2, 2 * A_TILE, LANES), F32),
                        pltpu.VMEM((nh, s, 2 * A_DV), BF16),
                        pltpu.VMEM((nh * LANES, s), BF16)],
        compiler_params=_params(2),
        name="attn_a_prompt",
    )(lamp, q, kt, v, bias, g)


def _attn_a_sample_body(lamp_ref, q_ref, ktc_ref, vc_ref, kn_ref, vn_ref, bias_ref, g_ref, o_ref,
                        *, lam_init):
    tq = q_ref.shape[1]
    past = ktc_ref.shape[1]
    lam = _diff_lambda(lamp_ref, lam_init)
    for h in range(A_HEADS):
        cols = slice(h * LANES, (h + 1) * LANES)
        qs = _stack_maps(q_ref[0, :, cols])
        bias = bias_ref[2 * h:2 * h + 2].reshape(2 * tq, past + LANES)
        s_c = jnp.dot(qs, ktc_ref[cols, :].astype(BF16), preferred_element_type=F32)
        s_c = s_c + bias[:, 0:past]
        s_n = lax.dot_general(qs, kn_ref[0, :, cols], NT_DIMS, preferred_element_type=F32)
        s_n = s_n + bias[:, past:past + tq]
        m = jnp.maximum(_row_max(s_c), jnp.broadcast_to(jnp.max(s_n, axis=-1, keepdims=True),
                                                        (2 * tq, LANES)))
        p_c = _exp_rows(s_c, m)
        p_n = jnp.exp2(s_n - m[:, 0:tq]).astype(BF16)
        v_c = vc_ref[pl.ds(h, past, stride=A_HEADS), :].astype(BF16)
        acc = (jnp.dot(p_c, _with_ones(v_c), preferred_element_type=F32)
               + jnp.dot(p_n, _with_ones(vn_ref[0, :, cols]), preferred_element_type=F32))
        o_ref[0, :, cols] = _diff_epilogue(acc, lam, g_ref[...], lam_init)


def _attn_a_sample(q, ktc, vc, kn, vn, bias, lamp, g, layer, *, lam_init):
    b, tq, _ = q.shape
    past = ktc.shape[3]
    new = lambda bi: (bi, 0, 0)
    return pl.pallas_call(
        functools.partial(_attn_a_sample_body, lam_init=lam_init),
        grid=(b,),
        in_specs=[pl.BlockSpec((None, 4, A_DK), lambda bi: (layer, 0, 0)),
                  pl.BlockSpec((1, tq, A_QK), new),
                  pl.BlockSpec((None, None, A_QK, past), lambda bi: (layer, bi, 0, 0)),
                  pl.BlockSpec((None, None, past * A_HEADS, A_DV), lambda bi: (layer, bi, 0, 0)),
                  pl.BlockSpec((1, tq, A_QK), new),
                  pl.BlockSpec((1, tq, A_V), new),
                  pl.BlockSpec((N_MAPS, tq, past + LANES), lambda bi: (0, 0, 0)),
                  pl.BlockSpec((None, 1, A_DV), lambda bi: (layer, 0, 0))],
        out_specs=pl.BlockSpec((1, tq, A_V), new),
        out_shape=jax.ShapeDtypeStruct((b, tq, A_V), BF16),
        compiler_params=_params(1),
        name="attn_a_sample",
    )(lamp, q, ktc, vc, kn, vn, bias, g)


_B_ORDER_A = (0, 2, 4, 6, 9, 11, 13, 15)
_B_ORDER_B = (1, 3, 5, 7, 8, 10, 12, 14)


def _attn_b_body(sink_ref, q_ref, k_ref, v_ref, bias_ref, o_ref, *, layer, tq, w, back, i_off, nsub):
    lane = lax.broadcasted_iota(jnp.int32, (tq, LANES), 1)
    lo = lane < HALF
    half = B_Q_HEADS // 4
    orders = (_B_ORDER_A, _B_ORDER_B)
    sinks = [jnp.concatenate([jnp.full((tq, LANES), sink_ref[layer, h] * LOG2E, F32)
                              for h in orders[t]], axis=0) for t in range(2)]
    outs = {}

    def chain(u, t):
        tile = pl.program_id(1) * nsub + u + i_off
        variant = 1 if (u > 0 or i_off > 0) else jnp.minimum(tile, 1)
        near_start = pl.multiple_of(jnp.maximum(tile * tq - back, 0), CHUNK)
        group = slice(t * LANES, (t + 1) * LANES)
        pieces = []
        for c in range(B_Q_HEADS // 2):
            qc = q_ref[0, u * tq:(u + 1) * tq, c * LANES:(c + 1) * LANES]
            keep_lo = (c < half) == (t == 0)
            pieces.append(jnp.where(lo == keep_lo, qc, jnp.zeros_like(qc)))
        qs = jnp.concatenate(pieces, axis=0)
        s = lax.dot_general(qs, k_ref[0, pl.ds(near_start, w), group], NT_DIMS,
                            preferred_element_type=F32)
        s = s + bias_ref[variant, t].reshape((B_Q_HEADS // 2) * tq, w)
        m = jnp.maximum(_row_max(s), sinks[t])
        p = _exp_rows(s, m)
        acc = jnp.dot(p, _with_ones(v_ref[0, pl.ds(near_start, w), group]),
                      preferred_element_type=F32)
        outs[u, t] = acc[:, :LANES] / (acc[:, LANES:] + jnp.exp2(sinks[t] - m))

    for u in range(nsub):
        for t in range(2):
            chain(u, t)

    for u in range(nsub):
        for c in range(B_Q_HEADS // 2):
            oa = outs[u, 0][c * tq:(c + 1) * tq]
            ob = outs[u, 1][c * tq:(c + 1) * tq]
            oc = jnp.where(lo, oa, ob) if c < half else jnp.where(lo, ob, oa)
            o_ref[0, u * tq:(u + 1) * tq, c * LANES:(c + 1) * LANES] = oc.astype(BF16)


def _attn_b(q, k, v, bias, sinks, layer, *, tq, w, back, i_off, nsub):
    b, sq, _ = q.shape
    sk = k.shape[1]
    rows = nsub * tq
    assert sq % rows == 0 and (sq // tq - 1 + i_off) * tq - back + w <= sk
    body = functools.partial(_attn_b_body, layer=layer, tq=tq, w=w, back=back, i_off=i_off,
                             nsub=nsub)
    return pl.pallas_call(
        body,
        grid=(b, sq // rows),
        in_specs=[pl.BlockSpec(memory_space=pltpu.SMEM),
                  pl.BlockSpec((1, rows, B_Q), lambda bi, i: (bi, i, 0)),
                  pl.BlockSpec((1, sk, 2 * B_KV), lambda bi, i: (bi, 0, 0)),
                  pl.BlockSpec((1, sk, 2 * B_KV), lambda bi, i: (bi, 0, 0)),
                  pl.BlockSpec((2, 2, B_Q_HEADS // 2, tq, w), lambda bi, i: (0, 0, 0, 0, 0))],
        out_specs=pl.BlockSpec((1, rows, B_Q), lambda bi, i: (bi, i, 0)),
        out_shape=jax.ShapeDtypeStruct((b, sq, B_Q), BF16),
        compiler_params=_params(2),
        name="attn_b",
    )(sinks, q, k, v, bias)


def _post_body(x_ref, o_ref, wo_ref, g_ref, wu_ref, wd_ref, gf_ref, y_ref, *refs, final, cast_weights):
    hn_ref = refs[-1]
    f = pl.program_id(1)
    if cast_weights:
        wob_ref, wub_ref, wdb_ref = refs[:3]

    @pl.when(f == 0)
    def _():
        if cast_weights:
            wob_ref[...] = wo_ref[...].astype(BF16)
        wo = (wob_ref if cast_weights else wo_ref)[...]
        x1 = x_ref[...] + jnp.dot(o_ref[...], wo, preferred_element_type=F32)
        y_ref[...] = x1
        hn_ref[...] = _rms(x1, g_ref[...]).astype(BF16)

    if cast_weights:
        wub_ref[...] = wu_ref[...].astype(BF16)
        wdb_ref[...] = wd_ref[...].astype(BF16)
        wu_ref, wd_ref = wub_ref, wdb_ref
    u = jnp.dot(hn_ref[...], wu_ref[...], preferred_element_type=F32)
    a = jnp.square(jnp.maximum(u, 0.0)).astype(BF16)
    y_ref[...] += jnp.dot(a, wd_ref[...], preferred_element_type=F32)

    if final:
        @pl.when(f == pl.num_programs(1) - 1)
        def _():
            y_ref[...] = _rms(y_ref[...], gf_ref[...])


def _post(x, o, wo, wo_layer, g, g_layer, wu, wd, w_layer, gf, *, tm, tf, final, cast_weights):
    m = x.shape[0]
    assert not cast_weights or m == tm
    row = lambda i, f: (i, 0)
    full = lambda i, f: (0, 0)
    out_specs = [pl.BlockSpec((tm, D_MODEL), row)]
    out_shape = [jax.ShapeDtypeStruct((m, D_MODEL), F32)]
    if cast_weights:
        out_specs += [pl.BlockSpec((None, D_MODEL, D_MODEL), lambda i, f: (0, 0, 0)),
                      pl.BlockSpec((None, D_MODEL, tf), lambda i, f: (0, 0, f)),
                      pl.BlockSpec((None, tf, D_MODEL), lambda i, f: (0, f, 0))]
        out_shape += [jax.ShapeDtypeStruct((1, D_MODEL, D_MODEL), BF16),
                      jax.ShapeDtypeStruct((1, D_MODEL, D_FF), BF16),
                      jax.ShapeDtypeStruct((1, D_FF, D_MODEL), BF16)]
    outs = pl.pallas_call(
        functools.partial(_post_body, final=final, cast_weights=cast_weights),
        grid=(m // tm, D_FF // tf),
        in_specs=[pl.BlockSpec((tm, D_MODEL), row),
                  pl.BlockSpec((tm, D_MODEL), row),
                  pl.BlockSpec((None, D_MODEL, D_MODEL), lambda i, f: (wo_layer, 0, 0)),
                  pl.BlockSpec((None, 1, D_MODEL), lambda i, f: (g_layer, 0, 0)),
                  pl.BlockSpec((None, D_MODEL, tf), lambda i, f: (w_layer, 0, f)),
                  pl.BlockSpec((None, tf, D_MODEL), lambda i, f: (w_layer, f, 0)),
                  pl.BlockSpec((1, D_MODEL), full)],
        out_specs=out_specs,
        out_shape=out_shape,
        scratch_shapes=[pltpu.VMEM((tm, D_MODEL), BF16)],
        compiler_params=_params(2),
        name="post",
    )(x, o, wo, g, wu, wd, gf)
    return outs if cast_weights else outs[0]


def _post_resident_body(x_ref, o_ref, wo_ref, g_ref, wu_ref, wd_ref, gf_ref, y_ref, hn_ref,
                        *, tf, final):
    x1 = x_ref[...] + jnp.dot(o_ref[...], wo_ref[...], preferred_element_type=F32)
    y_ref[...] = x1
    hn_ref[...] = _rms(x1, g_ref[...]).astype(BF16)
    mlp = None
    for f in range(D_FF // tf):
        u = jnp.dot(hn_ref[...], wu_ref[:, f * tf:(f + 1) * tf], preferred_element_type=F32)
        a = jnp.square(jnp.maximum(u, 0.0)).astype(BF16)
        part = jnp.dot(a, wd_ref[f * tf:(f + 1) * tf, :], preferred_element_type=F32)
        mlp = part if mlp is None else mlp + part
    y = y_ref[...] + mlp
    y_ref[...] = _rms(y, gf_ref[...]) if final else y


def _post_resident(x, o, wo, g, g_layer, wu, wd, gf, *, tm, tf, final):
    m = x.shape[0]
    row = lambda i: (i, 0)
    once = dict(pipeline_mode=pl.Buffered(1))
    return pl.pallas_call(
        functools.partial(_post_resident_body, tf=tf, final=final),
        grid=(m // tm,),
        in_specs=[pl.BlockSpec((tm, D_MODEL), row),
                  pl.BlockSpec((tm, D_MODEL), row),
                  pl.BlockSpec((None, D_MODEL, D_MODEL), lambda i: (0, 0, 0), **once),
                  pl.BlockSpec((None, 1, D_MODEL), lambda i: (g_layer, 0, 0)),
                  pl.BlockSpec((None, D_MODEL, D_FF), lambda i: (0, 0, 0), **once),
                  pl.BlockSpec((None, D_FF, D_MODEL), lambda i: (0, 0, 0), **once),
                  pl.BlockSpec((1, D_MODEL), lambda i: (0, 0))],
        out_specs=pl.BlockSpec((tm, D_MODEL), row),
        out_shape=jax.ShapeDtypeStruct((m, D_MODEL), F32),
        scratch_shapes=[pltpu.VMEM((tm, D_MODEL), BF16)],
        compiler_params=_params(1),
        name="post_resident",
    )(x, o, wo, g, wu, wd, gf)


_B_PROMPT = dict(tq=128, w=256, back=128)
_B_SAMPLE = dict(tq=64, w=256, back=128)
B_TILES_PER_STEP = 8
_A_SAMPLE_BACK = 128


def _a_prompt_bias(table):
    _check_far_bucket(A_TILE + 1)
    near = _near_bias(table, A_TILE, 2 * A_TILE, A_TILE, None)
    return jnp.stack([_far_bias(table, A_TILE, A_TILE), near[:, :, :A_TILE], near[:, :, A_TILE:]])


def _a_sample_bias(table, past, t):
    assert t <= LANES
    _check_far_bucket(_A_SAMPLE_BACK + 1)
    far = _far_bias(table, t, past - _A_SAMPLE_BACK)
    near = _near_bias(table, t, _A_SAMPLE_BACK + LANES, _A_SAMPLE_BACK, None)
    return jnp.concatenate([far, near], axis=2)


def _b_bias(table, tq, w, back):
    def one(bk):
        t = _near_bias(table, tq, w, bk, WINDOW // CHUNK)
        return jnp.stack([jnp.stack([t[h] for h in order]) for order in (_B_ORDER_A, _B_ORDER_B)])
    return jnp.stack([one(0), one(back)])


def _pad_keys(t, total):
    return jnp.pad(t, ((0, 0), (0, total - t.shape[1]), (0, 0)))


def _trunk(x, caches, wts, table, bf16_weights, *, tm, tm_post, tf):
    b, s, _ = x.shape
    sample = caches is not None
    n_a = wts["a_w_qkv"].shape[0]
    xm = x.reshape(b * s, D_MODEL)
    b_cfg = _B_SAMPLE if sample else _B_PROMPT
    b_bias = _b_bias(table, **b_cfg)
    if sample:
        a_bias = _a_sample_bias(table, caches["a_kt"].shape[3], s)
    else:
        a_bias = _a_prompt_bias(table)
    ak, av, bk, bv = [], [], [], []
    kt_buf = v_buf = None
    for i in range(DEPTH):
        j = i // 2
        g_mix = wts["norm_mix_g"][i][None, :]
        if sample:
            bf16_weights.append({})
        cast = bf16_weights[i]
        if i % 2 == 0:
            lam_init = _lambda_init(i)
            if sample:
                q, k, v, kb, vb, cast["qkv"], cast["kt"] = _proj_a_sample(xm, g_mix, wts["a_w_qkv"], j)
                ak.append(k.reshape(b, s, A_HEADS, 2, A_DK))
                av.append(v.reshape(b, s, A_HEADS, A_DV))
                o = _attn_a_sample(q.reshape(b, s, A_QK), caches["a_kt"], caches["a_v"],
                                   kb.reshape(b, s, A_QK), vb.reshape(b, s, A_V), a_bias,
                                   wts["a_lambda"], wts["a_subln_g"], j, lam_init=lam_init)
            else:
                q, kt_buf, v_buf, vb = _proj_a_prompt(
                    xm, g_mix, cast["qkv"], cast["kt"], 0, j, kt_buf, v_buf,
                    b=b, s=s, tm=tm if kt_buf is None else tm_post, n_a=n_a)
                o = _attn_a_prompt(q.reshape(b, s, A_QK), kt_buf, vb.reshape(b, s, A_V), a_bias,
                                   wts["a_lambda"], wts["a_subln_g"], j, lam_init=lam_init)
            wo, wo_layer = wts["a_w_o"], j
        else:
            if sample:
                q, k, v, kb, vb, cast["qkv"] = _proj_b(xm, g_mix, wts["b_w_qkv"], j, tm,
                                                       cast_weights=True)
            else:
                q, k, v, kb, vb = _proj_b(xm, g_mix, cast["qkv"], 0, tm_post, seq=s, tail=WINDOW)
            k = k.reshape(b, -1, B_KV_HEADS, B_HD)
            v = v.reshape(b, -1, B_KV_HEADS, B_HD)
            kb = kb.reshape(b, s, 2 * B_KV)
            vb = vb.reshape(b, s, 2 * B_KV)
            i_off = 0
            if sample:
                ck, cv, ckb, cvb = caches["b"][j]
                bk.append(jnp.concatenate([ck, k], axis=1)[:, s:])
                bv.append(jnp.concatenate([cv, v], axis=1)[:, s:])
                i_off = ckb.shape[1] // b_cfg["tq"]
                total = ckb.shape[1] - b_cfg["back"] + b_cfg["w"]
                kb = _pad_keys(jnp.concatenate([ckb, kb], axis=1), total)
                vb = _pad_keys(jnp.concatenate([cvb, vb], axis=1), total)
            else:
                bk.append(k)
                bv.append(v)
            o = _attn_b(q.reshape(b, s, B_Q), kb, vb, b_bias, wts["b_sinks"], j, i_off=i_off,
                        nsub=1 if sample else B_TILES_PER_STEP, **b_cfg)
            wo, wo_layer = wts["b_w_o"], j
        gf = wts["final_norm_g"][None, :]
        tiling = dict(tm=tm_post, tf=tf, final=(i == DEPTH - 1))
        if sample:
            xm, cast["wo"], cast["wu"], cast["wd"] = _post(
                xm, o.reshape(b * s, D_MODEL), wo, wo_layer, wts["norm_mlp_g"], i,
                wts["mlp_w_up"], wts["mlp_w_down"], i, gf, cast_weights=True, **tiling)
        else:
            xm = _post_resident(xm, o.reshape(b * s, D_MODEL), cast["wo"], wts["norm_mlp_g"], i,
                                cast["wu"], cast["wd"], gf, **tiling)
    if sample:
        a_k, a_v = jnp.stack(ak), jnp.stack(av)
    else:
        a_k = jnp.transpose(kt_buf.reshape(n_a, b, A_HEADS, 2, A_DK, s), (0, 1, 5, 2, 3, 4))
        a_v = v_buf.reshape(n_a, b, s, A_HEADS, A_DV)
    return xm.reshape(b, s, D_MODEL), a_k, a_v, jnp.stack(bk), jnp.stack(bv)


def _swap_halves(t):
    flat = t.reshape(t.shape[:-2] + (B_KV,))
    swapped = t[..., ::-1, :].reshape(t.shape[:-2] + (B_KV,))
    return jnp.concatenate([flat, swapped], axis=-1).astype(BF16)


def kernel(x_prompt, x_sample, cache_a_k, cache_a_v, cache_b_k, cache_b_v, rel_table,
           norm_mix_g, norm_mlp_g, final_norm_g, a_w_qkv, a_lambda, a_subln_g, a_w_o,
           b_w_qkv, b_sinks, b_w_o, mlp_w_up, mlp_w_down):
    wts = dict(norm_mix_g=norm_mix_g, norm_mlp_g=norm_mlp_g[:, None, :], final_norm_g=final_norm_g,
               a_w_qkv=a_w_qkv, a_lambda=a_lambda, a_subln_g=a_subln_g[:, None, :],
               a_w_o=a_w_o, b_w_qkv=b_w_qkv, b_sinks=b_sinks,
               b_w_o=b_w_o, mlp_w_up=mlp_w_up, mlp_w_down=mlp_w_down)
    n_a, db, past = cache_a_k.shape[:3]
    caches = dict(
        a_kt=jnp.transpose(cache_a_k, (0, 1, 3, 4, 5, 2)).reshape(n_a, db, A_QK, past),
        a_v=cache_a_v.reshape(n_a, db, past * A_HEADS, A_DV),
        b=[(cache_b_k[j], cache_b_v[j], _swap_halves(cache_b_k[j]), _swap_halves(cache_b_v[j]))
           for j in range(cache_b_k.shape[0])])
    bf16_weights = []
    ys, aks, avs, bks, bvs = _trunk(x_sample, caches, wts, rel_table, bf16_weights,
                                    tm=512, tm_post=512, tf=1024)
    yp, akp, avp, bkp, bvp = _trunk(x_prompt, None, wts, rel_table, bf16_weights,
                                    tm=512, tm_post=1024, tf=1024)
    return (yp, ys, akp, avp, bkp, bvp, aks, avs, bks, bvs)
```

```python
import functools
import math

import jax
import jax.numpy as jnp
from jax import lax
from jax.experimental import pallas as pl
from jax.experimental.pallas import tpu as pltpu

BF16 = jnp.bfloat16
F32 = jnp.float32

D_MODEL = 1024
DEPTH = 4
CHUNK = 64
A_HEADS = 8
A_DK = 64
A_DV = 128
A_QK = A_HEADS * 2 * A_DK
A_V = A_HEADS * A_DV
B_Q_HEADS = 16
B_KV_HEADS = 2
B_HD = 64
B_Q = B_Q_HEADS * B_HD
B_KV = B_KV_HEADS * B_HD
WINDOW = 128
NUM_BUCKETS = 32
MAX_DIST = 128
N_MAPS = 16
D_FF = 4 * D_MODEL
EPS = 1e-6
NEG = -1e30

LANES = 128
HALF = LANES // 2
VMEM_LIMIT = 56 * 1024 * 1024

NT_DIMS = (((1,), (1,)), ((), ()))

LOG2E = math.log2(math.e)


def _lambda_init(layer):
    return 0.8 - 0.6 * math.exp(-0.3 * layer)


def _rms(x, g):
    return x * lax.rsqrt(jnp.mean(x * x, axis=-1, keepdims=True) + EPS) * g


def _params(n_axes):
    return pltpu.CompilerParams(
        dimension_semantics=("arbitrary",) * n_axes, vmem_limit_bytes=VMEM_LIMIT)


def _lane_fold_max(s):
    mx = s[:, 0:LANES]
    for gi in range(1, s.shape[1] // LANES):
        mx = jnp.maximum(mx, s[:, gi * LANES:(gi + 1) * LANES])
    return mx


def _row_max(s):
    return jnp.broadcast_to(jnp.max(_lane_fold_max(s), axis=-1, keepdims=True), (s.shape[0], LANES))


def _exp_rows(s, m):
    return jnp.concatenate([jnp.exp2(s[:, gi * LANES:(gi + 1) * LANES] - m)
                            for gi in range(s.shape[1] // LANES)], axis=1).astype(BF16)


def _with_ones(v):
    return jnp.concatenate([v, jnp.ones(v.shape, v.dtype)], axis=1)


def _proj_a_prompt_body(x_ref, g_ref, w_ref, wkt_ref, *refs, first):
    q_ref, kt_ref, v_ref, vb_ref = refs[-4:]
    h = _rms(x_ref[...], g_ref[...]).astype(BF16)
    tm = h.shape[0]
    if first:
        for slot in range(1, kt_ref.shape[0]):
            kt_ref[slot] = jnp.zeros(kt_ref.shape[1:], F32)
            v_ref[slot] = jnp.zeros(v_ref.shape[1:], F32)
        kt_ref, v_ref = kt_ref.at[0], v_ref.at[0]
    v = jnp.dot(h, w_ref[:, 2 * A_QK:], preferred_element_type=F32)
    vb_ref[...] = v.astype(BF16)
    for hh in range(A_HEADS):
        v_ref[pl.ds(hh, tm, stride=A_HEADS), :] = v[:, hh * A_DV:(hh + 1) * A_DV]
    kt_ref[...] = lax.dot_general(wkt_ref[...], h, NT_DIMS, preferred_element_type=F32)
    q = jnp.dot(h, w_ref[:, 0:A_QK], preferred_element_type=F32)
    q_ref[...] = (q * (A_DK ** -0.5 * LOG2E)).astype(BF16)


def _proj_a_prompt(x, g, w, wkt, w_layer, layer, kt_prev, v_prev, *, b, s, tm, n_a):
    m = b * s
    per_b = s // tm
    row = lambda i: (i, 0)
    full = lambda i: (0, 0)
    stacked = lambda i: (w_layer, 0, 0)
    first = kt_prev is None
    assert first == (layer == 0)
    in_specs = [pl.BlockSpec((tm, D_MODEL), row),
                pl.BlockSpec((1, D_MODEL), full),
                pl.BlockSpec((None, D_MODEL, 2 * A_QK + A_V), stacked),
                pl.BlockSpec((None, A_QK, D_MODEL), stacked)]
    operands = [x, g, w, wkt]
    if first:
        kt_spec = pl.BlockSpec((n_a, None, A_QK, tm), lambda i: (0, i // per_b, 0, i % per_b))
        v_spec = pl.BlockSpec((n_a, tm * A_HEADS, A_DV), lambda i: (0, i, 0))
        aliases = {}
    else:
        kt_spec = pl.BlockSpec((None, None, A_QK, tm), lambda i: (layer, i // per_b, 0, i % per_b))
        v_spec = pl.BlockSpec((None, tm * A_HEADS, A_DV), lambda i: (layer, i, 0))
        in_specs += [pl.BlockSpec(memory_space=pl.ANY)] * 2
        operands += [kt_prev, v_prev]
        aliases = {4: 1, 5: 2}
    return pl.pallas_call(
        functools.partial(_proj_a_prompt_body, first=first),
        grid=(m // tm,),
        in_specs=in_specs,
        out_specs=[pl.BlockSpec((tm, A_QK), row),
                   kt_spec,
                   v_spec,
                   pl.BlockSpec((tm, A_V), row)],
        out_shape=[jax.ShapeDtypeStruct((m, A_QK), BF16),
                   jax.ShapeDtypeStruct((n_a, b, A_QK, s), F32),
                   jax.ShapeDtypeStruct((n_a, m * A_HEADS, A_DV), F32),
                   jax.ShapeDtypeStruct((m, A_V), BF16)],
        input_output_aliases=aliases,
        compiler_params=_params(1),
        name="proj_a_prompt",
    )(*operands)


def _proj_a_sample_body(x_ref, g_ref, w_ref, q_ref, k_ref, v_ref, kb_ref, vb_ref, wb_ref, wkt_ref):
    wb_ref[...] = w_ref[...].astype(BF16)
    wkt_ref[...] = w_ref[:, A_QK:2 * A_QK].T.astype(BF16)
    h = _rms(x_ref[...], g_ref[...]).astype(BF16)
    q = jnp.dot(h, wb_ref[:, 0:A_QK], preferred_element_type=F32)
    q_ref[...] = (q * (A_DK ** -0.5 * LOG2E)).astype(BF16)
    k = jnp.dot(h, wb_ref[:, A_QK:2 * A_QK], preferred_element_type=F32)
    k_ref[...] = k
    kb_ref[...] = k.astype(BF16)
    v = jnp.dot(h, wb_ref[:, 2 * A_QK:], preferred_element_type=F32)
    v_ref[...] = v
    vb_ref[...] = v.astype(BF16)


def _proj_a_sample(x, g, w, layer):
    m = x.shape[0]
    n_out = 2 * A_QK + A_V
    full = lambda i: (0, 0)
    full3 = lambda i: (0, 0, 0)
    return pl.pallas_call(
        _proj_a_sample_body,
        grid=(1,),
        in_specs=[pl.BlockSpec((m, D_MODEL), full),
                  pl.BlockSpec((1, D_MODEL), full),
                  pl.BlockSpec((None, D_MODEL, n_out), lambda i: (layer, 0, 0))],
        out_specs=[pl.BlockSpec((m, A_QK), full),
                   pl.BlockSpec((m, A_QK), full),
                   pl.BlockSpec((m, A_V), full),
                   pl.BlockSpec((m, A_QK), full),
                   pl.BlockSpec((m, A_V), full),
                   pl.BlockSpec((None, D_MODEL, n_out), full3),
                   pl.BlockSpec((None, A_QK, D_MODEL), full3)],
        out_shape=[jax.ShapeDtypeStruct((m, A_QK), BF16),
                   jax.ShapeDtypeStruct((m, A_QK), F32),
                   jax.ShapeDtypeStruct((m, A_V), F32),
                   jax.ShapeDtypeStruct((m, A_QK), BF16),
                   jax.ShapeDtypeStruct((m, A_V), BF16),
                   jax.ShapeDtypeStruct((1, D_MODEL, n_out), BF16),
                   jax.ShapeDtypeStruct((1, A_QK, D_MODEL), BF16)],
        compiler_params=_params(1),
        name="proj_a_sample",
    )(x, g, w)


def _proj_b_body(x_ref, g_ref, w_ref, q_ref, k_ref, v_ref, kb_ref, vb_ref, *wb_ref, tail):
    if wb_ref:
        wb_ref[0][...] = w_ref[...].astype(BF16)
        w_ref = wb_ref[0]
    h = _rms(x_ref[...], g_ref[...]).astype(BF16)
    kv = jnp.dot(h, w_ref[:, B_Q:], preferred_element_type=F32)
    k, v = kv[:, 0:B_KV], kv[:, B_KV:]
    kb_ref[...] = jnp.concatenate([k, pltpu.roll(k, HALF, axis=1)], axis=1).astype(BF16)
    vb_ref[...] = jnp.concatenate([v, pltpu.roll(v, HALF, axis=1)], axis=1).astype(BF16)
    keep = k.shape[0] if tail is None else tail
    k_ref[...] = k[k.shape[0] - keep:]
    v_ref[...] = v[v.shape[0] - keep:]
    q = jnp.dot(h, w_ref[:, 0:B_Q], preferred_element_type=F32)
    q_ref[...] = (q * (B_HD ** -0.5 * LOG2E)).astype(BF16)


def _proj_b(x, g, w, layer, tm, *, seq=None, tail=None, cast_weights=False):
    m = x.shape[0]
    assert not cast_weights or m == tm
    n_out = B_Q + 2 * B_KV
    row = lambda i: (i, 0)
    full = lambda i: (0, 0)
    wb_spec = [pl.BlockSpec((None, D_MODEL, n_out), lambda i: (0, 0, 0))] if cast_weights else []
    wb_shape = [jax.ShapeDtypeStruct((1, D_MODEL, n_out), BF16)] if cast_weights else []
    if tail is None:
        kv_spec = pl.BlockSpec((tm, B_KV), row)
        kv_shape = jax.ShapeDtypeStruct((m, B_KV), F32)
    else:
        assert seq % tm == 0 and tail <= tm
        per_seq = seq // tm
        kv_spec = pl.BlockSpec((None, tail, B_KV), lambda i: (i // per_seq, 0, 0))
        kv_shape = jax.ShapeDtypeStruct((m // seq, tail, B_KV), F32)
    return pl.pallas_call(
        functools.partial(_proj_b_body, tail=tail),
        grid=(m // tm,),
        in_specs=[pl.BlockSpec((tm, D_MODEL), row),
                  pl.BlockSpec((1, D_MODEL), full),
                  pl.BlockSpec((None, D_MODEL, B_Q + 2 * B_KV), lambda i: (layer, 0, 0))],
        out_specs=[pl.BlockSpec((tm, B_Q), row),
                   kv_spec,
                   kv_spec,
                   pl.BlockSpec((tm, 2 * B_KV), row),
                   pl.BlockSpec((tm, 2 * B_KV), row)] + wb_spec,
        out_shape=[jax.ShapeDtypeStruct((m, B_Q), BF16),
                   kv_shape,
                   kv_shape,
                   jax.ShapeDtypeStruct((m, 2 * B_KV), BF16),
                   jax.ShapeDtypeStruct((m, 2 * B_KV), BF16)] + wb_shape,
        compiler_params=_params(1),
        name="proj_b",
    )(x, g, w)


def _t5_bucket(rel):
    nb = NUM_BUCKETS // 2
    n = -rel
    ret = jnp.where(n < 0, nb, 0)
    n = jnp.abs(n)
    max_exact = nb // 2
    nf = jnp.maximum(n, 1).astype(jnp.float32)
    large = max_exact + (jnp.log(nf / max_exact) / math.log(MAX_DIST / max_exact)
                         * (nb - max_exact)).astype(jnp.int32)
    large = jnp.minimum(large, nb - 1)
    return ret + jnp.where(n < max_exact, n, large)


def _check_far_bucket(min_dist):
    nb, max_exact = NUM_BUCKETS // 2, NUM_BUCKETS // 4
    ratio = math.log(min_dist / max_exact) / math.log(MAX_DIST / max_exact)
    assert ratio * (nb - max_exact) >= nb - 1 - max_exact + 1e-3, "far keys need one shared bucket"


def _far_bias(table, tq, w):
    far = table[NUM_BUCKETS // 2 - 1].astype(F32) * LOG2E
    return jnp.broadcast_to(far[:, None, None], (N_MAPS, tq, w))


def _near_bias(table, tq, w, back, band):
    r = jnp.arange(tq)[:, None]
    j = jnp.arange(w)[None, :] - back
    onehot = (_t5_bucket(j - r)[None] == jnp.arange(NUM_BUCKETS)[:, None, None]).astype(F32)
    onehot = lax.optimization_barrier(onehot)
    bias = jnp.einsum("bm,brj->mrj", table.astype(F32), onehot, precision=lax.Precision.HIGHEST)
    kc, qc = j // CHUNK, r // CHUNK
    mask = kc <= qc
    if band is not None:
        mask = mask & (kc >= qc - band)
    return jnp.where(mask[None], bias * LOG2E, NEG)


A_TILE = 256
A_ROWS = 64
A_HEADS_PER_STEP = 1


def _diff_lambda(lamp_ref, lam_init):
    lp = lamp_ref[...]
    return (jnp.exp(jnp.sum(lp[0:1] * lp[1:2], axis=-1, keepdims=True))
            - jnp.exp(jnp.sum(lp[2:3] * lp[3:4], axis=-1, keepdims=True)) + lam_init)


def _stack_maps(q):
    lane = lax.broadcasted_iota(jnp.int32, q.shape, 1)
    zero = jnp.zeros_like(q)
    return jnp.concatenate([jnp.where(lane < HALF, q, zero), jnp.where(lane >= HALF, q, zero)], axis=0)


def _diff_epilogue(acc, lam, g, lam_init):
    tq = acc.shape[0] // 2
    o_all = acc[:, :A_DV] / acc[:, A_DV:]
    o = o_all[:tq] - lam * o_all[tq:]
    return (_rms(o, g) * (1.0 - lam_init)).astype(BF16)


def _attn_a_scores(i, q, kt_ref, bias_ref, s_ref, m_ref):
    tl = A_TILE
    qs = _stack_maps(q)
    n_far = max(i - 1, 0)
    for kt in range(i + 1):
        cols = slice(kt * tl, (kt + 1) * tl)
        s = jnp.dot(qs, kt_ref[:, cols], preferred_element_type=F32)
        near = kt >= n_far
        if near:
            s = s + bias_ref[2 - (i - kt)].reshape(2 * tl, tl)
        s_ref[:, cols] = s
        fold = _lane_fold_max(s)
        first = kt == (n_far if near else 0)
        m_ref[int(near)] = fold if first else jnp.maximum(m_ref[int(near)], fold)
        yield
    for r in range(0, 2 * tl, A_ROWS):
        rows = slice(r, r + A_ROWS)
        mx = m_ref[1, rows, :]
        if n_far:
            c = bias_ref[0, r // tl, r % tl:r % tl + A_ROWS, 0:LANES]
            mx = jnp.maximum(mx, m_ref[0, rows, :] + c)
        m = jnp.broadcast_to(jnp.max(mx, axis=-1, keepdims=True), (A_ROWS, LANES))
        m_ref[1, rows, :] = m
        if n_far:
            m_ref[0, rows, :] = m - c


def _attn_a_numerators(i, s_ref, m_ref, v1_ref, out):
    tl = A_TILE
    n_far = max(i - 1, 0)
    acc = None
    for kt in range(i + 1):
        cols = slice(kt * tl, (kt + 1) * tl)
        p = _exp_rows(s_ref[:, cols], m_ref[1 if kt >= n_far else 0])
        part = jnp.dot(p, v1_ref[cols, :], preferred_element_type=F32)
        acc = part if acc is None else acc + part
        yield
    out.append(acc)


def _interleave(*steps):
    steps = list(steps)
    while steps:
        for gen in list(steps):
            if next(gen, StopIteration) is StopIteration:
                steps.remove(gen)


def _attn_a_prompt_body(lamp_ref, q_ref, kt_ref, v_ref, bias_ref, g_ref, o_ref,
                        s_ref, m_ref, v1_ref, ktb_ref, *, nt, nh, lam_init):
    tl = A_TILE
    lam = _diff_lambda(lamp_ref, lam_init)
    ktb_ref[...] = kt_ref[...].astype(BF16)
    for hh in range(nh):
        v1_ref[hh, :, 0:A_DV] = v_ref[0, :, hh * A_DV:(hh + 1) * A_DV]
        v1_ref[hh, :, A_DV:] = jnp.ones((nt * tl, A_DV), BF16)
    work = [(i, hh) for i in reversed(range(nt)) for hh in range(nh)]

    def scores(i, hh):
        lanes = slice(hh * LANES, (hh + 1) * LANES)
        return _attn_a_scores(i, q_ref[0, i * tl:(i + 1) * tl, lanes], ktb_ref.at[lanes, :],
                              bias_ref.at[:, 2 * hh:2 * hh + 2], s_ref.at[hh, i % 2],
                              m_ref.at[hh, i % 2])

    _interleave(scores(*work[0]))
    for n, (i, hh) in enumerate(work):
        out = []
        steps = [_attn_a_numerators(i, s_ref.at[hh, i % 2], m_ref.at[hh, i % 2], v1_ref.at[hh], out)]
        if n + 1 < len(work):
            steps.insert(0, scores(*work[n + 1]))
        _interleave(*steps)
        acc = out[0]
        o_ref[0, i * tl:(i + 1) * tl, hh * LANES:(hh + 1) * LANES] = _diff_epilogue(
            acc, lam, g_ref[...], lam_init)


def _attn_a_prompt(q, kt, v, bias, lamp, g, layer, *, lam_init):
    b, s, _ = q.shape
    nt = s // A_TILE
    nh = A_HEADS_PER_STEP
    assert s % A_TILE == 0 and A_HEADS % nh == 0
    seq = pl.BlockSpec((1, s, nh * LANES), lambda h, bi: (bi, 0, h))
    return pl.pallas_call(
        functools.partial(_attn_a_prompt_body, nt=nt, nh=nh, lam_init=lam_init),
        grid=(A_HEADS // nh, b),
        in_specs=[pl.BlockSpec((None, 4, A_DK), lambda h, bi: (layer, 0, 0)),
                  seq,
                  pl.BlockSpec((None, None, nh * LANES, s), lambda h, bi: (layer, bi, h, 0)),
                  seq,
                  pl.BlockSpec((3, 2 * nh, A_TILE, A_TILE), lambda h, bi: (0, h, 0, 0)),
                  pl.BlockSpec((None, 1, A_DV), lambda h, bi: (layer, 0, 0))],
        out_specs=seq,
        out_shape=jax.ShapeDtypeStruct((b, s, A_V), BF16),
        scratch_shapes=[pltpu.VMEM((nh, 2, 2 * A_TILE, s), F32),
                        pltpu.VMEM((nh, 2, 2, 2 * A_TILE, LANES), F32),
                        pltpu.VMEM((nh, s, 2 * A_DV), BF16),
                        pltpu.VMEM((nh * LANES, s), BF16)],
        compiler_params=_params(2),
        name="attn_a_prompt",
    )(lamp, q, kt, v, bias, g)


def _attn_a_sample_body(lamp_ref, q_ref, ktc_ref, vc_ref, kn_ref, vn_ref, bias_ref, g_ref, o_ref,
                        *, lam_init):
    tq = q_ref.shape[1]
    past = ktc_ref.shape[1]
    lam = _diff_lambda(lamp_ref, lam_init)
    for h in range(A_HEADS):
        cols = slice(h * LANES, (h + 1) * LANES)
        qs = _stack_maps(q_ref[0, :, cols])
        bias = bias_ref[2 * h:2 * h + 2].reshape(2 * tq, past + LANES)
        s_c = jnp.dot(qs, ktc_ref[cols, :].astype(BF16), preferred_element_type=F32)
        s_c = s_c + bias[:, 0:past]
        s_n = lax.dot_general(qs, kn_ref[0, :, cols], NT_DIMS, preferred_element_type=F32)
        s_n = s_n + bias[:, past:past + tq]
        m = jnp.maximum(_row_max(s_c), jnp.broadcast_to(jnp.max(s_n, axis=-1, keepdims=True),
                                                        (2 * tq, LANES)))
        p_c = _exp_rows(s_c, m)
        p_n = jnp.exp2(s_n - m[:, 0:tq]).astype(BF16)
        v_c = vc_ref[pl.ds(h, past, stride=A_HEADS), :].astype(BF16)
        acc = (jnp.dot(p_c, _with_ones(v_c), preferred_element_type=F32)
               + jnp.dot(p_n, _with_ones(vn_ref[0, :, cols]), preferred_element_type=F32))
        o_ref[0, :, cols] = _diff_epilogue(acc, lam, g_ref[...], lam_init)


def _attn_a_sample(q, ktc, vc, kn, vn, bias, lamp, g, layer, *, lam_init):
    b, tq, _ = q.shape
    past = ktc.shape[3]
    new = lambda bi: (bi, 0, 0)
    return pl.pallas_call(
        functools.partial(_attn_a_sample_body, lam_init=lam_init),
        grid=(b,),
        in_specs=[pl.BlockSpec((None, 4, A_DK), lambda bi: (layer, 0, 0)),
                  pl.BlockSpec((1, tq, A_QK), new),
                  pl.BlockSpec((None, None, A_QK, past), lambda bi: (layer, bi, 0, 0)),
                  pl.BlockSpec((None, None, past * A_HEADS, A_DV), lambda bi: (layer, bi, 0, 0)),
                  pl.BlockSpec((1, tq, A_QK), new),
                  pl.BlockSpec((1, tq, A_V), new),
                  pl.BlockSpec((N_MAPS, tq, past + LANES), lambda bi: (0, 0, 0)),
                  pl.BlockSpec((None, 1, A_DV), lambda bi: (layer, 0, 0))],
        out_specs=pl.BlockSpec((1, tq, A_V), new),
        out_shape=jax.ShapeDtypeStruct((b, tq, A_V), BF16),
        compiler_params=_params(1),
        name="attn_a_sample",
    )(lamp, q, ktc, vc, kn, vn, bias, g)


_B_ORDER_A = (0, 2, 4, 6, 9, 11, 13, 15)
_B_ORDER_B = (1, 3, 5, 7, 8, 10, 12, 14)


def _attn_b_body(sink_ref, q_ref, k_ref, v_ref, bias_ref, o_ref, *, layer, tq, w, back, i_off, nsub):
    lane = lax.broadcasted_iota(jnp.int32, (tq, LANES), 1)
    lo = lane < HALF
    half = B_Q_HEADS // 4
    orders = (_B_ORDER_A, _B_ORDER_B)
    sinks = [jnp.concatenate([jnp.full((tq, LANES), sink_ref[layer, h] * LOG2E, F32)
                              for h in orders[t]], axis=0) for t in range(2)]
    outs = {}

    def chain(u, t):
        tile = pl.program_id(1) * nsub + u + i_off
        variant = 1 if (u > 0 or i_off > 0) else jnp.minimum(tile, 1)
        near_start = pl.multiple_of(jnp.maximum(tile * tq - back, 0), CHUNK)
        group = slice(t * LANES, (t + 1) * LANES)
        pieces = []
        for c in range(B_Q_HEADS // 2):
            qc = q_ref[0, u * tq:(u + 1) * tq, c * LANES:(c + 1) * LANES]
            keep_lo = (c < half) == (t == 0)
            pieces.append(jnp.where(lo == keep_lo, qc, jnp.zeros_like(qc)))
        qs = jnp.concatenate(pieces, axis=0)
        s = lax.dot_general(qs, k_ref[0, pl.ds(near_start, w), group], NT_DIMS,
                            preferred_element_type=F32)
        s = s + bias_ref[variant, t].reshape((B_Q_HEADS // 2) * tq, w)
        m = jnp.maximum(_row_max(s), sinks[t])
        p = _exp_rows(s, m)
        acc = jnp.dot(p, _with_ones(v_ref[0, pl.ds(near_start, w), group]),
                      preferred_element_type=F32)
        outs[u, t] = acc[:, :LANES] / (acc[:, LANES:] + jnp.exp2(sinks[t] - m))

    for u in range(nsub):
        for t in range(2):
            chain(u, t)

    for u in range(nsub):
        for c in range(B_Q_HEADS // 2):
            oa = outs[u, 0][c * tq:(c + 1) * tq]
            ob = outs[u, 1][c * tq:(c + 1) * tq]
            oc = jnp.where(lo, oa, ob) if c < half else jnp.where(lo, ob, oa)
            o_ref[0, u * tq:(u + 1) * tq, c * LANES:(c + 1) * LANES] = oc.astype(BF16)


def _attn_b(q, k, v, bias, sinks, layer, *, tq, w, back, i_off, nsub):
    b, sq, _ = q.shape
    sk = k.shape[1]
    rows = nsub * tq
    assert sq % rows == 0 and (sq // tq - 1 + i_off) * tq - back + w <= sk
    body = functools.partial(_attn_b_body, layer=layer, tq=tq, w=w, back=back, i_off=i_off,
                             nsub=nsub)
    return pl.pallas_call(
        body,
        grid=(b, sq // rows),
        in_specs=[pl.BlockSpec(memory_space=pltpu.SMEM),
                  pl.BlockSpec((1, rows, B_Q), lambda bi, i: (bi, i, 0)),
                  pl.BlockSpec((1, sk, 2 * B_KV), lambda bi, i: (bi, 0, 0)),
                  pl.BlockSpec((1, sk, 2 * B_KV), lambda bi, i: (bi, 0, 0)),
                  pl.BlockSpec((2, 2, B_Q_HEADS // 2, tq, w), lambda bi, i: (0, 0, 0, 0, 0))],
        out_specs=pl.BlockSpec((1, rows, B_Q), lambda bi, i: (bi, i, 0)),
        out_shape=jax.ShapeDtypeStruct((b, sq, B_Q), BF16),
        compiler_params=_params(2),
        name="attn_b",
    )(sinks, q, k, v, bias)


def _post_body(x_ref, o_ref, wo_ref, g_ref, wu_ref, wd_ref, gf_ref, y_ref, *refs, final, cast_weights):
    hn_ref = refs[-1]
    f = pl.program_id(1)
    if cast_weights:
        wob_ref, wub_ref, wdb_ref = refs[:3]

    @pl.when(f == 0)
    def _():
        if cast_weights:
            wob_ref[...] = wo_ref[...].astype(BF16)
        wo = (wob_ref if cast_weights else wo_ref)[...]
        x1 = x_ref[...] + jnp.dot(o_ref[...], wo, preferred_element_type=F32)
        y_ref[...] = x1
        hn_ref[...] = _rms(x1, g_ref[...]).astype(BF16)

    if cast_weights:
        wub_ref[...] = wu_ref[...].astype(BF16)
        wdb_ref[...] = wd_ref[...].astype(BF16)
        wu_ref, wd_ref = wub_ref, wdb_ref
    u = jnp.dot(hn_ref[...], wu_ref[...], preferred_element_type=F32)
    a = jnp.square(jnp.maximum(u, 0.0)).astype(BF16)
    y_ref[...] += jnp.dot(a, wd_ref[...], preferred_element_type=F32)

    if final:
        @pl.when(f == pl.num_programs(1) - 1)
        def _():
            y_ref[...] = _rms(y_ref[...], gf_ref[...])


def _post(x, o, wo, wo_layer, g, g_layer, wu, wd, w_layer, gf, *, tm, tf, final, cast_weights):
    m = x.shape[0]
    assert not cast_weights or m == tm
    row = lambda i, f: (i, 0)
    full = lambda i, f: (0, 0)
    out_specs = [pl.BlockSpec((tm, D_MODEL), row)]
    out_shape = [jax.ShapeDtypeStruct((m, D_MODEL), F32)]
    if cast_weights:
        out_specs += [pl.BlockSpec((None, D_MODEL, D_MODEL), lambda i, f: (0, 0, 0)),
                      pl.BlockSpec((None, D_MODEL, tf), lambda i, f: (0, 0, f)),
                      pl.BlockSpec((None, tf, D_MODEL), lambda i, f: (0, f, 0))]
        out_shape += [jax.ShapeDtypeStruct((1, D_MODEL, D_MODEL), BF16),
                      jax.ShapeDtypeStruct((1, D_MODEL, D_FF), BF16),
                      jax.ShapeDtypeStruct((1, D_FF, D_MODEL), BF16)]
    outs = pl.pallas_call(
        functools.partial(_post_body, final=final, cast_weights=cast_weights),
        grid=(m // tm, D_FF // tf),
        in_specs=[pl.BlockSpec((tm, D_MODEL), row),
                  pl.BlockSpec((tm, D_MODEL), row),
                  pl.BlockSpec((None, D_MODEL, D_MODEL), lambda i, f: (wo_layer, 0, 0)),
                  pl.BlockSpec((None, 1, D_MODEL), lambda i, f: (g_layer, 0, 0)),
                  pl.BlockSpec((None, D_MODEL, tf), lambda i, f: (w_layer, 0, f)),
                  pl.BlockSpec((None, tf, D_MODEL), lambda i, f: (w_layer, f, 0)),
                  pl.BlockSpec((1, D_MODEL), full)],
        out_specs=out_specs,
        out_shape=out_shape,
        scratch_shapes=[pltpu.VMEM((tm, D_MODEL), BF16)],
        compiler_params=_params(2),
        name="post",
    )(x, o, wo, g, wu, wd, gf)
    return outs if cast_weights else outs[0]


def _post_resident_body(x_ref, o_ref, wo_ref, g_ref, wu_ref, wd_ref, gf_ref, y_ref, hn_ref,
                        *, tf, final):
    x1 = x_ref[...] + jnp.dot(o_ref[...], wo_ref[...], preferred_element_type=F32)
    y_ref[...] = x1
    hn_ref[...] = _rms(x1, g_ref[...]).astype(BF16)
    mlp = None
    for f in range(D_FF // tf):
        u = jnp.dot(hn_ref[...], wu_ref[:, f * tf:(f + 1) * tf], preferred_element_type=F32)
        a = jnp.square(jnp.maximum(u, 0.0)).astype(BF16)
        part = jnp.dot(a, wd_ref[f * tf:(f + 1) * tf, :], preferred_element_type=F32)
        mlp = part if mlp is None else mlp + part
    y = y_ref[...] + mlp
    y_ref[...] = _rms(y, gf_ref[...]) if final else y


def _post_resident(x, o, wo, g, g_layer, wu, wd, gf, *, tm, tf, final):
    m = x.shape[0]
    row = lambda i: (i, 0)
    once = dict(pipeline_mode=pl.Buffered(1))
    return pl.pallas_call(
        functools.partial(_post_resident_body, tf=tf, final=final),
        grid=(m // tm,),
        in_specs=[pl.BlockSpec((tm, D_MODEL), row),
                  pl.BlockSpec((tm, D_MODEL), row),
                  pl.BlockSpec((None, D_MODEL, D_MODEL), lambda i: (0, 0, 0), **once),
                  pl.BlockSpec((None, 1, D_MODEL), lambda i: (g_layer, 0, 0)),
                  pl.BlockSpec((None, D_MODEL, D_FF), lambda i: (0, 0, 0), **once),
                  pl.BlockSpec((None, D_FF, D_MODEL), lambda i: (0, 0, 0), **once),
                  pl.BlockSpec((1, D_MODEL), lambda i: (0, 0))],
        out_specs=pl.BlockSpec((tm, D_MODEL), row),
        out_shape=jax.ShapeDtypeStruct((m, D_MODEL), F32),
        scratch_shapes=[pltpu.VMEM((tm, D_MODEL), BF16)],
        compiler_params=_params(1),
        name="post_resident",
    )(x, o, wo, g, wu, wd, gf)


_B_PROMPT = dict(tq=128, w=256, back=128)
_B_SAMPLE = dict(tq=64, w=256, back=128)
B_TILES_PER_STEP = 16
_A_SAMPLE_BACK = 128


def _a_prompt_bias(table):
    _check_far_bucket(A_TILE + 1)
    near = _near_bias(table, A_TILE, 2 * A_TILE, A_TILE, None)
    return jnp.stack([_far_bias(table, A_TILE, A_TILE), near[:, :, :A_TILE], near[:, :, A_TILE:]])


def _a_sample_bias(table, past, t):
    assert t <= LANES
    _check_far_bucket(_A_SAMPLE_BACK + 1)
    far = _far_bias(table, t, past - _A_SAMPLE_BACK)
    near = _near_bias(table, t, _A_SAMPLE_BACK + LANES, _A_SAMPLE_BACK, None)
    return jnp.concatenate([far, near], axis=2)


def _b_bias(table, tq, w, back):
    def one(bk):
        t = _near_bias(table, tq, w, bk, WINDOW // CHUNK)
        return jnp.stack([jnp.stack([t[h] for h in order]) for order in (_B_ORDER_A, _B_ORDER_B)])
    return jnp.stack([one(0), one(back)])


def _pad_keys(t, total):
    return jnp.pad(t, ((0, 0), (0, total - t.shape[1]), (0, 0)))


def _trunk(x, caches, wts, table, bf16_weights, *, tm, tm_post, tf):
    b, s, _ = x.shape
    sample = caches is not None
    n_a = wts["a_w_qkv"].shape[0]
    xm = x.reshape(b * s, D_MODEL)
    b_cfg = _B_SAMPLE if sample else _B_PROMPT
    b_bias = _b_bias(table, **b_cfg)
    if sample:
        a_bias = _a_sample_bias(table, caches["a_kt"].shape[3], s)
    else:
        a_bias = _a_prompt_bias(table)
    ak, av, bk, bv = [], [], [], []
    kt_buf = v_buf = None
    for i in range(DEPTH):
        j = i // 2
        g_mix = wts["norm_mix_g"][i][None, :]
        if sample:
            bf16_weights.append({})
        cast = bf16_weights[i]
        if i % 2 == 0:
            lam_init = _lambda_init(i)
            if sample:
                q, k, v, kb, vb, cast["qkv"], cast["kt"] = _proj_a_sample(xm, g_mix, wts["a_w_qkv"], j)
                ak.append(k.reshape(b, s, A_HEADS, 2, A_DK))
                av.append(v.reshape(b, s, A_HEADS, A_DV))
                o = _attn_a_sample(q.reshape(b, s, A_QK), caches["a_kt"], caches["a_v"],
                                   kb.reshape(b, s, A_QK), vb.reshape(b, s, A_V), a_bias,
                                   wts["a_lambda"], wts["a_subln_g"], j, lam_init=lam_init)
            else:
                q, kt_buf, v_buf, vb = _proj_a_prompt(
                    xm, g_mix, cast["qkv"], cast["kt"], 0, j, kt_buf, v_buf,
                    b=b, s=s, tm=tm if kt_buf is None else tm_post, n_a=n_a)
                o = _attn_a_prompt(q.reshape(b, s, A_QK), kt_buf, vb.reshape(b, s, A_V), a_bias,
                                   wts["a_lambda"], wts["a_subln_g"], j, lam_init=lam_init)
            wo, wo_layer = wts["a_w_o"], j
        else:
            if sample:
                q, k, v, kb, vb, cast["qkv"] = _proj_b(xm, g_mix, wts["b_w_qkv"], j, tm,
                                                       cast_weights=True)
            else:
                q, k, v, kb, vb = _proj_b(xm, g_mix, cast["qkv"], 0, tm_post, seq=s, tail=WINDOW)
            k = k.reshape(b, -1, B_KV_HEADS, B_HD)
            v = v.reshape(b, -1, B_KV_HEADS, B_HD)
            kb = kb.reshape(b, s, 2 * B_KV)
            vb = vb.reshape(b, s, 2 * B_KV)
            i_off = 0
            if sample:
                ck, cv, ckb, cvb = caches["b"][j]
                bk.append(jnp.concatenate([ck, k], axis=1)[:, s:])
                bv.append(jnp.concatenate([cv, v], axis=1)[:, s:])
                i_off = ckb.shape[1] // b_cfg["tq"]
                total = ckb.shape[1] - b_cfg["back"] + b_cfg["w"]
                kb = _pad_keys(jnp.concatenate([ckb, kb], axis=1), total)
                vb = _pad_keys(jnp.concatenate([cvb, vb], axis=1), total)
            else:
                bk.append(k)
                bv.append(v)
            o = _attn_b(q.reshape(b, s, B_Q), kb, vb, b_bias, wts["b_sinks"], j, i_off=i_off,
                        nsub=1 if sample else B_TILES_PER_STEP, **b_cfg)
            wo, wo_layer = wts["b_w_o"], j
        gf = wts["final_norm_g"][None, :]
        tiling = dict(tm=tm_post, tf=tf, final=(i == DEPTH - 1))
        if sample:
            xm, cast["wo"], cast["wu"], cast["wd"] = _post(
                xm, o.reshape(b * s, D_MODEL), wo, wo_layer, wts["norm_mlp_g"], i,
                wts["mlp_w_up"], wts["mlp_w_down"], i, gf, cast_weights=True, **tiling)
        else:
            xm = _post_resident(xm, o.reshape(b * s, D_MODEL), cast["wo"], wts["norm_mlp_g"], i,
                                cast["wu"], cast["wd"], gf, **tiling)
    if sample:
        a_k, a_v = jnp.stack(ak), jnp.stack(av)
    else:
        a_k = jnp.transpose(kt_buf.reshape(n_a, b, A_HEADS, 2, A_DK, s), (0, 1, 5, 2, 3, 4))
        a_v = v_buf.reshape(n_a, b, s, A_HEADS, A_DV)
    return xm.reshape(b, s, D_MODEL), a_k, a_v, jnp.stack(bk), jnp.stack(bv)


def _swap_halves(t):
    flat = t.reshape(t.shape[:-2] + (B_KV,))
    swapped = t[..., ::-1, :].reshape(t.shape[:-2] + (B_KV,))
    return jnp.concatenate([flat, swapped], axis=-1).astype(BF16)


def kernel(x_prompt, x_sample, cache_a_k, cache_a_v, cache_b_k, cache_b_v, rel_table,
           norm_mix_g, norm_mlp_g, final_norm_g, a_w_qkv, a_lambda, a_subln_g, a_w_o,
           b_w_qkv, b_sinks, b_w_o, mlp_w_up, mlp_w_down):
    wts = dict(norm_mix_g=norm_mix_g, norm_mlp_g=norm_mlp_g[:, None, :], final_norm_g=final_norm_g,
               a_w_qkv=a_w_qkv, a_lambda=a_lambda, a_subln_g=a_subln_g[:, None, :],
               a_w_o=a_w_o, b_w_qkv=b_w_qkv, b_sinks=b_sinks,
               b_w_o=b_w_o, mlp_w_up=mlp_w_up, mlp_w_down=mlp_w_down)
    n_a, db, past = cache_a_k.shape[:3]
    caches = dict(
        a_kt=jnp.transpose(cache_a_k, (0, 1, 3, 4, 5, 2)).reshape(n_a, db, A_QK, past),
        a_v=cache_a_v.reshape(n_a, db, past * A_HEADS, A_DV),
        b=[(cache_b_k[j], cache_b_v[j], _swap_halves(cache_b_k[j]), _swap_halves(cache_b_v[j]))
           for j in range(cache_b_k.shape[0])])
    bf16_weights = []
    ys, aks, avs, bks, bvs = _trunk(x_sample, caches, wts, rel_table, bf16_weights,
                                    tm=512, tm_post=512, tf=1024)
    yp, akp, avp, bkp, bvp = _trunk(x_prompt, None, wts, rel_table, bf16_weights,
                                    tm=512, tm_post=1024, tf=1024)
    return (yp, ys, akp, avp, bkp, bvp, aks, avs, bks, bvs)
```

```python
import functools
import math

import jax
import jax.numpy as jnp
from jax import lax
from jax.experimental import pallas as pl
from jax.experimental.pallas import tpu as pltpu

BF16 = jnp.bfloat16
F32 = jnp.float32

D_MODEL = 1024
DEPTH = 4
CHUNK = 64
A_HEADS = 8
A_DK = 64
A_DV = 128
A_QK = A_HEADS * 2 * A_DK
A_V = A_HEADS * A_DV
B_Q_HEADS = 16
B_KV_HEADS = 2
B_HD = 64
B_Q = B_Q_HEADS * B_HD
B_KV = B_KV_HEADS * B_HD
WINDOW = 128
NUM_BUCKETS = 32
MAX_DIST = 128
N_MAPS = 16
D_FF = 4 * D_MODEL
EPS = 1e-6
NEG = -1e30

LANES = 128
HALF = LANES // 2
VMEM_LIMIT = 56 * 1024 * 1024

NT_DIMS = (((1,), (1,)), ((), ()))

LOG2E = math.log2(math.e)


def _lambda_init(layer):
    return 0.8 - 0.6 * math.exp(-0.3 * layer)


def _rms(x, g):
    return x * lax.rsqrt(jnp.mean(x * x, axis=-1, keepdims=True) + EPS) * g


def _params(n_axes):
    return pltpu.CompilerParams(
        dimension_semantics=("arbitrary",) * n_axes, vmem_limit_bytes=VMEM_LIMIT)


def _lane_fold_max(s):
    mx = s[:, 0:LANES]
    for gi in range(1, s.shape[1] // LANES):
        mx = jnp.maximum(mx, s[:, gi * LANES:(gi + 1) * LANES])
    return mx


def _row_max(s):
    return jnp.broadcast_to(jnp.max(_lane_fold_max(s), axis=-1, keepdims=True), (s.shape[0], LANES))


def _exp_rows(s, m):
    return jnp.concatenate([jnp.exp2(s[:, gi * LANES:(gi + 1) * LANES] - m)
                            for gi in range(s.shape[1] // LANES)], axis=1).astype(BF16)


def _with_ones(v):
    return jnp.concatenate([v, jnp.ones(v.shape, v.dtype)], axis=1)


def _proj_a_prompt_body(x_ref, g_ref, w_ref, wkt_ref, *refs, first):
    q_ref, kt_ref, v_ref, vb_ref = refs[-4:]
    h = _rms(x_ref[...], g_ref[...]).astype(BF16)
    tm = h.shape[0]
    if first:
        for slot in range(1, kt_ref.shape[0]):
            kt_ref[slot] = jnp.zeros(kt_ref.shape[1:], F32)
            v_ref[slot] = jnp.zeros(v_ref.shape[1:], F32)
        kt_ref, v_ref = kt_ref.at[0], v_ref.at[0]
    v = jnp.dot(h, w_ref[:, 2 * A_QK:], preferred_element_type=F32)
    vb_ref[...] = v.astype(BF16)
    for hh in range(A_HEADS):
        v_ref[pl.ds(hh, tm, stride=A_HEADS), :] = v[:, hh * A_DV:(hh + 1) * A_DV]
    kt_ref[...] = lax.dot_general(wkt_ref[...], h, NT_DIMS, preferred_element_type=F32)
    q = jnp.dot(h, w_ref[:, 0:A_QK], preferred_element_type=F32)
    q_ref[...] = (q * (A_DK ** -0.5 * LOG2E)).astype(BF16)


def _proj_a_prompt(x, g, w, wkt, w_layer, layer, kt_prev, v_prev, *, b, s, tm, n_a):
    m = b * s
    per_b = s // tm
    row = lambda i: (i, 0)
    full = lambda i: (0, 0)
    stacked = lambda i: (w_layer, 0, 0)
    first = kt_prev is None
    assert first == (layer == 0)
    in_specs = [pl.BlockSpec((tm, D_MODEL), row),
                pl.BlockSpec((1, D_MODEL), full),
                pl.BlockSpec((None, D_MODEL, 2 * A_QK + A_V), stacked),
                pl.BlockSpec((None, A_QK, D_MODEL), stacked)]
    operands = [x, g, w, wkt]
    if first:
        kt_spec = pl.BlockSpec((n_a, None, A_QK, tm), lambda i: (0, i // per_b, 0, i % per_b))
        v_spec = pl.BlockSpec((n_a, tm * A_HEADS, A_DV), lambda i: (0, i, 0))
        aliases = {}
    else:
        kt_spec = pl.BlockSpec((None, None, A_QK, tm), lambda i: (layer, i // per_b, 0, i % per_b))
        v_spec = pl.BlockSpec((None, tm * A_HEADS, A_DV), lambda i: (layer, i, 0))
        in_specs += [pl.BlockSpec(memory_space=pl.ANY)] * 2
        operands += [kt_prev, v_prev]
        aliases = {4: 1, 5: 2}
    return pl.pallas_call(
        functools.partial(_proj_a_prompt_body, first=first),
        grid=(m // tm,),
        in_specs=in_specs,
        out_specs=[pl.BlockSpec((tm, A_QK), row),
                   kt_spec,
                   v_spec,
                   pl.BlockSpec((tm, A_V), row)],
        out_shape=[jax.ShapeDtypeStruct((m, A_QK), BF16),
                   jax.ShapeDtypeStruct((n_a, b, A_QK, s), F32),
                   jax.ShapeDtypeStruct((n_a, m * A_HEADS, A_DV), F32),
                   jax.ShapeDtypeStruct((m, A_V), BF16)],
        input_output_aliases=aliases,
        compiler_params=_params(1),
        name="proj_a_prompt",
    )(*operands)


def _proj_a_sample_body(x_ref, g_ref, w_ref, q_ref, k_ref, v_ref, kb_ref, vb_ref, wb_ref, wkt_ref):
    wb_ref[...] = w_ref[...].astype(BF16)
    wkt_ref[...] = w_ref[:, A_QK:2 * A_QK].T.astype(BF16)
    h = _rms(x_ref[...], g_ref[...]).astype(BF16)
    q = jnp.dot(h, wb_ref[:, 0:A_QK], preferred_element_type=F32)
    q_ref[...] = (q * (A_DK ** -0.5 * LOG2E)).astype(BF16)
    k = jnp.dot(h, wb_ref[:, A_QK:2 * A_QK], preferred_element_type=F32)
    k_ref[...] = k
    kb_ref[...] = k.astype(BF16)
    v = jnp.dot(h, wb_ref[:, 2 * A_QK:], preferred_element_type=F32)
    v_ref[...] = v
    vb_ref[...] = v.astype(BF16)


def _proj_a_sample(x, g, w, layer):
    m = x.shape[0]
    n_out = 2 * A_QK + A_V
    full = lambda i: (0, 0)
    full3 = lambda i: (0, 0, 0)
    return pl.pallas_call(
        _proj_a_sample_body,
        grid=(1,),
        in_specs=[pl.BlockSpec((m, D_MODEL), full),
                  pl.BlockSpec((1, D_MODEL), full),
                  pl.BlockSpec((None, D_MODEL, n_out), lambda i: (layer, 0, 0))],
        out_specs=[pl.BlockSpec((m, A_QK), full),
                   pl.BlockSpec((m, A_QK), full),
                   pl.BlockSpec((m, A_V), full),
                   pl.BlockSpec((m, A_QK), full),
                   pl.BlockSpec((m, A_V), full),
                   pl.BlockSpec((None, D_MODEL, n_out), full3),
                   pl.BlockSpec((None, A_QK, D_MODEL), full3)],
        out_shape=[jax.ShapeDtypeStruct((m, A_QK), BF16),
                   jax.ShapeDtypeStruct((m, A_QK), F32),
                   jax.ShapeDtypeStruct((m, A_V), F32),
                   jax.ShapeDtypeStruct((m, A_QK), BF16),
                   jax.ShapeDtypeStruct((m, A_V), BF16),
                   jax.ShapeDtypeStruct((1, D_MODEL, n_out), BF16),
                   jax.ShapeDtypeStruct((1, A_QK, D_MODEL), BF16)],
        compiler_params=_params(1),
        name="proj_a_sample",
    )(x, g, w)


def _proj_b_body(x_ref, g_ref, w_ref, q_ref, k_ref, v_ref, kb_ref, vb_ref, *wb_ref, tail):
    if wb_ref:
        wb_ref[0][...] = w_ref[...].astype(BF16)
        w_ref = wb_ref[0]
    h = _rms(x_ref[...], g_ref[...]).astype(BF16)
    kv = jnp.dot(h, w_ref[:, B_Q:], preferred_element_type=F32)
    k, v = kv[:, 0:B_KV], kv[:, B_KV:]
    kb_ref[...] = jnp.concatenate([k, pltpu.roll(k, HALF, axis=1)], axis=1).astype(BF16)
    vb_ref[...] = jnp.concatenate([v, pltpu.roll(v, HALF, axis=1)], axis=1).astype(BF16)
    keep = k.shape[0] if tail is None else tail
    k_ref[...] = k[k.shape[0] - keep:]
    v_ref[...] = v[v.shape[0] - keep:]
    q = jnp.dot(h, w_ref[:, 0:B_Q], preferred_element_type=F32)
    q_ref[...] = (q * (B_HD ** -0.5 * LOG2E)).astype(BF16)


def _proj_b(x, g, w, layer, tm, *, seq=None, tail=None, cast_weights=False):
    m = x.shape[0]
    assert not cast_weights or m == tm
    n_out = B_Q + 2 * B_KV
    row = lambda i: (i, 0)
    full = lambda i: (0, 0)
    wb_spec = [pl.BlockSpec((None, D_MODEL, n_out), lambda i: (0, 0, 0))] if cast_weights else []
    wb_shape = [jax.ShapeDtypeStruct((1, D_MODEL, n_out), BF16)] if cast_weights else []
    if tail is None:
        kv_spec = pl.BlockSpec((tm, B_KV), row)
        kv_shape = jax.ShapeDtypeStruct((m, B_KV), F32)
    else:
        assert seq % tm == 0 and tail <= tm
        per_seq = seq // tm
        kv_spec = pl.BlockSpec((None, tail, B_KV), lambda i: (i // per_seq, 0, 0))
        kv_shape = jax.ShapeDtypeStruct((m // seq, tail, B_KV), F32)
    return pl.pallas_call(
        functools.partial(_proj_b_body, tail=tail),
        grid=(m // tm,),
        in_specs=[pl.BlockSpec((tm, D_MODEL), row),
                  pl.BlockSpec((1, D_MODEL), full),
                  pl.BlockSpec((None, D_MODEL, B_Q + 2 * B_KV), lambda i: (layer, 0, 0))],
        out_specs=[pl.BlockSpec((tm, B_Q), row),
                   kv_spec,
                   kv_spec,
                   pl.BlockSpec((tm, 2 * B_KV), row),
                   pl.BlockSpec((tm, 2 * B_KV), row)] + wb_spec,
        out_shape=[jax.ShapeDtypeStruct((m, B_Q), BF16),
                   kv_shape,
                   kv_shape,
                   jax.ShapeDtypeStruct((m, 2 * B_KV), BF16),
                   jax.ShapeDtypeStruct((m, 2 * B_KV), BF16)] + wb_shape,
        compiler_params=_params(1),
        name="proj_b",
    )(x, g, w)


def _t5_bucket(rel):
    nb = NUM_BUCKETS // 2
    n = -rel
    ret = jnp.where(n < 0, nb, 0)
    n = jnp.abs(n)
    max_exact = nb // 2
    nf = jnp.maximum(n, 1).astype(jnp.float32)
    large = max_exact + (jnp.log(nf / max_exact) / math.log(MAX_DIST / max_exact)
                         * (nb - max_exact)).astype(jnp.int32)
    large = jnp.minimum(large, nb - 1)
    return ret + jnp.where(n < max_exact, n, large)


def _check_far_bucket(min_dist):
    nb, max_exact = NUM_BUCKETS // 2, NUM_BUCKETS // 4
    ratio = math.log(min_dist / max_exact) / math.log(MAX_DIST / max_exact)
    assert ratio * (nb - max_exact) >= nb - 1 - max_exact + 1e-3, "far keys need one shared bucket"


def _far_bias(table, tq, w):
    far = table[NUM_BUCKETS // 2 - 1].astype(F32) * LOG2E
    return jnp.broadcast_to(far[:, None, None], (N_MAPS, tq, w))


def _near_bias(table, tq, w, back, band):
    r = jnp.arange(tq)[:, None]
    j = jnp.arange(w)[None, :] - back
    onehot = (_t5_bucket(j - r)[None] == jnp.arange(NUM_BUCKETS)[:, None, None]).astype(F32)
    onehot = lax.optimization_barrier(onehot)
    bias = jnp.einsum("bm,brj->mrj", table.astype(F32), onehot, precision=lax.Precision.HIGHEST)
    kc, qc = j // CHUNK, r // CHUNK
    mask = kc <= qc
    if band is not None:
        mask = mask & (kc >= qc - band)
    return jnp.where(mask[None], bias * LOG2E, NEG)


A_TILE = 256
A_ROWS = 64
A_HEADS_PER_STEP = 1


def _diff_lambda(lamp_ref, lam_init):
    lp = lamp_ref[...]
    return (jnp.exp(jnp.sum(lp[0:1] * lp[1:2], axis=-1, keepdims=True))
            - jnp.exp(jnp.sum(lp[2:3] * lp[3:4], axis=-1, keepdims=True)) + lam_init)


def _stack_maps(q):
    lane = lax.broadcasted_iota(jnp.int32, q.shape, 1)
    zero = jnp.zeros_like(q)
    return jnp.concatenate([jnp.where(lane < HALF, q, zero), jnp.where(lane >= HALF, q, zero)], axis=0)


def _diff_epilogue(acc, lam, g, lam_init):
    tq = acc.shape[0] // 2
    o_all = acc[:, :A_DV] / acc[:, A_DV:]
    o = o_all[:tq] - lam * o_all[tq:]
    return (_rms(o, g) * (1.0 - lam_init)).astype(BF16)


def _attn_a_scores(i, q, kt_ref, bias_ref, s_ref, m_ref):
    tl = A_TILE
    qs = _stack_maps(q)
    n_far = max(i - 1, 0)
    for kt in range(i + 1):
        cols = slice(kt * tl, (kt + 1) * tl)
        s = jnp.dot(qs, kt_ref[:, cols], preferred_element_type=F32)
        near = kt >= n_far
        if near:
            s = s + bias_ref[2 - (i - kt)].reshape(2 * tl, tl)
        s_ref[:, cols] = s
        fold = _lane_fold_max(s)
        first = kt == (n_far if near else 0)
        m_ref[int(near)] = fold if first else jnp.maximum(m_ref[int(near)], fold)
        yield
    for r in range(0, 2 * tl, A_ROWS):
        rows = slice(r, r + A_ROWS)
        mx = m_ref[1, rows, :]
        if n_far:
            c = bias_ref[0, r // tl, r % tl:r % tl + A_ROWS, 0:LANES]
            mx = jnp.maximum(mx, m_ref[0, rows, :] + c)
        m = jnp.broadcast_to(jnp.max(mx, axis=-1, keepdims=True), (A_ROWS, LANES))
        m_ref[1, rows, :] = m
        if n_far:
            m_ref[0, rows, :] = m - c


def _attn_a_numerators(i, s_ref, m_ref, v1_ref, out):
    tl = A_TILE
    n_far = max(i - 1, 0)
    acc = None
    for kt in range(i + 1):
        cols = slice(kt * tl, (kt + 1) * tl)
        p = _exp_rows(s_ref[:, cols], m_ref[1 if kt >= n_far else 0])
        part = jnp.dot(p, v1_ref[cols, :], preferred_element_type=F32)
        acc = part if acc is None else acc + part
        yield
    out.append(acc)


def _interleave(*steps):
    steps = list(steps)
    while steps:
        for gen in list(steps):
            if next(gen, StopIteration) is StopIteration:
                steps.remove(gen)


def _attn_a_prompt_body(lamp_ref, q_ref, kt_ref, v_ref, bias_ref, g_ref, o_ref,
                        s_ref, m_ref, v1_ref, ktb_ref, *, nt, nh, lam_init):
    tl = A_TILE
    lam = _diff_lambda(lamp_ref, lam_init)
    ktb_ref[...] = kt_ref[...].astype(BF16)
    for hh in range(nh):
        v1_ref[hh, :, 0:A_DV] = v_ref[0, :, hh * A_DV:(hh + 1) * A_DV]
        v1_ref[hh, :, A_DV:] = jnp.ones((nt * tl, A_DV), BF16)
    work = [(i, hh) for i in reversed(range(nt)) for hh in range(nh)]

    def scores(i, hh):
        lanes = slice(hh * LANES, (hh + 1) * LANES)
        return _attn_a_scores(i, q_ref[0, i * tl:(i + 1) * tl, lanes], ktb_ref.at[lanes, :],
                              bias_ref.at[:, 2 * hh:2 * hh + 2], s_ref.at[hh, i % 2],
                              m_ref.at[hh, i % 2])

    _interleave(scores(*work[0]))
    for n, (i, hh) in enumerate(work):
        out = []
        steps = [_attn_a_numerators(i, s_ref.at[hh, i % 2], m_ref.at[hh, i % 2], v1_ref.at[hh], out)]
        if n + 1 < len(work):
            steps.insert(0, scores(*work[n + 1]))
        _interleave(*steps)
        acc = out[0]
        o_ref[0, i * tl:(i + 1) * tl, hh * LANES:(hh + 1) * LANES] = _diff_epilogue(
            acc, lam, g_ref[...], lam_init)


def _attn_a_prompt(q, kt, v, bias, lamp, g, layer, *, lam_init):
    b, s, _ = q.shape
    nt = s // A_TILE
    nh = A_HEADS_PER_STEP
    assert s % A_TILE == 0 and A_HEADS % nh == 0
    seq = pl.BlockSpec((1, s, nh * LANES), lambda h, bi: (bi, 0, h))
    return pl.pallas_call(
        functools.partial(_attn_a_prompt_body, nt=nt, nh=nh, lam_init=lam_init),
        grid=(A_HEADS // nh, b),
        in_specs=[pl.BlockSpec((None, 4, A_DK), lambda h, bi: (layer, 0, 0)),
                  seq,
                  pl.BlockSpec((None, None, nh * LANES, s), lambda h, bi: (layer, bi, h, 0)),
                  seq,
                  pl.BlockSpec((3, 2 * nh, A_TILE, A_TILE), lambda h, bi: (0, h, 0, 0)),
                  pl.BlockSpec((None, 1, A_DV), lambda h, bi: (layer, 0, 0))],
        out_specs=seq,
        out_shape=jax.ShapeDtypeStruct((b, s, A_V), BF16),
        scratch_shapes=[pltpu.VMEM((nh, 2, 2 * A_TILE, s), F32),
                        pltpu.VMEM((nh, 2, 2, 2 * A_TILE, LANES), F32),
                        pltpu.VMEM((nh, s, 2 * A_DV), BF16),
                        pltpu.VMEM((nh * LANES, s), BF16)],
        compiler_params=_params(2),
        name="attn_a_prompt",
    )(lamp, q, kt, v, bias, g)


def _attn_a_sample_body(lamp_ref, q_ref, ktc_ref, vc_ref, kn_ref, vn_ref, bias_ref, g_ref, o_ref,
                        *, lam_init):
    tq = q_ref.shape[1]
    past = ktc_ref.shape[1]
    lam = _diff_lambda(lamp_ref, lam_init)
    for h in range(A_HEADS):
        cols = slice(h * LANES, (h + 1) * LANES)
        qs = _stack_maps(q_ref[0, :, cols])
        bias = bias_ref[2 * h:2 * h + 2].reshape(2 * tq, past + LANES)
        s_c = jnp.dot(qs, ktc_ref[cols, :].astype(BF16), preferred_element_type=F32)
        s_c = s_c + bias[:, 0:past]
        s_n = lax.dot_general(qs, kn_ref[0, :, cols], NT_DIMS, preferred_element_type=F32)
        s_n = s_n + bias[:, past:past + tq]
        m = jnp.maximum(_row_max(s_c), jnp.broadcast_to(jnp.max(s_n, axis=-1, keepdims=True),
                                                        (2 * tq, LANES)))
        p_c = _exp_rows(s_c, m)
        p_n = jnp.exp2(s_n - m[:, 0:tq]).astype(BF16)
        v_c = vc_ref[pl.ds(h, past, stride=A_HEADS), :].astype(BF16)
        acc = (jnp.dot(p_c, _with_ones(v_c), preferred_element_type=F32)
               + jnp.dot(p_n, _with_ones(vn_ref[0, :, cols]), preferred_element_type=F32))
        o_ref[0, :, cols] = _diff_epilogue(acc, lam, g_ref[...], lam_init)


def _attn_a_sample(q, ktc, vc, kn, vn, bias, lamp, g, layer, *, lam_init):
    b, tq, _ = q.shape
    past = ktc.shape[3]
    new = lambda bi: (bi, 0, 0)
    return pl.pallas_call(
        functools.partial(_attn_a_sample_body, lam_init=lam_init),
        grid=(b,),
        in_specs=[pl.BlockSpec((None, 4, A_DK), lambda bi: (layer, 0, 0)),
                  pl.BlockSpec((1, tq, A_QK), new),
                  pl.BlockSpec((None, None, A_QK, past), lambda bi: (layer, bi, 0, 0)),
                  pl.BlockSpec((None, None, past * A_HEADS, A_DV), lambda bi: (layer, bi, 0, 0)),
                  pl.BlockSpec((1, tq, A_QK), new),
                  pl.BlockSpec((1, tq, A_V), new),
                  pl.BlockSpec((N_MAPS, tq, past + LANES), lambda bi: (0, 0, 0)),
                  pl.BlockSpec((None, 1, A_DV), lambda bi: (layer, 0, 0))],
        out_specs=pl.BlockSpec((1, tq, A_V), new),
        out_shape=jax.ShapeDtypeStruct((b, tq, A_V), BF16),
        compiler_params=_params(1),
        name="attn_a_sample",
    )(lamp, q, ktc, vc, kn, vn, bias, g)


_B_ORDER_A = (0, 2, 4, 6, 9, 11, 13, 15)
_B_ORDER_B = (1, 3, 5, 7, 8, 10, 12, 14)


def _attn_b_body(sink_ref, q_ref, k_ref, v_ref, bias_ref, o_ref, *, layer, tq, w, back, i_off, nsub):
    lane = lax.broadcasted_iota(jnp.int32, (tq, LANES), 1)
    lo = lane < HALF
    half = B_Q_HEADS // 4
    orders = (_B_ORDER_A, _B_ORDER_B)
    sinks = [jnp.concatenate([jnp.full((tq, LANES), sink_ref[layer, h] * LOG2E, F32)
                              for h in orders[t]], axis=0) for t in range(2)]
    outs = {}

    def chain(u, t):
        tile = pl.program_id(1) * nsub + u + i_off
        variant = 1 if (u > 0 or i_off > 0) else jnp.minimum(tile, 1)
        near_start = pl.multiple_of(jnp.maximum(tile * tq - back, 0), CHUNK)
        group = slice(t * LANES, (t + 1) * LANES)
        pieces = []
        for c in range(B_Q_HEADS // 2):
            qc = q_ref[0, u * tq:(u + 1) * tq, c * LANES:(c + 1) * LANES]
            keep_lo = (c < half) == (t == 0)
            pieces.append(jnp.where(lo == keep_lo, qc, jnp.zeros_like(qc)))
        qs = jnp.concatenate(pieces, axis=0)
        s = lax.dot_general(qs, k_ref[0, pl.ds(near_start, w), group], NT_DIMS,
                            preferred_element_type=F32)
        s = s + bias_ref[variant, t].reshape((B_Q_HEADS // 2) * tq, w)
        m = jnp.maximum(_row_max(s), sinks[t])
        p = _exp_rows(s, m)
        acc = jnp.dot(p, _with_ones(v_ref[0, pl.ds(near_start, w), group]),
                      preferred_element_type=F32)
        outs[u, t] = acc[:, :LANES] / (acc[:, LANES:] + jnp.exp2(sinks[t] - m))

    for u in range(nsub):
        for t in range(2):
            chain(u, t)

    for u in range(nsub):
        for c in range(B_Q_HEADS // 2):
            oa = outs[u, 0][c * tq:(c + 1) * tq]
            ob = outs[u, 1][c * tq:(c + 1) * tq]
            oc = jnp.where(lo, oa, ob) if c < half else jnp.where(lo, ob, oa)
            o_ref[0, u * tq:(u + 1) * tq, c * LANES:(c + 1) * LANES] = oc.astype(BF16)


def _attn_b(q, k, v, bias, sinks, layer, *, tq, w, back, i_off, nsub):
    b, sq, _ = q.shape
    sk = k.shape[1]
    rows = nsub * tq
    assert sq % rows == 0 and (sq // tq - 1 + i_off) * tq - back + w <= sk
    body = functools.partial(_attn_b_body, layer=layer, tq=tq, w=w, back=back, i_off=i_off,
                             nsub=nsub)
    return pl.pallas_call(
        body,
        grid=(b, sq // rows),
        in_specs=[pl.BlockSpec(memory_space=pltpu.SMEM),
                  pl.BlockSpec((1, rows, B_Q), lambda bi, i: (bi, i, 0)),
                  pl.BlockSpec((1, sk, 2 * B_KV), lambda bi, i: (bi, 0, 0)),
                  pl.BlockSpec((1, sk, 2 * B_KV), lambda bi, i: (bi, 0, 0)),
                  pl.BlockSpec((2, 2, B_Q_HEADS // 2, tq, w), lambda bi, i: (0, 0, 0, 0, 0))],
        out_specs=pl.BlockSpec((1, rows, B_Q), lambda bi, i: (bi, i, 0)),
        out_shape=jax.ShapeDtypeStruct((b, sq, B_Q), BF16),
        compiler_params=_params(2),
        name="attn_b",
    )(sinks, q, k, v, bias)


def _post_body(x_ref, o_ref, wo_ref, g_ref, wu_ref, wd_ref, gf_ref, y_ref, *refs, final, cast_weights):
    hn_ref = refs[-1]
    f = pl.program_id(1)
    if cast_weights:
        wob_ref, wub_ref, wdb_ref = refs[:3]

    @pl.when(f == 0)
    def _():
        if cast_weights:
            wob_ref[...] = wo_ref[...].astype(BF16)
        wo = (wob_ref if cast_weights else wo_ref)[...]
        x1 = x_ref[...] + jnp.dot(o_ref[...], wo, preferred_element_type=F32)
        y_ref[...] = x1
        hn_ref[...] = _rms(x1, g_ref[...]).astype(BF16)

    if cast_weights:
        wub_ref[...] = wu_ref[...].astype(BF16)
        wdb_ref[...] = wd_ref[...].astype(BF16)
        wu_ref, wd_ref = wub_ref, wdb_ref
    u = jnp.dot(hn_ref[...], wu_ref[...], preferred_element_type=F32)
    a = jnp.square(jnp.maximum(u, 0.0)).astype(BF16)
    y_ref[...] += jnp.dot(a, wd_ref[...], preferred_element_type=F32)

    if final:
        @pl.when(f == pl.num_programs(1) - 1)
        def _():
            y_ref[...] = _rms(y_ref[...], gf_ref[...])


def _post(x, o, wo, wo_layer, g, g_layer, wu, wd, w_layer, gf, *, tm, tf, final, cast_weights):
    m = x.shape[0]
    assert not cast_weights or m == tm
    row = lambda i, f: (i, 0)
    full = lambda i, f: (0, 0)
    out_specs = [pl.BlockSpec((tm, D_MODEL), row)]
    out_shape = [jax.ShapeDtypeStruct((m, D_MODEL), F32)]
    if cast_weights:
        out_specs += [pl.BlockSpec((None, D_MODEL, D_MODEL), lambda i, f: (0, 0, 0)),
                      pl.BlockSpec((None, D_MODEL, tf), lambda i, f: (0, 0, f)),
                      pl.BlockSpec((None, tf, D_MODEL), lambda i, f: (0, f, 0))]
        out_shape += [jax.ShapeDtypeStruct((1, D_MODEL, D_MODEL), BF16),
                      jax.ShapeDtypeStruct((1, D_MODEL, D_FF), BF16),
                      jax.ShapeDtypeStruct((1, D_FF, D_MODEL), BF16)]
    outs = pl.pallas_call(
        functools.partial(_post_body, final=final, cast_weights=cast_weights),
        grid=(m // tm, D_FF // tf),
        in_specs=[pl.BlockSpec((tm, D_MODEL), row),
                  pl.BlockSpec((tm, D_MODEL), row),
                  pl.BlockSpec((None, D_MODEL, D_MODEL), lambda i, f: (wo_layer, 0, 0)),
                  pl.BlockSpec((None, 1, D_MODEL), lambda i, f: (g_layer, 0, 0)),
                  pl.BlockSpec((None, D_MODEL, tf), lambda i, f: (w_layer, 0, f)),
                  pl.BlockSpec((None, tf, D_MODEL), lambda i, f: (w_layer, f, 0)),
                  pl.BlockSpec((1, D_MODEL), full)],
        out_specs=out_specs,
        out_shape=out_shape,
        scratch_shapes=[pltpu.VMEM((tm, D_MODEL), BF16)],
        compiler_params=_params(2),
        name="post",
    )(x, o, wo, g, wu, wd, gf)
    return outs if cast_weights else outs[0]


def _post_resident_body(x_ref, o_ref, wo_ref, g_ref, wu_ref, wd_ref, gf_ref, y_ref, hn_ref,
                        *, tf, final):
    x1 = x_ref[...] + jnp.dot(o_ref[...], wo_ref[...], preferred_element_type=F32)
    y_ref[...] = x1
    hn_ref[...] = _rms(x1, g_ref[...]).astype(BF16)
    mlp = None
    for f in range(D_FF // tf):
        u = jnp.dot(hn_ref[...], wu_ref[:, f * tf:(f + 1) * tf], preferred_element_type=F32)
        a = jnp.square(jnp.maximum(u, 0.0)).astype(BF16)
        part = jnp.dot(a, wd_ref[f * tf:(f + 1) * tf, :], preferred_element_type=F32)
        mlp = part if mlp is None else mlp + part
    y = y_ref[...] + mlp
    y_ref[...] = _rms(y, gf_ref[...]) if final else y


def _post_resident(x, o, wo, g, g_layer, wu, wd, gf, *, tm, tf, final):
    m = x.shape[0]
    row = lambda i: (i, 0)
    once = dict(pipeline_mode=pl.Buffered(1))
    return pl.pallas_call(
        functools.partial(_post_resident_body, tf=tf, final=final),
        grid=(m // tm,),
        in_specs=[pl.BlockSpec((tm, D_MODEL), row),
                  pl.BlockSpec((tm, D_MODEL), row),
                  pl.BlockSpec((None, D_MODEL, D_MODEL), lambda i: (0, 0, 0), **once),
                  pl.BlockSpec((None, 1, D_MODEL), lambda i: (g_layer, 0, 0)),
                  pl.BlockSpec((None, D_MODEL, D_FF), lambda i: (0, 0, 0), **once),
                  pl.BlockSpec((None, D_FF, D_MODEL), lambda i: (0, 0, 0), **once),
                  pl.BlockSpec((1, D_MODEL), lambda i: (0, 0))],
        out_specs=pl.BlockSpec((tm, D_MODEL), row),
        out_shape=jax.ShapeDtypeStruct((m, D_MODEL), F32),
        scratch_shapes=[pltpu.VMEM((tm, D_MODEL), BF16)],
        compiler_params=_params(1),
        name="post_resident",
    )(x, o, wo, g, wu, wd, gf)


_B_PROMPT = dict(tq=128, w=256, back=128)
_B_SAMPLE = dict(tq=64, w=256, back=128)
B_TILES_PER_STEP = 16
_A_SAMPLE_BACK = 128


def _a_prompt_bias(table):
    _check_far_bucket(A_TILE + 1)
    near = _near_bias(table, A_TILE, 2 * A_TILE, A_TILE, None)
    return jnp.stack([_far_bias(table, A_TILE, A_TILE), near[:, :, :A_TILE], near[:, :, A_TILE:]])


def _a_sample_bias(table, past, t):
    assert t <= LANES
    _check_far_bucket(_A_SAMPLE_BACK + 1)
    far = _far_bias(table, t, past - _A_SAMPLE_BACK)
    near = _near_bias(table, t, _A_SAMPLE_BACK + LANES, _A_SAMPLE_BACK, None)
    return jnp.concatenate([far, near], axis=2)


def _b_bias(table, tq, w, back):
    def one(bk):
        t = _near_bias(table, tq, w, bk, WINDOW // CHUNK)
        return jnp.stack([jnp.stack([t[h] for h in order]) for order in (_B_ORDER_A, _B_ORDER_B)])
    return jnp.stack([one(0), one(back)])


def _pad_keys(t, total):
    return jnp.pad(t, ((0, 0), (0, total - t.shape[1]), (0, 0)))


def _trunk(x, caches, wts, table, bf16_weights, *, tm, tm_post, tf):
    b, s, _ = x.shape
    sample = caches is not None
    n_a = wts["a_w_qkv"].shape[0]
    xm = x.reshape(b * s, D_MODEL)
    b_cfg = _B_SAMPLE if sample else _B_PROMPT
    b_bias = _b_bias(table, **b_cfg)
    if sample:
        a_bias = _a_sample_bias(table, caches["a_kt"].shape[3], s)
    else:
        a_bias = _a_prompt_bias(table)
    ak, av, bk, bv = [], [], [], []
    kt_buf = v_buf = None
    for i in range(DEPTH):
        j = i // 2
        g_mix = wts["norm_mix_g"][i][None, :]
        if sample:
            bf16_weights.append({})
        cast = bf16_weights[i]
        if i % 2 == 0:
            lam_init = _lambda_init(i)
            if sample:
                q, k, v, kb, vb, cast["qkv"], cast["kt"] = _proj_a_sample(xm, g_mix, wts["a_w_qkv"], j)
                ak.append(k.reshape(b, s, A_HEADS, 2, A_DK))
                av.append(v.reshape(b, s, A_HEADS, A_DV))
                o = _attn_a_sample(q.reshape(b, s, A_QK), caches["a_kt"], caches["a_v"],
                                   kb.reshape(b, s, A_QK), vb.reshape(b, s, A_V), a_bias,
                                   wts["a_lambda"], wts["a_subln_g"], j, lam_init=lam_init)
            else:
                q, kt_buf, v_buf, vb = _proj_a_prompt(
                    xm, g_mix, cast["qkv"], cast["kt"], 0, j, kt_buf, v_buf,
                    b=b, s=s, tm=tm if kt_buf is None else tm_post, n_a=n_a)
                o = _attn_a_prompt(q.reshape(b, s, A_QK), kt_buf, vb.reshape(b, s, A_V), a_bias,
                                   wts["a_lambda"], wts["a_subln_g"], j, lam_init=lam_init)
            wo, wo_layer = wts["a_w_o"], j
        else:
            if sample:
                q, k, v, kb, vb, cast["qkv"] = _proj_b(xm, g_mix, wts["b_w_qkv"], j, tm,
                                                       cast_weights=True)
            else:
                q, k, v, kb, vb = _proj_b(xm, g_mix, cast["qkv"], 0, tm_post, seq=s, tail=WINDOW)
            k = k.reshape(b, -1, B_KV_HEADS, B_HD)
            v = v.reshape(b, -1, B_KV_HEADS, B_HD)
            kb = kb.reshape(b, s, 2 * B_KV)
            vb = vb.reshape(b, s, 2 * B_KV)
            i_off = 0
            if sample:
                ck, cv, ckb, cvb = caches["b"][j]
                bk.append(jnp.concatenate([ck, k], axis=1)[:, s:])
                bv.append(jnp.concatenate([cv, v], axis=1)[:, s:])
                i_off = ckb.shape[1] // b_cfg["tq"]
                total = ckb.shape[1] - b_cfg["back"] + b_cfg["w"]
                kb = _pad_keys(jnp.concatenate([ckb, kb], axis=1), total)
                vb = _pad_keys(jnp.concatenate([cvb, vb], axis=1), total)
            else:
                bk.append(k)
                bv.append(v)
            o = _attn_b(q.reshape(b, s, B_Q), kb, vb, b_bias, wts["b_sinks"], j, i_off=i_off,
                        nsub=1 if sample else B_TILES_PER_STEP, **b_cfg)
            wo, wo_layer = wts["b_w_o"], j
        gf = wts["final_norm_g"][None, :]
        tiling = dict(tm=tm_post, tf=tf, final=(i == DEPTH - 1))
        if sample:
            xm, cast["wo"], cast["wu"], cast["wd"] = _post(
                xm, o.reshape(b * s, D_MODEL), wo, wo_layer, wts["norm_mlp_g"], i,
                wts["mlp_w_up"], wts["mlp_w_down"], i, gf, cast_weights=True, **tiling)
        else:
            xm = _post_resident(xm, o.reshape(b * s, D_MODEL), cast["wo"], wts["norm_mlp_g"], i,
                                cast["wu"], cast["wd"], gf, **tiling)
    if sample:
        a_k, a_v = jnp.stack(ak), jnp.stack(av)
    else:
        a_k = jnp.transpose(kt_buf.reshape(n_a, b, A_HEADS, 2, A_DK, s), (0, 1, 5, 2, 3, 4))
        a_v = v_buf.reshape(n_a, b, s, A_HEADS, A_DV)
    return xm.reshape(b, s, D_MODEL), a_k, a_v, jnp.stack(bk), jnp.stack(bv)


def _swap_halves(t):
    flat = t.reshape(t.shape[:-2] + (B_KV,))
    swapped = t[..., ::-1, :].reshape(t.shape[:-2] + (B_KV,))
    return jnp.concatenate([flat, swapped], axis=-1).astype(BF16)


def kernel(x_prompt, x_sample, cache_a_k, cache_a_v, cache_b_k, cache_b_v, rel_table,
           norm_mix_g, norm_mlp_g, final_norm_g, a_w_qkv, a_lambda, a_subln_g, a_w_o,
           b_w_qkv, b_sinks, b_w_o, mlp_w_up, mlp_w_down):
    wts = dict(norm_mix_g=norm_mix_g, norm_mlp_g=norm_mlp_g[:, None, :], final_norm_g=final_norm_g,
               a_w_qkv=a_w_qkv, a_lambda=a_lambda, a_subln_g=a_subln_g[:, None, :],
               a_w_o=a_w_o, b_w_qkv=b_w_qkv, b_sinks=b_sinks,
               b_w_o=b_w_o, mlp_w_up=mlp_w_up, mlp_w_down=mlp_w_down)
    n_a, db, past = cache_a_k.shape[:3]
    caches = dict(
        a_kt=jnp.transpose(cache_a_k, (0, 1, 3, 4, 5, 2)).reshape(n_a, db, A_QK, past),
        a_v=cache_a_v.reshape(n_a, db, past * A_HEADS, A_DV),
        b=[(cache_b_k[j], cache_b_v[j], _swap_halves(cache_b_k[j]), _swap_halves(cache_b_v[j]))
           for j in range(cache_b_k.shape[0])])
    bf16_weights = []
    ys, aks, avs, bks, bvs = _trunk(x_sample, caches, wts, rel_table, bf16_weights,
                                    tm=512, tm_post=512, tf=512)
    yp, akp, avp, bkp, bvp = _trunk(x_prompt, None, wts, rel_table, bf16_weights,
                                    tm=512, tm_post=1024, tf=1024)
    return (yp, ys, akp, avp, bkp, bvp, aks, avs, bks, bvs)
```

```python
import functools
import math

import jax
import jax.numpy as jnp
from jax import lax
from jax.experimental import pallas as pl
from jax.experimental.pallas import tpu as pltpu

BF16 = jnp.bfloat16
F32 = jnp.float32

D_MODEL = 1024
DEPTH = 4
CHUNK = 64
A_HEADS = 8
A_DK = 64
A_DV = 128
A_QK = A_HEADS * 2 * A_DK
A_V = A_HEADS * A_DV
B_Q_HEADS = 16
B_KV_HEADS = 2
B_HD = 64
B_Q = B_Q_HEADS * B_HD
B_KV = B_KV_HEADS * B_HD
WINDOW = 128
NUM_BUCKETS = 32
MAX_DIST = 128
N_MAPS = 16
D_FF = 4 * D_MODEL
EPS = 1e-6
NEG = -1e30

LANES = 128
HALF = LANES // 2
VMEM_LIMIT = 56 * 1024 * 1024

NT_DIMS = (((1,), (1,)), ((), ()))

LOG2E = math.log2(math.e)


def _lambda_init(layer):
    return 0.8 - 0.6 * math.exp(-0.3 * layer)


def _rms(x, g):
    return x * lax.rsqrt(jnp.mean(x * x, axis=-1, keepdims=True) + EPS) * g


def _params(n_axes):
    return pltpu.CompilerParams(
        dimension_semantics=("arbitrary",) * n_axes, vmem_limit_bytes=VMEM_LIMIT)


def _lane_fold_max(s):
    mx = s[:, 0:LANES]
    for gi in range(1, s.shape[1] // LANES):
        mx = jnp.maximum(mx, s[:, gi * LANES:(gi + 1) * LANES])
    return mx


def _row_max(s):
    return jnp.broadcast_to(jnp.max(_lane_fold_max(s), axis=-1, keepdims=True), (s.shape[0], LANES))


def _exp_rows(s, m):
    return jnp.concatenate([jnp.exp2(s[:, gi * LANES:(gi + 1) * LANES] - m)
                            for gi in range(s.shape[1] // LANES)], axis=1).astype(BF16)


def _with_ones(v):
    return jnp.concatenate([v, jnp.ones(v.shape, v.dtype)], axis=1)


def _proj_a_prompt_body(x_ref, g_ref, w_ref, wkt_ref, *refs, first):
    q_ref, kt_ref, v_ref, vb_ref = refs[-4:]
    h = _rms(x_ref[...], g_ref[...]).astype(BF16)
    tm = h.shape[0]
    if first:
        for slot in range(1, kt_ref.shape[0]):
            kt_ref[slot] = jnp.zeros(kt_ref.shape[1:], F32)
            v_ref[slot] = jnp.zeros(v_ref.shape[1:], F32)
        kt_ref, v_ref = kt_ref.at[0], v_ref.at[0]
    v = jnp.dot(h, w_ref[:, 2 * A_QK:], preferred_element_type=F32)
    vb_ref[...] = v.astype(BF16)
    for hh in range(A_HEADS):
        v_ref[pl.ds(hh, tm, stride=A_HEADS), :] = v[:, hh * A_DV:(hh + 1) * A_DV]
    kt_ref[...] = lax.dot_general(wkt_ref[...], h, NT_DIMS, preferred_element_type=F32)
    q = jnp.dot(h, w_ref[:, 0:A_QK], preferred_element_type=F32)
    q_ref[...] = (q * (A_DK ** -0.5 * LOG2E)).astype(BF16)


def _proj_a_prompt(x, g, w, wkt, w_layer, layer, kt_prev, v_prev, *, b, s, tm, n_a):
    m = b * s
    per_b = s // tm
    row = lambda i: (i, 0)
    full = lambda i: (0, 0)
    stacked = lambda i: (w_layer, 0, 0)
    first = kt_prev is None
    assert first == (layer == 0)
    in_specs = [pl.BlockSpec((tm, D_MODEL), row),
                pl.BlockSpec((1, D_MODEL), full),
                pl.BlockSpec((None, D_MODEL, 2 * A_QK + A_V), stacked),
                pl.BlockSpec((None, A_QK, D_MODEL), stacked)]
    operands = [x, g, w, wkt]
    if first:
        kt_spec = pl.BlockSpec((n_a, None, A_QK, tm), lambda i: (0, i // per_b, 0, i % per_b))
        v_spec = pl.BlockSpec((n_a, tm * A_HEADS, A_DV), lambda i: (0, i, 0))
        aliases = {}
    else:
        kt_spec = pl.BlockSpec((None, None, A_QK, tm), lambda i: (layer, i // per_b, 0, i % per_b))
        v_spec = pl.BlockSpec((None, tm * A_HEADS, A_DV), lambda i: (layer, i, 0))
        in_specs += [pl.BlockSpec(memory_space=pl.ANY)] * 2
        operands += [kt_prev, v_prev]
        aliases = {4: 1, 5: 2}
    return pl.pallas_call(
        functools.partial(_proj_a_prompt_body, first=first),
        grid=(m // tm,),
        in_specs=in_specs,
        out_specs=[pl.BlockSpec((tm, A_QK), row),
                   kt_spec,
                   v_spec,
                   pl.BlockSpec((tm, A_V), row)],
        out_shape=[jax.ShapeDtypeStruct((m, A_QK), BF16),
                   jax.ShapeDtypeStruct((n_a, b, A_QK, s), F32),
                   jax.ShapeDtypeStruct((n_a, m * A_HEADS, A_DV), F32),
                   jax.ShapeDtypeStruct((m, A_V), BF16)],
        input_output_aliases=aliases,
        compiler_params=_params(1),
        name="proj_a_prompt",
    )(*operands)


def _proj_a_sample_body(x_ref, g_ref, w_ref, q_ref, k_ref, v_ref, kb_ref, vb_ref, wb_ref, wkt_ref):
    wb_ref[...] = w_ref[...].astype(BF16)
    wkt_ref[...] = w_ref[:, A_QK:2 * A_QK].T.astype(BF16)
    h = _rms(x_ref[...], g_ref[...]).astype(BF16)
    q = jnp.dot(h, wb_ref[:, 0:A_QK], preferred_element_type=F32)
    q_ref[...] = (q * (A_DK ** -0.5 * LOG2E)).astype(BF16)
    k = jnp.dot(h, wb_ref[:, A_QK:2 * A_QK], preferred_element_type=F32)
    k_ref[...] = k
    kb_ref[...] = k.astype(BF16)
    v = jnp.dot(h, wb_ref[:, 2 * A_QK:], preferred_element_type=F32)
    v_ref[...] = v
    vb_ref[...] = v.astype(BF16)


def _proj_a_sample(x, g, w, layer):
    m = x.shape[0]
    n_out = 2 * A_QK + A_V
    full = lambda i: (0, 0)
    full3 = lambda i: (0, 0, 0)
    return pl.pallas_call(
        _proj_a_sample_body,
        grid=(1,),
        in_specs=[pl.BlockSpec((m, D_MODEL), full),
                  pl.BlockSpec((1, D_MODEL), full),
                  pl.BlockSpec((None, D_MODEL, n_out), lambda i: (layer, 0, 0))],
        out_specs=[pl.BlockSpec((m, A_QK), full),
                   pl.BlockSpec((m, A_QK), full),
                   pl.BlockSpec((m, A_V), full),
                   pl.BlockSpec((m, A_QK), full),
                   pl.BlockSpec((m, A_V), full),
                   pl.BlockSpec((None, D_MODEL, n_out), full3),
                   pl.BlockSpec((None, A_QK, D_MODEL), full3)],
        out_shape=[jax.ShapeDtypeStruct((m, A_QK), BF16),
                   jax.ShapeDtypeStruct((m, A_QK), F32),
                   jax.ShapeDtypeStruct((m, A_V), F32),
                   jax.ShapeDtypeStruct((m, A_QK), BF16),
                   jax.ShapeDtypeStruct((m, A_V), BF16),
                   jax.ShapeDtypeStruct((1, D_MODEL, n_out), BF16),
                   jax.ShapeDtypeStruct((1, A_QK, D_MODEL), BF16)],
        compiler_params=_params(1),
        name="proj_a_sample",
    )(x, g, w)


def _proj_b_body(x_ref, g_ref, w_ref, q_ref, k_ref, v_ref, kb_ref, vb_ref, *wb_ref, tail):
    if wb_ref:
        wb_ref[0][...] = w_ref[...].astype(BF16)
        w_ref = wb_ref[0]
    h = _rms(x_ref[...], g_ref[...]).astype(BF16)
    kv = jnp.dot(h, w_ref[:, B_Q:], preferred_element_type=F32)
    k, v = kv[:, 0:B_KV], kv[:, B_KV:]
    kb_ref[...] = jnp.concatenate([k, pltpu.roll(k, HALF, axis=1)], axis=1).astype(BF16)
    vb_ref[...] = jnp.concatenate([v, pltpu.roll(v, HALF, axis=1)], axis=1).astype(BF16)
    keep = k.shape[0] if tail is None else tail
    k_ref[...] = k[k.shape[0] - keep:]
    v_ref[...] = v[v.shape[0] - keep:]
    q = jnp.dot(h, w_ref[:, 0:B_Q], preferred_element_type=F32)
    q_ref[...] = (q * (B_HD ** -0.5 * LOG2E)).astype(BF16)


def _proj_b(x, g, w, layer, tm, *, seq=None, tail=None, cast_weights=False):
    m = x.shape[0]
    assert not cast_weights or m == tm
    n_out = B_Q + 2 * B_KV
    row = lambda i: (i, 0)
    full = lambda i: (0, 0)
    wb_spec = [pl.BlockSpec((None, D_MODEL, n_out), lambda i: (0, 0, 0))] if cast_weights else []
    wb_shape = [jax.ShapeDtypeStruct((1, D_MODEL, n_out), BF16)] if cast_weights else []
    if tail is None:
        kv_spec = pl.BlockSpec((tm, B_KV), row)
        kv_shape = jax.ShapeDtypeStruct((m, B_KV), F32)
    else:
        assert seq % tm == 0 and tail <= tm
        per_seq = seq // tm
        kv_spec = pl.BlockSpec((None, tail, B_KV), lambda i: (i // per_seq, 0, 0))
        kv_shape = jax.ShapeDtypeStruct((m // seq, tail, B_KV), F32)
    return pl.pallas_call(
        functools.partial(_proj_b_body, tail=tail),
        grid=(m // tm,),
        in_specs=[pl.BlockSpec((tm, D_MODEL), row),
                  pl.BlockSpec((1, D_MODEL), full),
                  pl.BlockSpec((None, D_MODEL, B_Q + 2 * B_KV), lambda i: (layer, 0, 0))],
        out_specs=[pl.BlockSpec((tm, B_Q), row),
                   kv_spec,
                   kv_spec,
                   pl.BlockSpec((tm, 2 * B_KV), row),
                   pl.BlockSpec((tm, 2 * B_KV), row)] + wb_spec,
        out_shape=[jax.ShapeDtypeStruct((m, B_Q), BF16),
                   kv_shape,
                   kv_shape,
                   jax.ShapeDtypeStruct((m, 2 * B_KV), BF16),
                   jax.ShapeDtypeStruct((m, 2 * B_KV), BF16)] + wb_shape,
        compiler_params=_params(1),
        name="proj_b",
    )(x, g, w)


def _t5_bucket(rel):
    nb = NUM_BUCKETS // 2
    n = -rel
    ret = jnp.where(n < 0, nb, 0)
    n = jnp.abs(n)
    max_exact = nb // 2
    nf = jnp.maximum(n, 1).astype(jnp.float32)
    large = max_exact + (jnp.log(nf / max_exact) / math.log(MAX_DIST / max_exact)
                         * (nb - max_exact)).astype(jnp.int32)
    large = jnp.minimum(large, nb - 1)
    return ret + jnp.where(n < max_exact, n, large)


def _check_far_bucket(min_dist):
    nb, max_exact = NUM_BUCKETS // 2, NUM_BUCKETS // 4
    ratio = math.log(min_dist / max_exact) / math.log(MAX_DIST / max_exact)
    assert ratio * (nb - max_exact) >= nb - 1 - max_exact + 1e-3, "far keys need one shared bucket"


def _far_bias(table, tq, w):
    far = table[NUM_BUCKETS // 2 - 1].astype(F32) * LOG2E
    return jnp.broadcast_to(far[:, None, None], (N_MAPS, tq, w))


def _near_bias(table, tq, w, back, band):
    r = jnp.arange(tq)[:, None]
    j = jnp.arange(w)[None, :] - back
    onehot = (_t5_bucket(j - r)[None] == jnp.arange(NUM_BUCKETS)[:, None, None]).astype(F32)
    onehot = lax.optimization_barrier(onehot)
    bias = jnp.einsum("bm,brj->mrj", table.astype(F32), onehot, precision=lax.Precision.HIGHEST)
    kc, qc = j // CHUNK, r // CHUNK
    mask = kc <= qc
    if band is not None:
        mask = mask & (kc >= qc - band)
    return jnp.where(mask[None], bias * LOG2E, NEG)


A_TILE = 256
A_ROWS = 64
A_HEADS_PER_STEP = 1


def _diff_lambda(lamp_ref, lam_init):
    lp = lamp_ref[...]
    return (jnp.exp(jnp.sum(lp[0:1] * lp[1:2], axis=-1, keepdims=True))
            - jnp.exp(jnp.sum(lp[2:3] * lp[3:4], axis=-1, keepdims=True)) + lam_init)


def _stack_maps(q):
    lane = lax.broadcasted_iota(jnp.int32, q.shape, 1)
    zero = jnp.zeros_like(q)
    return jnp.concatenate([jnp.where(lane < HALF, q, zero), jnp.where(lane >= HALF, q, zero)], axis=0)


def _diff_epilogue(acc, lam, g, lam_init):
    tq = acc.shape[0] // 2
    o_all = acc[:, :A_DV] / acc[:, A_DV:]
    o = o_all[:tq] - lam * o_all[tq:]
    return (_rms(o, g) * (1.0 - lam_init)).astype(BF16)


def _attn_a_scores(i, q, kt_ref, bias_ref, s_ref, m_ref):
    tl = A_TILE
    qs = _stack_maps(q)
    n_far = max(i - 1, 0)
    for kt in range(i + 1):
        cols = slice(kt * tl, (kt + 1) * tl)
        s = jnp.dot(qs, kt_ref[:, cols], preferred_element_type=F32)
        near = kt >= n_far
        if near:
            s = s + bias_ref[2 - (i - kt)].reshape(2 * tl, tl)
        s_ref[:, cols] = s
        fold = _lane_fold_max(s)
        first = kt == (n_far if near else 0)
        m_ref[int(near)] = fold if first else jnp.maximum(m_ref[int(near)], fold)
        yield
    for r in range(0, 2 * tl, A_ROWS):
        rows = slice(r, r + A_ROWS)
        mx = m_ref[1, rows, :]
        if n_far:
            c = bias_ref[0, r // tl, r % tl:r % tl + A_ROWS, 0:LANES]
            mx = jnp.maximum(mx, m_ref[0, rows, :] + c)
        m = jnp.broadcast_to(jnp.max(mx, axis=-1, keepdims=True), (A_ROWS, LANES))
        m_ref[1, rows, :] = m
        if n_far:
            m_ref[0, rows, :] = m - c


def _attn_a_numerators(i, s_ref, m_ref, v1_ref, out):
    tl = A_TILE
    n_far = max(i - 1, 0)
    acc = None
    for kt in range(i + 1):
        cols = slice(kt * tl, (kt + 1) * tl)
        p = _exp_rows(s_ref[:, cols], m_ref[1 if kt >= n_far else 0])
        part = jnp.dot(p, v1_ref[cols, :], preferred_element_type=F32)
        acc = part if acc is None else acc + part
        yield
    out.append(acc)


def _interleave(*steps):
    steps = list(steps)
    while steps:
        for gen in list(steps):
            if next(gen, StopIteration) is StopIteration:
                steps.remove(gen)


def _attn_a_prompt_body(lamp_ref, q_ref, kt_ref, v_ref, bias_ref, g_ref, o_ref,
                        s_ref, m_ref, v1_ref, ktb_ref, *, nt, nh, lam_init):
    tl = A_TILE
    lam = _diff_lambda(lamp_ref, lam_init)
    ktb_ref[...] = kt_ref[...].astype(BF16)
    for hh in range(nh):
        v1_ref[hh, :, 0:A_DV] = v_ref[0, :, hh * A_DV:(hh + 1) * A_DV]
        v1_ref[hh, :, A_DV:] = jnp.ones((nt * tl, A_DV), BF16)
    work = [(i, hh) for i in reversed(range(nt)) for hh in range(nh)]

    def scores(i, hh):
        lanes = slice(hh * LANES, (hh + 1) * LANES)
        return _attn_a_scores(i, q_ref[0, i * tl:(i + 1) * tl, lanes], ktb_ref.at[lanes, :],
                              bias_ref.at[:, 2 * hh:2 * hh + 2], s_ref.at[hh, i % 2],
                              m_ref.at[hh, i % 2])

    _interleave(scores(*work[0]))
    for n, (i, hh) in enumerate(work):
        out = []
        steps = [_attn_a_numerators(i, s_ref.at[hh, i % 2], m_ref.at[hh, i % 2], v1_ref.at[hh], out)]
        if n + 1 < len(work):
            steps.insert(0, scores(*work[n + 1]))
        _interleave(*steps)
        acc = out[0]
        o_ref[0, i * tl:(i + 1) * tl, hh * LANES:(hh + 1) * LANES] = _diff_epilogue(
            acc, lam, g_ref[...], lam_init)


def _attn_a_prompt(q, kt, v, bias, lamp, g, layer, *, lam_init):
    b, s, _ = q.shape
    nt = s // A_TILE
    nh = A_HEADS_PER_STEP
    assert s % A_TILE == 0 and A_HEADS % nh == 0
    seq = pl.BlockSpec((1, s, nh * LANES), lambda h, bi: (bi, 0, h))
    return pl.pallas_call(
        functools.partial(_attn_a_prompt_body, nt=nt, nh=nh, lam_init=lam_init),
        grid=(A_HEADS // nh, b),
        in_specs=[pl.BlockSpec((None, 4, A_DK), lambda h, bi: (layer, 0, 0)),
                  seq,
                  pl.BlockSpec((None, None, nh * LANES, s), lambda h, bi: (layer, bi, h, 0)),
                  seq,
                  pl.BlockSpec((3, 2 * nh, A_TILE, A_TILE), lambda h, bi: (0, h, 0, 0)),
                  pl.BlockSpec((None, 1, A_DV), lambda h, bi: (layer, 0, 0))],
        out_specs=seq,
        out_shape=jax.ShapeDtypeStruct((b, s, A_V), BF16),
        scratch_shapes=[pltpu.VMEM((nh, 2, 2 * A_TILE, s), F32),
                        pltpu.VMEM((nh, 2, 2, 2 * A_TILE, LANES), F32),
                        pltpu.VMEM((nh, s, 2 * A_DV), BF16),
                        pltpu.VMEM((nh * LANES, s), BF16)],
        compiler_params=_params(2),
        name="attn_a_prompt",
    )(lamp, q, kt, v, bias, g)


def _attn_a_sample_body(lamp_ref, q_ref, ktc_ref, vc_ref, kn_ref, vn_ref, bias_ref, g_ref, o_ref,
                        *, lam_init):
    tq = q_ref.shape[1]
    past = ktc_ref.shape[1]
    lam = _diff_lambda(lamp_ref, lam_init)
    for h in range(A_HEADS):
        cols = slice(h * LANES, (h + 1) * LANES)
        qs = _stack_maps(q_ref[0, :, cols])
        bias = bias_ref[2 * h:2 * h + 2].reshape(2 * tq, past + LANES)
        s_c = jnp.dot(qs, ktc_ref[cols, :].astype(BF16), preferred_element_type=F32)
        s_c = s_c + bias[:, 0:past]
        s_n = lax.dot_general(qs, kn_ref[0, :, cols], NT_DIMS, preferred_element_type=F32)
        s_n = s_n + bias[:, past:past + tq]
        m = jnp.maximum(_row_max(s_c), jnp.broadcast_to(jnp.max(s_n, axis=-1, keepdims=True),
                                                        (2 * tq, LANES)))
        p_c = _exp_rows(s_c, m)
        p_n = jnp.exp2(s_n - m[:, 0:tq]).astype(BF16)
        v_c = vc_ref[pl.ds(h, past, stride=A_HEADS), :].astype(BF16)
        acc = (jnp.dot(p_c, _with_ones(v_c), preferred_element_type=F32)
               + jnp.dot(p_n, _with_ones(vn_ref[0, :, cols]), preferred_element_type=F32))
        o_ref[0, :, cols] = _diff_epilogue(acc, lam, g_ref[...], lam_init)


def _attn_a_sample(q, ktc, vc, kn, vn, bias, lamp, g, layer, *, lam_init):
    b, tq, _ = q.shape
    past = ktc.shape[3]
    new = lambda bi: (bi, 0, 0)
    return pl.pallas_call(
        functools.partial(_attn_a_sample_body, lam_init=lam_init),
        grid=(b,),
        in_specs=[pl.BlockSpec((None, 4, A_DK), lambda bi: (layer, 0, 0)),
                  pl.BlockSpec((1, tq, A_QK), new),
                  pl.BlockSpec((None, None, A_QK, past), lambda bi: (layer, bi, 0, 0)),
                  pl.BlockSpec((None, None, past * A_HEADS, A_DV), lambda bi: (layer, bi, 0, 0)),
                  pl.BlockSpec((1, tq, A_QK), new),
                  pl.BlockSpec((1, tq, A_V), new),
                  pl.BlockSpec((N_MAPS, tq, past + LANES), lambda bi: (0, 0, 0)),
                  pl.BlockSpec((None, 1, A_DV), lambda bi: (layer, 0, 0))],
        out_specs=pl.BlockSpec((1, tq, A_V), new),
        out_shape=jax.ShapeDtypeStruct((b, tq, A_V), BF16),
        compiler_params=_params(1),
        name="attn_a_sample",
    )(lamp, q, ktc, vc, kn, vn, bias, g)


_B_ORDER_A = (0, 2, 4, 6, 9, 11, 13, 15)
_B_ORDER_B = (1, 3, 5, 7, 8, 10, 12, 14)


def _attn_b_body(sink_ref, q_ref, k_ref, v_ref, bias_ref, o_ref, s_ref, p_ref, e_ref,
                 *, layer, tq, w, back, i_off, nsub):
    lane = lax.broadcasted_iota(jnp.int32, (tq, LANES), 1)
    lo = lane < HALF
    half = B_Q_HEADS // 4
    orders = (_B_ORDER_A, _B_ORDER_B)
    sinks = [jnp.concatenate([jnp.full((tq, LANES), sink_ref[layer, h] * LOG2E, F32)
                              for h in orders[t]], axis=0) for t in range(2)]
    outs = {}

    def chain(u, t):
        tile = pl.program_id(1) * nsub + u + i_off
        variant = 1 if (u > 0 or i_off > 0) else jnp.minimum(tile, 1)
        near_start = pl.multiple_of(jnp.maximum(tile * tq - back, 0), CHUNK)
        group = slice(t * LANES, (t + 1) * LANES)
        pieces = []
        for c in range(B_Q_HEADS // 2):
            qc = q_ref[0, u * tq:(u + 1) * tq, c * LANES:(c + 1) * LANES]
            keep_lo = (c < half) == (t == 0)
            pieces.append(jnp.where(lo == keep_lo, qc, jnp.zeros_like(qc)))
        qs = jnp.concatenate(pieces, axis=0)
        s = lax.dot_general(qs, k_ref[0, pl.ds(near_start, w), group], NT_DIMS,
                            preferred_element_type=F32)
        buf = (2 * u + t) % 2
        s_ref[buf] = s + bias_ref[variant, t].reshape((B_Q_HEADS // 2) * tq, w)
        for r in range(0, (B_Q_HEADS // 2) * tq, A_ROWS):
            rows = slice(r, r + A_ROWS)
            sr = s_ref[buf, rows, :]
            sink = sinks[t][rows]
            m = jnp.maximum(_row_max(sr), sink)
            p_ref[buf, rows, :] = _exp_rows(sr, m)
            e_ref[buf, rows, :] = jnp.exp2(sink - m)
        acc = jnp.dot(p_ref[buf], _with_ones(v_ref[0, pl.ds(near_start, w), group]),
                      preferred_element_type=F32)
        outs[u, t] = acc[:, :LANES] / (acc[:, LANES:] + e_ref[buf])

    for u in range(nsub):
        for t in range(2):
            chain(u, t)

    for u in range(nsub):
        for c in range(B_Q_HEADS // 2):
            oa = outs[u, 0][c * tq:(c + 1) * tq]
            ob = outs[u, 1][c * tq:(c + 1) * tq]
            oc = jnp.where(lo, oa, ob) if c < half else jnp.where(lo, ob, oa)
            o_ref[0, u * tq:(u + 1) * tq, c * LANES:(c + 1) * LANES] = oc.astype(BF16)


def _attn_b(q, k, v, bias, sinks, layer, *, tq, w, back, i_off, nsub):
    b, sq, _ = q.shape
    sk = k.shape[1]
    rows = nsub * tq
    assert sq % rows == 0 and (sq // tq - 1 + i_off) * tq - back + w <= sk
    body = functools.partial(_attn_b_body, layer=layer, tq=tq, w=w, back=back, i_off=i_off,
                             nsub=nsub)
    return pl.pallas_call(
        body,
        grid=(b, sq // rows),
        in_specs=[pl.BlockSpec(memory_space=pltpu.SMEM),
                  pl.BlockSpec((1, rows, B_Q), lambda bi, i: (bi, i, 0)),
                  pl.BlockSpec((1, sk, 2 * B_KV), lambda bi, i: (bi, 0, 0)),
                  pl.BlockSpec((1, sk, 2 * B_KV), lambda bi, i: (bi, 0, 0)),
                  pl.BlockSpec((2, 2, B_Q_HEADS // 2, tq, w), lambda bi, i: (0, 0, 0, 0, 0))],
        out_specs=pl.BlockSpec((1, rows, B_Q), lambda bi, i: (bi, i, 0)),
        out_shape=jax.ShapeDtypeStruct((b, sq, B_Q), BF16),
        scratch_shapes=[pltpu.VMEM((2, (B_Q_HEADS // 2) * tq, w), F32),
                        pltpu.VMEM((2, (B_Q_HEADS // 2) * tq, w), BF16),
                        pltpu.VMEM((2, (B_Q_HEADS // 2) * tq, LANES), F32)],
        compiler_params=_params(2),
        name="attn_b",
    )(sinks, q, k, v, bias)


def _post_body(x_ref, o_ref, wo_ref, g_ref, wu_ref, wd_ref, gf_ref, y_ref, *refs, final, cast_weights):
    hn_ref = refs[-1]
    f = pl.program_id(1)
    if cast_weights:
        wob_ref, wub_ref, wdb_ref = refs[:3]

    @pl.when(f == 0)
    def _():
        if cast_weights:
            wob_ref[...] = wo_ref[...].astype(BF16)
        wo = (wob_ref if cast_weights else wo_ref)[...]
        x1 = x_ref[...] + jnp.dot(o_ref[...], wo, preferred_element_type=F32)
        y_ref[...] = x1
        hn_ref[...] = _rms(x1, g_ref[...]).astype(BF16)

    if cast_weights:
        wub_ref[...] = wu_ref[...].astype(BF16)
        wdb_ref[...] = wd_ref[...].astype(BF16)
        wu_ref, wd_ref = wub_ref, wdb_ref
    u = jnp.dot(hn_ref[...], wu_ref[...], preferred_element_type=F32)
    a = jnp.square(jnp.maximum(u, 0.0)).astype(BF16)
    y_ref[...] += jnp.dot(a, wd_ref[...], preferred_element_type=F32)

    if final:
        @pl.when(f == pl.num_programs(1) - 1)
        def _():
            y_ref[...] = _rms(y_ref[...], gf_ref[...])


def _post(x, o, wo, wo_layer, g, g_layer, wu, wd, w_layer, gf, *, tm, tf, final, cast_weights):
    m = x.shape[0]
    assert not cast_weights or m == tm
    row = lambda i, f: (i, 0)
    full = lambda i, f: (0, 0)
    out_specs = [pl.BlockSpec((tm, D_MODEL), row)]
    out_shape = [jax.ShapeDtypeStruct((m, D_MODEL), F32)]
    if cast_weights:
        out_specs += [pl.BlockSpec((None, D_MODEL, D_MODEL), lambda i, f: (0, 0, 0)),
                      pl.BlockSpec((None, D_MODEL, tf), lambda i, f: (0, 0, f)),
                      pl.BlockSpec((None, tf, D_MODEL), lambda i, f: (0, f, 0))]
        out_shape += [jax.ShapeDtypeStruct((1, D_MODEL, D_MODEL), BF16),
                      jax.ShapeDtypeStruct((1, D_MODEL, D_FF), BF16),
                      jax.ShapeDtypeStruct((1, D_FF, D_MODEL), BF16)]
    outs = pl.pallas_call(
        functools.partial(_post_body, final=final, cast_weights=cast_weights),
        grid=(m // tm, D_FF // tf),
        in_specs=[pl.BlockSpec((tm, D_MODEL), row),
                  pl.BlockSpec((tm, D_MODEL), row),
                  pl.BlockSpec((None, D_MODEL, D_MODEL), lambda i, f: (wo_layer, 0, 0)),
                  pl.BlockSpec((None, 1, D_MODEL), lambda i, f: (g_layer, 0, 0)),
                  pl.BlockSpec((None, D_MODEL, tf), lambda i, f: (w_layer, 0, f)),
                  pl.BlockSpec((None, tf, D_MODEL), lambda i, f: (w_layer, f, 0)),
                  pl.BlockSpec((1, D_MODEL), full)],
        out_specs=out_specs,
        out_shape=out_shape,
        scratch_shapes=[pltpu.VMEM((tm, D_MODEL), BF16)],
        compiler_params=_params(2),
        name="post",
    )(x, o, wo, g, wu, wd, gf)
    return outs if cast_weights else outs[0]


def _post_resident_body(x_ref, o_ref, wo_ref, g_ref, wu_ref, wd_ref, gf_ref, y_ref, hn_ref,
                        *, tf, final):
    x1 = x_ref[...] + jnp.dot(o_ref[...], wo_ref[...], preferred_element_type=F32)
    y_ref[...] = x1
    hn_ref[...] = _rms(x1, g_ref[...]).astype(BF16)
    mlp = None
    for f in range(D_FF // tf):
        u = jnp.dot(hn_ref[...], wu_ref[:, f * tf:(f + 1) * tf], preferred_element_type=F32)
        a = jnp.square(jnp.maximum(u, 0.0)).astype(BF16)
        part = jnp.dot(a, wd_ref[f * tf:(f + 1) * tf, :], preferred_element_type=F32)
        mlp = part if mlp is None else mlp + part
    y = y_ref[...] + mlp
    y_ref[...] = _rms(y, gf_ref[...]) if final else y


def _post_resident(x, o, wo, g, g_layer, wu, wd, gf, *, tm, tf, final):
    m = x.shape[0]
    row = lambda i: (i, 0)
    once = dict(pipeline_mode=pl.Buffered(1))
    return pl.pallas_call(
        functools.partial(_post_resident_body, tf=tf, final=final),
        grid=(m // tm,),
        in_specs=[pl.BlockSpec((tm, D_MODEL), row),
                  pl.BlockSpec((tm, D_MODEL), row),
                  pl.BlockSpec((None, D_MODEL, D_MODEL), lambda i: (0, 0, 0), **once),
                  pl.BlockSpec((None, 1, D_MODEL), lambda i: (g_layer, 0, 0)),
                  pl.BlockSpec((None, D_MODEL, D_FF), lambda i: (0, 0, 0), **once),
                  pl.BlockSpec((None, D_FF, D_MODEL), lambda i: (0, 0, 0), **once),
                  pl.BlockSpec((1, D_MODEL), lambda i: (0, 0))],
        out_specs=pl.BlockSpec((tm, D_MODEL), row),
        out_shape=jax.ShapeDtypeStruct((m, D_MODEL), F32),
        scratch_shapes=[pltpu.VMEM((tm, D_MODEL), BF16)],
        compiler_params=_params(1),
        name="post_resident",
    )(x, o, wo, g, wu, wd, gf)


_B_PROMPT = dict(tq=128, w=256, back=128)
_B_SAMPLE = dict(tq=64, w=256, back=128)
B_TILES_PER_STEP = 16
_A_SAMPLE_BACK = 128


def _a_prompt_bias(table):
    _check_far_bucket(A_TILE + 1)
    near = _near_bias(table, A_TILE, 2 * A_TILE, A_TILE, None)
    return jnp.stack([_far_bias(table, A_TILE, A_TILE), near[:, :, :A_TILE], near[:, :, A_TILE:]])


def _a_sample_bias(table, past, t):
    assert t <= LANES
    _check_far_bucket(_A_SAMPLE_BACK + 1)
    far = _far_bias(table, t, past - _A_SAMPLE_BACK)
    near = _near_bias(table, t, _A_SAMPLE_BACK + LANES, _A_SAMPLE_BACK, None)
    return jnp.concatenate([far, near], axis=2)


def _b_bias(table, tq, w, back):
    def one(bk):
        t = _near_bias(table, tq, w, bk, WINDOW // CHUNK)
        return jnp.stack([jnp.stack([t[h] for h in order]) for order in (_B_ORDER_A, _B_ORDER_B)])
    return jnp.stack([one(0), one(back)])


def _pad_keys(t, total):
    return jnp.pad(t, ((0, 0), (0, total - t.shape[1]), (0, 0)))


def _trunk(x, caches, wts, table, bf16_weights, *, tm, tm_post, tf):
    b, s, _ = x.shape
    sample = caches is not None
    n_a = wts["a_w_qkv"].shape[0]
    xm = x.reshape(b * s, D_MODEL)
    b_cfg = _B_SAMPLE if sample else _B_PROMPT
    b_bias = _b_bias(table, **b_cfg)
    if sample:
        a_bias = _a_sample_bias(table, caches["a_kt"].shape[3], s)
    else:
        a_bias = _a_prompt_bias(table)
    ak, av, bk, bv = [], [], [], []
    kt_buf = v_buf = None
    for i in range(DEPTH):
        j = i // 2
        g_mix = wts["norm_mix_g"][i][None, :]
        if sample:
            bf16_weights.append({})
        cast = bf16_weights[i]
        if i % 2 == 0:
            lam_init = _lambda_init(i)
            if sample:
                q, k, v, kb, vb, cast["qkv"], cast["kt"] = _proj_a_sample(xm, g_mix, wts["a_w_qkv"], j)
                ak.append(k.reshape(b, s, A_HEADS, 2, A_DK))
                av.append(v.reshape(b, s, A_HEADS, A_DV))
                o = _attn_a_sample(q.reshape(b, s, A_QK), caches["a_kt"], caches["a_v"],
                                   kb.reshape(b, s, A_QK), vb.reshape(b, s, A_V), a_bias,
                                   wts["a_lambda"], wts["a_subln_g"], j, lam_init=lam_init)
            else:
                q, kt_buf, v_buf, vb = _proj_a_prompt(
                    xm, g_mix, cast["qkv"], cast["kt"], 0, j, kt_buf, v_buf,
                    b=b, s=s, tm=tm if kt_buf is None else tm_post, n_a=n_a)
                o = _attn_a_prompt(q.reshape(b, s, A_QK), kt_buf, vb.reshape(b, s, A_V), a_bias,
                                   wts["a_lambda"], wts["a_subln_g"], j, lam_init=lam_init)
            wo, wo_layer = wts["a_w_o"], j
        else:
            if sample:
                q, k, v, kb, vb, cast["qkv"] = _proj_b(xm, g_mix, wts["b_w_qkv"], j, tm,
                                                       cast_weights=True)
            else:
                q, k, v, kb, vb = _proj_b(xm, g_mix, cast["qkv"], 0, tm_post, seq=s, tail=WINDOW)
            k = k.reshape(b, -1, B_KV_HEADS, B_HD)
            v = v.reshape(b, -1, B_KV_HEADS, B_HD)
            kb = kb.reshape(b, s, 2 * B_KV)
            vb = vb.reshape(b, s, 2 * B_KV)
            i_off = 0
            if sample:
                ck, cv, ckb, cvb = caches["b"][j]
                bk.append(jnp.concatenate([ck, k], axis=1)[:, s:])
                bv.append(jnp.concatenate([cv, v], axis=1)[:, s:])
                i_off = ckb.shape[1] // b_cfg["tq"]
                total = ckb.shape[1] - b_cfg["back"] + b_cfg["w"]
                kb = _pad_keys(jnp.concatenate([ckb, kb], axis=1), total)
                vb = _pad_keys(jnp.concatenate([cvb, vb], axis=1), total)
            else:
                bk.append(k)
                bv.append(v)
            o = _attn_b(q.reshape(b, s, B_Q), kb, vb, b_bias, wts["b_sinks"], j, i_off=i_off,
                        nsub=1 if sample else B_TILES_PER_STEP, **b_cfg)
            wo, wo_layer = wts["b_w_o"], j
        gf = wts["final_norm_g"][None, :]
        tiling = dict(tm=tm_post, tf=tf, final=(i == DEPTH - 1))
        if sample:
            xm, cast["wo"], cast["wu"], cast["wd"] = _post(
                xm, o.reshape(b * s, D_MODEL), wo, wo_layer, wts["norm_mlp_g"], i,
                wts["mlp_w_up"], wts["mlp_w_down"], i, gf, cast_weights=True, **tiling)
        else:
            xm = _post_resident(xm, o.reshape(b * s, D_MODEL), cast["wo"], wts["norm_mlp_g"], i,
                                cast["wu"], cast["wd"], gf, **tiling)
    if sample:
        a_k, a_v = jnp.stack(ak), jnp.stack(av)
    else:
        a_k = jnp.transpose(kt_buf.reshape(n_a, b, A_HEADS, 2, A_DK, s), (0, 1, 5, 2, 3, 4))
        a_v = v_buf.reshape(n_a, b, s, A_HEADS, A_DV)
    return xm.reshape(b, s, D_MODEL), a_k, a_v, jnp.stack(bk), jnp.stack(bv)


def _swap_halves(t):
    flat = t.reshape(t.shape[:-2] + (B_KV,))
    swapped = t[..., ::-1, :].reshape(t.shape[:-2] + (B_KV,))
    return jnp.concatenate([flat, swapped], axis=-1).astype(BF16)


def kernel(x_prompt, x_sample, cache_a_k, cache_a_v, cache_b_k, cache_b_v, rel_table,
           norm_mix_g, norm_mlp_g, final_norm_g, a_w_qkv, a_lambda, a_subln_g, a_w_o,
           b_w_qkv, b_sinks, b_w_o, mlp_w_up, mlp_w_down):
    wts = dict(norm_mix_g=norm_mix_g, norm_mlp_g=norm_mlp_g[:, None, :], final_norm_g=final_norm_g,
               a_w_qkv=a_w_qkv, a_lambda=a_lambda, a_subln_g=a_subln_g[:, None, :],
               a_w_o=a_w_o, b_w_qkv=b_w_qkv, b_sinks=b_sinks,
               b_w_o=b_w_o, mlp_w_up=mlp_w_up, mlp_w_down=mlp_w_down)
    n_a, db, past = cache_a_k.shape[:3]
    caches = dict(
        a_kt=jnp.transpose(cache_a_k, (0, 1, 3, 4, 5, 2)).reshape(n_a, db, A_QK, past),
        a_v=cache_a_v.reshape(n_a, db, past * A_HEADS, A_DV),
        b=[(cache_b_k[j], cache_b_v[j], _swap_halves(cache_b_k[j]), _swap_halves(cache_b_v[j]))
           for j in range(cache_b_k.shape[0])])
    bf16_weights = []
    ys, aks, avs, bks, bvs = _trunk(x_sample, caches, wts, rel_table, bf16_weights,
                                    tm=512, tm_post=512, tf=1024)
    yp, akp, avp, bkp, bvp = _trunk(x_prompt, None, wts, rel_table, bf16_weights,
                                    tm=512, tm_post=1024, tf=1024)
    return (yp, ys, akp, avp, bkp, bvp, aks, avs, bks, bvs)
```
